```python
import math
import jax, jax.numpy as jnp
from jax import lax
import numpy as np

D_MODEL = 1024
BATCH = 16
SEQ = 256
DEPTH = 4
DEC_BATCH = 4
DEC_SEQ = 1024
PAST_LEN = 512

GRID_W = 64
N_MIXERS = 2
N_HYENA_LAYERS = (DEPTH + N_MIXERS - 1) // N_MIXERS
N_ATTN_LAYERS = DEPTH // N_MIXERS
HEAD_DIM = 128
N_HEADS = D_MODEL // HEAD_DIM
N_KV_HEADS = 2
GROUP = N_HEADS // N_KV_HEADS
QKV_DIM = (N_HEADS + 2 * N_KV_HEADS) * HEAD_DIM
Q_BLOCK = 128
ROPE_THETA = 10000.0
D_FF = ((8 * D_MODEL + 3 * 256 - 1) // (3 * 256)) * 256
N_BANDS = 16
FILTER_EMB = 1 + 2 * N_BANDS
FILTER_HIDDEN = 64
DECAY_TARGET = 1e-2
FAST_DECAY_PCT = 0.3
SLOW_DECAY_PCT = 1.5
MIN_DECAY = math.log(DECAY_TARGET) / SLOW_DECAY_PCT
MAX_DECAY = math.log(DECAY_TARGET) / FAST_DECAY_PCT
EPS = 1e-6

kernel_name = "hyena_gqa_prefix_dit_step"


def rmsnorm(x, g):
    x32 = x.astype(jnp.float32)
    y = x32 * lax.rsqrt(jnp.mean(x32 * x32, axis=-1, keepdims=True) + EPS)
    return (y * g.astype(jnp.float32)).astype(x.dtype)


def adaln(cond, w, b):
    m = jax.nn.silu(cond) @ w + b
    return jnp.split(m[..., None, :], 6, axis=-1)


def modulate(x, g, shift, scale):
    return rmsnorm(x, g) * (1.0 + scale) + shift


def short_conv3(z, w, b):
    zp = jnp.pad(z, ((0, 0), (1, 1), (0, 0)))
    return zp[:, :-2] * w[0] + zp[:, 1:-1] * w[1] + zp[:, 2:] * w[2] + b


def hyena_filters(L, w1, b1, freq, w2, b2, w3):
    f32 = jnp.float32
    t = jnp.arange(L, dtype=f32) / L
    bands = jnp.arange(1, N_BANDS + 1, dtype=f32)
    ang = 2.0 * math.pi * t[:, None] * bands[None, :]
    feats = jnp.concatenate([t[:, None], jnp.cos(ang), jnp.sin(ang)], axis=-1)
    freq = freq.astype(f32)
    h = jnp.sin(freq[0] * (feats @ w1.astype(f32) + b1.astype(f32)))
    h = jnp.sin(freq[1] * (h @ w2.astype(f32) + b2.astype(f32)))
    h = h @ w3.astype(f32)
    deltas = jnp.abs(jnp.linspace(MIN_DECAY, MAX_DECAY, D_MODEL, dtype=f32))
    window = jnp.exp(-t[:, None] * deltas[None, :])
    h_f = h[:, :D_MODEL] * window
    h_b = h[:, D_MODEL:] * window * (jnp.arange(L) > 0).astype(f32)[:, None]
    norm = jnp.sqrt(jnp.sum(h_f * h_f + h_b * h_b, axis=0, keepdims=True) + EPS)
    return h_f / norm, h_b / norm


def bidir_long_conv(u, h_f, h_b, bias):
    L = u.shape[1]
    n = 2 * L
    u32 = u.astype(jnp.float32)
    Hf = jnp.fft.rfft(h_f, n=n, axis=0)
    Hb = jnp.fft.rfft(h_b, n=n, axis=0)
    y_f = jnp.fft.irfft(jnp.fft.rfft(u32, n=n, axis=1) * Hf, n=n, axis=1)[:, :L]
    y_b = jnp.flip(jnp.fft.irfft(jnp.fft.rfft(jnp.flip(u32, axis=1), n=n, axis=1) * Hb, n=n, axis=1)[:, :L], axis=1)
    return (y_f + y_b + u32 * bias.astype(jnp.float32)).astype(u.dtype)


def hyena_mixer(h, w_in, conv_w, conv_b, f_w1, f_b1, f_freq, f_w2, f_b2, f_w3, bias, w_out):
    L = h.shape[1]
    z = short_conv3(h @ w_in, conv_w, conv_b)
    x0, x1, v = jnp.split(z, 3, axis=-1)
    h_f, h_b = hyena_filters(L, f_w1, f_b1, f_freq, f_w2, f_b2, f_w3)
    y = x0 * bidir_long_conv(x1 * v, h_f, h_b, bias)
    return y @ w_out


def qkv_heads(h, w_qkv, q_g, k_g):
    B, T, _ = h.shape
    qkv = h @ w_qkv
    q = qkv[..., :N_HEADS * HEAD_DIM].reshape(B, T, N_HEADS, HEAD_DIM)
    k = qkv[..., N_HEADS * HEAD_DIM:(N_HEADS + N_KV_HEADS) * HEAD_DIM].reshape(B, T, N_KV_HEADS, HEAD_DIM)
    v = qkv[..., (N_HEADS + N_KV_HEADS) * HEAD_DIM:].reshape(B, T, N_KV_HEADS, HEAD_DIM)
    return rmsnorm(q, q_g), rmsnorm(k, k_g), v


def axial_rope_tables(T):
    ROWS = T // GRID_W
    row = jnp.repeat(jnp.arange(ROWS), GRID_W).astype(jnp.float32)
    col = jnp.tile(jnp.arange(GRID_W), ROWS).astype(jnp.float32)
    half = HEAD_DIM // 2
    freqs = ROPE_THETA ** (-jnp.arange(0, half, 2, dtype=jnp.float32) / half)
    ang = jnp.concatenate([row[:, None] * freqs[None, :], col[:, None] * freqs[None, :]], axis=-1)
    return jnp.cos(ang), jnp.sin(ang)


def apply_rope(x, cos, sin):
    B, T, H, Dh = x.shape
    xr = x.astype(jnp.float32).reshape(B, T, H, Dh // 2, 2)
    a, b = xr[..., 0], xr[..., 1]
    c = cos[None, :, None, :]
    s = sin[None, :, None, :]
    out = jnp.stack([a * c - b * s, a * s + b * c], axis=-1)
    return out.reshape(B, T, H, Dh).astype(x.dtype)


def block_attention(q, k, v):
    B, Tq, _, _ = q.shape
    nblk = Tq // Q_BLOCK
    scale = HEAD_DIM ** -0.5
    qb = q.reshape(B, nblk, Q_BLOCK, N_KV_HEADS, GROUP, HEAD_DIM).swapaxes(0, 1)
    k32 = k.astype(jnp.float32)

    def one_block(qblk):
        s = jnp.einsum('bqkgd,bskd->bkgqs', qblk.astype(jnp.float32), k32) * scale
        p = jax.nn.softmax(s, axis=-1)
        return jnp.einsum('bkgqs,bskd->bqkgd', p.astype(v.dtype), v)

    o = lax.map(one_block, qb)
    return o.swapaxes(0, 1).reshape(B, Tq, N_HEADS * HEAD_DIM)


def swiglu(h, wg, wu, wd):
    return (jax.nn.silu(h @ wg) * (h @ wu)) @ wd


def setup_inputs(seed: int = 0) -> dict:
    key = jax.random.key(seed)
    ks = jax.random.split(key, 32)
    f32 = jnp.float32

    def nrm(k, shape, s):
        return jax.random.normal(k, shape, f32) * s

    D = D_MODEL
    NH, NA = N_HYENA_LAYERS, N_ATTN_LAYERS
    return {
        "x_prompt": nrm(ks[0], (BATCH, SEQ, D), 1.0),
        "x_sample": nrm(ks[1], (DEC_BATCH, DEC_SEQ, D), 1.0),
        "cache_k": nrm(ks[2], (DEC_BATCH, NA, PAST_LEN, N_KV_HEADS, HEAD_DIM), 1.0),
        "cache_v": nrm(ks[3], (DEC_BATCH, NA, PAST_LEN, N_KV_HEADS, HEAD_DIM), 1.0),
        "c": nrm(ks[4], (DEC_BATCH, D), 1.0),
        "c_ctx": nrm(ks[5], (D,), 1.0),
        "mod_w": nrm(ks[6], (DEPTH, D, 6 * D), 0.5 * D ** -0.5),
        "mod_b": nrm(ks[7], (DEPTH, 6 * D), 0.02),
        "norm_mix": 1.0 + nrm(ks[8], (DEPTH, D), 0.02),
        "norm_ffn": 1.0 + nrm(ks[9], (DEPTH, D), 0.02),
        "hy_w_in": nrm(ks[10], (NH, D, 3 * D), D ** -0.5),
        "hy_conv_w": nrm(ks[11], (NH, 3, 3 * D), 3 ** -0.5),
        "hy_conv_b": nrm(ks[12], (NH, 3 * D), 0.02),
        "hy_f_w1": nrm(ks[13], (NH, FILTER_EMB, FILTER_HIDDEN), FILTER_EMB ** -0.5),
        "hy_f_b1": nrm(ks[14], (NH, FILTER_HIDDEN), 0.02),
        "hy_f_freq": 1.0 + nrm(ks[15], (NH, 2, FILTER_HIDDEN), 0.02),
        "hy_f_w2": nrm(ks[16], (NH, FILTER_HIDDEN, FILTER_HIDDEN), FILTER_HIDDEN ** -0.5),
        "hy_f_b2": nrm(ks[17], (NH, FILTER_HIDDEN), 0.02),
        "hy_f_w3": nrm(ks[18], (NH, FILTER_HIDDEN, 2 * D), FILTER_HIDDEN ** -0.5),
        "hy_bias": nrm(ks[19], (NH, D), 0.1),
        "hy_w_out": nrm(ks[20], (NH, D, D), D ** -0.5),
        "at_w_qkv": nrm(ks[21], (NA, D, QKV_DIM), D ** -0.5),
        "at_q_norm": 1.0 + nrm(ks[22], (NA, HEAD_DIM), 0.02),
        "at_k_norm": 1.0 + nrm(ks[23], (NA, HEAD_DIM), 0.02),
        "at_w_out": nrm(ks[24], (NA, N_HEADS * HEAD_DIM, D), (N_HEADS * HEAD_DIM) ** -0.5),
        "ffn_w_gate": nrm(ks[25], (DEPTH, D, D_FF), D ** -0.5),
        "ffn_w_up": nrm(ks[26], (DEPTH, D, D_FF), D ** -0.5),
        "ffn_w_down": nrm(ks[27], (DEPTH, D_FF, D), D_FF ** -0.5),
        "final_norm": 1.0 + nrm(ks[28], (D,), 0.02),
    }


def reference(x_prompt, x_sample, cache_k, cache_v, c, c_ctx,
              mod_w, mod_b, norm_mix, norm_ffn,
              hy_w_in, hy_conv_w, hy_conv_b, hy_f_w1, hy_f_b1, hy_f_freq, hy_f_w2, hy_f_b2, hy_f_w3,
              hy_bias, hy_w_out,
              at_w_qkv, at_q_norm, at_k_norm, at_w_out,
              ffn_w_gate, ffn_w_up, ffn_w_down, final_norm):
    y_p = x_prompt
    y_s = x_sample
    cos_s, sin_s = axial_rope_tables(y_s.shape[1])
    new_k_list, new_v_list = [], []

    for i in range(DEPTH):
        sh_p, sc_p, g_p, sh2_p, sc2_p, g2_p = adaln(c_ctx, mod_w[i], mod_b[i])
        sh_s, sc_s, g_s, sh2_s, sc2_s, g2_s = adaln(c, mod_w[i], mod_b[i])
        h_p = modulate(y_p, norm_mix[i], sh_p, sc_p)
        h_s = modulate(y_s, norm_mix[i], sh_s, sc_s)
        j = i // N_MIXERS
        if i % N_MIXERS == 0:
            hy = (hy_w_in[j], hy_conv_w[j], hy_conv_b[j], hy_f_w1[j], hy_f_b1[j], hy_f_freq[j],
                  hy_f_w2[j], hy_f_b2[j], hy_f_w3[j], hy_bias[j], hy_w_out[j])
            m_p = hyena_mixer(h_p, *hy)
            m_s = hyena_mixer(h_s, *hy)
        else:
            q_p, k_p, v_p = qkv_heads(h_p, at_w_qkv[j], at_q_norm[j], at_k_norm[j])
            new_k_list.append(k_p)
            new_v_list.append(v_p)
            m_p = block_attention(q_p, k_p, v_p) @ at_w_out[j]
            q_s, k_s, v_s = qkv_heads(h_s, at_w_qkv[j], at_q_norm[j], at_k_norm[j])
            q_s = apply_rope(q_s, cos_s, sin_s)
            k_s = apply_rope(k_s, cos_s, sin_s)
            k_all = jnp.concatenate([k_s, cache_k[:, j]], axis=1)
            v_all = jnp.concatenate([v_s, cache_v[:, j]], axis=1)
            m_s = block_attention(q_s, k_all, v_all) @ at_w_out[j]
        y_p = y_p + g_p * m_p
        y_s = y_s + g_s * m_s
        f_p = modulate(y_p, norm_ffn[i], sh2_p, sc2_p)
        f_s = modulate(y_s, norm_ffn[i], sh2_s, sc2_s)
        y_p = y_p + g2_p * swiglu(f_p, ffn_w_gate[i], ffn_w_up[i], ffn_w_down[i])
        y_s = y_s + g2_s * swiglu(f_s, ffn_w_gate[i], ffn_w_up[i], ffn_w_down[i])

    y_prompt = rmsnorm(y_p, final_norm)
    y_sample = rmsnorm(y_s, final_norm)
    new_k = jnp.stack(new_k_list, axis=1)
    new_v = jnp.stack(new_v_list, axis=1)
    return (y_prompt, y_sample, new_k, new_v)
```

```python
import functools
import math

import numpy as np
import jax
import jax.numpy as jnp
from jax import lax
from jax.experimental import pallas as pl
from jax.experimental.pallas import tpu as pltpu

D_MODEL = 1024
BATCH = 16
SEQ = 256
DEPTH = 4
DEC_BATCH = 4
DEC_SEQ = 1024
PAST_LEN = 512
GRID_W = 64
N_MIXERS = 2
HEAD_DIM = 128
N_HEADS = D_MODEL // HEAD_DIM
N_KV_HEADS = 2
GROUP = N_HEADS // N_KV_HEADS
QKV_DIM = (N_HEADS + 2 * N_KV_HEADS) * HEAD_DIM
ROPE_THETA = 10000.0
D_FF = ((8 * D_MODEL + 3 * 256 - 1) // (3 * 256)) * 256
N_BANDS = 16
FILTER_EMB = 1 + 2 * N_BANDS
FILTER_HIDDEN = 64
MIN_DECAY = math.log(1e-2) / 1.5
MAX_DECAY = math.log(1e-2) / 0.3
EPS = 1e-6

N_PROMPT = BATCH * SEQ
N_SAMPLE = DEC_BATCH * DEC_SEQ
N_TOK = N_PROMPT + N_SAMPLE
N_MOD = 6
MOD_ROWS = 8
FEAT_PAD = 128
FF_CHUNK = 256
N_FF_CHUNKS = D_FF // FF_CHUNK
V7X_VMEM_LIMIT = 56 * 1024 * 1024

F32 = jnp.float32
BF16 = jnp.bfloat16


def _params(n_axes, vmem=V7X_VMEM_LIMIT):
    return pltpu.CompilerParams(dimension_semantics=("arbitrary",) * n_axes, vmem_limit_bytes=vmem)


def _mod_row(i, tm):
    n_prompt_tiles = N_PROMPT // tm
    tiles_per_req = DEC_SEQ // tm
    return jnp.where(i < n_prompt_tiles, 0, 1 + jnp.maximum(i - n_prompt_tiles, 0) // tiles_per_req)


def _mod_spec(layer, which, tm):
    def index(i, *_):
        return (layer * MOD_ROWS * N_MOD + _mod_row(i, tm) * N_MOD + which, 0, 0)
    return pl.BlockSpec((1, 1, D_MODEL), index)


def _split_bf16(a):
    hi = a.astype(BF16)
    lo = (a - hi.astype(F32)).astype(BF16)
    return hi, lo


def _dot(a, b):
    return jnp.dot(a, b, preferred_element_type=F32)


def _dot3(a, b):
    a_hi, a_lo = _split_bf16(a)
    b_hi, b_lo = _split_bf16(b)
    return _dot(a_hi, b_hi) + (_dot(a_lo, b_hi) + _dot(a_hi, b_lo))


def _silu(a):
    return a / (1.0 + jnp.exp(-a))


def _rmsnorm(x, g):
    return x * lax.rsqrt(jnp.mean(x * x, axis=-1, keepdims=True) + EPS) * g


def _modulate(x, g, shift, scale):
    return _rmsnorm(x, g) * (1.0 + scale) + shift


def _adaln_kernel(c_ref, w_ref, b_ref, o_ref):
    s = _silu(c_ref[...])
    o_ref[0] = _dot3(s, w_ref[0]) + b_ref[0]


def _adaln(cond, mod_w, mod_b):
    tn = 1536
    n_out = N_MOD * D_MODEL
    return pl.pallas_call(
        _adaln_kernel,
        grid=(DEPTH, n_out // tn),
        in_specs=[
            pl.BlockSpec((MOD_ROWS, D_MODEL), lambda l, j: (0, 0)),
            pl.BlockSpec((1, D_MODEL, tn), lambda l, j: (l, 0, j)),
            pl.BlockSpec((1, 1, tn), lambda l, j: (l, 0, j)),
        ],
        out_specs=pl.BlockSpec((1, MOD_ROWS, tn), lambda l, j: (l, 0, j)),
        out_shape=jax.ShapeDtypeStruct((DEPTH, MOD_ROWS, n_out), F32),
        compiler_params=_params(2),
        name="adaln",
    )(cond, mod_w, mod_b.reshape(DEPTH, 1, n_out))


def _cast_kernel(x_ref, o_ref):
    o_ref[...] = x_ref[...].astype(BF16)


def _cast_bf16(x, rows):
    m, n = x.shape
    return pl.pallas_call(
        _cast_kernel,
        grid=(m // rows,),
        in_specs=[pl.BlockSpec((rows, n), lambda i: (i, 0))],
        out_specs=pl.BlockSpec((rows, n), lambda i: (i, 0)),
        out_shape=jax.ShapeDtypeStruct((m, n), BF16),
        compiler_params=_params(1),
        name="cast_table",
    )(x)


def _in_proj_kernel(x_ref, g_ref, sh_ref, sc_ref, w_ref, o_ref, h_s, w_s):
    i, j = pl.program_id(0), pl.program_id(1)

    @pl.when(i == 0)
    def _():
        w_s[j] = w_ref[...].astype(BF16)

    @pl.when(j == 0)
    def _():
        h_s[...] = _modulate(x_ref[...], g_ref[...], sh_ref[0], sc_ref[0]).astype(BF16)

    o_ref[...] = _dot(h_s[...], w_s[j])


def _in_proj(x, g, mods, layer, w):
    tm, tn = 1024, 1024
    n = w.shape[1]
    nj = n // tn
    return pl.pallas_call(
        _in_proj_kernel,
        grid=(N_TOK // tm, nj),
        in_specs=[
            pl.BlockSpec((tm, D_MODEL), lambda i, j: (i, 0)),
            pl.BlockSpec((1, D_MODEL), lambda i, j: (0, 0)),
            _mod_spec(layer, 0, tm),
            _mod_spec(layer, 1, tm),
            pl.BlockSpec((D_MODEL, tn), lambda i, j: (0, jnp.where(i == 0, j, nj - 1))),
        ],
        out_specs=pl.BlockSpec((tm, tn), lambda i, j: (i, j)),
        out_shape=jax.ShapeDtypeStruct((N_TOK, n), F32),
        scratch_shapes=[pltpu.VMEM((tm, D_MODEL), BF16), pltpu.VMEM((nj, D_MODEL, tn), BF16)],
        compiler_params=_params(2),
        name="hyena_in_proj",
    )(x, g.reshape(1, D_MODEL), mods, mods, w)


def _dft_tables(L):
    n = 2 * L
    k = np.arange(L, dtype=np.float64)[:, None]
    t = np.arange(L, dtype=np.float64)[None, :]
    ang = 2.0 * np.pi * k * t / n
    top = np.cos(ang)
    bot = -np.sin(ang)
    bot[0, :] = np.where(np.arange(L) % 2 == 0, 1.0, -1.0)
    fwd = np.concatenate([top, bot], axis=0)
    wk = np.full((L,), 2.0)
    wk[0] = 1.0
    inv_top = (np.cos(ang) * wk[:, None]).T / n
    inv_bot = (-2.0 * np.sin(ang)).T / n
    inv_bot[:, 0] = np.where(np.arange(L) % 2 == 0, 1.0, -1.0) / n
    inv = np.concatenate([inv_top, inv_bot], axis=1)
    return fwd.astype(np.float32), inv.astype(np.float32)


def _filter_feats(L):
    t = np.arange(L, dtype=np.float32) / np.float32(L)
    bands = np.arange(1, N_BANDS + 1, dtype=np.float32)
    ang = (np.float32(2.0 * math.pi) * t[:, None]) * bands[None, :]
    feats = np.concatenate([t[:, None], np.cos(ang), np.sin(ang)], axis=-1).astype(np.float32)
    return np.pad(feats, ((0, 0), (0, FEAT_PAD - FILTER_EMB)))


def _filter_kernel(feats_ref, w1_ref, b1_ref, fr_ref, w2_ref, b2_ref, w3f_ref, w3b_ref, dl_ref, fwd_ref,
                   g_ref, fhi_s, flo_s, *, L):
    @pl.when(pl.program_id(0) == 0)
    def _():
        hi, lo = _split_bf16(fwd_ref[...])
        fhi_s[...] = hi
        flo_s[...] = lo

    tn = g_ref.shape[1]
    h = jnp.sin(fr_ref[0:1, :] * (_dot3(feats_ref[...], w1_ref[...]) + b1_ref[...]))
    h = jnp.sin(fr_ref[1:2, :] * (_dot3(h, w2_ref[...]) + b2_ref[...]))
    rows = lax.broadcasted_iota(jnp.int32, (L, tn), 0)
    t = rows.astype(F32) / L
    window = jnp.exp(-t * dl_ref[...])
    hf = _dot3(h, w3f_ref[...]) * window
    hb = jnp.where(rows > 0, _dot3(h, w3b_ref[...]) * window, 0.0)
    norm = jnp.sqrt(jnp.sum(hf * hf + hb * hb, axis=0, keepdims=True) + EPS)
    hf = hf / norm
    hb = hb / norm

    def table_dot(lo_row, x):
        x_hi, x_lo = _split_bf16(x)
        m_hi = fhi_s[lo_row:lo_row + L, :]
        m_lo = flo_s[lo_row:lo_row + L, :]
        return _dot(m_hi, x_hi) + (_dot(m_lo, x_hi) + _dot(m_hi, x_lo))

    g_top = table_dot(0, hf + hb)
    g_bot = table_dot(L, hf - hb)
    sign = jnp.where(rows % 2 == 0, 1.0, -1.0)
    nyquist_b = jnp.sum(sign * hb, axis=0, keepdims=True)
    g_bot = g_bot + jnp.where(rows == 0, 2.0 * nyquist_b, 0.0)
    g_ref[0:L, :] = g_top
    g_ref[L:2 * L, :] = g_bot


def _hyena_spectrum(L, fwd, f_w1, f_b1, f_freq, f_w2, f_b2, f_w3):
    tn = 256
    nj = D_MODEL // tn
    feats = jnp.asarray(_filter_feats(L))
    w1 = jnp.pad(f_w1, ((0, FEAT_PAD - FILTER_EMB), (0, 0)))
    deltas = jnp.asarray(np.abs(np.linspace(MIN_DECAY, MAX_DECAY, D_MODEL, dtype=np.float32)).reshape(1, D_MODEL))
    const = lambda j: (0, 0)
    return pl.pallas_call(
        functools.partial(_filter_kernel, L=L),
        grid=(nj,),
        in_specs=[
            pl.BlockSpec((L, FEAT_PAD), const),
            pl.BlockSpec((FEAT_PAD, FILTER_HIDDEN), const),
            pl.BlockSpec((1, FILTER_HIDDEN), const),
            pl.BlockSpec((2, FILTER_HIDDEN), const),
            pl.BlockSpec((FILTER_HIDDEN, FILTER_HIDDEN), const),
            pl.BlockSpec((1, FILTER_HIDDEN), const),
            pl.BlockSpec((FILTER_HIDDEN, tn), lambda j: (0, j)),
            pl.BlockSpec((FILTER_HIDDEN, tn), lambda j: (0, nj + j)),
            pl.BlockSpec((1, tn), lambda j: (0, j)),
            pl.BlockSpec((2 * L, L), const),
        ],
        out_specs=pl.BlockSpec((2 * L, tn), lambda j: (0, j)),
        out_shape=jax.ShapeDtypeStruct((2 * L, D_MODEL), F32),
        scratch_shapes=[pltpu.VMEM((2 * L, L), BF16), pltpu.VMEM((2 * L, L), BF16)],
        compiler_params=_params(1),
        name="hyena_spectrum_%d" % L,
    )(feats, w1, f_b1.reshape(1, -1), f_freq, f_w2, f_b2.reshape(1, -1), f_w3, f_w3, deltas, fwd)


def _hyena_conv_kernel(x0_ref, x1_ref, v_ref, cw0_ref, cw1_ref, cwv_ref, cb0_ref, cb1_ref, cbv_ref,
                       g_ref, bias_ref, fwd_ref, inv_ref, o_ref, *, L):
    tn = o_ref.shape[1]
    rows = lax.broadcasted_iota(jnp.int32, (L, tn), 0)
    first = rows == 0
    last = rows == L - 1

    def short_conv(z_ref, cw_ref, cb_ref):
        z = z_ref[...]
        z_prev = jnp.where(first, 0.0, pltpu.roll(z, 1, 0))
        z_next = jnp.where(last, 0.0, pltpu.roll(z, L - 1, 0))
        return z_prev * cw_ref[0:1, :] + z * cw_ref[1:2, :] + z_next * cw_ref[2:3, :] + cb_ref[...]

    u = short_conv(x1_ref, cw1_ref, cb1_ref) * short_conv(v_ref, cwv_ref, cbv_ref)
    spec = _dot(fwd_ref[...], u.astype(BF16))
    u_top, u_bot = spec[0:L, :], spec[L:2 * L, :]
    g_top, g_bot = g_ref[0:L, :], g_ref[L:2 * L, :]
    y_top = u_top * g_top - jnp.where(first, 0.0, u_bot * g_bot)
    y_bot = jnp.where(first, u_bot * g_bot, u_top * g_bot + u_bot * g_top)
    y_spec = jnp.concatenate([y_top, y_bot], axis=0).astype(BF16)
    y = _dot(inv_ref[...], y_spec)
    x0 = short_conv(x0_ref, cw0_ref, cb0_ref)
    o_ref[...] = (x0 * (y + u * bias_ref[...])).astype(BF16)


def _hyena_conv(z, conv_w, conv_b, spectrum, bias, fwd, inv, L, row_block0, n_seq, tn):
    nj = D_MODEL // tn
    cb = conv_b.reshape(1, 3 * D_MODEL)

    def zspec(part):
        return pl.BlockSpec((L, tn), lambda j, b: (row_block0 + b, part * nj + j))

    def cwspec(part):
        return pl.BlockSpec((3, tn), lambda j, b: (0, part * nj + j))

    def cbspec(part):
        return pl.BlockSpec((1, tn), lambda j, b: (0, part * nj + j))

    return pl.pallas_call(
        functools.partial(_hyena_conv_kernel, L=L),
        grid=(nj, n_seq),
        in_specs=[
            zspec(0), zspec(1), zspec(2),
            cwspec(0), cwspec(1), cwspec(2),
            cbspec(0), cbspec(1), cbspec(2),
            pl.BlockSpec((2 * L, tn), lambda j, b: (0, j)),
            pl.BlockSpec((1, tn), lambda j, b: (0, j)),
            pl.BlockSpec((2 * L, L), lambda j, b: (0, 0)),
            pl.BlockSpec((L, 2 * L), lambda j, b: (0, 0)),
        ],
        out_specs=pl.BlockSpec((L, tn), lambda j, b: (b, j)),
        out_shape=jax.ShapeDtypeStruct((n_seq * L, D_MODEL), BF16),
        compiler_params=_params(2),
        name="hyena_conv_%d" % L,
    )(z, z, z, conv_w, conv_w, conv_w, cb, cb, cb, spectrum, bias.reshape(1, D_MODEL), fwd, inv)


def _out_proj_kernel(ap_ref, as_ref, y_ref, gate_ref, w_ref, o_ref, w_s, *, n_prompt_tiles):
    i = pl.program_id(0)

    @pl.when(i == 0)
    def _():
        w_s[...] = w_ref[...].astype(BF16)

    def finish(a):
        o_ref[...] = y_ref[...] + gate_ref[0] * _dot(a, w_s[...])

    @pl.when(i < n_prompt_tiles)
    def _():
        finish(ap_ref[...])

    @pl.when(i >= n_prompt_tiles)
    def _():
        finish(as_ref[...])


def _out_proj(a_prompt, a_sample, y, mods, layer, w):
    tm = 512
    npt = N_PROMPT // tm
    return pl.pallas_call(
        functools.partial(_out_proj_kernel, n_prompt_tiles=npt),
        grid=(N_TOK // tm,),
        in_specs=[
            pl.BlockSpec((tm, D_MODEL), lambda i: (jnp.minimum(i, npt - 1), 0)),
            pl.BlockSpec((tm, D_MODEL), lambda i: (jnp.maximum(i - npt, 0), 0)),
            pl.BlockSpec((tm, D_MODEL), lambda i: (i, 0)),
            _mod_spec(layer, 2, tm),
            pl.BlockSpec((D_MODEL, D_MODEL), lambda i: (0, 0)),
        ],
        out_specs=pl.BlockSpec((tm, D_MODEL), lambda i: (i, 0)),
        out_shape=jax.ShapeDtypeStruct((N_TOK, D_MODEL), F32),
        scratch_shapes=[pltpu.VMEM((D_MODEL, D_MODEL), BF16)],
        compiler_params=_params(1),
        name="out_proj",
    )(a_prompt, a_sample, y, mods, w)


def _rope_tables():
    rows = DEC_SEQ // GRID_W
    row = np.repeat(np.arange(rows), GRID_W).astype(np.float32)
    col = np.tile(np.arange(GRID_W), rows).astype(np.float32)
    half = HEAD_DIM // 2
    freqs = (np.float32(ROPE_THETA) ** (-np.arange(0, half, 2, dtype=np.float32) / np.float32(half))).astype(np.float32)
    ang = np.concatenate([row[:, None] * freqs[None, :], col[:, None] * freqs[None, :]], axis=-1)
    cos = np.repeat(np.cos(ang), 2, axis=-1)
    sin = np.repeat(np.sin(ang), 2, axis=-1)
    sign = np.where(np.arange(HEAD_DIM) % 2 == 0, -1.0, 1.0)[None, :]
    return cos.astype(np.float32), (sin * sign).astype(np.float32)


def _qkv_kernel(x_ref, g_ref, sh_ref, sc_ref, w_ref, qg_ref, kg_ref, cos_ref, sin_ref,
                q_ref, k_ref, v_ref, kf_ref, vf_ref, w_s, *, n_prompt_tiles):
    i = pl.program_id(0)
    tm = x_ref.shape[0]

    @pl.when(i == 0)
    def _():
        w_s[...] = w_ref[...].astype(BF16)

    h = _modulate(x_ref[...], g_ref[...], sh_ref[0], sc_ref[0]).astype(BF16)
    qkv = _dot(h, w_s[...])
    k_col = N_HEADS * HEAD_DIM
    v_col = k_col + N_KV_HEADS * HEAD_DIM
    v = qkv[:, v_col:]
    vf_ref[...] = v
    v_ref[...] = v.astype(BF16)

    def head_norm(col, gain_ref):
        return _rmsnorm(qkv[:, col:col + HEAD_DIM], gain_ref[...])

    lane = lax.broadcasted_iota(jnp.int32, (tm, HEAD_DIM), 1)
    even = lane % 2 == 0
    q_scale = HEAD_DIM ** -0.5

    def rotate(x):
        partner = jnp.where(even, pltpu.roll(x, HEAD_DIM - 1, 1), pltpu.roll(x, 1, 1))
        return x * cos_ref[...] + partner * sin_ref[...]

    for kv in range(N_KV_HEADS):
        col = k_col + kv * HEAD_DIM
        kf_ref[:, kv * HEAD_DIM:(kv + 1) * HEAD_DIM] = head_norm(col, kg_ref)

    @pl.when(i < n_prompt_tiles)
    def _():
        for hd in range(N_HEADS):
            q_ref[:, hd * HEAD_DIM:(hd + 1) * HEAD_DIM] = (head_norm(hd * HEAD_DIM, qg_ref) * q_scale).astype(BF16)
        for kv in range(N_KV_HEADS):
            k_ref[:, kv * HEAD_DIM:(kv + 1) * HEAD_DIM] = head_norm(k_col + kv * HEAD_DIM, kg_ref).astype(BF16)

    @pl.when(i >= n_prompt_tiles)
    def _():
        for hd in range(N_HEADS):
            q = rotate(head_norm(hd * HEAD_DIM, qg_ref))
            q_ref[:, hd * HEAD_DIM:(hd + 1) * HEAD_DIM] = (q * q_scale).astype(BF16)
        for kv in range(N_KV_HEADS):
            k = rotate(head_norm(k_col + kv * HEAD_DIM, kg_ref))
            k_ref[:, kv * HEAD_DIM:(kv + 1) * HEAD_DIM] = k.astype(BF16)


def _qkv(x, g, mods, layer, w, q_gain, k_gain):
    tm = 512
    npt = N_PROMPT // tm
    tiles_per_req = DEC_SEQ // tm
    cos, sin = _rope_tables()
    kv_dim = N_KV_HEADS * HEAD_DIM
    rope_spec = pl.BlockSpec((tm, HEAD_DIM), lambda i: (jnp.maximum(i - npt, 0) % tiles_per_req, 0))
    row = lambda n: pl.BlockSpec((tm, n), lambda i: (i, 0))
    return pl.pallas_call(
        functools.partial(_qkv_kernel, n_prompt_tiles=npt),
        grid=(N_TOK // tm,),
        in_specs=[
            row(D_MODEL),
            pl.BlockSpec((1, D_MODEL), lambda i: (0, 0)),
            _mod_spec(layer, 0, tm),
            _mod_spec(layer, 1, tm),
            pl.BlockSpec((D_MODEL, QKV_DIM), lambda i: (0, 0)),
            pl.BlockSpec((1, HEAD_DIM), lambda i: (0, 0)),
            pl.BlockSpec((1, HEAD_DIM), lambda i: (0, 0)),
            rope_spec, rope_spec,
        ],
        out_specs=[row(D_MODEL), row(kv_dim), row(kv_dim), row(kv_dim), row(kv_dim)],
        out_shape=[
            jax.ShapeDtypeStruct((N_TOK, D_MODEL), BF16),
            jax.ShapeDtypeStruct((N_TOK, kv_dim), BF16),
            jax.ShapeDtypeStruct((N_TOK, kv_dim), BF16),
            jax.ShapeDtypeStruct((N_TOK, kv_dim), F32),
            jax.ShapeDtypeStruct((N_TOK, kv_dim), F32),
        ],
        scratch_shapes=[pltpu.VMEM((D_MODEL, QKV_DIM), BF16)],
        compiler_params=_params(1),
        name="qkv_proj",
    )(x, g.reshape(1, D_MODEL), mods, mods, w, q_gain.reshape(1, HEAD_DIM), k_gain.reshape(1, HEAD_DIM),
      jnp.asarray(cos), jnp.asarray(sin))


def _attention_kernel(*refs, with_cache):
    if with_cache:
        q_ref, k_ref, v_ref, ck_ref, cv_ref, o_ref = refs
        k = jnp.concatenate([k_ref[...], ck_ref[0, 0].astype(BF16)], axis=0)
        v = jnp.concatenate([v_ref[...], cv_ref[0, 0].astype(BF16)], axis=0)
    else:
        q_ref, k_ref, v_ref, o_ref = refs
        k = k_ref[...]
        v = v_ref[...]
    for g in range(GROUP):
        q = q_ref[:, g * HEAD_DIM:(g + 1) * HEAD_DIM]
        s = lax.dot_general(q, k, (((1,), (1,)), ((), ())), preferred_element_type=F32)
        p = jnp.exp(s - jnp.max(s, axis=-1, keepdims=True))
        denom = jnp.sum(p, axis=-1, keepdims=True)
        o = _dot(p.astype(BF16), v) / denom
        o_ref[:, g * HEAD_DIM:(g + 1) * HEAD_DIM] = o.astype(BF16)


def _attention(q, k, v, row0, n_req, seq, tq, cache_k=None, cache_v=None, cache_layer=0):
    qb0, kb0 = row0 // tq, row0 // seq
    nq = seq // tq
    gd = GROUP * HEAD_DIM
    in_specs = [
        pl.BlockSpec((tq, gd), lambda b, h, t: (qb0 + b * nq + t, h)),
        pl.BlockSpec((seq, HEAD_DIM), lambda b, h, t: (kb0 + b, h)),
        pl.BlockSpec((seq, HEAD_DIM), lambda b, h, t: (kb0 + b, h)),
    ]
    args = [q, k, v]
    with_cache = cache_k is not None
    if with_cache:
        cspec = pl.BlockSpec((1, 1, PAST_LEN, HEAD_DIM), lambda b, h, t: (b, cache_layer, 0, h))
        in_specs += [cspec, cspec]
        args += [cache_k, cache_v]
    return pl.pallas_call(
        functools.partial(_attention_kernel, with_cache=with_cache),
        grid=(n_req, N_KV_HEADS, nq),
        in_specs=in_specs,
        out_specs=pl.BlockSpec((tq, gd), lambda b, h, t: (b * nq + t, h)),
        out_shape=jax.ShapeDtypeStruct((n_req * seq, D_MODEL), BF16),
        compiler_params=_params(3),
        name="attention_%d" % seq,
    )(*args)


def _ffn_kernel(y_ref, g_ref, sh_ref, sc_ref, gate_ref, wg_ref, wu_ref, wd_ref, fin_ref, o_ref,
                f_s, acc_s, wg_s, wu_s, wd_s, *, final):
    i, j = pl.program_id(0), pl.program_id(1)

    @pl.when(i == 0)
    def _():
        wg_s[j] = wg_ref[...].astype(BF16)
        wu_s[j] = wu_ref[...].astype(BF16)
        wd_s[j] = wd_ref[...].astype(BF16)

    @pl.when(j == 0)
    def _():
        f_s[...] = _modulate(y_ref[...], g_ref[...], sh_ref[0], sc_ref[0]).astype(BF16)
        acc_s[...] = jnp.zeros_like(acc_s)

    f = f_s[...]
    hidden = _silu(_dot(f, wg_s[j])) * _dot(f, wu_s[j])
    acc_s[...] += _dot(hidden.astype(BF16), wd_s[j])

    @pl.when(j == N_FF_CHUNKS - 1)
    def _():
        out = y_ref[...] + gate_ref[0] * acc_s[...]
        if final:
            out = _rmsnorm(out, fin_ref[...])
        o_ref[...] = out


def _ffn(y, g, mods, layer, wg, wu, wd, final_norm, final):
    tm = 1024
    nf = N_FF_CHUNKS
    chunk = lambda i, j: jnp.where(i == 0, j, nf - 1)
    return pl.pallas_call(
        functools.partial(_ffn_kernel, final=final),
        grid=(N_TOK // tm, nf),
        in_specs=[
            pl.BlockSpec((tm, D_MODEL), lambda i, j: (i, 0)),
            pl.BlockSpec((1, D_MODEL), lambda i, j: (0, 0)),
            _mod_spec(layer, 3, tm),
            _mod_spec(layer, 4, tm),
            _mod_spec(layer, 5, tm),
            pl.BlockSpec((D_MODEL, FF_CHUNK), lambda i, j: (0, chunk(i, j))),
            pl.BlockSpec((D_MODEL, FF_CHUNK), lambda i, j: (0, chunk(i, j))),
            pl.BlockSpec((FF_CHUNK, D_MODEL), lambda i, j: (chunk(i, j), 0)),
            pl.BlockSpec((1, D_MODEL), lambda i, j: (0, 0)),
        ],
        out_specs=pl.BlockSpec((tm, D_MODEL), lambda i, j: (i, 0)),
        out_shape=jax.ShapeDtypeStruct((N_TOK, D_MODEL), F32),
        scratch_shapes=[
            pltpu.VMEM((tm, D_MODEL), BF16),
            pltpu.VMEM((tm, D_MODEL), F32),
            pltpu.VMEM((nf, D_MODEL, FF_CHUNK), BF16),
            pltpu.VMEM((nf, D_MODEL, FF_CHUNK), BF16),
            pltpu.VMEM((nf, FF_CHUNK, D_MODEL), BF16),
        ],
        compiler_params=_params(2),
        name="ffn",
    )(y, g.reshape(1, D_MODEL), mods, mods, mods, wg, wu, wd, final_norm.reshape(1, D_MODEL))


def kernel(x_prompt, x_sample, cache_k, cache_v, c, c_ctx, mod_w, mod_b, norm_mix, norm_ffn, hy_w_in, hy_conv_w, hy_conv_b, hy_f_w1, hy_f_b1, hy_f_freq, hy_f_w2, hy_f_b2, hy_f_w3, hy_bias, hy_w_out, at_w_qkv, at_q_norm, at_k_norm, at_w_out, ffn_w_gate, ffn_w_up, ffn_w_down, final_norm):
    y = jnp.concatenate([x_prompt.reshape(N_PROMPT, D_MODEL), x_sample.reshape(N_SAMPLE, D_MODEL)], axis=0)
    cond = jnp.concatenate([c_ctx[None, :], c, jnp.zeros((MOD_ROWS - 1 - DEC_BATCH, D_MODEL), F32)], axis=0)
    mods = _adaln(cond, mod_w, mod_b).reshape(DEPTH * MOD_ROWS * N_MOD, 1, D_MODEL)

    tables = {}
    for L in (SEQ, DEC_SEQ):
        fwd, inv = _dft_tables(L)
        fwd = jnp.asarray(fwd)
        tables[L] = (fwd, _cast_bf16(fwd, 2 * L // 4), _cast_bf16(jnp.asarray(inv), L // 4))

    kv_dim = N_KV_HEADS * HEAD_DIM
    cache_k = cache_k.reshape(DEC_BATCH, -1, PAST_LEN, kv_dim)
    cache_v = cache_v.reshape(DEC_BATCH, -1, PAST_LEN, kv_dim)
    new_k, new_v = [], []
    for layer in range(DEPTH):
        j = layer // N_MIXERS
        if layer % N_MIXERS == 0:
            z = _in_proj(y, norm_mix[layer], mods, layer, hy_w_in[j])
            mixed = []
            for L, row_block0, n_seq, tn in ((SEQ, 0, BATCH, D_MODEL), (DEC_SEQ, N_PROMPT // DEC_SEQ, DEC_BATCH, 256)):
                fwd_f32, fwd_bf16, inv_bf16 = tables[L]
                spectrum = _hyena_spectrum(L, fwd_f32, hy_f_w1[j], hy_f_b1[j], hy_f_freq[j], hy_f_w2[j],
                                           hy_f_b2[j], hy_f_w3[j])
                mixed.append(_hyena_conv(z, hy_conv_w[j], hy_conv_b[j], spectrum, hy_bias[j], fwd_bf16, inv_bf16,
                                         L, row_block0, n_seq, tn))
            y = _out_proj(mixed[0], mixed[1], y, mods, layer, hy_w_out[j])
        else:
            q, k, v, k_f32, v_f32 = _qkv(y, norm_mix[layer], mods, layer, at_w_qkv[j], at_q_norm[j], at_k_norm[j])
            new_k.append(k_f32[:N_PROMPT].reshape(BATCH, SEQ, N_KV_HEADS, HEAD_DIM))
            new_v.append(v_f32[:N_PROMPT].reshape(BATCH, SEQ, N_KV_HEADS, HEAD_DIM))
            o_prompt = _attention(q, k, v, 0, BATCH, SEQ, SEQ)
            o_sample = _attention(q, k, v, N_PROMPT, DEC_BATCH, DEC_SEQ, 512, cache_k, cache_v, j)
            y = _out_proj(o_prompt, o_sample, y, mods, layer, at_w_out[j])
        y = _ffn(y, norm_ffn[layer], mods, layer, ffn_w_gate[layer], ffn_w_up[layer], ffn_w_down[layer],
                 final_norm, final=(layer == DEPTH - 1))

    y_prompt = y[:N_PROMPT].reshape(BATCH, SEQ, D_MODEL)
    y_sample = y[N_PROMPT:].reshape(DEC_BATCH, DEC_SEQ, D_MODEL)
    return (y_prompt, y_sample, jnp.stack(new_k, axis=1), jnp.stack(new_v, axis=1))
```

```python
import functools
import math

import numpy as np
import jax
import jax.numpy as jnp
from jax import lax
from jax.experimental import pallas as pl
from jax.experimental.pallas import tpu as pltpu

D_MODEL = 1024
BATCH = 16
SEQ = 256
DEPTH = 4
DEC_BATCH = 4
DEC_SEQ = 1024
PAST_LEN = 512
GRID_W = 64
N_MIXERS = 2
HEAD_DIM = 128
N_HEADS = D_MODEL // HEAD_DIM
N_KV_HEADS = 2
GROUP = N_HEADS // N_KV_HEADS
KV_DIM = N_KV_HEADS * HEAD_DIM
QKV_DIM = (N_HEADS + 2 * N_KV_HEADS) * HEAD_DIM
ROPE_THETA = 10000.0
D_FF = ((8 * D_MODEL + 3 * 256 - 1) // (3 * 256)) * 256
N_BANDS = 16
FILTER_EMB = 1 + 2 * N_BANDS
FILTER_HIDDEN = 64
MIN_DECAY = math.log(1e-2) / 1.5
MAX_DECAY = math.log(1e-2) / 0.3
EPS = 1e-6

N_PROMPT = BATCH * SEQ
N_SAMPLE = DEC_BATCH * DEC_SEQ
N_TOK = N_PROMPT + N_SAMPLE
N_MOD = 6
MOD_ROWS = 8
FEAT_PAD = 128
FF_CHUNK = 256
N_FF_CHUNKS = D_FF // FF_CHUNK
V7X_VMEM_LIMIT = 56 * 1024 * 1024

F32 = jnp.float32
BF16 = jnp.bfloat16


def _params(n_axes, vmem=V7X_VMEM_LIMIT):
    return pltpu.CompilerParams(dimension_semantics=("arbitrary",) * n_axes, vmem_limit_bytes=vmem)


def _mod_row(tile, tm):
    n_prompt_tiles = N_PROMPT // tm
    tiles_per_req = DEC_SEQ // tm
    return jnp.where(tile < n_prompt_tiles, 0, 1 + jnp.maximum(tile - n_prompt_tiles, 0) // tiles_per_req)


def _mod_spec(layer, which, tm, tile_of=lambda i, *_: i):
    def index(*ids):
        return (layer * MOD_ROWS * N_MOD + _mod_row(tile_of(*ids), tm) * N_MOD + which, 0, 0)
    return pl.BlockSpec((1, 1, D_MODEL), index)


def _pair_specs(tm, width, tile_of=lambda i, *_: i):
    npt = N_PROMPT // tm
    return [pl.BlockSpec((tm, width), lambda *ids: (jnp.minimum(tile_of(*ids), npt - 1), 0)),
            pl.BlockSpec((tm, width), lambda *ids: (jnp.maximum(tile_of(*ids) - npt, 0), 0))]


def _pair_shapes(width, dtype):
    return [jax.ShapeDtypeStruct((N_PROMPT, width), dtype), jax.ShapeDtypeStruct((N_SAMPLE, width), dtype)]


def _stream_specs(n_arrays, tm, tile_of=lambda i, *_: i):
    if n_arrays == 2:
        return _pair_specs(tm, D_MODEL, tile_of)
    return [pl.BlockSpec((tm, D_MODEL), lambda *ids: (tile_of(*ids), 0))]


def _read_stream(refs, is_prompt):
    if len(refs) == 2:
        return jnp.where(is_prompt, refs[0][...], refs[1][...])
    return refs[0][...]


def _split_bf16(a):
    hi = a.astype(BF16)
    lo = (a - hi.astype(F32)).astype(BF16)
    return hi, lo


def _dot(a, b):
    return jnp.dot(a, b, preferred_element_type=F32)


def _dot3(a, b):
    a_hi, a_lo = _split_bf16(a)
    b_hi, b_lo = _split_bf16(b)
    return _dot(a_hi, b_hi) + (_dot(a_lo, b_hi) + _dot(a_hi, b_lo))


def _silu(a):
    return a / (1.0 + jnp.exp(-a))


def _rmsnorm(x, g):
    return x * lax.rsqrt(jnp.mean(x * x, axis=-1, keepdims=True) + EPS) * g


def _modulate(x, g, shift, scale):
    return _rmsnorm(x, g) * (1.0 + scale) + shift


def _adaln_kernel(c_ref, w_ref, b_ref, o_ref):
    s = _silu(c_ref[...])
    o_ref[0] = _dot3(s, w_ref[0]) + b_ref[0]


def _adaln(cond, mod_w, mod_b):
    tn = 1536
    n_out = N_MOD * D_MODEL
    return pl.pallas_call(
        _adaln_kernel,
        grid=(DEPTH, n_out // tn),
        in_specs=[
            pl.BlockSpec((MOD_ROWS, D_MODEL), lambda l, j: (0, 0)),
            pl.BlockSpec((1, D_MODEL, tn), lambda l, j: (l, 0, j)),
            pl.BlockSpec((1, 1, tn), lambda l, j: (l, 0, j)),
        ],
        out_specs=pl.BlockSpec((1, MOD_ROWS, tn), lambda l, j: (l, 0, j)),
        out_shape=jax.ShapeDtypeStruct((DEPTH, MOD_ROWS, n_out), F32),
        compiler_params=_params(2),
        name="adaln",
    )(cond, mod_w, mod_b.reshape(DEPTH, 1, n_out))


def _cast_kernel(x_ref, o_ref):
    o_ref[...] = x_ref[...].astype(BF16)


def _cast_bf16(x, rows):
    m, n = x.shape
    return pl.pallas_call(
        _cast_kernel,
        grid=(m // rows,),
        in_specs=[pl.BlockSpec((rows, n), lambda i: (i, 0))],
        out_specs=pl.BlockSpec((rows, n), lambda i: (i, 0)),
        out_shape=jax.ShapeDtypeStruct((m, n), BF16),
        compiler_params=_params(1),
        name="cast_table",
    )(x)


def _in_proj_kernel(*refs, n_stream, n_prompt_tiles):
    x_refs, refs = refs[:n_stream], refs[n_stream:]
    (g_ref, sh_ref, sc_ref, w0_ref, w1_ref, w2_ref, cw0_ref, cw1_ref, cw2_ref, cb0_ref, cb1_ref, cb2_ref,
     x0_ref, u_ref, h_s, w_s) = refs
    i, c = pl.program_id(0), pl.program_id(1)
    tm, tc = x0_ref.shape

    @pl.when(i == 0)
    def _():
        w_s[c, 0] = w0_ref[...].astype(BF16)
        w_s[c, 1] = w1_ref[...].astype(BF16)
        w_s[c, 2] = w2_ref[...].astype(BF16)

    @pl.when(c == 0)
    def _():
        x = _read_stream(x_refs, i < n_prompt_tiles)
        h_s[...] = _modulate(x, g_ref[...], sh_ref[0], sc_ref[0]).astype(BF16)

    h = h_s[...]
    seq_mask = jnp.where(i < n_prompt_tiles, SEQ - 1, DEC_SEQ - 1)
    pos = lax.broadcasted_iota(jnp.int32, (tm, tc), 0) & seq_mask
    first = pos == 0
    last = pos == seq_mask

    def short_conv(part, cw_ref, cb_ref):
        z = _dot(h, w_s[c, part])
        z_prev = jnp.where(first, 0.0, pltpu.roll(z, 1, 0))
        z_next = jnp.where(last, 0.0, pltpu.roll(z, tm - 1, 0))
        return z_prev * cw_ref[0:1, :] + z * cw_ref[1:2, :] + z_next * cw_ref[2:3, :] + cb_ref[...]

    x0_ref[...] = short_conv(0, cw0_ref, cb0_ref).astype(BF16)
    u_ref[...] = (short_conv(1, cw1_ref, cb1_ref) * short_conv(2, cw2_ref, cb2_ref)).astype(BF16)


def _in_proj(stream, norm_mix, mods, layer, w_in, conv_w, conv_b, jl):
    tm, tc = 1024, 512
    nc = D_MODEL // tc
    npt = N_PROMPT // tm

    def wspec(part):
        return pl.BlockSpec((None, D_MODEL, tc), lambda i, c: (jl, 0, part * nc + jnp.where(i == 0, c, nc - 1)))

    def cwspec(part):
        return pl.BlockSpec((None, 3, tc), lambda i, c: (jl, 0, part * nc + c))

    def cbspec(part):
        return pl.BlockSpec((None, 1, tc), lambda i, c: (jl, 0, part * nc + c))

    out = pl.BlockSpec((tm, tc), lambda i, c: (i, c))
    cb = conv_b.reshape(-1, 1, 3 * D_MODEL)
    return pl.pallas_call(
        functools.partial(_in_proj_kernel, n_stream=len(stream), n_prompt_tiles=npt),
        grid=(N_TOK // tm, nc),
        in_specs=_stream_specs(len(stream), tm) + [
            pl.BlockSpec((None, 1, D_MODEL), lambda i, c: (layer, 0, 0)),
            _mod_spec(layer, 0, tm),
            _mod_spec(layer, 1, tm),
            wspec(0), wspec(1), wspec(2),
            cwspec(0), cwspec(1), cwspec(2),
            cbspec(0), cbspec(1), cbspec(2),
        ],
        out_specs=[out, out],
        out_shape=[jax.ShapeDtypeStruct((N_TOK, D_MODEL), BF16)] * 2,
        scratch_shapes=[pltpu.VMEM((tm, D_MODEL), BF16), pltpu.VMEM((nc, 3, D_MODEL, tc), BF16)],
        compiler_params=_params(2),
        name="hyena_in_proj",
    )(*stream, norm_mix, mods, mods, w_in, w_in, w_in, conv_w, conv_w, conv_w, cb, cb, cb)


def _dft_tables(L):
    n = 2 * L
    k = np.arange(L, dtype=np.float64)[:, None]
    t = np.arange(L, dtype=np.float64)[None, :]
    ang = 2.0 * np.pi * k * t / n
    top = np.cos(ang)
    bot = -np.sin(ang)
    bot[0, :] = np.where(np.arange(L) % 2 == 0, 1.0, -1.0)
    fwd = np.concatenate([top, bot], axis=0)
    wk = np.full((L,), 2.0)
    wk[0] = 1.0
    inv_top = (np.cos(ang) * wk[:, None]).T / n
    inv_bot = (-2.0 * np.sin(ang)).T / n
    inv_bot[:, 0] = np.where(np.arange(L) % 2 == 0, 1.0, -1.0) / n
    inv = np.concatenate([inv_top, inv_bot], axis=1)
    return fwd.astype(np.float32), inv.astype(np.float32)


def _filter_feats(L):
    t = np.arange(L, dtype=np.float32) / np.float32(L)
    bands = np.arange(1, N_BANDS + 1, dtype=np.float32)
    ang = (np.float32(2.0 * math.pi) * t[:, None]) * bands[None, :]
    feats = np.concatenate([t[:, None], np.cos(ang), np.sin(ang)], axis=-1).astype(np.float32)
    return np.pad(feats, ((0, 0), (0, FEAT_PAD - FILTER_EMB)))


def _filter_kernel(feats_ref, w1_ref, b1_ref, fr_ref, w2_ref, b2_ref, w3f_ref, w3b_ref, dl_ref, fwd_ref,
                   g_ref, fhi_s, flo_s, *, L):
    @pl.when(pl.program_id(0) == 0)
    def _():
        hi, lo = _split_bf16(fwd_ref[...])
        fhi_s[...] = hi
        flo_s[...] = lo

    tn = g_ref.shape[1]
    h = jnp.sin(fr_ref[0:1, :] * (_dot3(feats_ref[...], w1_ref[...]) + b1_ref[...]))
    h = jnp.sin(fr_ref[1:2, :] * (_dot3(h, w2_ref[...]) + b2_ref[...]))
    rows = lax.broadcasted_iota(jnp.int32, (L, tn), 0)
    t = rows.astype(F32) / L
    window = jnp.exp(-t * dl_ref[...])
    hf = _dot3(h, w3f_ref[...]) * window
    hb = jnp.where(rows > 0, _dot3(h, w3b_ref[...]) * window, 0.0)
    norm = jnp.sqrt(jnp.sum(hf * hf + hb * hb, axis=0, keepdims=True) + EPS)
    hf = hf / norm
    hb = hb / norm

    def table_dot(lo_row, x):
        x_hi, x_lo = _split_bf16(x)
        m_hi = fhi_s[lo_row:lo_row + L, :]
        m_lo = flo_s[lo_row:lo_row + L, :]
        return _dot(m_hi, x_hi) + (_dot(m_lo, x_hi) + _dot(m_hi, x_lo))

    g_top = table_dot(0, hf + hb)
    g_bot = table_dot(L, hf - hb)
    sign = jnp.where(rows % 2 == 0, 1.0, -1.0)
    nyquist_b = jnp.sum(sign * hb, axis=0, keepdims=True)
    g_bot = g_bot + jnp.where(rows == 0, 2.0 * nyquist_b, 0.0)
    g_ref[0:L, :] = g_top
    g_ref[L:2 * L, :] = g_bot


def _hyena_spectrum(L, fwd, jl, f_w1, f_b1, f_freq, f_w2, f_b2, f_w3):
    tn = 256
    nj = D_MODEL // tn
    feats = jnp.asarray(_filter_feats(L))
    deltas = jnp.asarray(np.abs(np.linspace(MIN_DECAY, MAX_DECAY, D_MODEL, dtype=np.float32)).reshape(1, D_MODEL))
    const = lambda j: (0, 0)
    layer = lambda j: (jl, 0, 0)
    return pl.pallas_call(
        functools.partial(_filter_kernel, L=L),
        grid=(nj,),
        in_specs=[
            pl.BlockSpec((L, FEAT_PAD), const),
            pl.BlockSpec((None, FEAT_PAD, FILTER_HIDDEN), layer),
            pl.BlockSpec((None, 1, FILTER_HIDDEN), layer),
            pl.BlockSpec((None, 2, FILTER_HIDDEN), layer),
            pl.BlockSpec((None, FILTER_HIDDEN, FILTER_HIDDEN), layer),
            pl.BlockSpec((None, 1, FILTER_HIDDEN), layer),
            pl.BlockSpec((None, FILTER_HIDDEN, tn), lambda j: (jl, 0, j)),
            pl.BlockSpec((None, FILTER_HIDDEN, tn), lambda j: (jl, 0, nj + j)),
            pl.BlockSpec((1, tn), lambda j: (0, j)),
            pl.BlockSpec((2 * L, L), const),
        ],
        out_specs=pl.BlockSpec((2 * L, tn), lambda j: (0, j)),
        out_shape=jax.ShapeDtypeStruct((2 * L, D_MODEL), F32),
        scratch_shapes=[pltpu.VMEM((2 * L, L), BF16), pltpu.VMEM((2 * L, L), BF16)],
        compiler_params=_params(1),
        name="hyena_spectrum_%d" % L,
    )(feats, f_w1, f_b1, f_freq, f_w2, f_b2, f_w3, f_w3, deltas, fwd)


def _hyena_conv_kernel(u_ref, x0_ref, g_ref, bias_ref, fwd_ref, inv_ref, o_ref, *, L, col_chains):
    n_seq = o_ref.shape[0] // L
    tn = o_ref.shape[1] // col_chains
    first = lax.broadcasted_iota(jnp.int32, (L, tn), 0) == 0
    for s in range(n_seq):
        for cc in range(col_chains):
            rows, cols = slice(s * L, (s + 1) * L), slice(cc * tn, (cc + 1) * tn)
            u = u_ref[rows, cols]
            spec = _dot(fwd_ref[...], u)
            u_top, u_bot = spec[0:L, :], spec[L:2 * L, :]
            g_top, g_bot = g_ref[0:L, cols], g_ref[L:2 * L, cols]
            y_top = u_top * g_top - jnp.where(first, 0.0, u_bot * g_bot)
            y_bot = jnp.where(first, u_bot * g_bot, u_top * g_bot + u_bot * g_top)
            y_spec = jnp.concatenate([y_top, y_bot], axis=0).astype(BF16)
            y = _dot(inv_ref[...], y_spec)
            gated = x0_ref[rows, cols].astype(F32) * (y + u.astype(F32) * bias_ref[:, cols])
            o_ref[rows, cols] = gated.astype(BF16)


def _hyena_conv(u, x0, spectrum, bias, jl, fwd, inv, L, row_block0, n_steps, seq_per_step, tn, col_chains):
    nj = D_MODEL // tn
    rows = seq_per_step * L
    act = pl.BlockSpec((rows, tn), lambda j, b: (row_block0 + b, j))
    return pl.pallas_call(
        functools.partial(_hyena_conv_kernel, L=L, col_chains=col_chains),
        grid=(nj, n_steps),
        in_specs=[
            act, act,
            pl.BlockSpec((2 * L, tn), lambda j, b: (0, j)),
            pl.BlockSpec((None, 1, tn), lambda j, b: (jl, 0, j)),
            pl.BlockSpec((2 * L, L), lambda j, b: (0, 0)),
            pl.BlockSpec((L, 2 * L), lambda j, b: (0, 0)),
        ],
        out_specs=pl.BlockSpec((rows, tn), lambda j, b: (b, j)),
        out_shape=jax.ShapeDtypeStruct((n_steps * rows, D_MODEL), BF16),
        compiler_params=_params(2),
        name="hyena_conv_%d" % L,
    )(u, x0, spectrum, bias, fwd, inv)


def _rope_tables():
    rows = DEC_SEQ // GRID_W
    row = np.repeat(np.arange(rows), GRID_W).astype(np.float32)
    col = np.tile(np.arange(GRID_W), rows).astype(np.float32)
    half = HEAD_DIM // 2
    freqs = (np.float32(ROPE_THETA) ** (-np.arange(0, half, 2, dtype=np.float32) / np.float32(half))).astype(np.float32)
    ang = np.concatenate([row[:, None] * freqs[None, :], col[:, None] * freqs[None, :]], axis=-1)
    cos = np.repeat(np.cos(ang), 2, axis=-1)
    sin = np.repeat(np.sin(ang), 2, axis=-1)
    sign = np.where(np.arange(HEAD_DIM) % 2 == 0, -1.0, 1.0)[None, :]
    return cos.astype(np.float32), (sin * sign).astype(np.float32)


def _qkv_kernel(x_ref, g_ref, sh_ref, sc_ref, w_ref, qg_ref, kg_ref, cos_ref, sin_ref,
                q_ref, k_ref, v_ref, kf_ref, vf_ref, w_s, *, n_prompt_tiles):
    i = pl.program_id(0)
    tm = q_ref.shape[0]

    @pl.when(i == 0)
    def _():
        w_s[...] = w_ref[...].astype(BF16)

    h = _modulate(x_ref[...], g_ref[...], sh_ref[0], sc_ref[0]).astype(BF16)
    qkv = _dot(h, w_s[...])
    k_col = N_HEADS * HEAD_DIM
    v_col = k_col + KV_DIM
    v = qkv[:, v_col:]
    v_ref[...] = v.astype(BF16)

    def head_norm(col, gain_ref):
        return _rmsnorm(qkv[:, col:col + HEAD_DIM], gain_ref[...])

    lane = lax.broadcasted_iota(jnp.int32, (tm, HEAD_DIM), 1)
    even = lane % 2 == 0
    q_scale = HEAD_DIM ** -0.5

    def rotate(x):
        partner = jnp.where(even, pltpu.roll(x, HEAD_DIM - 1, 1), pltpu.roll(x, 1, 1))
        return x * cos_ref[...] + partner * sin_ref[...]

    @pl.when(i < n_prompt_tiles)
    def _():
        vf_ref[...] = v
        for hd in range(N_HEADS):
            q_ref[:, hd * HEAD_DIM:(hd + 1) * HEAD_DIM] = (head_norm(hd * HEAD_DIM, qg_ref) * q_scale).astype(BF16)
        for kv in range(N_KV_HEADS):
            k = head_norm(k_col + kv * HEAD_DIM, kg_ref)
            kf_ref[:, kv * HEAD_DIM:(kv + 1) * HEAD_DIM] = k
            k_ref[:, kv * HEAD_DIM:(kv + 1) * HEAD_DIM] = k.astype(BF16)

    @pl.when(i >= n_prompt_tiles)
    def _():
        for hd in range(N_HEADS):
            q = rotate(head_norm(hd * HEAD_DIM, qg_ref))
            q_ref[:, hd * HEAD_DIM:(hd + 1) * HEAD_DIM] = (q * q_scale).astype(BF16)
        for kv in range(N_KV_HEADS):
            k = rotate(head_norm(k_col + kv * HEAD_DIM, kg_ref))
            k_ref[:, kv * HEAD_DIM:(kv + 1) * HEAD_DIM] = k.astype(BF16)


def _qkv(y, norm_mix, mods, layer, w_qkv, q_gain, k_gain, jl):
    tm = 512
    npt = N_PROMPT // tm
    tiles_per_req = DEC_SEQ // tm
    cos, sin = _rope_tables()
    rope_spec = pl.BlockSpec((tm, HEAD_DIM), lambda i: (jnp.maximum(i - npt, 0) % tiles_per_req, 0))
    row = lambda n: pl.BlockSpec((tm, n), lambda i: (i, 0))
    prompt_row = pl.BlockSpec((tm, KV_DIM), lambda i: (jnp.minimum(i, npt - 1), 0))
    gain = pl.BlockSpec((None, 1, HEAD_DIM), lambda i: (jl, 0, 0))
    return pl.pallas_call(
        functools.partial(_qkv_kernel, n_prompt_tiles=npt),
        grid=(N_TOK // tm,),
        in_specs=[
            row(D_MODEL),
            pl.BlockSpec((None, 1, D_MODEL), lambda i: (layer, 0, 0)),
            _mod_spec(layer, 0, tm),
            _mod_spec(layer, 1, tm),
            pl.BlockSpec((None, D_MODEL, QKV_DIM), lambda i: (jl, 0, 0)),
            gain, gain,
            rope_spec, rope_spec,
        ],
        out_specs=[row(D_MODEL), row(KV_DIM), row(KV_DIM), prompt_row, prompt_row],
        out_shape=[
            jax.ShapeDtypeStruct((N_TOK, D_MODEL), BF16),
            jax.ShapeDtypeStruct((N_TOK, KV_DIM), BF16),
            jax.ShapeDtypeStruct((N_TOK, KV_DIM), BF16),
            jax.ShapeDtypeStruct((N_PROMPT, KV_DIM), F32),
            jax.ShapeDtypeStruct((N_PROMPT, KV_DIM), F32),
        ],
        scratch_shapes=[pltpu.VMEM((D_MODEL, QKV_DIM), BF16)],
        compiler_params=_params(1),
        name="qkv_proj",
    )(y, norm_mix, mods, mods, w_qkv, q_gain, k_gain, jnp.asarray(cos), jnp.asarray(sin))


def _attention_kernel(*refs, with_cache):
    if with_cache:
        q_ref, k_ref, v_ref, ck_ref, cv_ref, o_ref = refs
    else:
        q_ref, k_ref, v_ref, o_ref = refs
    for kv in range(k_ref.shape[1] // HEAD_DIM):
        kv_cols = slice(kv * HEAD_DIM, (kv + 1) * HEAD_DIM)
        k = k_ref[:, kv_cols]
        v = v_ref[:, kv_cols]
        if with_cache:
            k = jnp.concatenate([k, ck_ref[0, 0][:, kv_cols].astype(BF16)], axis=0)
            v = jnp.concatenate([v, cv_ref[0, 0][:, kv_cols].astype(BF16)], axis=0)
        for g in range(GROUP):
            cols = slice((kv * GROUP + g) * HEAD_DIM, (kv * GROUP + g + 1) * HEAD_DIM)
            s = lax.dot_general(q_ref[:, cols], k, (((1,), (1,)), ((), ())), preferred_element_type=F32)
            p = jnp.exp(s - jnp.max(s, axis=-1, keepdims=True))
            denom = jnp.sum(p, axis=-1, keepdims=True)
            o_ref[:, cols] = (_dot(p.astype(BF16), v) / denom).astype(BF16)


def _attention(q, k, v, row0, n_req, seq, tq, kv_per_step, cache_k=None, cache_v=None, cache_layer=0):
    qb0, kb0 = row0 // tq, row0 // seq
    nq = seq // tq
    qw, kw = kv_per_step * GROUP * HEAD_DIM, kv_per_step * HEAD_DIM
    in_specs = [
        pl.BlockSpec((tq, qw), lambda b, h, t: (qb0 + b * nq + t, h)),
        pl.BlockSpec((seq, kw), lambda b, h, t: (kb0 + b, h)),
        pl.BlockSpec((seq, kw), lambda b, h, t: (kb0 + b, h)),
    ]
    args = [q, k, v]
    with_cache = cache_k is not None
    if with_cache:
        cspec = pl.BlockSpec((1, 1, PAST_LEN, kw), lambda b, h, t: (b, cache_layer, 0, h))
        in_specs += [cspec, cspec]
        args += [cache_k, cache_v]
    return pl.pallas_call(
        functools.partial(_attention_kernel, with_cache=with_cache),
        grid=(n_req, N_KV_HEADS // kv_per_step, nq),
        in_specs=in_specs,
        out_specs=pl.BlockSpec((tq, qw), lambda b, h, t: (b * nq + t, h)),
        out_shape=jax.ShapeDtypeStruct((n_req * seq, D_MODEL), BF16),
        compiler_params=_params(3),
        name="attention_%d" % seq,
    )(*args)


def _ffn_kernel(*refs, n_in, n_out, n_prompt_tiles, final):
    y_refs, refs = refs[:n_in], refs[n_in:]
    (ap_ref, as_ref, wo_ref, gm_ref, g_ref, sh_ref, sc_ref, gate_ref, wg_ref, wu_ref, wd_ref, fin_ref) = refs[:12]
    o_refs, refs = refs[12:12 + n_out], refs[12 + n_out:]
    wo_s, wg_s, wu_s, wd_s, ymid_s, f_s, acc_s = refs
    s = pl.program_id(0)
    nf = N_FF_CHUNKS
    tile = jnp.maximum(s - (nf - 1), 0)
    is_prompt = tile < n_prompt_tiles

    def prologue():
        a = jnp.where(is_prompt, ap_ref[...], as_ref[...])
        y = _read_stream(y_refs, is_prompt)
        y_mid = y + gm_ref[0] * _dot(a, wo_s[...])
        ymid_s[...] = y_mid
        f_s[...] = _modulate(y_mid, g_ref[...], sh_ref[0], sc_ref[0]).astype(BF16)

    def chunk(c):
        f = f_s[...]
        hidden = _silu(_dot(f, wg_s[c])) * _dot(f, wu_s[c])
        return _dot(hidden.astype(BF16), wd_s[c])

    def epilogue(acc):
        out = ymid_s[...] + gate_ref[0] * acc
        if final:
            out = _rmsnorm(out, fin_ref[...])
        if n_out == 1:
            o_refs[0][...] = out
        else:
            @pl.when(is_prompt)
            def _():
                o_refs[0][...] = out

            @pl.when(jnp.logical_not(is_prompt))
            def _():
                o_refs[1][...] = out

    @pl.when(s == 0)
    def _():
        wo_s[...] = wo_ref[...].astype(BF16)
        prologue()

    @pl.when(s < nf)
    def _():
        wg_s[s] = wg_ref[...].astype(BF16)
        wu_s[s] = wu_ref[...].astype(BF16)
        wd_s[s] = wd_ref[...].astype(BF16)
        part = chunk(s)

        @pl.when(s == 0)
        def _():
            acc_s[...] = part

        @pl.when(s > 0)
        def _():
            acc_s[...] += part

        @pl.when(s == nf - 1)
        def _():
            epilogue(acc_s[...])

    @pl.when(s >= nf)
    def _():
        prologue()
        acc = chunk(0)
        for c in range(1, nf):
            acc = acc + chunk(c)
        epilogue(acc)


def _ffn(stream, ap, as_, w_out, jl, norm_ffn, mods, layer, wg, wu, wd, final_norm, final):
    tm = 512
    nf = N_FF_CHUNKS
    npt = N_PROMPT // tm
    n_steps = nf + N_TOK // tm - 1
    n_out = 2 if final else 1
    tile_of = lambda s: jnp.maximum(s - (nf - 1), 0)
    chunk_of = lambda s: jnp.minimum(s, nf - 1)
    out_shape = _pair_shapes(D_MODEL, F32) if final else [jax.ShapeDtypeStruct((N_TOK, D_MODEL), F32)]
    return pl.pallas_call(
        functools.partial(_ffn_kernel, n_in=len(stream), n_out=n_out, n_prompt_tiles=npt, final=final),
        grid=(n_steps,),
        in_specs=_stream_specs(len(stream), tm, tile_of) + _pair_specs(tm, D_MODEL, tile_of) + [
            pl.BlockSpec((None, D_MODEL, D_MODEL), lambda s: (jl, 0, 0), pipeline_mode=pl.Buffered(1)),
            _mod_spec(layer, 2, tm, tile_of),
            pl.BlockSpec((None, 1, D_MODEL), lambda s: (layer, 0, 0)),
            _mod_spec(layer, 3, tm, tile_of),
            _mod_spec(layer, 4, tm, tile_of),
            _mod_spec(layer, 5, tm, tile_of),
            pl.BlockSpec((None, D_MODEL, FF_CHUNK), lambda s: (layer, 0, chunk_of(s))),
            pl.BlockSpec((None, D_MODEL, FF_CHUNK), lambda s: (layer, 0, chunk_of(s))),
            pl.BlockSpec((None, FF_CHUNK, D_MODEL), lambda s: (layer, chunk_of(s), 0)),
            pl.BlockSpec((1, D_MODEL), lambda s: (0, 0)),
        ],
        out_specs=_stream_specs(n_out, tm, tile_of),
        out_shape=out_shape,
        scratch_shapes=[
            pltpu.VMEM((D_MODEL, D_MODEL), BF16),
            pltpu.VMEM((nf, D_MODEL, FF_CHUNK), BF16),
            pltpu.VMEM((nf, D_MODEL, FF_CHUNK), BF16),
            pltpu.VMEM((nf, FF_CHUNK, D_MODEL), BF16),
            pltpu.VMEM((tm, D_MODEL), F32),
            pltpu.VMEM((tm, D_MODEL), BF16),
            pltpu.VMEM((tm, D_MODEL), F32),
        ],
        compiler_params=_params(1),
        name="ffn",
    )(*stream, ap, as_, w_out, mods, norm_ffn, mods, mods, mods, wg, wu, wd, final_norm.reshape(1, D_MODEL))


def kernel(x_prompt, x_sample, cache_k, cache_v, c, c_ctx, mod_w, mod_b, norm_mix, norm_ffn, hy_w_in, hy_conv_w, hy_conv_b, hy_f_w1, hy_f_b1, hy_f_freq, hy_f_w2, hy_f_b2, hy_f_w3, hy_bias, hy_w_out, at_w_qkv, at_q_norm, at_k_norm, at_w_out, ffn_w_gate, ffn_w_up, ffn_w_down, final_norm):
    stream = (x_prompt.reshape(N_PROMPT, D_MODEL), x_sample.reshape(N_SAMPLE, D_MODEL))
    cond =jnp.concatenate([c_ctx[None, :], c, jnp.zeros((MOD_ROWS - 1 - DEC_BATCH, D_MODEL), F32)], axis=0)
    mods = _adaln(cond, mod_w, mod_b).reshape(DEPTH * MOD_ROWS * N_MOD, 1, D_MODEL)

    tables = {}
    for L in (SEQ, DEC_SEQ):
        fwd, inv = _dft_tables(L)
        fwd = jnp.asarray(fwd)
        tables[L] = (fwd, _cast_bf16(fwd, 2 * L // 4), _cast_bf16(jnp.asarray(inv), L // 4))

    norm_mix = norm_mix.reshape(DEPTH, 1, D_MODEL)
    norm_ffn = norm_ffn.reshape(DEPTH, 1, D_MODEL)
    f_w1 = jnp.pad(hy_f_w1, ((0, 0), (0, FEAT_PAD - FILTER_EMB), (0, 0)))
    f_b1 = hy_f_b1.reshape(-1, 1, FILTER_HIDDEN)
    f_b2 = hy_f_b2.reshape(-1, 1, FILTER_HIDDEN)
    bias = hy_bias.reshape(-1, 1, D_MODEL)
    q_gain = at_q_norm.reshape(-1, 1, HEAD_DIM)
    k_gain = at_k_norm.reshape(-1, 1, HEAD_DIM)
    cache_k = cache_k.reshape(DEC_BATCH, -1, PAST_LEN, KV_DIM)
    cache_v = cache_v.reshape(DEC_BATCH, -1, PAST_LEN, KV_DIM)

    new_k, new_v = [], []
    for layer in range(DEPTH):
        jl = layer // N_MIXERS
        if layer % N_MIXERS == 0:
            x0, u = _in_proj(stream, norm_mix, mods, layer, hy_w_in, hy_conv_w, hy_conv_b, jl)
            mixed = []
            for L, row_block0, n_steps, seq_per_step, tn, chains in (
                    (SEQ, 0, BATCH // 2, 2, D_MODEL, 1),
                    (DEC_SEQ, N_PROMPT // DEC_SEQ, DEC_BATCH, 1, 512, 2)):
                fwd_f32, fwd_bf16, inv_bf16 = tables[L]
                spectrum = _hyena_spectrum(L, fwd_f32, jl, f_w1, f_b1, hy_f_freq, hy_f_w2, f_b2, hy_f_w3)
                mixed.append(_hyena_conv(u, x0, spectrum, bias, jl, fwd_bf16, inv_bf16, L, row_block0, n_steps,
                                         seq_per_step, tn, chains))
            w_out = hy_w_out
        else:
            q, k, v, k_f32, v_f32 = _qkv(stream[0], norm_mix, mods, layer, at_w_qkv, q_gain, k_gain, jl)
            new_k.append(k_f32.reshape(BATCH, SEQ, N_KV_HEADS, HEAD_DIM))
            new_v.append(v_f32.reshape(BATCH, SEQ, N_KV_HEADS, HEAD_DIM))
            mixed = [_attention(q, k, v, 0, BATCH, SEQ, SEQ, N_KV_HEADS),
                     _attention(q, k, v, N_PROMPT, DEC_BATCH, DEC_SEQ, 512, 1, cache_k, cache_v, jl)]
            w_out = at_w_out
        stream = tuple(_ffn(stream, mixed[0], mixed[1], w_out, jl, norm_ffn, mods, layer,
                            ffn_w_gate, ffn_w_up, ffn_w_down, final_norm, final=(layer == DEPTH - 1)))

    y_prompt, y_sample = stream
    return (y_prompt.reshape(BATCH, SEQ, D_MODEL), y_sample.reshape(DEC_BATCH, DEC_SEQ, D_MODEL),
            jnp.stack(new_k, axis=1), jnp.stack(new_v, axis=1))
```

```python
import functools
import math

import numpy as np
import jax
import jax.numpy as jnp
from jax import lax
from jax.experimental import pallas as pl
from jax.experimental.pallas import tpu as pltpu

D_MODEL = 1024
BATCH = 16
SEQ = 256
DEPTH = 4
DEC_BATCH = 4
DEC_SEQ = 1024
PAST_LEN = 512
GRID_W = 64
N_MIXERS = 2
HEAD_DIM = 128
N_HEADS = D_MODEL // HEAD_DIM
N_KV_HEADS = 2
GROUP = N_HEADS // N_KV_HEADS
KV_DIM = N_KV_HEADS * HEAD_DIM
HEAD_PAIR = 2 * HEAD_DIM
QKV_DIM = (N_HEADS + 2 * N_KV_HEADS) * HEAD_DIM
ROPE_THETA = 10000.0
D_FF = ((8 * D_MODEL + 3 * 256 - 1) // (3 * 256)) * 256
N_BANDS = 16
FILTER_EMB = 1 + 2 * N_BANDS
FILTER_HIDDEN = 64
MIN_DECAY = math.log(1e-2) / 1.5
MAX_DECAY = math.log(1e-2) / 0.3
EPS = 1e-6

N_PROMPT = BATCH * SEQ
N_SAMPLE = DEC_BATCH * DEC_SEQ
N_TOK = N_PROMPT + N_SAMPLE
N_MOD = 6
MOD_ROWS = 8
FEAT_PAD = 128
CONV_PAD = 8
FF_CHUNK = 256
N_FF_CHUNKS = D_FF // FF_CHUNK
V7X_VMEM_LIMIT = 56 * 1024 * 1024

F32 = jnp.float32
BF16 = jnp.bfloat16


def _params(n_axes, vmem=V7X_VMEM_LIMIT):
    return pltpu.CompilerParams(dimension_semantics=("arbitrary",) * n_axes, vmem_limit_bytes=vmem)


def _mod_row(tile, tm):
    n_prompt_tiles = N_PROMPT // tm
    tiles_per_req = DEC_SEQ // tm
    return jnp.where(tile < n_prompt_tiles, 0, 1 + jnp.maximum(tile - n_prompt_tiles, 0) // tiles_per_req)


def _mod_spec(layer, which, tm, tile_of=lambda i, *_: i):
    def index(*ids):
        return (layer * MOD_ROWS * N_MOD + _mod_row(tile_of(*ids), tm) * N_MOD + which, 0, 0)
    return pl.BlockSpec((1, 1, D_MODEL), index)


def _pair_specs(tm, width, tile_of=lambda i, *_: i):
    npt = N_PROMPT // tm
    return [pl.BlockSpec((tm, width), lambda *ids: (jnp.minimum(tile_of(*ids), npt - 1), 0)),
            pl.BlockSpec((tm, width), lambda *ids: (jnp.maximum(tile_of(*ids) - npt, 0), 0))]


def _pair_shapes(width, dtype):
    return [jax.ShapeDtypeStruct((N_PROMPT, width), dtype), jax.ShapeDtypeStruct((N_SAMPLE, width), dtype)]


def _stream_specs(n_arrays, tm, tile_of=lambda i, *_: i):
    if n_arrays == 2:
        return _pair_specs(tm, D_MODEL, tile_of)
    return [pl.BlockSpec((tm, D_MODEL), lambda *ids: (tile_of(*ids), 0))]


def _read_stream(refs, is_prompt):
    if len(refs) == 2:
        return jnp.where(is_prompt, refs[0][...], refs[1][...])
    return refs[0][...]


def _split_bf16(a):
    hi = a.astype(BF16)
    lo = (a - hi.astype(F32)).astype(BF16)
    return hi, lo


def _dot(a, b):
    return jnp.dot(a, b, preferred_element_type=F32)


def _dot3(a, b):
    a_hi, a_lo = _split_bf16(a)
    b_hi, b_lo = _split_bf16(b)
    return _dot(a_hi, b_hi) + (_dot(a_lo, b_hi) + _dot(a_hi, b_lo))


def _silu(a):
    return a / (1.0 + jnp.exp(-a))


def _rmsnorm(x, g):
    return x * lax.rsqrt(jnp.mean(x * x, axis=-1, keepdims=True) + EPS) * g


def _modulate(x, g, shift, scale):
    return _rmsnorm(x, g) * (1.0 + scale) + shift


def _adaln_kernel(c_ref, w_ref, b_ref, o_ref):
    s = _silu(c_ref[...])
    o_ref[0] = _dot3(s, w_ref[0]) + b_ref[0]


def _adaln(cond, mod_w, mod_b):
    tn = 1536
    n_out = N_MOD * D_MODEL
    return pl.pallas_call(
        _adaln_kernel,
        grid=(DEPTH, n_out // tn),
        in_specs=[
            pl.BlockSpec((MOD_ROWS, D_MODEL), lambda l, j: (0, 0)),
            pl.BlockSpec((1, D_MODEL, tn), lambda l, j: (l, 0, j)),
            pl.BlockSpec((1, 1, tn), lambda l, j: (l, 0, j)),
        ],
        out_specs=pl.BlockSpec((1, MOD_ROWS, tn), lambda l, j: (l, 0, j)),
        out_shape=jax.ShapeDtypeStruct((DEPTH, MOD_ROWS, n_out), F32),
        compiler_params=_params(2),
        name="adaln",
    )(cond, mod_w, mod_b.reshape(DEPTH, 1, n_out))


def _cast_kernel(x_ref, o_ref):
    o_ref[...] = x_ref[...].astype(BF16)


def _cast_bf16(x, rows):
    m, n = x.shape
    return pl.pallas_call(
        _cast_kernel,
        grid=(m // rows,),
        in_specs=[pl.BlockSpec((rows, n), lambda i: (i, 0))],
        out_specs=pl.BlockSpec((rows, n), lambda i: (i, 0)),
        out_shape=jax.ShapeDtypeStruct((m, n), BF16),
        compiler_params=_params(1),
        name="cast_table",
    )(x)


def _in_proj_kernel(*refs, n_stream, n_prompt_tiles):
    x_refs, refs = refs[:n_stream], refs[n_stream:]
    (g_ref, sh_ref, sc_ref, w0_ref, w1_ref, w2_ref, cw0_ref, cw1_ref, cw2_ref, cb0_ref, cb1_ref, cb2_ref,
     x0_ref, u_ref, h_s, w_s, za_s, zb_s) = refs
    i, c = pl.program_id(0), pl.program_id(1)
    tm, tc = x0_ref.shape

    @pl.when(i == 0)
    def _():
        w_s[c, 0] = w0_ref[...].astype(BF16)
        w_s[c, 1] = w1_ref[...].astype(BF16)
        w_s[c, 2] = w2_ref[...].astype(BF16)

    @pl.when(c == 0)
    def _():
        x = _read_stream(x_refs, i < n_prompt_tiles)
        h_s[...] = _modulate(x, g_ref[...], sh_ref[0], sc_ref[0]).astype(BF16)

    def tile(seq_len):
        stride = seq_len + CONV_PAD
        starts = [CONV_PAD + s * stride for s in range(tm // seq_len)]
        for z_s in (za_s, zb_s):
            for s in range(tm // seq_len + 1):
                z_s[s * stride:s * stride + CONV_PAD, :] = jnp.zeros((CONV_PAD, tc), F32)

        def project(part, z_s):
            for s, start in enumerate(starts):
                z_s[start:start + seq_len, :] = _dot(h_s[s * seq_len:(s + 1) * seq_len, :], w_s[c, part])

        def short_conv(z_s, start, cw_ref, cb_ref):
            return (z_s[start - 1:start - 1 + seq_len, :] * cw_ref[0:1, :] + z_s[start:start + seq_len, :] * cw_ref[1:2, :]
                    + z_s[start + 1:start + 1 + seq_len, :] * cw_ref[2:3, :] + cb_ref[...])

        project(0, za_s)
        for s, start in enumerate(starts):
            x0_ref[s * seq_len:(s + 1) * seq_len, :] = short_conv(za_s, start, cw0_ref, cb0_ref).astype(BF16)
        project(1, zb_s)
        project(2, za_s)
        for s, start in enumerate(starts):
            u = short_conv(zb_s, start, cw1_ref, cb1_ref) * short_conv(za_s, start, cw2_ref, cb2_ref)
            u_ref[s * seq_len:(s + 1) * seq_len, :] = u.astype(BF16)

    @pl.when(i < n_prompt_tiles)
    def _():
        tile(SEQ)

    @pl.when(i >= n_prompt_tiles)
    def _():
        tile(DEC_SEQ)


def _in_proj(stream, norm_mix, mods, layer, w_in, conv_w, conv_b, jl):
    tm, tc = 1024, 512
    nc = D_MODEL // tc
    npt = N_PROMPT // tm
    padded_rows = tm + (tm // SEQ + 1) * CONV_PAD

    def wspec(part):
        return pl.BlockSpec((None, D_MODEL, tc), lambda i, c: (jl, 0, part * nc + jnp.where(i == 0, c, nc - 1)))

    def cwspec(part):
        return pl.BlockSpec((None, 3, tc), lambda i, c: (jl, 0, part * nc + c))

    def cbspec(part):
        return pl.BlockSpec((None, 1, tc), lambda i, c: (jl, 0, part * nc + c))

    out = pl.BlockSpec((tm, tc), lambda i, c: (i, c))
    cb = conv_b.reshape(-1, 1, 3 * D_MODEL)
    return pl.pallas_call(
        functools.partial(_in_proj_kernel, n_stream=len(stream), n_prompt_tiles=npt),
        grid=(N_TOK // tm, nc),
        in_specs=_stream_specs(len(stream), tm) + [
            pl.BlockSpec((None, 1, D_MODEL), lambda i, c: (layer, 0, 0)),
            _mod_spec(layer, 0, tm),
            _mod_spec(layer, 1, tm),
            wspec(0), wspec(1), wspec(2),
            cwspec(0), cwspec(1), cwspec(2),
            cbspec(0), cbspec(1), cbspec(2),
        ],
        out_specs=[out, out],
        out_shape=[jax.ShapeDtypeStruct((N_TOK, D_MODEL), BF16)] * 2,
        scratch_shapes=[pltpu.VMEM((tm, D_MODEL), BF16), pltpu.VMEM((nc, 3, D_MODEL, tc), BF16),
                        ] + [pltpu.VMEM((padded_rows, tc), F32)] * 2,
        compiler_params=_params(2),
        name="hyena_in_proj",
    )(*stream, norm_mix, mods, mods, w_in, w_in, w_in, conv_w, conv_w, conv_w, cb, cb, cb)


def _dft_tables(L):
    n = 2 * L
    k = np.arange(L, dtype=np.float64)[:, None]
    t = np.arange(L, dtype=np.float64)[None, :]
    ang = 2.0 * np.pi * k * t / n
    top = np.cos(ang)
    bot = -np.sin(ang)
    bot[0, :] = np.where(np.arange(L) % 2 == 0, 1.0, -1.0)
    fwd = np.concatenate([top, bot], axis=0)
    wk = np.full((L,), 2.0)
    wk[0] = 1.0
    inv_top = (np.cos(ang) * wk[:, None]).T / n
    inv_bot = (-2.0 * np.sin(ang)).T / n
    inv_bot[:, 0] = np.where(np.arange(L) % 2 == 0, 1.0, -1.0) / n
    inv = np.concatenate([inv_top, inv_bot], axis=1)
    return fwd.astype(np.float32), inv.astype(np.float32)


def _filter_feats(L):
    t = np.arange(L, dtype=np.float32) / np.float32(L)
    bands = np.arange(1, N_BANDS + 1, dtype=np.float32)
    ang = (np.float32(2.0 * math.pi) * t[:, None]) * bands[None, :]
    feats = np.concatenate([t[:, None], np.cos(ang), np.sin(ang)], axis=-1).astype(np.float32)
    return np.pad(feats, ((0, 0), (0, FEAT_PAD - FILTER_EMB)))


def _filter_kernel(feats_ref, w1_ref, b1_ref, fr_ref, w2_ref, b2_ref, w3f_ref, w3b_ref, dl_ref, fwd_ref,
                   g_ref, h_s, *, L):
    @pl.when(pl.program_id(0) == 0)
    def _():
        h1 = jnp.sin(fr_ref[0:1, :] * (_dot3(feats_ref[...], w1_ref[...]) + b1_ref[...]))
        h_s[...] = jnp.sin(fr_ref[1:2, :] * (_dot3(h1, w2_ref[...]) + b2_ref[...]))

    tn = g_ref.shape[1]
    h = h_s[...]
    rows = lax.broadcasted_iota(jnp.int32, (L, tn), 0)
    t = rows.astype(F32) / L
    window = jnp.exp(-t * dl_ref[...])
    hf = _dot3(h, w3f_ref[...]) * window
    hb = jnp.where(rows > 0, _dot3(h, w3b_ref[...]) * window, 0.0)
    norm = jnp.sqrt(jnp.sum(hf * hf + hb * hb, axis=0, keepdims=True) + EPS)
    hf = hf / norm
    hb = hb / norm

    g_top = _dot(fwd_ref[0:L, :], (hf + hb).astype(BF16))
    g_bot = _dot(fwd_ref[L:2 * L, :], (hf - hb).astype(BF16))
    sign = jnp.where(rows % 2 == 0, 1.0, -1.0)
    nyquist_b = jnp.sum(sign * hb, axis=0, keepdims=True)
    g_bot = g_bot + jnp.where(rows == 0, 2.0 * nyquist_b, 0.0)
    g_ref[0:L, :] = g_top
    g_ref[L:2 * L, :] = g_bot


def _hyena_spectrum(L, fwd, jl, f_w1, f_b1, f_freq, f_w2, f_b2, f_w3):
    tn = 512
    nj = D_MODEL // tn
    feats = jnp.asarray(_filter_feats(L))
    deltas = jnp.asarray(np.abs(np.linspace(MIN_DECAY, MAX_DECAY, D_MODEL, dtype=np.float32)).reshape(1, D_MODEL))
    const = lambda j: (0, 0)
    layer = lambda j: (jl, 0, 0)
    return pl.pallas_call(
        functools.partial(_filter_kernel, L=L),
        grid=(nj,),
        in_specs=[
            pl.BlockSpec((L, FEAT_PAD), const),
            pl.BlockSpec((None, FEAT_PAD, FILTER_HIDDEN), layer),
            pl.BlockSpec((None, 1, FILTER_HIDDEN), layer),
            pl.BlockSpec((None, 2, FILTER_HIDDEN), layer),
            pl.BlockSpec((None, FILTER_HIDDEN, FILTER_HIDDEN), layer),
            pl.BlockSpec((None, 1, FILTER_HIDDEN), layer),
            pl.BlockSpec((None, FILTER_HIDDEN, tn), lambda j: (jl, 0, j)),
            pl.BlockSpec((None, FILTER_HIDDEN, tn), lambda j: (jl, 0, nj + j)),
            pl.BlockSpec((1, tn), lambda j: (0, j)),
            pl.BlockSpec((2 * L, L), const),
        ],
        out_specs=pl.BlockSpec((2 * L, tn), lambda j: (0, j)),
        out_shape=jax.ShapeDtypeStruct((2 * L, D_MODEL), F32),
        scratch_shapes=[pltpu.VMEM((L, FILTER_HIDDEN), F32)],
        compiler_params=_params(1),
        name="hyena_spectrum_%d" % L,
    )(feats, f_w1, f_b1, f_freq, f_w2, f_b2, f_w3, f_w3, deltas, fwd)


def _hyena_conv_kernel(u_ref, x0_ref, g_ref, bias_ref, fwd_ref, inv_ref, o_ref, *, L, col_chains):
    n_seq = o_ref.shape[0] // L
    tn = o_ref.shape[1] // col_chains
    first = lax.broadcasted_iota(jnp.int32, (L, tn), 0) == 0
    for s in range(n_seq):
        for cc in range(col_chains):
            rows, cols = slice(s * L, (s + 1) * L), slice(cc * tn, (cc + 1) * tn)
            u = u_ref[rows, cols]
            spec = _dot(fwd_ref[...], u)
            u_top, u_bot = spec[0:L, :], spec[L:2 * L, :]
            g_top, g_bot = g_ref[0:L, cols], g_ref[L:2 * L, cols]
            y_top = u_top * g_top - jnp.where(first, 0.0, u_bot * g_bot)
            y_bot = jnp.where(first, u_bot * g_bot, u_top * g_bot + u_bot * g_top)
            y_spec = jnp.concatenate([y_top, y_bot], axis=0).astype(BF16)
            y = _dot(inv_ref[...], y_spec)
            gated = x0_ref[rows, cols].astype(F32) * (y + u.astype(F32) * bias_ref[:, cols])
            o_ref[rows, cols] = gated.astype(BF16)


def _hyena_conv(u, x0, spectrum, bias, jl, fwd, inv, L, row_block0, n_steps, seq_per_step, tn, col_chains):
    nj = D_MODEL // tn
    rows = seq_per_step * L
    act = pl.BlockSpec((rows, tn), lambda j, b: (row_block0 + b, j))
    return pl.pallas_call(
        functools.partial(_hyena_conv_kernel, L=L, col_chains=col_chains),
        grid=(nj, n_steps),
        in_specs=[
            act, act,
            pl.BlockSpec((2 * L, tn), lambda j, b: (0, j)),
            pl.BlockSpec((None, 1, tn), lambda j, b: (jl, 0, j)),
            pl.BlockSpec((2 * L, L), lambda j, b: (0, 0)),
            pl.BlockSpec((L, 2 * L), lambda j, b: (0, 0)),
        ],
        out_specs=pl.BlockSpec((rows, tn), lambda j, b: (b, j)),
        out_shape=jax.ShapeDtypeStruct((n_steps * rows, D_MODEL), BF16),
        compiler_params=_params(2),
        name="hyena_conv_%d" % L,
    )(u, x0, spectrum, bias, fwd, inv)


def _rope_tables():
    rows = DEC_SEQ // GRID_W
    row = np.repeat(np.arange(rows), GRID_W).astype(np.float32)
    col = np.tile(np.arange(GRID_W), rows).astype(np.float32)
    half = HEAD_DIM // 2
    freqs = (np.float32(ROPE_THETA) ** (-np.arange(0, half, 2, dtype=np.float32) / np.float32(half))).astype(np.float32)
    ang = np.concatenate([row[:, None] * freqs[None, :], col[:, None] * freqs[None, :]], axis=-1)
    cos = np.repeat(np.cos(ang), 2, axis=-1)
    sin = np.repeat(np.sin(ang), 2, axis=-1)
    sign = np.where(np.arange(HEAD_DIM) % 2 == 0, -1.0, 1.0)[None, :]
    pair = lambda a: np.tile(a, (1, 2)).astype(np.float32)
    return pair(cos), pair(sin * sign)


def _qkv_kernel(*refs, n_prev, n_prompt_tiles):
    x_ref, g_ref, sh_ref, sc_ref, w_ref, qg_ref, kg_ref, cos_ref, sin_ref = refs[:9]
    prev_refs, refs = refs[9:9 + 2 * (n_prev > 0)], refs[9 + 2 * (n_prev > 0):]
    q_ref, k_ref, v_ref, kf_ref, vf_ref, w_s = refs
    i = pl.program_id(0)
    tm = q_ref.shape[0]

    @pl.when(i == 0)
    def _():
        w_s[...] = w_ref[...].astype(BF16)

    h = _modulate(x_ref[...], g_ref[...], sh_ref[0], sc_ref[0]).astype(BF16)
    qkv = _dot(h, w_s[...])
    k_col = N_HEADS * HEAD_DIM
    v_col = k_col + KV_DIM
    v = qkv[:, v_col:]
    v_ref[...] = v.astype(BF16)
    q_gain = qg_ref[...] * HEAD_DIM ** -0.5

    def pair_norm(col, gain):
        return jnp.concatenate([_rmsnorm(qkv[:, c:c + HEAD_DIM], gain) for c in (col, col + HEAD_DIM)], axis=1)

    @pl.when(i < n_prompt_tiles)
    def _():
        for col in range(0, k_col, HEAD_PAIR):
            q_ref[:, col:col + HEAD_PAIR] = pair_norm(col, q_gain).astype(BF16)
        k = pair_norm(k_col, kg_ref[...])
        k_ref[...] = k.astype(BF16)
        n_seq = tm // SEQ
        if n_prev:
            kf_ref[:, 0:n_prev] = prev_refs[0][...]
            vf_ref[:, 0:n_prev] = prev_refs[1][...]
        kf_ref[:, n_prev] = k.reshape(n_seq, SEQ, KV_DIM)
        vf_ref[:, n_prev] = v.reshape(n_seq, SEQ, KV_DIM)

    @pl.when(i >= n_prompt_tiles)
    def _():
        r = lax.broadcasted_iota(jnp.int32, (HEAD_PAIR, HEAD_PAIR), 0)
        col_id = lax.broadcasted_iota(jnp.int32, (HEAD_PAIR, HEAD_PAIR), 1)
        swap = jnp.where((r ^ 1) == col_id, 1.0, 0.0).astype(BF16)

        def rotate(x):
            return x * cos_ref[...] + _dot(x.astype(BF16), swap) * sin_ref[...]

        for col in range(0, k_col, HEAD_PAIR):
            q_ref[:, col:col + HEAD_PAIR] = rotate(pair_norm(col, q_gain)).astype(BF16)
        k_ref[...] = rotate(pair_norm(k_col, kg_ref[...])).astype(BF16)


def _qkv(y, norm_mix, mods, layer, w_qkv, q_gain, k_gain, jl, prev_kv):
    tm = 512
    npt = N_PROMPT // tm
    tiles_per_req = DEC_SEQ // tm
    cos, sin = _rope_tables()
    rope_spec = pl.BlockSpec((tm, HEAD_PAIR), lambda i: (jnp.maximum(i - npt, 0) % tiles_per_req, 0))
    row = lambda n: pl.BlockSpec((tm, n), lambda i: (i, 0))
    cache = lambda n: pl.BlockSpec((tm // SEQ, n, SEQ, KV_DIM), lambda i: (jnp.minimum(i, npt - 1), 0, 0, 0))
    gain = pl.BlockSpec((None, 1, HEAD_DIM), lambda i: (jl, 0, 0))
    return pl.pallas_call(
        functools.partial(_qkv_kernel, n_prev=jl, n_prompt_tiles=npt),
        grid=(N_TOK // tm,),
        in_specs=[
            row(D_MODEL),
            pl.BlockSpec((None, 1, D_MODEL), lambda i: (layer, 0, 0)),
            _mod_spec(layer, 0, tm),
            _mod_spec(layer, 1, tm),
            pl.BlockSpec((None, D_MODEL, QKV_DIM), lambda i: (jl, 0, 0)),
            gain, gain,
            rope_spec, rope_spec,
        ] + [cache(jl)] * len(prev_kv),
        out_specs=[row(D_MODEL), row(KV_DIM), row(KV_DIM), cache(jl + 1), cache(jl + 1)],
        out_shape=[
            jax.ShapeDtypeStruct((N_TOK, D_MODEL), BF16),
            jax.ShapeDtypeStruct((N_TOK, KV_DIM), BF16),
            jax.ShapeDtypeStruct((N_TOK, KV_DIM), BF16),
            jax.ShapeDtypeStruct((BATCH, jl + 1, SEQ, KV_DIM), F32),
            jax.ShapeDtypeStruct((BATCH, jl + 1, SEQ, KV_DIM), F32),
        ],
        scratch_shapes=[pltpu.VMEM((D_MODEL, QKV_DIM), BF16)],
        compiler_params=_params(1),
        name="qkv_proj",
    )(y, norm_mix, mods, mods, w_qkv, q_gain, k_gain, jnp.asarray(cos), jnp.asarray(sin), *prev_kv)


def _attention_kernel(*refs, with_cache):
    if with_cache:
        q_ref, k_ref, v_ref, ck_ref, cv_ref, o_ref = refs
    else:
        q_ref, k_ref, v_ref, o_ref = refs
    for kv in range(k_ref.shape[1] // HEAD_DIM):
        kv_cols = slice(kv * HEAD_DIM, (kv + 1) * HEAD_DIM)
        k = k_ref[:, kv_cols]
        v = v_ref[:, kv_cols]
        if with_cache:
            k = jnp.concatenate([k, ck_ref[0, 0][:, kv_cols].astype(BF16)], axis=0)
            v = jnp.concatenate([v, cv_ref[0, 0][:, kv_cols].astype(BF16)], axis=0)
        for g in range(GROUP):
            cols = slice((kv * GROUP + g) * HEAD_DIM, (kv * GROUP + g + 1) * HEAD_DIM)
            s = lax.dot_general(q_ref[:, cols], k, (((1,), (1,)), ((), ())), preferred_element_type=F32)
            p = jnp.exp(s - jnp.max(s, axis=-1, keepdims=True))
            denom = jnp.sum(p, axis=-1, keepdims=True)
            o_ref[:, cols] = (_dot(p.astype(BF16), v) / denom).astype(BF16)


def _attention(q, k, v, row0, n_req, seq, tq, kv_per_step, cache_k=None, cache_v=None, cache_layer=0):
    qb0, kb0 = row0 // tq, row0 // seq
    nq = seq // tq
    qw, kw = kv_per_step * GROUP * HEAD_DIM, kv_per_step * HEAD_DIM
    in_specs = [
        pl.BlockSpec((tq, qw), lambda b, h, t: (qb0 + b * nq + t, h)),
        pl.BlockSpec((seq, kw), lambda b, h, t: (kb0 + b, h)),
        pl.BlockSpec((seq, kw), lambda b, h, t: (kb0 + b, h)),
    ]
    args = [q, k, v]
    with_cache = cache_k is not None
    if with_cache:
        cspec = pl.BlockSpec((1, 1, PAST_LEN, kw), lambda b, h, t: (b, cache_layer, 0, h))
        in_specs += [cspec, cspec]
        args += [cache_k, cache_v]
    return pl.pallas_call(
        functools.partial(_attention_kernel, with_cache=with_cache),
        grid=(n_req, N_KV_HEADS // kv_per_step, nq),
        in_specs=in_specs,
        out_specs=pl.BlockSpec((tq, qw), lambda b, h, t: (b * nq + t, h)),
        out_shape=jax.ShapeDtypeStruct((n_req * seq, D_MODEL), BF16),
        compiler_params=_params(3),
        name="attention_%d" % seq,
    )(*args)


def _ffn_kernel(*refs, n_in, n_out, n_prompt_tiles, final):
    y_refs, refs = refs[:n_in], refs[n_in:]
    (ap_ref, as_ref, wo_ref, gm_ref, g_ref, sh_ref, sc_ref, gate_ref, wg_ref, wu_ref, wd_ref, fin_ref) = refs[:12]
    o_refs, refs = refs[12:12 + n_out], refs[12 + n_out:]
    wo_s, wg_s, wu_s, wd_s, ymid_s, f_s, acc_s = refs
    s = pl.program_id(0)
    nf = N_FF_CHUNKS
    tile = jnp.maximum(s - (nf - 1), 0)
    is_prompt = tile < n_prompt_tiles

    def prologue():
        a = jnp.where(is_prompt, ap_ref[...], as_ref[...])
        y = _read_stream(y_refs, is_prompt)
        y_mid = y + gm_ref[0] * _dot(a, wo_s[...])
        ymid_s[...] = y_mid
        f_s[...] = _modulate(y_mid, g_ref[...], sh_ref[0], sc_ref[0]).astype(BF16)

    def chunk(c):
        f = f_s[...]
        hidden = _silu(_dot(f, wg_s[c])) * _dot(f, wu_s[c])
        return _dot(hidden.astype(BF16), wd_s[c])

    def epilogue(acc):
        out = ymid_s[...] + gate_ref[0] * acc
        if final:
            out = _rmsnorm(out, fin_ref[...])
        if n_out == 1:
            o_refs[0][...] = out
        else:
            @pl.when(is_prompt)
            def _():
                o_refs[0][...] = out

            @pl.when(jnp.logical_not(is_prompt))
            def _():
                o_refs[1][...] = out

    @pl.when(s == 0)
    def _():
        wo_s[...] = wo_ref[...].astype(BF16)
        prologue()

    @pl.when(s < nf)
    def _():
        wg_s[s] = wg_ref[...].astype(BF16)
        wu_s[s] = wu_ref[...].astype(BF16)
        wd_s[s] = wd_ref[...].astype(BF16)
        part = chunk(s)

        @pl.when(s == 0)
        def _():
            acc_s[...] = part

        @pl.when(s > 0)
        def _():
            acc_s[...] += part

        @pl.when(s == nf - 1)
        def _():
            epilogue(acc_s[...])

    @pl.when(s >= nf)
    def _():
        prologue()
        acc = chunk(0)
        for c in range(1, nf):
            acc = acc + chunk(c)
        epilogue(acc)


def _ffn(stream, ap, as_, w_out, jl, norm_ffn, mods, layer, wg, wu, wd, final_norm, final):
    tm = 512
    nf = N_FF_CHUNKS
    npt = N_PROMPT // tm
    n_steps = nf + N_TOK // tm - 1
    n_out = 2 if final else 1
    tile_of = lambda s: jnp.maximum(s - (nf - 1), 0)
    chunk_of = lambda s: jnp.minimum(s, nf - 1)
    out_shape = _pair_shapes(D_MODEL, F32) if final else [jax.ShapeDtypeStruct((N_TOK, D_MODEL), F32)]
    return pl.pallas_call(
        functools.partial(_ffn_kernel, n_in=len(stream), n_out=n_out, n_prompt_tiles=npt, final=final),
        grid=(n_steps,),
        in_specs=_stream_specs(len(stream), tm, tile_of) + _pair_specs(tm, D_MODEL, tile_of) + [
            pl.BlockSpec((None, D_MODEL, D_MODEL), lambda s: (jl, 0, 0), pipeline_mode=pl.Buffered(1)),
            _mod_spec(layer, 2, tm, tile_of),
            pl.BlockSpec((None, 1, D_MODEL), lambda s: (layer, 0, 0)),
            _mod_spec(layer, 3, tm, tile_of),
            _mod_spec(layer, 4, tm, tile_of),
            _mod_spec(layer, 5, tm, tile_of),
            pl.BlockSpec((None, D_MODEL, FF_CHUNK), lambda s: (layer, 0, chunk_of(s))),
            pl.BlockSpec((None, D_MODEL, FF_CHUNK), lambda s: (layer, 0, chunk_of(s))),
            pl.BlockSpec((None, FF_CHUNK, D_MODEL), lambda s: (layer, chunk_of(s), 0)),
            pl.BlockSpec((1, D_MODEL), lambda s: (0, 0)),
        ],
        out_specs=_stream_specs(n_out, tm, tile_of),
        out_shape=out_shape,
        scratch_shapes=[
            pltpu.VMEM((D_MODEL, D_MODEL), BF16),
            pltpu.VMEM((nf, D_MODEL, FF_CHUNK), BF16),
            pltpu.VMEM((nf, D_MODEL, FF_CHUNK), BF16),
            pltpu.VMEM((nf, FF_CHUNK, D_MODEL), BF16),
            pltpu.VMEM((tm, D_MODEL), F32),
            pltpu.VMEM((tm, D_MODEL), BF16),
            pltpu.VMEM((tm, D_MODEL), F32),
        ],
        compiler_params=_params(1),
        name="ffn",
    )(*stream, ap, as_, w_out, mods, norm_ffn, mods, mods, mods, wg, wu, wd, final_norm.reshape(1, D_MODEL))


def kernel(x_prompt, x_sample, cache_k, cache_v, c, c_ctx, mod_w, mod_b, norm_mix, norm_ffn, hy_w_in, hy_conv_w, hy_conv_b, hy_f_w1, hy_f_b1, hy_f_freq, hy_f_w2, hy_f_b2, hy_f_w3, hy_bias, hy_w_out, at_w_qkv, at_q_norm, at_k_norm, at_w_out, ffn_w_gate, ffn_w_up, ffn_w_down, final_norm):
    stream = (x_prompt.reshape(N_PROMPT, D_MODEL), x_sample.reshape(N_SAMPLE, D_MODEL))
    cond = jnp.concatenate([c_ctx[None, :], c, jnp.zeros((MOD_ROWS - 1 - DEC_BATCH, D_MODEL), F32)], axis=0)
    mods = _adaln(cond, mod_w, mod_b).reshape(DEPTH * MOD_ROWS * N_MOD, 1, D_MODEL)

    tables = {}
    for L in (SEQ, DEC_SEQ):
        fwd, inv = _dft_tables(L)
        tables[L] = (_cast_bf16(jnp.asarray(fwd), 2 * L // 4), _cast_bf16(jnp.asarray(inv), L // 4))

    norm_mix = norm_mix.reshape(DEPTH, 1, D_MODEL)
    norm_ffn = norm_ffn.reshape(DEPTH, 1, D_MODEL)
    f_w1 = jnp.pad(hy_f_w1, ((0, 0), (0, FEAT_PAD - FILTER_EMB), (0, 0)))
    f_b1 = hy_f_b1.reshape(-1, 1, FILTER_HIDDEN)
    f_b2 = hy_f_b2.reshape(-1, 1, FILTER_HIDDEN)
    bias = hy_bias.reshape(-1, 1, D_MODEL)
    q_gain = at_q_norm.reshape(-1, 1, HEAD_DIM)
    k_gain = at_k_norm.reshape(-1, 1, HEAD_DIM)
    cache_k = cache_k.reshape(DEC_BATCH, -1, PAST_LEN, KV_DIM)
    cache_v = cache_v.reshape(DEC_BATCH, -1, PAST_LEN, KV_DIM)

    new_kv = ()
    for layer in range(DEPTH):
        jl = layer // N_MIXERS
        if layer % N_MIXERS == 0:
            x0, u = _in_proj(stream, norm_mix, mods, layer, hy_w_in, hy_conv_w, hy_conv_b, jl)
            mixed = []
            for L, row_block0, n_steps, seq_per_step, tn, chains in (
                    (SEQ, 0, BATCH // 2, 2, D_MODEL, 1),
                    (DEC_SEQ, N_PROMPT // DEC_SEQ, DEC_BATCH, 1, 512, 2)):
                fwd_bf16, inv_bf16 = tables[L]
                spectrum = _hyena_spectrum(L, fwd_bf16, jl, f_w1, f_b1, hy_f_freq, hy_f_w2, f_b2, hy_f_w3)
                mixed.append(_hyena_conv(u, x0, spectrum, bias, jl, fwd_bf16, inv_bf16, L, row_block0, n_steps,
                                         seq_per_step, tn, chains))
            w_out = hy_w_out
        else:
            q, k, v, *new_kv = _qkv(stream[0], norm_mix, mods, layer, at_w_qkv, q_gain, k_gain, jl, new_kv)
            mixed = [_attention(q, k, v, 0, BATCH, SEQ, SEQ, N_KV_HEADS),
                     _attention(q, k, v, N_PROMPT, DEC_BATCH, DEC_SEQ, 512, 1, cache_k, cache_v, jl)]
            w_out = at_w_out
        stream = tuple(_ffn(stream, mixed[0], mixed[1], w_out, jl, norm_ffn, mods, layer,
                            ffn_w_gate, ffn_w_up, ffn_w_down, final_norm, final=(layer == DEPTH - 1)))

    y_prompt, y_sample = stream
    new_k, new_v = (a.reshape(BATCH, -1, SEQ, N_KV_HEADS, HEAD_DIM) for a in new_kv)
    return (y_prompt.reshape(BATCH, SEQ, D_MODEL), y_sample.reshape(DEC_BATCH, DEC_SEQ, D_MODEL), new_k, new_v)
```

```python
import functools
import math

import numpy as np
import jax
import jax.numpy as jnp
from jax import lax
from jax.experimental import pallas as pl
from jax.experimental.pallas import tpu as pltpu

D_MODEL = 1024
BATCH = 16
SEQ = 256
DEPTH = 4
DEC_BATCH = 4
DEC_SEQ = 1024
PAST_LEN = 512
GRID_W = 64
N_MIXERS = 2
HEAD_DIM = 128
N_HEADS = D_MODEL // HEAD_DIM
N_KV_HEADS = 2
GROUP = N_HEADS // N_KV_HEADS
KV_DIM = N_KV_HEADS * HEAD_DIM
HEAD_PAIR = 2 * HEAD_DIM
QKV_DIM = (N_HEADS + 2 * N_KV_HEADS) * HEAD_DIM
ROPE_THETA = 10000.0
D_FF = ((8 * D_MODEL + 3 * 256 - 1) // (3 * 256)) * 256
N_BANDS = 16
FILTER_EMB = 1 + 2 * N_BANDS
FILTER_HIDDEN = 64
MIN_DECAY = math.log(1e-2) / 1.5
MAX_DECAY = math.log(1e-2) / 0.3
EPS = 1e-6

N_PROMPT = BATCH * SEQ
N_SAMPLE = DEC_BATCH * DEC_SEQ
N_TOK = N_PROMPT + N_SAMPLE
N_MOD = 6
MOD_ROWS = 8
FEAT_PAD = 128
CONV_PAD = 8
FF_CHUNK = 256
N_FF_CHUNKS = D_FF // FF_CHUNK
V7X_VMEM_LIMIT = 56 * 1024 * 1024

F32 = jnp.float32
BF16 = jnp.bfloat16


def _params(n_axes, vmem=V7X_VMEM_LIMIT):
    return pltpu.CompilerParams(dimension_semantics=("arbitrary",) * n_axes, vmem_limit_bytes=vmem)


def _mod_row(tile, tm):
    n_prompt_tiles = N_PROMPT // tm
    tiles_per_req = DEC_SEQ // tm
    return jnp.where(tile < n_prompt_tiles, 0, 1 + jnp.maximum(tile - n_prompt_tiles, 0) // tiles_per_req)


def _mod_spec(layer, which, tm, tile_of=lambda i, *_: i):
    def index(*ids):
        return (layer * MOD_ROWS * N_MOD + _mod_row(tile_of(*ids), tm) * N_MOD + which, 0, 0)
    return pl.BlockSpec((1, 1, D_MODEL), index)


def _pair_specs(tm, width, tile_of=lambda i, *_: i):
    npt = N_PROMPT // tm
    return [pl.BlockSpec((tm, width), lambda *ids: (jnp.minimum(tile_of(*ids), npt - 1), 0)),
            pl.BlockSpec((tm, width), lambda *ids: (jnp.maximum(tile_of(*ids) - npt, 0), 0))]


def _pair_shapes(width, dtype):
    return [jax.ShapeDtypeStruct((N_PROMPT, width), dtype), jax.ShapeDtypeStruct((N_SAMPLE, width), dtype)]


def _stream_specs(n_arrays, tm, tile_of=lambda i, *_: i):
    if n_arrays == 2:
        return _pair_specs(tm, D_MODEL, tile_of)
    return [pl.BlockSpec((tm, D_MODEL), lambda *ids: (tile_of(*ids), 0))]


def _read_stream(refs, is_prompt):
    if len(refs) == 2:
        return jnp.where(is_prompt, refs[0][...], refs[1][...])
    return refs[0][...]


def _split_bf16(a):
    hi = a.astype(BF16)
    lo = (a - hi.astype(F32)).astype(BF16)
    return hi, lo


def _dot(a, b):
    return jnp.dot(a, b, preferred_element_type=F32)


def _dot3(a, b):
    a_hi, a_lo = _split_bf16(a)
    b_hi, b_lo = _split_bf16(b)
    return _dot(a_hi, b_hi) + (_dot(a_lo, b_hi) + _dot(a_hi, b_lo))


def _silu(a):
    return a / (1.0 + jnp.exp(-a))


def _rmsnorm(x, g):
    return x * lax.rsqrt(jnp.mean(x * x, axis=-1, keepdims=True) + EPS) * g


def _modulate(x, g, shift, scale):
    return _rmsnorm(x, g) * (1.0 + scale) + shift


def _adaln_kernel(c_ref, w_ref, b_ref, o_ref):
    s = _silu(c_ref[...])
    o_ref[0] = _dot3(s, w_ref[0]) + b_ref[0]


def _adaln(cond, mod_w, mod_b):
    tn = 1536
    n_out = N_MOD * D_MODEL
    return pl.pallas_call(
        _adaln_kernel,
        grid=(DEPTH, n_out // tn),
        in_specs=[
            pl.BlockSpec((MOD_ROWS, D_MODEL), lambda l, j: (0, 0)),
            pl.BlockSpec((1, D_MODEL, tn), lambda l, j: (l, 0, j)),
            pl.BlockSpec((1, 1, tn), lambda l, j: (l, 0, j)),
        ],
        out_specs=pl.BlockSpec((1, MOD_ROWS, tn), lambda l, j: (l, 0, j)),
        out_shape=jax.ShapeDtypeStruct((DEPTH, MOD_ROWS, n_out), F32),
        compiler_params=_params(2),
        name="adaln",
    )(cond, mod_w, mod_b.reshape(DEPTH, 1, n_out))


def _cast_kernel(x_ref, o_ref):
    o_ref[...] = x_ref[...].astype(BF16)


def _cast_bf16(x, rows):
    m, n = x.shape
    return pl.pallas_call(
        _cast_kernel,
        grid=(m // rows,),
        in_specs=[pl.BlockSpec((rows, n), lambda i: (i, 0))],
        out_specs=pl.BlockSpec((rows, n), lambda i: (i, 0)),
        out_shape=jax.ShapeDtypeStruct((m, n), BF16),
        compiler_params=_params(1),
        name="cast_table",
    )(x)


def _in_proj_kernel(*refs, n_stream, n_prompt_tiles):
    x_refs, refs = refs[:n_stream], refs[n_stream:]
    (g_ref, sh_ref, sc_ref, w0_ref, w1_ref, w2_ref, cw0_ref, cw1_ref, cw2_ref, cb0_ref, cb1_ref, cb2_ref,
     x0_ref, u_ref, h_s, w_s, za_s, zb_s) = refs
    i, c = pl.program_id(0), pl.program_id(1)
    tm, tc = x0_ref.shape

    @pl.when(i == 0)
    def _():
        w_s[c, 0] = w0_ref[...].astype(BF16)
        w_s[c, 1] = w1_ref[...].astype(BF16)
        w_s[c, 2] = w2_ref[...].astype(BF16)

    @pl.when(c == 0)
    def _():
        x = _read_stream(x_refs, i < n_prompt_tiles)
        h_s[...] = _modulate(x, g_ref[...], sh_ref[0], sc_ref[0]).astype(BF16)

    def tile(seq_len):
        stride = seq_len + CONV_PAD
        starts = [CONV_PAD + s * stride for s in range(tm // seq_len)]
        for z_s in (za_s, zb_s):
            for s in range(tm // seq_len + 1):
                z_s[s * stride:s * stride + CONV_PAD, :] = jnp.zeros((CONV_PAD, tc), F32)

        def project(part, z_s):
            for s, start in enumerate(starts):
                z_s[start:start + seq_len, :] = _dot(h_s[s * seq_len:(s + 1) * seq_len, :], w_s[c, part])

        def short_conv(z_s, start, cw_ref, cb_ref):
            return (z_s[start - 1:start - 1 + seq_len, :] * cw_ref[0:1, :] + z_s[start:start + seq_len, :] * cw_ref[1:2, :]
                    + z_s[start + 1:start + 1 + seq_len, :] * cw_ref[2:3, :] + cb_ref[...])

        project(0, za_s)
        for s, start in enumerate(starts):
            x0_ref[s * seq_len:(s + 1) * seq_len, :] = short_conv(za_s, start, cw0_ref, cb0_ref).astype(BF16)
        project(1, zb_s)
        project(2, za_s)
        for s, start in enumerate(starts):
            u = short_conv(zb_s, start, cw1_ref, cb1_ref) * short_conv(za_s, start, cw2_ref, cb2_ref)
            u_ref[s * seq_len:(s + 1) * seq_len, :] = u.astype(BF16)

    @pl.when(i < n_prompt_tiles)
    def _():
        tile(SEQ)

    @pl.when(i >= n_prompt_tiles)
    def _():
        tile(DEC_SEQ)


def _in_proj(stream, norm_mix, mods, layer, w_in, conv_w, conv_b, jl):
    tm, tc = 1024, 512
    nc = D_MODEL // tc
    npt = N_PROMPT // tm
    padded_rows = tm + (tm // SEQ + 1) * CONV_PAD

    def wspec(part):
        return pl.BlockSpec((None, D_MODEL, tc), lambda i, c: (jl, 0, part * nc + jnp.where(i == 0, c, nc - 1)))

    def cwspec(part):
        return pl.BlockSpec((None, 3, tc), lambda i, c: (jl, 0, part * nc + c))

    def cbspec(part):
        return pl.BlockSpec((None, 1, tc), lambda i, c: (jl, 0, part * nc + c))

    out = pl.BlockSpec((tm, tc), lambda i, c: (i, c))
    cb = conv_b.reshape(-1, 1, 3 * D_MODEL)
    return pl.pallas_call(
        functools.partial(_in_proj_kernel, n_stream=len(stream), n_prompt_tiles=npt),
        grid=(N_TOK // tm, nc),
        in_specs=_stream_specs(len(stream), tm) + [
            pl.BlockSpec((None, 1, D_MODEL), lambda i, c: (layer, 0, 0)),
            _mod_spec(layer, 0, tm),
            _mod_spec(layer, 1, tm),
            wspec(0), wspec(1), wspec(2),
            cwspec(0), cwspec(1), cwspec(2),
            cbspec(0), cbspec(1), cbspec(2),
        ],
        out_specs=[out, out],
        out_shape=[jax.ShapeDtypeStruct((N_TOK, D_MODEL), BF16)] * 2,
        scratch_shapes=[pltpu.VMEM((tm, D_MODEL), BF16), pltpu.VMEM((nc, 3, D_MODEL, tc), BF16),
                        ] + [pltpu.VMEM((padded_rows, tc), F32)] * 2,
        compiler_params=_params(2),
        name="hyena_in_proj",
    )(*stream, norm_mix, mods, mods, w_in, w_in, w_in, conv_w, conv_w, conv_w, cb, cb, cb)


def _dft_tables(L):
    n = 2 * L
    k = np.arange(L, dtype=np.float64)[:, None]
    t = np.arange(L, dtype=np.float64)[None, :]
    ang = 2.0 * np.pi * k * t / n
    top = np.cos(ang)
    bot = -np.sin(ang)
    bot[0, :] = np.where(np.arange(L) % 2 == 0, 1.0, -1.0)
    fwd = np.concatenate([top, bot], axis=0)
    wk = np.full((L,), 2.0)
    wk[0] = 1.0
    inv_top = (np.cos(ang) * wk[:, None]).T / n
    inv_bot = (-2.0 * np.sin(ang)).T / n
    inv_bot[:, 0] = np.where(np.arange(L) % 2 == 0, 1.0, -1.0) / n
    inv = np.concatenate([inv_top, inv_bot], axis=1)
    return fwd.astype(np.float32), inv.astype(np.float32)


def _filter_feats(L):
    t = np.arange(L, dtype=np.float32) / np.float32(L)
    bands = np.arange(1, N_BANDS + 1, dtype=np.float32)
    ang = (np.float32(2.0 * math.pi) * t[:, None]) * bands[None, :]
    feats = np.concatenate([t[:, None], np.cos(ang), np.sin(ang)], axis=-1).astype(np.float32)
    return np.pad(feats, ((0, 0), (0, FEAT_PAD - FILTER_EMB)))


def _filter_kernel(feats_ref, w1_ref, b1_ref, fr_ref, w2_ref, b2_ref, w3f_ref, w3b_ref, dl_ref, fwd_ref,
                   g_ref, h_s, *, L):
    @pl.when(pl.program_id(0) == 0)
    def _():
        h1 = jnp.sin(fr_ref[0:1, :] * (_dot3(feats_ref[...], w1_ref[...]) + b1_ref[...]))
        h_s[...] = jnp.sin(fr_ref[1:2, :] * (_dot3(h1, w2_ref[...]) + b2_ref[...]))

    tn = g_ref.shape[1]
    h = h_s[...]
    rows = lax.broadcasted_iota(jnp.int32, (L, tn), 0)
    t = rows.astype(F32) / L
    window = jnp.exp(-t * dl_ref[...])
    hf = _dot3(h, w3f_ref[...]) * window
    hb = jnp.where(rows > 0, _dot3(h, w3b_ref[...]) * window, 0.0)
    norm = jnp.sqrt(jnp.sum(hf * hf + hb * hb, axis=0, keepdims=True) + EPS)
    hf = hf / norm
    hb = hb / norm

    g_top = _dot(fwd_ref[0:L, :], (hf + hb).astype(BF16))
    g_bot = _dot(fwd_ref[L:2 * L, :], (hf - hb).astype(BF16))
    sign = jnp.where(rows % 2 == 0, 1.0, -1.0)
    nyquist_b = jnp.sum(sign * hb, axis=0, keepdims=True)
    g_bot = g_bot + jnp.where(rows == 0, 2.0 * nyquist_b, 0.0)
    g_ref[0:L, :] = g_top
    g_ref[L:2 * L, :] = g_bot


def _hyena_spectrum(L, fwd, jl, f_w1, f_b1, f_freq, f_w2, f_b2, f_w3):
    tn = 512
    nj = D_MODEL // tn
    feats = jnp.asarray(_filter_feats(L))
    deltas = jnp.asarray(np.abs(np.linspace(MIN_DECAY, MAX_DECAY, D_MODEL, dtype=np.float32)).reshape(1, D_MODEL))
    const = lambda j: (0, 0)
    layer = lambda j: (jl, 0, 0)
    return pl.pallas_call(
        functools.partial(_filter_kernel, L=L),
        grid=(nj,),
        in_specs=[
            pl.BlockSpec((L, FEAT_PAD), const),
            pl.BlockSpec((None, FEAT_PAD, FILTER_HIDDEN), layer),
            pl.BlockSpec((None, 1, FILTER_HIDDEN), layer),
            pl.BlockSpec((None, 2, FILTER_HIDDEN), layer),
            pl.BlockSpec((None, FILTER_HIDDEN, FILTER_HIDDEN), layer),
            pl.BlockSpec((None, 1, FILTER_HIDDEN), layer),
            pl.BlockSpec((None, FILTER_HIDDEN, tn), lambda j: (jl, 0, j)),
            pl.BlockSpec((None, FILTER_HIDDEN, tn), lambda j: (jl, 0, nj + j)),
            pl.BlockSpec((1, tn), lambda j: (0, j)),
            pl.BlockSpec((2 * L, L), const),
        ],
        out_specs=pl.BlockSpec((2 * L, tn), lambda j: (0, j)),
        out_shape=jax.ShapeDtypeStruct((2 * L, D_MODEL), F32),
        scratch_shapes=[pltpu.VMEM((L, FILTER_HIDDEN), F32)],
        compiler_params=_params(1),
        name="hyena_spectrum_%d" % L,
    )(feats, f_w1, f_b1, f_freq, f_w2, f_b2, f_w3, f_w3, deltas, fwd)


def _hyena_conv_kernel(u_ref, x0_ref, g_ref, bias_ref, fwd_ref, inv_ref, o_ref, *, L, col_chains):
    n_seq = o_ref.shape[0] // L
    tn = o_ref.shape[1] // col_chains
    first = lax.broadcasted_iota(jnp.int32, (L, tn), 0) == 0
    for s in range(n_seq):
        for cc in range(col_chains):
            rows, cols = slice(s * L, (s + 1) * L), slice(cc * tn, (cc + 1) * tn)
            u = u_ref[rows, cols]
            spec = _dot(fwd_ref[...], u)
            u_top, u_bot = spec[0:L, :], spec[L:2 * L, :]
            g_top, g_bot = g_ref[0:L, cols], g_ref[L:2 * L, cols]
            y_top = u_top * g_top - jnp.where(first, 0.0, u_bot * g_bot)
            y_bot = jnp.where(first, u_bot * g_bot, u_top * g_bot + u_bot * g_top)
            y_spec = jnp.concatenate([y_top, y_bot], axis=0).astype(BF16)
            y = _dot(inv_ref[...], y_spec)
            gated = x0_ref[rows, cols].astype(F32) * (y + u.astype(F32) * bias_ref[:, cols])
            o_ref[rows, cols] = gated.astype(BF16)


def _hyena_conv(u, x0, spectrum, bias, jl, fwd, inv, L, row_block0, n_steps, seq_per_step, tn, col_chains):
    nj = D_MODEL // tn
    rows = seq_per_step * L
    act = pl.BlockSpec((rows, tn), lambda j, b: (row_block0 + b, j))
    return pl.pallas_call(
        functools.partial(_hyena_conv_kernel, L=L, col_chains=col_chains),
        grid=(nj, n_steps),
        in_specs=[
            act, act,
            pl.BlockSpec((2 * L, tn), lambda j, b: (0, j)),
            pl.BlockSpec((None, 1, tn), lambda j, b: (jl, 0, j)),
            pl.BlockSpec((2 * L, L), lambda j, b: (0, 0)),
            pl.BlockSpec((L, 2 * L), lambda j, b: (0, 0)),
        ],
        out_specs=pl.BlockSpec((rows, tn), lambda j, b: (b, j)),
        out_shape=jax.ShapeDtypeStruct((n_steps * rows, D_MODEL), BF16),
        compiler_params=_params(2),
        name="hyena_conv_%d" % L,
    )(u, x0, spectrum, bias, fwd, inv)


def _rope_tables():
    rows = DEC_SEQ // GRID_W
    row = np.repeat(np.arange(rows), GRID_W).astype(np.float32)
    col = np.tile(np.arange(GRID_W), rows).astype(np.float32)
    half = HEAD_DIM // 2
    freqs = (np.float32(ROPE_THETA) ** (-np.arange(0, half, 2, dtype=np.float32) / np.float32(half))).astype(np.float32)
    ang = np.concatenate([row[:, None] * freqs[None, :], col[:, None] * freqs[None, :]], axis=-1)
    cos = np.repeat(np.cos(ang), 2, axis=-1)
    sin = np.repeat(np.sin(ang), 2, axis=-1)
    sign = np.where(np.arange(HEAD_DIM) % 2 == 0, -1.0, 1.0)[None, :]
    pair = lambda a: np.tile(a, (1, 2)).astype(np.float32)
    return pair(cos), pair(sin * sign)


def _qkv_kernel(*refs, n_prev, n_prompt_tiles):
    x_ref, g_ref, sh_ref, sc_ref, w_ref, qg_ref, kg_ref, cos_ref, sin_ref = refs[:9]
    prev_refs, refs = refs[9:9 + 2 * (n_prev > 0)], refs[9 + 2 * (n_prev > 0):]
    q_ref, k_ref, v_ref, kf_ref, vf_ref, w_s = refs
    i = pl.program_id(0)
    tm = q_ref.shape[0]

    @pl.when(i == 0)
    def _():
        w_s[...] = w_ref[...].astype(BF16)

    h = _modulate(x_ref[...], g_ref[...], sh_ref[0], sc_ref[0]).astype(BF16)
    qkv = _dot(h, w_s[...])
    k_col = N_HEADS * HEAD_DIM
    v_col = k_col + KV_DIM
    v = qkv[:, v_col:]
    v_ref[...] = v.astype(BF16)
    q_gain = qg_ref[...] * HEAD_DIM ** -0.5

    def pair_norm(col, gain):
        return jnp.concatenate([_rmsnorm(qkv[:, c:c + HEAD_DIM], gain) for c in (col, col + HEAD_DIM)], axis=1)

    @pl.when(i < n_prompt_tiles)
    def _():
        for col in range(0, k_col, HEAD_PAIR):
            q_ref[:, col:col + HEAD_PAIR] = pair_norm(col, q_gain).astype(BF16)
        k = pair_norm(k_col, kg_ref[...])
        k_ref[...] = k.astype(BF16)
        n_seq = tm // SEQ
        if n_prev:
            kf_ref[:, 0:n_prev] = prev_refs[0][...]
            vf_ref[:, 0:n_prev] = prev_refs[1][...]
        kf_ref[:, n_prev] = k.reshape(n_seq, SEQ, KV_DIM)
        vf_ref[:, n_prev] = v.reshape(n_seq, SEQ, KV_DIM)

    @pl.when(i >= n_prompt_tiles)
    def _():
        r = lax.broadcasted_iota(jnp.int32, (HEAD_PAIR, HEAD_PAIR), 0)
        col_id = lax.broadcasted_iota(jnp.int32, (HEAD_PAIR, HEAD_PAIR), 1)
        swap = jnp.where((r ^ 1) == col_id, 1.0, 0.0).astype(BF16)

        def rotate(x):
            return x * cos_ref[...] + _dot(x.astype(BF16), swap) * sin_ref[...]

        for col in range(0, k_col, HEAD_PAIR):
            q_ref[:, col:col + HEAD_PAIR] = rotate(pair_norm(col, q_gain)).astype(BF16)
        k_ref[...] = rotate(pair_norm(k_col, kg_ref[...])).astype(BF16)


def _qkv(y, norm_mix, mods, layer, w_qkv, q_gain, k_gain, jl, prev_kv):
    tm = 512
    npt = N_PROMPT // tm
    tiles_per_req = DEC_SEQ // tm
    cos, sin = _rope_tables()
    rope_spec = pl.BlockSpec((tm, HEAD_PAIR), lambda i: (jnp.maximum(i - npt, 0) % tiles_per_req, 0))
    row = lambda n: pl.BlockSpec((tm, n), lambda i: (i, 0))
    cache = lambda n: pl.BlockSpec((tm // SEQ, n, SEQ, KV_DIM), lambda i: (jnp.minimum(i, npt - 1), 0, 0, 0))
    gain = pl.BlockSpec((None, 1, HEAD_DIM), lambda i: (jl, 0, 0))
    return pl.pallas_call(
        functools.partial(_qkv_kernel, n_prev=jl, n_prompt_tiles=npt),
        grid=(N_TOK // tm,),
        in_specs=[
            row(D_MODEL),
            pl.BlockSpec((None, 1, D_MODEL), lambda i: (layer, 0, 0)),
            _mod_spec(layer, 0, tm),
            _mod_spec(layer, 1, tm),
            pl.BlockSpec((None, D_MODEL, QKV_DIM), lambda i: (jl, 0, 0)),
            gain, gain,
            rope_spec, rope_spec,
        ] + [cache(jl)] * len(prev_kv),
        out_specs=[row(D_MODEL), row(KV_DIM), row(KV_DIM), cache(jl + 1), cache(jl + 1)],
        out_shape=[
            jax.ShapeDtypeStruct((N_TOK, D_MODEL), BF16),
            jax.ShapeDtypeStruct((N_TOK, KV_DIM), BF16),
            jax.ShapeDtypeStruct((N_TOK, KV_DIM), BF16),
            jax.ShapeDtypeStruct((BATCH, jl + 1, SEQ, KV_DIM), F32),
            jax.ShapeDtypeStruct((BATCH, jl + 1, SEQ, KV_DIM), F32),
        ],
        scratch_shapes=[pltpu.VMEM((D_MODEL, QKV_DIM), BF16)],
        compiler_params=_params(1),
        name="qkv_proj",
    )(y, norm_mix, mods, mods, w_qkv, q_gain, k_gain, jnp.asarray(cos), jnp.asarray(sin), *prev_kv)


def _attention_kernel(*refs, with_cache, seq):
    if with_cache:
        q_ref, k_ref, v_ref, ck_ref, cv_ref, o_ref = refs
    else:
        q_ref, k_ref, v_ref, o_ref = refs
    n_req = k_ref.shape[0] // seq
    tq = q_ref.shape[0] // n_req
    for r in range(n_req):
        q_rows, k_rows = slice(r * tq, (r + 1) * tq), slice(r * seq, (r + 1) * seq)
        for kv in range(k_ref.shape[1] // HEAD_DIM):
            kv_cols = slice(kv * HEAD_DIM, (kv + 1) * HEAD_DIM)
            k = k_ref[k_rows, kv_cols]
            v = v_ref[k_rows, kv_cols]
            if with_cache:
                k = jnp.concatenate([k, ck_ref[0, 0][:, kv_cols].astype(BF16)], axis=0)
                v = jnp.concatenate([v, cv_ref[0, 0][:, kv_cols].astype(BF16)], axis=0)
            v_ones = jnp.concatenate([v, jnp.ones_like(v)], axis=1)
            for g in range(GROUP):
                cols = slice((kv * GROUP + g) * HEAD_DIM, (kv * GROUP + g + 1) * HEAD_DIM)
                s = lax.dot_general(q_ref[q_rows, cols], k, (((1,), (1,)), ((), ())), preferred_element_type=F32)
                p = jnp.exp(s - jnp.max(s, axis=-1, keepdims=True))
                o = _dot(p.astype(BF16), v_ones)
                o_ref[q_rows, cols] = (o[:, :HEAD_DIM] / o[:, HEAD_DIM:]).astype(BF16)


def _attention(q, k, v, row0, n_req, seq, tq, kv_per_step, req_per_step=1, cache_k=None, cache_v=None,
               cache_layer=0):
    nq = seq // tq
    n_req //= req_per_step
    tq, seq_rows = tq * req_per_step, seq * req_per_step
    qb0, kb0 = row0 // tq, row0 // seq_rows
    qw, kw = kv_per_step * GROUP * HEAD_DIM, kv_per_step * HEAD_DIM
    in_specs = [
        pl.BlockSpec((tq, qw), lambda b, h, t: (qb0 + b * nq + t, h)),
        pl.BlockSpec((seq_rows, kw), lambda b, h, t: (kb0 + b, h)),
        pl.BlockSpec((seq_rows, kw), lambda b, h, t: (kb0 + b, h)),
    ]
    args = [q, k, v]
    with_cache = cache_k is not None
    if with_cache:
        cspec = pl.BlockSpec((1, 1, PAST_LEN, kw), lambda b, h, t: (b, cache_layer, 0, h))
        in_specs += [cspec, cspec]
        args += [cache_k, cache_v]
    return pl.pallas_call(
        functools.partial(_attention_kernel, with_cache=with_cache, seq=seq),
        grid=(n_req, N_KV_HEADS // kv_per_step, nq),
        in_specs=in_specs,
        out_specs=pl.BlockSpec((tq, qw), lambda b, h, t: (b * nq + t, h)),
        out_shape=jax.ShapeDtypeStruct((n_req * seq_rows, D_MODEL), BF16),
        compiler_params=_params(3),
        name="attention_%d" % seq,
    )(*args)


def _ffn_kernel(*refs, n_in, n_out, n_prompt_tiles, final):
    y_refs, refs = refs[:n_in], refs[n_in:]
    (ap_ref, as_ref, wo_ref, gm_ref, g_ref, sh_ref, sc_ref, gate_ref, wg_ref, wu_ref, wd_ref, fin_ref) = refs[:12]
    o_refs, refs = refs[12:12 + n_out], refs[12 + n_out:]
    wo_s, wg_s, wu_s, wd_s, ymid_s, f_s, acc_s = refs
    s = pl.program_id(0)
    nf = N_FF_CHUNKS
    tile = jnp.maximum(s - (nf - 1), 0)
    is_prompt = tile < n_prompt_tiles

    def prologue():
        a = jnp.where(is_prompt, ap_ref[...], as_ref[...])
        y = _read_stream(y_refs, is_prompt)
        y_mid = y + gm_ref[0] * _dot(a, wo_s[...])
        ymid_s[...] = y_mid
        f_s[...] = _modulate(y_mid, g_ref[...], sh_ref[0], sc_ref[0]).astype(BF16)

    def chunk(c):
        f = f_s[...]
        hidden = _silu(_dot(f, wg_s[c])) * _dot(f, wu_s[c])
        return _dot(hidden.astype(BF16), wd_s[c])

    def epilogue(acc):
        out = ymid_s[...] + gate_ref[0] * acc
        if final:
            out = _rmsnorm(out, fin_ref[...])
        if n_out == 1:
            o_refs[0][...] = out
        else:
            @pl.when(is_prompt)
            def _():
                o_refs[0][...] = out

            @pl.when(jnp.logical_not(is_prompt))
            def _():
                o_refs[1][...] = out

    @pl.when(s == 0)
    def _():
        wo_s[...] = wo_ref[...].astype(BF16)
        prologue()

    @pl.when(s < nf)
    def _():
        wg_s[s] = wg_ref[...].astype(BF16)
        wu_s[s] = wu_ref[...].astype(BF16)
        wd_s[s] = wd_ref[...].astype(BF16)
        part = chunk(s)

        @pl.when(s == 0)
        def _():
            acc_s[...] = part

        @pl.when(s > 0)
        def _():
            acc_s[...] += part

        @pl.when(s == nf - 1)
        def _():
            epilogue(acc_s[...])

    @pl.when(s >= nf)
    def _():
        prologue()
        acc = chunk(0)
        for c in range(1, nf):
            acc = acc + chunk(c)
        epilogue(acc)


def _ffn(stream, ap, as_, w_out, jl, norm_ffn, mods, layer, wg, wu, wd, final_norm, final):
    tm = 512
    nf = N_FF_CHUNKS
    npt = N_PROMPT // tm
    n_steps = nf + N_TOK // tm - 1
    n_out = 2 if final else 1
    tile_of = lambda s: jnp.maximum(s - (nf - 1), 0)
    chunk_of = lambda s: jnp.minimum(s, nf - 1)
    out_shape = _pair_shapes(D_MODEL, F32) if final else [jax.ShapeDtypeStruct((N_TOK, D_MODEL), F32)]
    return pl.pallas_call(
        functools.partial(_ffn_kernel, n_in=len(stream), n_out=n_out, n_prompt_tiles=npt, final=final),
        grid=(n_steps,),
        in_specs=_stream_specs(len(stream), tm, tile_of) + _pair_specs(tm, D_MODEL, tile_of) + [
            pl.BlockSpec((None, D_MODEL, D_MODEL), lambda s: (jl, 0, 0), pipeline_mode=pl.Buffered(1)),
            _mod_spec(layer, 2, tm, tile_of),
            pl.BlockSpec((None, 1, D_MODEL), lambda s: (layer, 0, 0)),
            _mod_spec(layer, 3, tm, tile_of),
            _mod_spec(layer, 4, tm, tile_of),
            _mod_spec(layer, 5, tm, tile_of),
            pl.BlockSpec((None, D_MODEL, FF_CHUNK), lambda s: (layer, 0, chunk_of(s))),
            pl.BlockSpec((None, D_MODEL, FF_CHUNK), lambda s: (layer, 0, chunk_of(s))),
            pl.BlockSpec((None, FF_CHUNK, D_MODEL), lambda s: (layer, chunk_of(s), 0)),
            pl.BlockSpec((1, D_MODEL), lambda s: (0, 0)),
        ],
        out_specs=_stream_specs(n_out, tm, tile_of),
        out_shape=out_shape,
        scratch_shapes=[
            pltpu.VMEM((D_MODEL, D_MODEL), BF16),
            pltpu.VMEM((nf, D_MODEL, FF_CHUNK), BF16),
            pltpu.VMEM((nf, D_MODEL, FF_CHUNK), BF16),
            pltpu.VMEM((nf, FF_CHUNK, D_MODEL), BF16),
            pltpu.VMEM((tm, D_MODEL), F32),
            pltpu.VMEM((tm, D_MODEL), BF16),
            pltpu.VMEM((tm, D_MODEL), F32),
        ],
        compiler_params=_params(1),
        name="ffn",
    )(*stream, ap, as_, w_out, mods, norm_ffn, mods, mods, mods, wg, wu, wd, final_norm.reshape(1, D_MODEL))


def kernel(x_prompt, x_sample, cache_k, cache_v, c, c_ctx, mod_w, mod_b, norm_mix, norm_ffn, hy_w_in, hy_conv_w, hy_conv_b, hy_f_w1, hy_f_b1, hy_f_freq, hy_f_w2, hy_f_b2, hy_f_w3, hy_bias, hy_w_out, at_w_qkv, at_q_norm, at_k_norm, at_w_out, ffn_w_gate, ffn_w_up, ffn_w_down, final_norm):
    stream = (x_prompt.reshape(N_PROMPT, D_MODEL), x_sample.reshape(N_SAMPLE, D_MODEL))
    cond = jnp.concatenate([c_ctx[None, :], c, jnp.zeros((MOD_ROWS - 1 - DEC_BATCH, D_MODEL), F32)], axis=0)
    mods = _adaln(cond, mod_w, mod_b).reshape(DEPTH * MOD_ROWS * N_MOD, 1, D_MODEL)

    tables = {}
    for L in (SEQ, DEC_SEQ):
        fwd, inv = _dft_tables(L)
        tables[L] = (_cast_bf16(jnp.asarray(fwd), 2 * L // 4), _cast_bf16(jnp.asarray(inv), L // 4))

    norm_mix = norm_mix.reshape(DEPTH, 1, D_MODEL)
    norm_ffn = norm_ffn.reshape(DEPTH, 1, D_MODEL)
    f_w1 = jnp.pad(hy_f_w1, ((0, 0), (0, FEAT_PAD - FILTER_EMB), (0, 0)))
    f_b1 = hy_f_b1.reshape(-1, 1, FILTER_HIDDEN)
    f_b2 = hy_f_b2.reshape(-1, 1, FILTER_HIDDEN)
    bias = hy_bias.reshape(-1, 1, D_MODEL)
    q_gain = at_q_norm.reshape(-1, 1, HEAD_DIM)
    k_gain = at_k_norm.reshape(-1, 1, HEAD_DIM)
    cache_k = cache_k.reshape(DEC_BATCH, -1, PAST_LEN, KV_DIM)
    cache_v = cache_v.reshape(DEC_BATCH, -1, PAST_LEN, KV_DIM)

    new_kv = ()
    for layer in range(DEPTH):
        jl = layer // N_MIXERS
        if layer % N_MIXERS == 0:
            x0, u = _in_proj(stream, norm_mix, mods, layer, hy_w_in, hy_conv_w, hy_conv_b, jl)
            mixed = []
            for L, row_block0, n_steps, seq_per_step, tn, chains in (
                    (SEQ, 0, BATCH // 2, 2, D_MODEL, 1),
                    (DEC_SEQ, N_PROMPT // DEC_SEQ, DEC_BATCH, 1, 512, 1)):
                fwd_bf16, inv_bf16 = tables[L]
                spectrum = _hyena_spectrum(L, fwd_bf16, jl, f_w1, f_b1, hy_f_freq, hy_f_w2, f_b2, hy_f_w3)
                mixed.append(_hyena_conv(u, x0, spectrum, bias, jl, fwd_bf16, inv_bf16, L, row_block0, n_steps,
                                         seq_per_step, tn, chains))
            w_out = hy_w_out
        else:
            q, k, v, *new_kv = _qkv(stream[0], norm_mix, mods, layer, at_w_qkv, q_gain, k_gain, jl, new_kv)
            mixed = [_attention(q, k, v, 0, BATCH, SEQ, SEQ, N_KV_HEADS, req_per_step=2),
                     _attention(q, k, v, N_PROMPT, DEC_BATCH, DEC_SEQ, 512, 1, cache_k=cache_k, cache_v=cache_v,
                                cache_layer=jl)]
            w_out = at_w_out
        stream = tuple(_ffn(stream, mixed[0], mixed[1], w_out, jl, norm_ffn, mods, layer,
                            ffn_w_gate, ffn_w_up, ffn_w_down, final_norm, final=(layer == DEPTH - 1)))

    y_prompt, y_sample = stream
    new_k, new_v = (a.reshape(BATCH, -1, SEQ, N_KV_HEADS, HEAD_DIM) for a in new_kv)
    return (y_prompt.reshape(BATCH, SEQ, D_MODEL), y_sample.reshape(DEC_BATCH, DEC_SEQ, D_MODEL), new_k, new_v)
```

```python
import functools
import math

import numpy as np
import jax
import jax.numpy as jnp
from jax import lax
from jax.experimental import pallas as pl
from jax.experimental.pallas import tpu as pltpu

D_MODEL = 1024
BATCH = 16
SEQ = 256
DEPTH = 4
DEC_BATCH = 4
DEC_SEQ = 1024
PAST_LEN = 512
GRID_W = 64
N_MIXERS = 2
HEAD_DIM = 128
N_HEADS = D_MODEL // HEAD_DIM
N_KV_HEADS = 2
GROUP = N_HEADS // N_KV_HEADS
KV_DIM = N_KV_HEADS * HEAD_DIM
HEAD_PAIR = 2 * HEAD_DIM
QKV_DIM = (N_HEADS + 2 * N_KV_HEADS) * HEAD_DIM
ROPE_THETA = 10000.0
D_FF = ((8 * D_MODEL + 3 * 256 - 1) // (3 * 256)) * 256
N_BANDS = 16
FILTER_EMB = 1 + 2 * N_BANDS
FILTER_HIDDEN = 64
MIN_DECAY = math.log(1e-2) / 1.5
MAX_DECAY = math.log(1e-2) / 0.3
EPS = 1e-6

N_PROMPT = BATCH * SEQ
N_SAMPLE = DEC_BATCH * DEC_SEQ
N_TOK = N_PROMPT + N_SAMPLE
N_MOD = 6
MOD_ROWS = 8
FEAT_PAD = 128
CONV_PAD = 8
FF_CHUNK = 256
N_FF_CHUNKS = D_FF // FF_CHUNK
V7X_VMEM_LIMIT = 56 * 1024 * 1024

F32 = jnp.float32
BF16 = jnp.bfloat16


def _params(n_axes, vmem=V7X_VMEM_LIMIT):
    return pltpu.CompilerParams(dimension_semantics=("arbitrary",) * n_axes, vmem_limit_bytes=vmem)


def _mod_row(tile, tm):
    n_prompt_tiles = N_PROMPT // tm
    tiles_per_req = DEC_SEQ // tm
    return jnp.where(tile < n_prompt_tiles, 0, 1 + jnp.maximum(tile - n_prompt_tiles, 0) // tiles_per_req)


def _mod_spec(layer, which, tm, tile_of=lambda i, *_: i):
    def index(*ids):
        return ((layer * N_MOD + which) * MOD_ROWS + _mod_row(tile_of(*ids), tm), 0, 0)
    return pl.BlockSpec((1, 1, D_MODEL), index)


def _pair_specs(tm, width, tile_of=lambda i, *_: i):
    npt = N_PROMPT // tm
    return [pl.BlockSpec((tm, width), lambda *ids: (jnp.minimum(tile_of(*ids), npt - 1), 0)),
            pl.BlockSpec((tm, width), lambda *ids: (jnp.maximum(tile_of(*ids) - npt, 0), 0))]


def _pair_shapes(width, dtype):
    return [jax.ShapeDtypeStruct((N_PROMPT, width), dtype), jax.ShapeDtypeStruct((N_SAMPLE, width), dtype)]


def _stream_specs(n_arrays, tm, tile_of=lambda i, *_: i):
    if n_arrays == 2:
        return _pair_specs(tm, D_MODEL, tile_of)
    return [pl.BlockSpec((tm, D_MODEL), lambda *ids: (tile_of(*ids), 0))]


def _read_stream(refs, is_prompt):
    if len(refs) == 2:
        return jnp.where(is_prompt, refs[0][...], refs[1][...])
    return refs[0][...]


def _split_bf16(a):
    hi = a.astype(BF16)
    lo = (a - hi.astype(F32)).astype(BF16)
    return hi, lo


def _dot(a, b):
    return jnp.dot(a, b, preferred_element_type=F32)


def _dot3(a, b):
    a_hi, a_lo = _split_bf16(a)
    b_hi, b_lo = _split_bf16(b)
    return _dot(a_hi, b_hi) + (_dot(a_lo, b_hi) + _dot(a_hi, b_lo))


def _silu(a):
    return a / (1.0 + jnp.exp(-a))


def _rmsnorm(x, g):
    return x * lax.rsqrt(jnp.mean(x * x, axis=-1, keepdims=True) + EPS) * g


def _modulate(x, g, shift, scale):
    return _rmsnorm(x, g) * (1.0 + scale) + shift


def _adaln_kernel(c_ref, w_ref, b_ref, o_ref):
    s = _silu(c_ref[...])
    m = _dot3(s, w_ref[0]) + b_ref[0]
    for v in range(m.shape[1] // D_MODEL):
        o_ref[v * MOD_ROWS:(v + 1) * MOD_ROWS, 0, :] = m[:, v * D_MODEL:(v + 1) * D_MODEL]


def _adaln(cond, mod_w, mod_b):
    per_step = 2
    tn = per_step * D_MODEL
    return pl.pallas_call(
        _adaln_kernel,
        grid=(DEPTH, N_MOD // per_step),
        in_specs=[
            pl.BlockSpec((MOD_ROWS, D_MODEL), lambda l, j: (0, 0)),
            pl.BlockSpec((1, D_MODEL, tn), lambda l, j: (l, 0, j)),
            pl.BlockSpec((1, 1, tn), lambda l, j: (l, 0, j)),
        ],
        out_specs=pl.BlockSpec((per_step * MOD_ROWS, 1, D_MODEL), lambda l, j: (l * (N_MOD // per_step) + j, 0, 0)),
        out_shape=jax.ShapeDtypeStruct((DEPTH * N_MOD * MOD_ROWS, 1, D_MODEL), F32),
        compiler_params=_params(2),
        name="adaln",
    )(cond, mod_w, mod_b.reshape(DEPTH, 1, N_MOD * D_MODEL))


def _cast_kernel(x_ref, o_ref):
    o_ref[...] = x_ref[...].astype(BF16)


def _cast_bf16(x, rows):
    m, n = x.shape
    return pl.pallas_call(
        _cast_kernel,
        grid=(m // rows,),
        in_specs=[pl.BlockSpec((rows, n), lambda i: (i, 0))],
        out_specs=pl.BlockSpec((rows, n), lambda i: (i, 0)),
        out_shape=jax.ShapeDtypeStruct((m, n), BF16),
        compiler_params=_params(1),
        name="cast_table",
    )(x)


def _in_proj_kernel(*refs, n_stream, n_prompt_tiles):
    x_refs, refs = refs[:n_stream], refs[n_stream:]
    (g_ref, sh_ref, sc_ref, w0_ref, w1_ref, w2_ref, cw0_ref, cw1_ref, cw2_ref, cb0_ref, cb1_ref, cb2_ref,
     x0_ref, u_ref, h_s, w_s, *z_refs) = refs
    i, c = pl.program_id(0), pl.program_id(1)
    tm, tc = x0_ref.shape

    @pl.when(i == 0)
    def _():
        w_s[c, 0] = w0_ref[...].astype(BF16)
        w_s[c, 1] = w1_ref[...].astype(BF16)
        w_s[c, 2] = w2_ref[...].astype(BF16)

    @pl.when(c == 0)
    def _():
        x = _read_stream(x_refs, i < n_prompt_tiles)
        h_s[...] = _modulate(x, g_ref[...], sh_ref[0], sc_ref[0]).astype(BF16)

    n_chain = tm // SEQ
    za_refs, zb_refs = z_refs[:n_chain], z_refs[n_chain:]
    lo, hi = CONV_PAD, CONV_PAD + SEQ

    def tile(seq_len):
        def project(part, bufs):
            for b in range(n_chain):
                z = _dot(h_s[b * SEQ:(b + 1) * SEQ, :], w_s[c, part])
                bufs[b][lo:hi, :] = z
                if (b * SEQ) % seq_len:
                    bufs[b - 1][hi:hi + 1, :] = z[0:1, :]
                else:
                    bufs[b][lo - 1:lo, :] = jnp.zeros((1, tc), F32)
                if ((b + 1) * SEQ) % seq_len:
                    bufs[b + 1][lo - 1:lo, :] = z[SEQ - 1:SEQ, :]
                else:
                    bufs[b][hi:hi + 1, :] = jnp.zeros((1, tc), F32)

        def short_conv(z_s, cw_ref, cb_ref):
            return (z_s[lo - 1:hi - 1, :] * cw_ref[0:1, :] + z_s[lo:hi, :] * cw_ref[1:2, :]
                    + z_s[lo + 1:hi + 1, :] * cw_ref[2:3, :] + cb_ref[...])

        project(0, za_refs)
        for b in range(n_chain):
            x0_ref[b * SEQ:(b + 1) * SEQ, :] = short_conv(za_refs[b], cw0_ref, cb0_ref).astype(BF16)
        project(1, zb_refs)
        project(2, za_refs)
        for b in range(n_chain):
            u = short_conv(zb_refs[b], cw1_ref, cb1_ref) * short_conv(za_refs[b], cw2_ref, cb2_ref)
            u_ref[b * SEQ:(b + 1) * SEQ, :] = u.astype(BF16)

    @pl.when(i < n_prompt_tiles)
    def _():
        tile(SEQ)

    @pl.when(i >= n_prompt_tiles)
    def _():
        tile(DEC_SEQ)


def _in_proj(stream, norm_mix, mods, layer, w_in, conv_w, conv_b, jl):
    tm, tc = 1024, 512
    nc = D_MODEL // tc
    npt = N_PROMPT // tm

    def wspec(part):
        return pl.BlockSpec((None, D_MODEL, tc), lambda i, c: (jl, 0, part * nc + jnp.where(i == 0, c, nc - 1)))

    def cwspec(part):
        return pl.BlockSpec((None, 3, tc), lambda i, c: (jl, 0, part * nc + c))

    def cbspec(part):
        return pl.BlockSpec((None, 1, tc), lambda i, c: (jl, 0, part * nc + c))

    out = pl.BlockSpec((tm, tc), lambda i, c: (i, c))
    cb = conv_b.reshape(-1, 1, 3 * D_MODEL)
    return pl.pallas_call(
        functools.partial(_in_proj_kernel, n_stream=len(stream), n_prompt_tiles=npt),
        grid=(N_TOK // tm, nc),
        in_specs=_stream_specs(len(stream), tm) + [
            pl.BlockSpec((None, 1, D_MODEL), lambda i, c: (layer, 0, 0)),
            _mod_spec(layer, 0, tm),
            _mod_spec(layer, 1, tm),
            wspec(0), wspec(1), wspec(2),
            cwspec(0), cwspec(1), cwspec(2),
            cbspec(0), cbspec(1), cbspec(2),
        ],
        out_specs=[out, out],
        out_shape=[jax.ShapeDtypeStruct((N_TOK, D_MODEL), BF16)] * 2,
        scratch_shapes=[pltpu.VMEM((tm, D_MODEL), BF16), pltpu.VMEM((nc, 3, D_MODEL, tc), BF16),
                        ] + [pltpu.VMEM((SEQ + 2 * CONV_PAD, tc), F32)] * (2 * (tm // SEQ)),
        compiler_params=_params(2),
        name="hyena_in_proj",
    )(*stream, norm_mix, mods, mods, w_in, w_in, w_in, conv_w, conv_w, conv_w, cb, cb, cb)


def _dft_tables(L):
    n = 2 * L
    k = np.arange(L, dtype=np.float64)[:, None]
    t = np.arange(L, dtype=np.float64)[None, :]
    ang = 2.0 * np.pi * k * t / n
    top = np.cos(ang)
    bot = -np.sin(ang)
    bot[0, :] = np.where(np.arange(L) % 2 == 0, 1.0, -1.0)
    fwd = np.concatenate([top, bot], axis=0)
    wk = np.full((L,), 2.0)
    wk[0] = 1.0
    inv_top = (np.cos(ang) * wk[:, None]).T / n
    inv_bot = (-2.0 * np.sin(ang)).T / n
    inv_bot[:, 0] = np.where(np.arange(L) % 2 == 0, 1.0, -1.0) / n
    inv = np.concatenate([inv_top, inv_bot], axis=1)
    return fwd.astype(np.float32), inv.astype(np.float32)


def _filter_feats(L):
    t = np.arange(L, dtype=np.float32) / np.float32(L)
    bands = np.arange(1, N_BANDS + 1, dtype=np.float32)
    ang = (np.float32(2.0 * math.pi) * t[:, None]) * bands[None, :]
    feats = np.concatenate([t[:, None], np.cos(ang), np.sin(ang)], axis=-1).astype(np.float32)
    return np.pad(feats, ((0, 0), (0, FEAT_PAD - FILTER_EMB)))


def _filter_kernel(feats_ref, w1_ref, b1_ref, fr_ref, w2_ref, b2_ref, w3f_ref, w3b_ref, dl_ref, fwd_ref,
                   g_ref, h_s, *, L):
    @pl.when(pl.program_id(0) == 0)
    def _():
        h1 = jnp.sin(fr_ref[0:1, :] * (_dot3(feats_ref[...], w1_ref[...]) + b1_ref[...]))
        h_s[...] = jnp.sin(fr_ref[1:2, :] * (_dot3(h1, w2_ref[...]) + b2_ref[...]))

    tn = g_ref.shape[1]
    h = h_s[...]
    rows = lax.broadcasted_iota(jnp.int32, (L, tn), 0)
    t = rows.astype(F32) / L
    window = jnp.exp(-t * dl_ref[...])
    hf = _dot3(h, w3f_ref[...]) * window
    hb = jnp.where(rows > 0, _dot3(h, w3b_ref[...]) * window, 0.0)
    norm = jnp.sqrt(jnp.sum(hf * hf + hb * hb, axis=0, keepdims=True) + EPS)
    hf = hf / norm
    hb = hb / norm

    g_top = _dot(fwd_ref[0:L, :], (hf + hb).astype(BF16))
    g_bot = _dot(fwd_ref[L:2 * L, :], (hf - hb).astype(BF16))
    sign = jnp.where(rows % 2 == 0, 1.0, -1.0)
    nyquist_b = jnp.sum(sign * hb, axis=0, keepdims=True)
    g_bot = g_bot + jnp.where(rows == 0, 2.0 * nyquist_b, 0.0)
    g_ref[0:L, :] = g_top
    g_ref[L:2 * L, :] = g_bot


def _hyena_spectrum(L, fwd, jl, f_w1, f_b1, f_freq, f_w2, f_b2, f_w3):
    tn = 512
    nj = D_MODEL // tn
    feats = jnp.asarray(_filter_feats(L))
    deltas = jnp.asarray(np.abs(np.linspace(MIN_DECAY, MAX_DECAY, D_MODEL, dtype=np.float32)).reshape(1, D_MODEL))
    const = lambda j: (0, 0)
    layer = lambda j: (jl, 0, 0)
    return pl.pallas_call(
        functools.partial(_filter_kernel, L=L),
        grid=(nj,),
        in_specs=[
            pl.BlockSpec((L, FEAT_PAD), const),
            pl.BlockSpec((None, FEAT_PAD, FILTER_HIDDEN), layer),
            pl.BlockSpec((None, 1, FILTER_HIDDEN), layer),
            pl.BlockSpec((None, 2, FILTER_HIDDEN), layer),
            pl.BlockSpec((None, FILTER_HIDDEN, FILTER_HIDDEN), layer),
            pl.BlockSpec((None, 1, FILTER_HIDDEN), layer),
            pl.BlockSpec((None, FILTER_HIDDEN, tn), lambda j: (jl, 0, j)),
            pl.BlockSpec((None, FILTER_HIDDEN, tn), lambda j: (jl, 0, nj + j)),
            pl.BlockSpec((1, tn), lambda j: (0, j)),
            pl.BlockSpec((2 * L, L), const),
        ],
        out_specs=pl.BlockSpec((2 * L, tn), lambda j: (0, j)),
        out_shape=jax.ShapeDtypeStruct((2 * L, D_MODEL), F32),
        scratch_shapes=[pltpu.VMEM((L, FILTER_HIDDEN), F32)],
        compiler_params=_params(1),
        name="hyena_spectrum_%d" % L,
    )(feats, f_w1, f_b1, f_freq, f_w2, f_b2, f_w3, f_w3, deltas, fwd)


def _hyena_conv_kernel(u_ref, x0_ref, g_ref, bias_ref, fwd_ref, inv_ref, o_ref, *, L, col_chains):
    n_seq = o_ref.shape[0] // L
    tn = o_ref.shape[1] // col_chains
    first = lax.broadcasted_iota(jnp.int32, (L, tn), 0) == 0
    for s in range(n_seq):
        for cc in range(col_chains):
            rows, cols = slice(s * L, (s + 1) * L), slice(cc * tn, (cc + 1) * tn)
            u = u_ref[rows, cols]
            spec = _dot(fwd_ref[...], u)
            u_top, u_bot = spec[0:L, :], spec[L:2 * L, :]
            g_top, g_bot = g_ref[0:L, cols], g_ref[L:2 * L, cols]
            y_top = u_top * g_top - jnp.where(first, 0.0, u_bot * g_bot)
            y_bot = jnp.where(first, u_bot * g_bot, u_top * g_bot + u_bot * g_top)
            y_spec = jnp.concatenate([y_top, y_bot], axis=0).astype(BF16)
            y = _dot(inv_ref[...], y_spec)
            gated = x0_ref[rows, cols].astype(F32) * (y + u.astype(F32) * bias_ref[:, cols])
            o_ref[rows, cols] = gated.astype(BF16)


def _hyena_conv(u, x0, spectrum, bias, jl, fwd, inv, L, row_block0, n_steps, seq_per_step, tn, col_chains):
    nj = D_MODEL // tn
    rows = seq_per_step * L
    act = pl.BlockSpec((rows, tn), lambda j, b: (row_block0 + b, j))
    return pl.pallas_call(
        functools.partial(_hyena_conv_kernel, L=L, col_chains=col_chains),
        grid=(nj, n_steps),
        in_specs=[
            act, act,
            pl.BlockSpec((2 * L, tn), lambda j, b: (0, j)),
            pl.BlockSpec((None, 1, tn), lambda j, b: (jl, 0, j)),
            pl.BlockSpec((2 * L, L), lambda j, b: (0, 0)),
            pl.BlockSpec((L, 2 * L), lambda j, b: (0, 0)),
        ],
        out_specs=pl.BlockSpec((rows, tn), lambda j, b: (b, j)),
        out_shape=jax.ShapeDtypeStruct((n_steps * rows, D_MODEL), BF16),
        compiler_params=_params(2),
        name="hyena_conv_%d" % L,
    )(u, x0, spectrum, bias, fwd, inv)


def _rope_tables():
    rows = DEC_SEQ // GRID_W
    row = np.repeat(np.arange(rows), GRID_W).astype(np.float32)
    col = np.tile(np.arange(GRID_W), rows).astype(np.float32)
    half = HEAD_DIM // 2
    freqs = (np.float32(ROPE_THETA) ** (-np.arange(0, half, 2, dtype=np.float32) / np.float32(half))).astype(np.float32)
    ang = np.concatenate([row[:, None] * freqs[None, :], col[:, None] * freqs[None, :]], axis=-1)
    cos = np.repeat(np.cos(ang), 2, axis=-1)
    sin = np.repeat(np.sin(ang), 2, axis=-1)
    sign = np.where(np.arange(HEAD_DIM) % 2 == 0, -1.0, 1.0)[None, :]
    pair = lambda a: np.tile(a, (1, 2)).astype(np.float32)
    return pair(cos), pair(sin * sign)


def _qkv_kernel(*refs, n_prev, n_prompt_tiles):
    x_ref, g_ref, sh_ref, sc_ref, w_ref, qg_ref, kg_ref, cos_ref, sin_ref = refs[:9]
    prev_refs, refs = refs[9:9 + 2 * (n_prev > 0)], refs[9 + 2 * (n_prev > 0):]
    q_ref, k_ref, v_ref, kf_ref, vf_ref, w_s = refs
    i = pl.program_id(0)
    tm = q_ref.shape[0]

    @pl.when(i == 0)
    def _():
        w_s[...] = w_ref[...].astype(BF16)

    k_col = N_HEADS * HEAD_DIM
    v_col = k_col + KV_DIM
    q_gain = qg_ref[...] * HEAD_DIM ** -0.5

    def project(rows):
        h = _modulate(x_ref[rows, :], g_ref[...], sh_ref[0], sc_ref[0]).astype(BF16)
        qkv = _dot(h, w_s[...])
        v = qkv[:, v_col:]
        v_ref[rows, :] = v.astype(BF16)

        def pair_norm(col, gain):
            return jnp.concatenate([_rmsnorm(qkv[:, c:c + HEAD_DIM], gain) for c in (col, col + HEAD_DIM)], axis=1)

        return pair_norm, v

    @pl.when(i < n_prompt_tiles)
    def _():
        if n_prev:
            kf_ref[:, 0:n_prev] = prev_refs[0][...]
            vf_ref[:, 0:n_prev] = prev_refs[1][...]
        for r in range(tm // SEQ):
            rows = slice(r * SEQ, (r + 1) * SEQ)
            pair_norm, v = project(rows)
            for col in range(0, k_col, HEAD_PAIR):
                q_ref[rows, col:col + HEAD_PAIR] = pair_norm(col, q_gain).astype(BF16)
            k = pair_norm(k_col, kg_ref[...])
            k_ref[rows, :] = k.astype(BF16)
            for kv in range(N_KV_HEADS):
                cols = slice(kv * HEAD_DIM, (kv + 1) * HEAD_DIM)
                head_rows = pl.ds(kv, SEQ, stride=N_KV_HEADS)
                kf_ref[r, n_prev, head_rows, :] = k[:, cols]
                vf_ref[r, n_prev, head_rows, :] = v[:, cols]

    @pl.when(i >= n_prompt_tiles)
    def _():
        r_id = lax.broadcasted_iota(jnp.int32, (HEAD_PAIR, HEAD_PAIR), 0)
        col_id = lax.broadcasted_iota(jnp.int32, (HEAD_PAIR, HEAD_PAIR), 1)
        swap = jnp.where((r_id ^ 1) == col_id, 1.0, 0.0).astype(BF16)
        for r in range(tm // SEQ):
            rows = slice(r * SEQ, (r + 1) * SEQ)
            pair_norm, _ = project(rows)

            def rotate(x):
                return x * cos_ref[rows, :] + _dot(x.astype(BF16), swap) * sin_ref[rows, :]

            for col in range(0, k_col, HEAD_PAIR):
                q_ref[rows, col:col + HEAD_PAIR] = rotate(pair_norm(col, q_gain)).astype(BF16)
            k_ref[rows, :] = rotate(pair_norm(k_col, kg_ref[...])).astype(BF16)


def _qkv(y, norm_mix, mods, layer, w_qkv, q_gain, k_gain, jl, prev_kv):
    tm = 512
    npt = N_PROMPT // tm
    tiles_per_req = DEC_SEQ // tm
    cos, sin = _rope_tables()
    rope_spec = pl.BlockSpec((tm, HEAD_PAIR), lambda i: (jnp.maximum(i - npt, 0) % tiles_per_req, 0))
    row = lambda n: pl.BlockSpec((tm, n), lambda i: (i, 0))
    cache_rows = SEQ * N_KV_HEADS
    cache = lambda n: pl.BlockSpec((tm // SEQ, n, cache_rows, HEAD_DIM), lambda i: (jnp.minimum(i, npt - 1), 0, 0, 0))
    gain = pl.BlockSpec((None, 1, HEAD_DIM), lambda i: (jl, 0, 0))
    return pl.pallas_call(
        functools.partial(_qkv_kernel, n_prev=jl, n_prompt_tiles=npt),
        grid=(N_TOK // tm,),
        in_specs=[
            row(D_MODEL),
            pl.BlockSpec((None, 1, D_MODEL), lambda i: (layer, 0, 0)),
            _mod_spec(layer, 0, tm),
            _mod_spec(layer, 1, tm),
            pl.BlockSpec((None, D_MODEL, QKV_DIM), lambda i: (jl, 0, 0)),
            gain, gain,
            rope_spec, rope_spec,
        ] + [cache(jl)] * len(prev_kv),
        out_specs=[row(D_MODEL), row(KV_DIM), row(KV_DIM), cache(jl + 1), cache(jl + 1)],
        out_shape=[
            jax.ShapeDtypeStruct((N_TOK, D_MODEL), BF16),
            jax.ShapeDtypeStruct((N_TOK, KV_DIM), BF16),
            jax.ShapeDtypeStruct((N_TOK, KV_DIM), BF16),
            jax.ShapeDtypeStruct((BATCH, jl + 1, cache_rows, HEAD_DIM), F32),
            jax.ShapeDtypeStruct((BATCH, jl + 1, cache_rows, HEAD_DIM), F32),
        ],
        scratch_shapes=[pltpu.VMEM((D_MODEL, QKV_DIM), BF16)],
        compiler_params=_params(1),
        name="qkv_proj",
    )(y, norm_mix, mods, mods, w_qkv, q_gain, k_gain, jnp.asarray(cos), jnp.asarray(sin), *prev_kv)


def _attention_kernel(*refs, with_cache, seq):
    if with_cache:
        q_ref, k_ref, v_ref, ck_ref, cv_ref, o_ref = refs
    else:
        q_ref, k_ref, v_ref, o_ref = refs
    n_req = k_ref.shape[0] // seq
    tq = q_ref.shape[0] // n_req
    for r in range(n_req):
        q_rows, k_rows = slice(r * tq, (r + 1) * tq), slice(r * seq, (r + 1) * seq)
        for kv in range(k_ref.shape[1] // HEAD_DIM):
            kv_cols = slice(kv * HEAD_DIM, (kv + 1) * HEAD_DIM)
            k = k_ref[k_rows, kv_cols]
            v = v_ref[k_rows, kv_cols]
            if with_cache:
                head_rows = pl.ds(kv, PAST_LEN, stride=N_KV_HEADS)
                k = jnp.concatenate([k, ck_ref[0, 0, head_rows, :].astype(BF16)], axis=0)
                v = jnp.concatenate([v, cv_ref[0, 0, head_rows, :].astype(BF16)], axis=0)
            v_ones = jnp.concatenate([v, jnp.ones_like(v)], axis=1)
            for g in range(GROUP):
                cols = slice((kv * GROUP + g) * HEAD_DIM, (kv * GROUP + g + 1) * HEAD_DIM)
                s = lax.dot_general(q_ref[q_rows, cols], k, (((1,), (1,)), ((), ())), preferred_element_type=F32)
                p = jnp.exp(s - jnp.max(s, axis=-1, keepdims=True))
                o = _dot(p.astype(BF16), v_ones)
                o_ref[q_rows, cols] = (o[:, :HEAD_DIM] / o[:, HEAD_DIM:]).astype(BF16)


def _attention(q, k, v, row0, n_req, seq, tq, kv_per_step, req_per_step=1, cache_k=None, cache_v=None,
               cache_layer=0):
    nq = seq // tq
    n_req //= req_per_step
    tq, seq_rows = tq * req_per_step, seq * req_per_step
    qb0, kb0 = row0 // tq, row0 // seq_rows
    qw, kw = kv_per_step * GROUP * HEAD_DIM, kv_per_step * HEAD_DIM
    in_specs = [
        pl.BlockSpec((tq, qw), lambda b, h, t: (qb0 + b * nq + t, h)),
        pl.BlockSpec((seq_rows, kw), lambda b, h, t: (kb0 + b, h)),
        pl.BlockSpec((seq_rows, kw), lambda b, h, t: (kb0 + b, h)),
    ]
    args = [q, k, v]
    with_cache = cache_k is not None
    if with_cache:
        assert kv_per_step == N_KV_HEADS
        cspec = pl.BlockSpec((1, 1, PAST_LEN * N_KV_HEADS, HEAD_DIM), lambda b, h, t: (b, cache_layer, 0, 0))
        in_specs += [cspec, cspec]
        args += [cache_k, cache_v]
    return pl.pallas_call(
        functools.partial(_attention_kernel, with_cache=with_cache, seq=seq),
        grid=(n_req, N_KV_HEADS // kv_per_step, nq),
        in_specs=in_specs,
        out_specs=pl.BlockSpec((tq, qw), lambda b, h, t: (b * nq + t, h)),
        out_shape=jax.ShapeDtypeStruct((n_req * seq_rows, D_MODEL), BF16),
        compiler_params=_params(3),
        name="attention_%d" % seq,
    )(*args)


def _ffn_kernel(*refs, n_in, n_out, n_prompt_tiles, final):
    y_refs, refs = refs[:n_in], refs[n_in:]
    (ap_ref, as_ref, wo_ref, gm_ref, g_ref, sh_ref, sc_ref, gate_ref, wg_ref, wu_ref, wd_ref, fin_ref) = refs[:12]
    o_refs, refs = refs[12:12 + n_out], refs[12 + n_out:]
    wo_s, wg_s, wu_s, wd_s, ymid_s, f_s, acc_s = refs
    s = pl.program_id(0)
    nf = N_FF_CHUNKS
    tile = jnp.maximum(s - (nf - 1), 0)
    is_prompt = tile < n_prompt_tiles

    def prologue():
        a = jnp.where(is_prompt, ap_ref[...], as_ref[...])
        y = _read_stream(y_refs, is_prompt)
        y_mid = y + gm_ref[0] * _dot(a, wo_s[...])
        ymid_s[...] = y_mid
        f_s[...] = _modulate(y_mid, g_ref[...], sh_ref[0], sc_ref[0]).astype(BF16)

    def chunk(c):
        f = f_s[...]
        hidden = _silu(_dot(f, wg_s[c])) * _dot(f, wu_s[c])
        return _dot(hidden.astype(BF16), wd_s[c])

    def epilogue(acc):
        out = ymid_s[...] + gate_ref[0] * acc
        if final:
            out = _rmsnorm(out, fin_ref[...])
        if n_out == 1:
            o_refs[0][...] = out
        else:
            @pl.when(is_prompt)
            def _():
                o_refs[0][...] = out

            @pl.when(jnp.logical_not(is_prompt))
            def _():
                o_refs[1][...] = out

    @pl.when(s == 0)
    def _():
        wo_s[...] = wo_ref[...].astype(BF16)
        prologue()

    @pl.when(s < nf)
    def _():
        wg_s[s] = wg_ref[...].astype(BF16)
        wu_s[s] = wu_ref[...].astype(BF16)
        wd_s[s] = wd_ref[...].astype(BF16)
        part = chunk(s)

        @pl.when(s == 0)
        def _():
            acc_s[...] = part

        @pl.when(s > 0)
        def _():
            acc_s[...] += part

        @pl.when(s == nf - 1)
        def _():
            epilogue(acc_s[...])

    @pl.when(s >= nf)
    def _():
        prologue()
        acc = chunk(0)
        for c in range(1, nf):
            acc = acc + chunk(c)
        epilogue(acc)


def _ffn(stream, ap, as_, w_out, jl, norm_ffn, mods, layer, wg, wu, wd, final_norm, final):
    tm = 512
    nf = N_FF_CHUNKS
    npt = N_PROMPT // tm
    n_steps = nf + N_TOK // tm - 1
    n_out = 2 if final else 1
    tile_of = lambda s: jnp.maximum(s - (nf - 1), 0)
    chunk_of = lambda s: jnp.minimum(s, nf - 1)
    out_shape = _pair_shapes(D_MODEL, F32) if final else [jax.ShapeDtypeStruct((N_TOK, D_MODEL), F32)]
    return pl.pallas_call(
        functools.partial(_ffn_kernel, n_in=len(stream), n_out=n_out, n_prompt_tiles=npt, final=final),
        grid=(n_steps,),
        in_specs=_stream_specs(len(stream), tm, tile_of) + _pair_specs(tm, D_MODEL, tile_of) + [
            pl.BlockSpec((None, D_MODEL, D_MODEL), lambda s: (jl, 0, 0), pipeline_mode=pl.Buffered(1)),
            _mod_spec(layer, 2, tm, tile_of),
            pl.BlockSpec((None, 1, D_MODEL), lambda s: (layer, 0, 0)),
            _mod_spec(layer, 3, tm, tile_of),
            _mod_spec(layer, 4, tm, tile_of),
            _mod_spec(layer, 5, tm, tile_of),
            pl.BlockSpec((None, D_MODEL, FF_CHUNK), lambda s: (layer, 0, chunk_of(s))),
            pl.BlockSpec((None, D_MODEL, FF_CHUNK), lambda s: (layer, 0, chunk_of(s))),
            pl.BlockSpec((None, FF_CHUNK, D_MODEL), lambda s: (layer, chunk_of(s), 0)),
            pl.BlockSpec((1, D_MODEL), lambda s: (0, 0)),
        ],
        out_specs=_stream_specs(n_out, tm, tile_of),
        out_shape=out_shape,
        scratch_shapes=[
            pltpu.VMEM((D_MODEL, D_MODEL), BF16),
            pltpu.VMEM((nf, D_MODEL, FF_CHUNK), BF16),
            pltpu.VMEM((nf, D_MODEL, FF_CHUNK), BF16),
            pltpu.VMEM((nf, FF_CHUNK, D_MODEL), BF16),
            pltpu.VMEM((tm, D_MODEL), F32),
            pltpu.VMEM((tm, D_MODEL), BF16),
            pltpu.VMEM((tm, D_MODEL), F32),
        ],
        compiler_params=_params(1),
        name="ffn",
    )(*stream, ap, as_, w_out, mods, norm_ffn, mods, mods, mods, wg, wu, wd, final_norm.reshape(1, D_MODEL))


def kernel(x_prompt, x_sample, cache_k, cache_v, c, c_ctx, mod_w, mod_b, norm_mix, norm_ffn, hy_w_in, hy_conv_w, hy_conv_b, hy_f_w1, hy_f_b1, hy_f_freq, hy_f_w2, hy_f_b2, hy_f_w3, hy_bias, hy_w_out, at_w_qkv, at_q_norm, at_k_norm, at_w_out, ffn_w_gate, ffn_w_up, ffn_w_down, final_norm):
    stream = (x_prompt.reshape(N_PROMPT, D_MODEL), x_sample.reshape(N_SAMPLE, D_MODEL))
    cond = jnp.concatenate([c_ctx[None, :], c, jnp.zeros((MOD_ROWS - 1 - DEC_BATCH, D_MODEL), F32)], axis=0)
    mods = _adaln(cond, mod_w, mod_b)

    tables = {}
    for L in (SEQ, DEC_SEQ):
        fwd, inv = _dft_tables(L)
        tables[L] = (_cast_bf16(jnp.asarray(fwd), 2 * L // 4), _cast_bf16(jnp.asarray(inv), L // 4))

    norm_mix = norm_mix.reshape(DEPTH, 1, D_MODEL)
    norm_ffn = norm_ffn.reshape(DEPTH, 1, D_MODEL)
    f_w1 = jnp.pad(hy_f_w1, ((0, 0), (0, FEAT_PAD - FILTER_EMB), (0, 0)))
    f_b1 = hy_f_b1.reshape(-1, 1, FILTER_HIDDEN)
    f_b2 = hy_f_b2.reshape(-1, 1, FILTER_HIDDEN)
    bias = hy_bias.reshape(-1, 1, D_MODEL)
    q_gain = at_q_norm.reshape(-1, 1, HEAD_DIM)
    k_gain = at_k_norm.reshape(-1, 1, HEAD_DIM)
    cache_k = cache_k.reshape(DEC_BATCH, -1, PAST_LEN * N_KV_HEADS, HEAD_DIM)
    cache_v = cache_v.reshape(DEC_BATCH, -1, PAST_LEN * N_KV_HEADS, HEAD_DIM)

    new_kv = ()
    for layer in range(DEPTH):
        jl = layer // N_MIXERS
        if layer % N_MIXERS == 0:
            x0, u = _in_proj(stream, norm_mix, mods, layer, hy_w_in, hy_conv_w, hy_conv_b, jl)
            mixed = []
            for L, row_block0, n_steps, seq_per_step, tn, chains in (
                    (SEQ, 0, BATCH // 2, 2, D_MODEL, 1),
                    (DEC_SEQ, N_PROMPT // DEC_SEQ, DEC_BATCH, 1, 512, 1)):
                fwd_bf16, inv_bf16 = tables[L]
                spectrum = _hyena_spectrum(L, fwd_bf16, jl, f_w1, f_b1, hy_f_freq, hy_f_w2, f_b2, hy_f_w3)
                mixed.append(_hyena_conv(u, x0, spectrum, bias, jl, fwd_bf16, inv_bf16, L, row_block0, n_steps,
                                         seq_per_step, tn, chains))
            w_out = hy_w_out
        else:
            q, k, v, *new_kv = _qkv(stream[0], norm_mix, mods, layer, at_w_qkv, q_gain, k_gain, jl, new_kv)
            mixed = [_attention(q, k, v, 0, BATCH, SEQ, SEQ, N_KV_HEADS, req_per_step=2),
                     _attention(q, k, v, N_PROMPT, DEC_BATCH, DEC_SEQ, 512, N_KV_HEADS, cache_k=cache_k, cache_v=cache_v,
                                cache_layer=jl)]
            w_out = at_w_out
        stream = tuple(_ffn(stream, mixed[0], mixed[1], w_out, jl, norm_ffn, mods, layer,
                            ffn_w_gate, ffn_w_up, ffn_w_down, final_norm, final=(layer == DEPTH - 1)))

    y_prompt, y_sample = stream
    new_k, new_v = (a.reshape(BATCH, -1, SEQ, N_KV_HEADS, HEAD_DIM) for a in new_kv)
    return (y_prompt.reshape(BATCH, SEQ, D_MODEL), y_sample.reshape(DEC_BATCH, DEC_SEQ, D_MODEL), new_k, new_v)
```

```python
import functools
import math

import numpy as np
import jax
import jax.numpy as jnp
from jax import lax
from jax.experimental import pallas as pl
from jax.experimental.pallas import tpu as pltpu

D_MODEL = 1024
BATCH = 16
SEQ = 256
DEPTH = 4
DEC_BATCH = 4
DEC_SEQ = 1024
PAST_LEN = 512
GRID_W = 64
N_MIXERS = 2
HEAD_DIM = 128
N_HEADS = D_MODEL // HEAD_DIM
N_KV_HEADS = 2
GROUP = N_HEADS // N_KV_HEADS
KV_DIM = N_KV_HEADS * HEAD_DIM
HEAD_PAIR = 2 * HEAD_DIM
QKV_DIM = (N_HEADS + 2 * N_KV_HEADS) * HEAD_DIM
ROPE_THETA = 10000.0
D_FF = ((8 * D_MODEL + 3 * 256 - 1) // (3 * 256)) * 256
N_BANDS = 16
FILTER_EMB = 1 + 2 * N_BANDS
FILTER_HIDDEN = 64
MIN_DECAY = math.log(1e-2) / 1.5
MAX_DECAY = math.log(1e-2) / 0.3
EPS = 1e-6

N_PROMPT = BATCH * SEQ
N_SAMPLE = DEC_BATCH * DEC_SEQ
N_TOK = N_PROMPT + N_SAMPLE
N_MOD = 6
MOD_ROWS = 8
FEAT_PAD = 128
CONV_PAD = 8
FF_CHUNK = 256
N_FF_CHUNKS = D_FF // FF_CHUNK
V7X_VMEM_LIMIT = 56 * 1024 * 1024

F32 = jnp.float32
BF16 = jnp.bfloat16


def _params(n_axes, vmem=V7X_VMEM_LIMIT):
    return pltpu.CompilerParams(dimension_semantics=("arbitrary",) * n_axes, vmem_limit_bytes=vmem)


def _mod_row(tile, tm):
    n_prompt_tiles = N_PROMPT // tm
    tiles_per_req = DEC_SEQ // tm
    return jnp.where(tile < n_prompt_tiles, 0, 1 + jnp.maximum(tile - n_prompt_tiles, 0) // tiles_per_req)


def _mod_spec(layer, which, tm, tile_of=lambda i, *_: i):
    def index(*ids):
        return ((layer * N_MOD + which) * MOD_ROWS + _mod_row(tile_of(*ids), tm), 0, 0)
    return pl.BlockSpec((1, 1, D_MODEL), index)


def _pair_specs(tm, width, tile_of=lambda i, *_: i):
    npt = N_PROMPT // tm
    return [pl.BlockSpec((tm, width), lambda *ids: (jnp.minimum(tile_of(*ids), npt - 1), 0)),
            pl.BlockSpec((tm, width), lambda *ids: (jnp.maximum(tile_of(*ids) - npt, 0), 0))]


def _pair_shapes(width, dtype):
    return [jax.ShapeDtypeStruct((N_PROMPT, width), dtype), jax.ShapeDtypeStruct((N_SAMPLE, width), dtype)]


def _stream_specs(n_arrays, tm, tile_of=lambda i, *_: i):
    if n_arrays == 2:
        return _pair_specs(tm, D_MODEL, tile_of)
    return [pl.BlockSpec((tm, D_MODEL), lambda *ids: (tile_of(*ids), 0))]


def _read_stream(refs, is_prompt):
    if len(refs) == 2:
        return jnp.where(is_prompt, refs[0][...], refs[1][...])
    return refs[0][...]


def _split_bf16(a):
    hi = a.astype(BF16)
    lo = (a - hi.astype(F32)).astype(BF16)
    return hi, lo


def _dot(a, b):
    return jnp.dot(a, b, preferred_element_type=F32)


def _dot3(a, b):
    a_hi, a_lo = _split_bf16(a)
    b_hi, b_lo = _split_bf16(b)
    return _dot(a_hi, b_hi) + (_dot(a_lo, b_hi) + _dot(a_hi, b_lo))


def _silu(a):
    return a / (1.0 + jnp.exp(-a))


def _rmsnorm(x, g):
    return x * lax.rsqrt(jnp.mean(x * x, axis=-1, keepdims=True) + EPS) * g


def _modulate(x, g, shift, scale):
    return _rmsnorm(x, g) * (1.0 + scale) + shift


def _adaln_kernel(c_ref, w_ref, b_ref, o_ref):
    s = _silu(c_ref[...])
    m = _dot3(s, w_ref[0]) + b_ref[0]
    for v in range(m.shape[1] // D_MODEL):
        o_ref[v * MOD_ROWS:(v + 1) * MOD_ROWS, 0, :] = m[:, v * D_MODEL:(v + 1) * D_MODEL]


def _adaln(cond, mod_w, mod_b):
    per_step = 2
    tn = per_step * D_MODEL
    return pl.pallas_call(
        _adaln_kernel,
        grid=(DEPTH, N_MOD // per_step),
        in_specs=[
            pl.BlockSpec((MOD_ROWS, D_MODEL), lambda l, j: (0, 0)),
            pl.BlockSpec((1, D_MODEL, tn), lambda l, j: (l, 0, j)),
            pl.BlockSpec((1, 1, tn), lambda l, j: (l, 0, j)),
        ],
        out_specs=pl.BlockSpec((per_step * MOD_ROWS, 1, D_MODEL), lambda l, j: (l * (N_MOD // per_step) + j, 0, 0)),
        out_shape=jax.ShapeDtypeStruct((DEPTH * N_MOD * MOD_ROWS, 1, D_MODEL), F32),
        compiler_params=_params(2),
        name="adaln",
    )(cond, mod_w, mod_b.reshape(DEPTH, 1, N_MOD * D_MODEL))


def _cast_kernel(x_ref, o_ref):
    o_ref[...] = x_ref[...].astype(BF16)


def _cast_bf16(x, rows):
    m, n = x.shape
    return pl.pallas_call(
        _cast_kernel,
        grid=(m // rows,),
        in_specs=[pl.BlockSpec((rows, n), lambda i: (i, 0))],
        out_specs=pl.BlockSpec((rows, n), lambda i: (i, 0)),
        out_shape=jax.ShapeDtypeStruct((m, n), BF16),
        compiler_params=_params(1),
        name="cast_table",
    )(x)


def _in_proj_kernel(*refs, n_stream, n_prompt_tiles):
    x_refs, refs = refs[:n_stream], refs[n_stream:]
    (g_ref, sh_ref, sc_ref, w0_ref, w1_ref, w2_ref, cw0_ref, cw1_ref, cw2_ref, cb0_ref, cb1_ref, cb2_ref,
     x0_ref, u_ref, h_s, w_s) = refs
    i, c = pl.program_id(0), pl.program_id(1)
    tm, tc = x0_ref.shape

    @pl.when(i == 0)
    def _():
        w_s[c, 0] = w0_ref[...].astype(BF16)
        w_s[c, 1] = w1_ref[...].astype(BF16)
        w_s[c, 2] = w2_ref[...].astype(BF16)

    @pl.when(c == 0)
    def _():
        x = _read_stream(x_refs, i < n_prompt_tiles)
        h_s[...] = _modulate(x, g_ref[...], sh_ref[0], sc_ref[0]).astype(BF16)

    n_chain = tm // SEQ
    pad = jnp.zeros((CONV_PAD, tc), F32)

    def tile(seq_len):
        def short_conv(part, cw_ref, cb_ref):
            zs = [_dot(h_s[b * SEQ:(b + 1) * SEQ, :], w_s[c, part]) for b in range(n_chain)]
            out = []
            for b, z in enumerate(zs):
                before = zs[b - 1][SEQ - CONV_PAD:, :] if (b * SEQ) % seq_len else pad
                after = zs[b + 1][:CONV_PAD, :] if ((b + 1) * SEQ) % seq_len else pad
                ext = jnp.concatenate([before, z, after], axis=0)
                rows = ext.shape[0]
                z_prev = pltpu.roll(ext, 1, 0)[CONV_PAD:CONV_PAD + SEQ, :]
                z_next = pltpu.roll(ext, rows - 1, 0)[CONV_PAD:CONV_PAD + SEQ, :]
                out.append(z_prev * cw_ref[0:1, :] + z * cw_ref[1:2, :] + z_next * cw_ref[2:3, :] + cb_ref[...])
            return out

        for b, x0 in enumerate(short_conv(0, cw0_ref, cb0_ref)):
            x0_ref[b * SEQ:(b + 1) * SEQ, :] = x0.astype(BF16)
        for b, (x1, v) in enumerate(zip(short_conv(1, cw1_ref, cb1_ref), short_conv(2, cw2_ref, cb2_ref))):
            u_ref[b * SEQ:(b + 1) * SEQ, :] = (x1 * v).astype(BF16)

    @pl.when(i < n_prompt_tiles)
    def _():
        tile(SEQ)

    @pl.when(i >= n_prompt_tiles)
    def _():
        tile(DEC_SEQ)


def _in_proj(stream, norm_mix, mods, layer, w_in, conv_w, conv_b, jl):
    tm, tc = 1024, 512
    nc = D_MODEL // tc
    npt = N_PROMPT // tm

    def wspec(part):
        return pl.BlockSpec((None, D_MODEL, tc), lambda i, c: (jl, 0, part * nc + jnp.where(i == 0, c, nc - 1)))

    def cwspec(part):
        return pl.BlockSpec((None, 3, tc), lambda i, c: (jl, 0, part * nc + c))

    def cbspec(part):
        return pl.BlockSpec((None, 1, tc), lambda i, c: (jl, 0, part * nc + c))

    out = pl.BlockSpec((tm, tc), lambda i, c: (i, c))
    cb = conv_b.reshape(-1, 1, 3 * D_MODEL)
    return pl.pallas_call(
        functools.partial(_in_proj_kernel, n_stream=len(stream), n_prompt_tiles=npt),
        grid=(N_TOK // tm, nc),
        in_specs=_stream_specs(len(stream), tm) + [
            pl.BlockSpec((None, 1, D_MODEL), lambda i, c: (layer, 0, 0)),
            _mod_spec(layer, 0, tm),
            _mod_spec(layer, 1, tm),
            wspec(0), wspec(1), wspec(2),
            cwspec(0), cwspec(1), cwspec(2),
            cbspec(0), cbspec(1), cbspec(2),
        ],
        out_specs=[out, out],
        out_shape=[jax.ShapeDtypeStruct((N_TOK, D_MODEL), BF16)] * 2,
        scratch_shapes=[pltpu.VMEM((tm, D_MODEL), BF16), pltpu.VMEM((nc, 3, D_MODEL, tc), BF16)],
        compiler_params=_params(2),
        name="hyena_in_proj",
    )(*stream, norm_mix, mods, mods, w_in, w_in, w_in, conv_w, conv_w, conv_w, cb, cb, cb)


def _dft_tables(L):
    n = 2 * L
    k = np.arange(L, dtype=np.float64)[:, None]
    t = np.arange(L, dtype=np.float64)[None, :]
    ang = 2.0 * np.pi * k * t / n
    top = np.cos(ang)
    bot = -np.sin(ang)
    bot[0, :] = np.where(np.arange(L) % 2 == 0, 1.0, -1.0)
    fwd = np.concatenate([top, bot], axis=0)
    wk = np.full((L,), 2.0)
    wk[0] = 1.0
    inv_top = (np.cos(ang) * wk[:, None]).T / n
    inv_bot = (-2.0 * np.sin(ang)).T / n
    inv_bot[:, 0] = np.where(np.arange(L) % 2 == 0, 1.0, -1.0) / n
    inv = np.concatenate([inv_top, inv_bot], axis=1)
    return fwd.astype(np.float32), inv.astype(np.float32)


def _filter_feats(L):
    t = np.arange(L, dtype=np.float32) / np.float32(L)
    bands = np.arange(1, N_BANDS + 1, dtype=np.float32)
    ang = (np.float32(2.0 * math.pi) * t[:, None]) * bands[None, :]
    feats = np.concatenate([t[:, None], np.cos(ang), np.sin(ang)], axis=-1).astype(np.float32)
    return np.pad(feats, ((0, 0), (0, FEAT_PAD - FILTER_EMB)))


def _filter_kernel(feats_ref, w1_ref, b1_ref, fr_ref, w2_ref, b2_ref, w3f_ref, w3b_ref, dl_ref, fwd_ref,
                   g_ref, h_s, *, L):
    @pl.when(pl.program_id(0) == 0)
    def _():
        h1 = jnp.sin(fr_ref[0:1, :] * (_dot3(feats_ref[...], w1_ref[...]) + b1_ref[...]))
        h_s[...] = jnp.sin(fr_ref[1:2, :] * (_dot3(h1, w2_ref[...]) + b2_ref[...]))

    tn = g_ref.shape[1]
    h = h_s[...]
    rows = lax.broadcasted_iota(jnp.int32, (L, tn), 0)
    t = rows.astype(F32) / L
    window = jnp.exp(-t * dl_ref[...])
    hf = _dot3(h, w3f_ref[...]) * window
    hb = jnp.where(rows > 0, _dot3(h, w3b_ref[...]) * window, 0.0)
    norm = jnp.sqrt(jnp.sum(hf * hf + hb * hb, axis=0, keepdims=True) + EPS)
    hf = hf / norm
    hb = hb / norm

    g_top = _dot(fwd_ref[0:L, :], (hf + hb).astype(BF16))
    g_bot = _dot(fwd_ref[L:2 * L, :], (hf - hb).astype(BF16))
    sign = jnp.where(rows % 2 == 0, 1.0, -1.0)
    nyquist_b = jnp.sum(sign * hb, axis=0, keepdims=True)
    g_bot = g_bot + jnp.where(rows == 0, 2.0 * nyquist_b, 0.0)
    g_ref[0:L, :] = g_top
    g_ref[L:2 * L, :] = g_bot


def _hyena_spectrum(L, fwd, jl, f_w1, f_b1, f_freq, f_w2, f_b2, f_w3):
    tn = 512
    nj = D_MODEL // tn
    feats = jnp.asarray(_filter_feats(L))
    deltas = jnp.asarray(np.abs(np.linspace(MIN_DECAY, MAX_DECAY, D_MODEL, dtype=np.float32)).reshape(1, D_MODEL))
    const = lambda j: (0, 0)
    layer = lambda j: (jl, 0, 0)
    return pl.pallas_call(
        functools.partial(_filter_kernel, L=L),
        grid=(nj,),
        in_specs=[
            pl.BlockSpec((L, FEAT_PAD), const),
            pl.BlockSpec((None, FEAT_PAD, FILTER_HIDDEN), layer),
            pl.BlockSpec((None, 1, FILTER_HIDDEN), layer),
            pl.BlockSpec((None, 2, FILTER_HIDDEN), layer),
            pl.BlockSpec((None, FILTER_HIDDEN, FILTER_HIDDEN), layer),
            pl.BlockSpec((None, 1, FILTER_HIDDEN), layer),
            pl.BlockSpec((None, FILTER_HIDDEN, tn), lambda j: (jl, 0, j)),
            pl.BlockSpec((None, FILTER_HIDDEN, tn), lambda j: (jl, 0, nj + j)),
            pl.BlockSpec((1, tn), lambda j: (0, j)),
            pl.BlockSpec((2 * L, L), const),
        ],
        out_specs=pl.BlockSpec((2 * L, tn), lambda j: (0, j)),
        out_shape=jax.ShapeDtypeStruct((2 * L, D_MODEL), F32),
        scratch_shapes=[pltpu.VMEM((L, FILTER_HIDDEN), F32)],
        compiler_params=_params(1),
        name="hyena_spectrum_%d" % L,
    )(feats, f_w1, f_b1, f_freq, f_w2, f_b2, f_w3, f_w3, deltas, fwd)


def _hyena_conv_kernel(u_ref, x0_ref, g_ref, bias_ref, fwd_ref, inv_ref, o_ref, *, L, col_chains):
    n_seq = o_ref.shape[0] // L
    tn = o_ref.shape[1] // col_chains
    first = lax.broadcasted_iota(jnp.int32, (L, tn), 0) == 0
    for s in range(n_seq):
        for cc in range(col_chains):
            rows, cols = slice(s * L, (s + 1) * L), slice(cc * tn, (cc + 1) * tn)
            u = u_ref[rows, cols]
            spec = _dot(fwd_ref[...], u)
            u_top, u_bot = spec[0:L, :], spec[L:2 * L, :]
            g_top, g_bot = g_ref[0:L, cols], g_ref[L:2 * L, cols]
            y_top = u_top * g_top - jnp.where(first, 0.0, u_bot * g_bot)
            y_bot = jnp.where(first, u_bot * g_bot, u_top * g_bot + u_bot * g_top)
            y_spec = jnp.concatenate([y_top, y_bot], axis=0).astype(BF16)
            y = _dot(inv_ref[...], y_spec)
            gated = x0_ref[rows, cols].astype(F32) * (y + u.astype(F32) * bias_ref[:, cols])
            o_ref[rows, cols] = gated.astype(BF16)


def _hyena_conv(u, x0, spectrum, bias, jl, fwd, inv, L, row_block0, n_steps, seq_per_step, tn, col_chains):
    nj = D_MODEL // tn
    rows = seq_per_step * L
    act = pl.BlockSpec((rows, tn), lambda j, b: (row_block0 + b, j))
    return pl.pallas_call(
        functools.partial(_hyena_conv_kernel, L=L, col_chains=col_chains),
        grid=(nj, n_steps),
        in_specs=[
            act, act,
            pl.BlockSpec((2 * L, tn), lambda j, b: (0, j)),
            pl.BlockSpec((None, 1, tn), lambda j, b: (jl, 0, j)),
            pl.BlockSpec((2 * L, L), lambda j, b: (0, 0)),
            pl.BlockSpec((L, 2 * L), lambda j, b: (0, 0)),
        ],
        out_specs=pl.BlockSpec((rows, tn), lambda j, b: (b, j)),
        out_shape=jax.ShapeDtypeStruct((n_steps * rows, D_MODEL), BF16),
        compiler_params=_params(2),
        name="hyena_conv_%d" % L,
    )(u, x0, spectrum, bias, fwd, inv)


def _rope_tables():
    rows = DEC_SEQ // GRID_W
    row = np.repeat(np.arange(rows), GRID_W).astype(np.float32)
    col = np.tile(np.arange(GRID_W), rows).astype(np.float32)
    half = HEAD_DIM // 2
    freqs = (np.float32(ROPE_THETA) ** (-np.arange(0, half, 2, dtype=np.float32) / np.float32(half))).astype(np.float32)
    ang = np.concatenate([row[:, None] * freqs[None, :], col[:, None] * freqs[None, :]], axis=-1)
    cos = np.repeat(np.cos(ang), 2, axis=-1)
    sin = np.repeat(np.sin(ang), 2, axis=-1)
    sign = np.where(np.arange(HEAD_DIM) % 2 == 0, -1.0, 1.0)[None, :]
    pair = lambda a: np.tile(a, (1, 2)).astype(np.float32)
    return pair(cos), pair(sin * sign)


def _qkv_kernel(*refs, n_prev, n_prompt_tiles):
    x_ref, g_ref, sh_ref, sc_ref, w_ref, qg_ref, kg_ref, cos_ref, sin_ref = refs[:9]
    prev_refs, refs = refs[9:9 + 2 * (n_prev > 0)], refs[9 + 2 * (n_prev > 0):]
    q_ref, k_ref, v_ref, kf_ref, vf_ref, w_s = refs
    i = pl.program_id(0)
    tm = q_ref.shape[0]

    @pl.when(i == 0)
    def _():
        w_s[...] = w_ref[...].astype(BF16)

    k_col = N_HEADS * HEAD_DIM
    v_col = k_col + KV_DIM
    q_gain = qg_ref[...] * HEAD_DIM ** -0.5

    def project(rows):
        h = _modulate(x_ref[rows, :], g_ref[...], sh_ref[0], sc_ref[0]).astype(BF16)
        qkv = _dot(h, w_s[...])
        v = qkv[:, v_col:]
        v_ref[rows, :] = v.astype(BF16)

        def pair_norm(col, gain):
            return jnp.concatenate([_rmsnorm(qkv[:, c:c + HEAD_DIM], gain) for c in (col, col + HEAD_DIM)], axis=1)

        return pair_norm, v

    @pl.when(i < n_prompt_tiles)
    def _():
        if n_prev:
            kf_ref[:, 0:n_prev] = prev_refs[0][...]
            vf_ref[:, 0:n_prev] = prev_refs[1][...]
        for r in range(tm // SEQ):
            rows = slice(r * SEQ, (r + 1) * SEQ)
            pair_norm, v = project(rows)
            for col in range(0, k_col, HEAD_PAIR):
                q_ref[rows, col:col + HEAD_PAIR] = pair_norm(col, q_gain).astype(BF16)
            k = pair_norm(k_col, kg_ref[...])
            k_ref[rows, :] = k.astype(BF16)
            for kv in range(N_KV_HEADS):
                cols = slice(kv * HEAD_DIM, (kv + 1) * HEAD_DIM)
                head_rows = pl.ds(kv, SEQ, stride=N_KV_HEADS)
                kf_ref[r, n_prev, head_rows, :] = k[:, cols]
                vf_ref[r, n_prev, head_rows, :] = v[:, cols]

    @pl.when(i >= n_prompt_tiles)
    def _():
        r_id = lax.broadcasted_iota(jnp.int32, (HEAD_PAIR, HEAD_PAIR), 0)
        col_id = lax.broadcasted_iota(jnp.int32, (HEAD_PAIR, HEAD_PAIR), 1)
        swap = jnp.where((r_id ^ 1) == col_id, 1.0, 0.0).astype(BF16)
        for r in range(tm // SEQ):
            rows = slice(r * SEQ, (r + 1) * SEQ)
            pair_norm, _ = project(rows)

            def rotate(x):
                return x * cos_ref[rows, :] + _dot(x.astype(BF16), swap) * sin_ref[rows, :]

            for col in range(0, k_col, HEAD_PAIR):
                q_ref[rows, col:col + HEAD_PAIR] = rotate(pair_norm(col, q_gain)).astype(BF16)
            k_ref[rows, :] = rotate(pair_norm(k_col, kg_ref[...])).astype(BF16)


def _qkv(y, norm_mix, mods, layer, w_qkv, q_gain, k_gain, jl, prev_kv):
    tm = 512
    npt = N_PROMPT // tm
    tiles_per_req = DEC_SEQ // tm
    cos, sin = _rope_tables()
    rope_spec = pl.BlockSpec((tm, HEAD_PAIR), lambda i: (jnp.maximum(i - npt, 0) % tiles_per_req, 0))
    row = lambda n: pl.BlockSpec((tm, n), lambda i: (i, 0))
    cache_rows = SEQ * N_KV_HEADS
    cache = lambda n: pl.BlockSpec((tm // SEQ, n, cache_rows, HEAD_DIM), lambda i: (jnp.minimum(i, npt - 1), 0, 0, 0))
    gain = pl.BlockSpec((None, 1, HEAD_DIM), lambda i: (jl, 0, 0))
    return pl.pallas_call(
        functools.partial(_qkv_kernel, n_prev=jl, n_prompt_tiles=npt),
        grid=(N_TOK // tm,),
        in_specs=[
            row(D_MODEL),
            pl.BlockSpec((None, 1, D_MODEL), lambda i: (layer, 0, 0)),
            _mod_spec(layer, 0, tm),
            _mod_spec(layer, 1, tm),
            pl.BlockSpec((None, D_MODEL, QKV_DIM), lambda i: (jl, 0, 0)),
            gain, gain,
            rope_spec, rope_spec,
        ] + [cache(jl)] * len(prev_kv),
        out_specs=[row(D_MODEL), row(KV_DIM), row(KV_DIM), cache(jl + 1), cache(jl + 1)],
        out_shape=[
            jax.ShapeDtypeStruct((N_TOK, D_MODEL), BF16),
            jax.ShapeDtypeStruct((N_TOK, KV_DIM), BF16),
            jax.ShapeDtypeStruct((N_TOK, KV_DIM), BF16),
            jax.ShapeDtypeStruct((BATCH, jl + 1, cache_rows, HEAD_DIM), F32),
            jax.ShapeDtypeStruct((BATCH, jl + 1, cache_rows, HEAD_DIM), F32),
        ],
        scratch_shapes=[pltpu.VMEM((D_MODEL, QKV_DIM), BF16)],
        compiler_params=_params(1),
        name="qkv_proj",
    )(y, norm_mix, mods, mods, w_qkv, q_gain, k_gain, jnp.asarray(cos), jnp.asarray(sin), *prev_kv)


def _attention_kernel(*refs, with_cache, seq):
    if with_cache:
        q_ref, k_ref, v_ref, ck_ref, cv_ref, o_ref = refs
    else:
        q_ref, k_ref, v_ref, o_ref = refs
    n_req = k_ref.shape[0] // seq
    tq = q_ref.shape[0] // n_req
    for r in range(n_req):
        q_rows, k_rows = slice(r * tq, (r + 1) * tq), slice(r * seq, (r + 1) * seq)
        for kv in range(k_ref.shape[1] // HEAD_DIM):
            kv_cols = slice(kv * HEAD_DIM, (kv + 1) * HEAD_DIM)
            k = k_ref[k_rows, kv_cols]
            v = v_ref[k_rows, kv_cols]
            if with_cache:
                head_rows = pl.ds(kv, PAST_LEN, stride=N_KV_HEADS)
                k = jnp.concatenate([k, ck_ref[0, 0, head_rows, :].astype(BF16)], axis=0)
                v = jnp.concatenate([v, cv_ref[0, 0, head_rows, :].astype(BF16)], axis=0)
            v_ones = jnp.concatenate([v, jnp.ones_like(v)], axis=1)
            for g in range(GROUP):
                cols = slice((kv * GROUP + g) * HEAD_DIM, (kv * GROUP + g + 1) * HEAD_DIM)
                s = lax.dot_general(q_ref[q_rows, cols], k, (((1,), (1,)), ((), ())), preferred_element_type=F32)
                p = jnp.exp(s - jnp.max(s, axis=-1, keepdims=True))
                o = _dot(p.astype(BF16), v_ones)
                o_ref[q_rows, cols] = (o[:, :HEAD_DIM] / o[:, HEAD_DIM:]).astype(BF16)


def _attention(q, k, v, row0, n_req, seq, tq, kv_per_step, req_per_step=1, cache_k=None, cache_v=None,
               cache_layer=0):
    nq = seq // tq
    n_req //= req_per_step
    tq, seq_rows = tq * req_per_step, seq * req_per_step
    qb0, kb0 = row0 // tq, row0 // seq_rows
    qw, kw = kv_per_step * GROUP * HEAD_DIM, kv_per_step * HEAD_DIM
    in_specs = [
        pl.BlockSpec((tq, qw), lambda b, h, t: (qb0 + b * nq + t, h)),
        pl.BlockSpec((seq_rows, kw), lambda b, h, t: (kb0 + b, h)),
        pl.BlockSpec((seq_rows, kw), lambda b, h, t: (kb0 + b, h)),
    ]
    args = [q, k, v]
    with_cache = cache_k is not None
    if with_cache:
        assert kv_per_step == N_KV_HEADS
        cspec = pl.BlockSpec((1, 1, PAST_LEN * N_KV_HEADS, HEAD_DIM), lambda b, h, t: (b, cache_layer, 0, 0))
        in_specs += [cspec, cspec]
        args += [cache_k, cache_v]
    return pl.pallas_call(
        functools.partial(_attention_kernel, with_cache=with_cache, seq=seq),
        grid=(n_req, N_KV_HEADS // kv_per_step, nq),
        in_specs=in_specs,
        out_specs=pl.BlockSpec((tq, qw), lambda b, h, t: (b * nq + t, h)),
        out_shape=jax.ShapeDtypeStruct((n_req * seq_rows, D_MODEL), BF16),
        compiler_params=_params(3),
        name="attention_%d" % seq,
    )(*args)


def _ffn_kernel(*refs, n_in, n_out, n_prompt_tiles, final):
    y_refs, refs = refs[:n_in], refs[n_in:]
    (ap_ref, as_ref, wo_ref, gm_ref, g_ref, sh_ref, sc_ref, gate_ref, wg_ref, wu_ref, wd_ref, fin_ref) = refs[:12]
    o_refs, refs = refs[12:12 + n_out], refs[12 + n_out:]
    wo_s, wg_s, wu_s, wd_s, ymid_s, f_s, acc_s = refs
    s = pl.program_id(0)
    nf = N_FF_CHUNKS
    tile = jnp.maximum(s - (nf - 1), 0)
    is_prompt = tile < n_prompt_tiles

    def prologue():
        a = jnp.where(is_prompt, ap_ref[...], as_ref[...])
        y = _read_stream(y_refs, is_prompt)
        y_mid = y + gm_ref[0] * _dot(a, wo_s[...])
        ymid_s[...] = y_mid
        f_s[...] = _modulate(y_mid, g_ref[...], sh_ref[0], sc_ref[0]).astype(BF16)

    def chunk(c):
        f = f_s[...]
        hidden = _silu(_dot(f, wg_s[c])) * _dot(f, wu_s[c])
        return _dot(hidden.astype(BF16), wd_s[c])

    def epilogue(acc):
        out = ymid_s[...] + gate_ref[0] * acc
        if final:
            out = _rmsnorm(out, fin_ref[...])
        if n_out == 1:
            o_refs[0][...] = out
        else:
            @pl.when(is_prompt)
            def _():
                o_refs[0][...] = out

            @pl.when(jnp.logical_not(is_prompt))
            def _():
                o_refs[1][...] = out

    @pl.when(s == 0)
    def _():
        wo_s[...] = wo_ref[...].astype(BF16)
        prologue()

    @pl.when(s < nf)
    def _():
        wg_s[s] = wg_ref[...].astype(BF16)
        wu_s[s] = wu_ref[...].astype(BF16)
        wd_s[s] = wd_ref[...].astype(BF16)
        part = chunk(s)

        @pl.when(s == 0)
        def _():
            acc_s[...] = part

        @pl.when(s > 0)
        def _():
            acc_s[...] += part

        @pl.when(s == nf - 1)
        def _():
            epilogue(acc_s[...])

    @pl.when(s >= nf)
    def _():
        prologue()
        acc = chunk(0)
        for c in range(1, nf):
            acc = acc + chunk(c)
        epilogue(acc)


def _ffn(stream, ap, as_, w_out, jl, norm_ffn, mods, layer, wg, wu, wd, final_norm, final):
    tm = 512
    nf = N_FF_CHUNKS
    npt = N_PROMPT // tm
    n_steps = nf + N_TOK // tm - 1
    n_out = 2 if final else 1
    tile_of = lambda s: jnp.maximum(s - (nf - 1), 0)
    chunk_of = lambda s: jnp.minimum(s, nf - 1)
    out_shape = _pair_shapes(D_MODEL, F32) if final else [jax.ShapeDtypeStruct((N_TOK, D_MODEL), F32)]
    return pl.pallas_call(
        functools.partial(_ffn_kernel, n_in=len(stream), n_out=n_out, n_prompt_tiles=npt, final=final),
        grid=(n_steps,),
        in_specs=_stream_specs(len(stream), tm, tile_of) + _pair_specs(tm, D_MODEL, tile_of) + [
            pl.BlockSpec((None, D_MODEL, D_MODEL), lambda s: (jl, 0, 0), pipeline_mode=pl.Buffered(1)),
            _mod_spec(layer, 2, tm, tile_of),
            pl.BlockSpec((None, 1, D_MODEL), lambda s: (layer, 0, 0)),
            _mod_spec(layer, 3, tm, tile_of),
            _mod_spec(layer, 4, tm, tile_of),
            _mod_spec(layer, 5, tm, tile_of),
            pl.BlockSpec((None, D_MODEL, FF_CHUNK), lambda s: (layer, 0, chunk_of(s))),
            pl.BlockSpec((None, D_MODEL, FF_CHUNK), lambda s: (layer, 0, chunk_of(s))),
            pl.BlockSpec((None, FF_CHUNK, D_MODEL), lambda s: (layer, chunk_of(s), 0)),
            pl.BlockSpec((1, D_MODEL), lambda s: (0, 0)),
        ],
        out_specs=_stream_specs(n_out, tm, tile_of),
        out_shape=out_shape,
        scratch_shapes=[
            pltpu.VMEM((D_MODEL, D_MODEL), BF16),
            pltpu.VMEM((nf, D_MODEL, FF_CHUNK), BF16),
            pltpu.VMEM((nf, D_MODEL, FF_CHUNK), BF16),
            pltpu.VMEM((nf, FF_CHUNK, D_MODEL), BF16),
            pltpu.VMEM((tm, D_MODEL), F32),
            pltpu.VMEM((tm, D_MODEL), BF16),
            pltpu.VMEM((tm, D_MODEL), F32),
        ],
        compiler_params=_params(1),
        name="ffn",
    )(*stream, ap, as_, w_out, mods, norm_ffn, mods, mods, mods, wg, wu, wd, final_norm.reshape(1, D_MODEL))


def kernel(x_prompt, x_sample, cache_k, cache_v, c, c_ctx, mod_w, mod_b, norm_mix, norm_ffn, hy_w_in, hy_conv_w, hy_conv_b, hy_f_w1, hy_f_b1, hy_f_freq, hy_f_w2, hy_f_b2, hy_f_w3, hy_bias, hy_w_out, at_w_qkv, at_q_norm, at_k_norm, at_w_out, ffn_w_gate, ffn_w_up, ffn_w_down, final_norm):
    stream = (x_prompt.reshape(N_PROMPT, D_MODEL), x_sample.reshape(N_SAMPLE, D_MODEL))
    cond = jnp.concatenate([c_ctx[None, :], c, jnp.zeros((MOD_ROWS - 1 - DEC_BATCH, D_MODEL), F32)], axis=0)
    mods = _adaln(cond, mod_w, mod_b)

    tables = {}
    for L in (SEQ, DEC_SEQ):
        fwd, inv = _dft_tables(L)
        tables[L] = (_cast_bf16(jnp.asarray(fwd), 2 * L // 4), _cast_bf16(jnp.asarray(inv), L // 4))

    norm_mix = norm_mix.reshape(DEPTH, 1, D_MODEL)
    norm_ffn = norm_ffn.reshape(DEPTH, 1, D_MODEL)
    f_w1 = jnp.pad(hy_f_w1, ((0, 0), (0, FEAT_PAD - FILTER_EMB), (0, 0)))
    f_b1 = hy_f_b1.reshape(-1, 1, FILTER_HIDDEN)
    f_b2 = hy_f_b2.reshape(-1, 1, FILTER_HIDDEN)
    bias = hy_bias.reshape(-1, 1, D_MODEL)
    q_gain = at_q_norm.reshape(-1, 1, HEAD_DIM)
    k_gain = at_k_norm.reshape(-1, 1, HEAD_DIM)
    cache_k = cache_k.reshape(DEC_BATCH, -1, PAST_LEN * N_KV_HEADS, HEAD_DIM)
    cache_v = cache_v.reshape(DEC_BATCH, -1, PAST_LEN * N_KV_HEADS, HEAD_DIM)

    new_kv = ()
    for layer in range(DEPTH):
        jl = layer // N_MIXERS
        if layer % N_MIXERS == 0:
            x0, u = _in_proj(stream, norm_mix, mods, layer, hy_w_in, hy_conv_w, hy_conv_b, jl)
            mixed = []
            for L, row_block0, n_steps, seq_per_step, tn, chains in (
                    (SEQ, 0, BATCH // 4, 4, D_MODEL, 1),
                    (DEC_SEQ, N_PROMPT // DEC_SEQ, DEC_BATCH, 1, 512, 1)):
                fwd_bf16, inv_bf16 = tables[L]
                spectrum = _hyena_spectrum(L, fwd_bf16, jl, f_w1, f_b1, hy_f_freq, hy_f_w2, f_b2, hy_f_w3)
                mixed.append(_hyena_conv(u, x0, spectrum, bias, jl, fwd_bf16, inv_bf16, L, row_block0, n_steps,
                                         seq_per_step, tn, chains))
            w_out = hy_w_out
        else:
            q, k, v, *new_kv = _qkv(stream[0], norm_mix, mods, layer, at_w_qkv, q_gain, k_gain, jl, new_kv)
            mixed = [_attention(q, k, v, 0, BATCH, SEQ, SEQ, N_KV_HEADS, req_per_step=4),
                     _attention(q, k, v, N_PROMPT, DEC_BATCH, DEC_SEQ, 512, N_KV_HEADS, cache_k=cache_k, cache_v=cache_v,
                                cache_layer=jl)]
            w_out = at_w_out
        stream = tuple(_ffn(stream, mixed[0], mixed[1], w_out, jl, norm_ffn, mods, layer,
                            ffn_w_gate, ffn_w_up, ffn_w_down, final_norm, final=(layer == DEPTH - 1)))

    y_prompt, y_sample = stream
    new_k, new_v = (a.reshape(BATCH, -1, SEQ, N_KV_HEADS, HEAD_DIM) for a in new_kv)
    return (y_prompt.reshape(BATCH, SEQ, D_MODEL), y_sample.reshape(DEC_BATCH, DEC_SEQ, D_MODEL), new_k, new_v)
```

```python
import functools
import math

import numpy as np
import jax
import jax.numpy as jnp
from jax import lax
from jax.experimental import pallas as pl
from jax.experimental.pallas import tpu as pltpu

D_MODEL = 1024
BATCH = 16
SEQ = 256
DEPTH = 4
DEC_BATCH = 4
DEC_SEQ = 1024
PAST_LEN = 512
GRID_W = 64
N_MIXERS = 2
HEAD_DIM = 128
N_HEADS = D_MODEL // HEAD_DIM
N_KV_HEADS = 2
GROUP = N_HEADS // N_KV_HEADS
KV_DIM = N_KV_HEADS * HEAD_DIM
HEAD_PAIR = 2 * HEAD_DIM
QKV_DIM = (N_HEADS + 2 * N_KV_HEADS) * HEAD_DIM
ROPE_THETA = 10000.0
D_FF = ((8 * D_MODEL + 3 * 256 - 1) // (3 * 256)) * 256
N_BANDS = 16
FILTER_EMB = 1 + 2 * N_BANDS
FILTER_HIDDEN = 64
MIN_DECAY = math.log(1e-2) / 1.5
MAX_DECAY = math.log(1e-2) / 0.3
EPS = 1e-6

N_PROMPT = BATCH * SEQ
N_SAMPLE = DEC_BATCH * DEC_SEQ
N_TOK = N_PROMPT + N_SAMPLE
N_MOD = 6
MOD_ROWS = 8
FEAT_PAD = 128
CONV_PAD = 8
FF_CHUNK = 256
N_FF_CHUNKS = D_FF // FF_CHUNK
V7X_VMEM_LIMIT = 56 * 1024 * 1024

F32 = jnp.float32
BF16 = jnp.bfloat16


def _params(n_axes, vmem=V7X_VMEM_LIMIT):
    return pltpu.CompilerParams(dimension_semantics=("arbitrary",) * n_axes, vmem_limit_bytes=vmem)


def _mod_row(tile, tm):
    n_prompt_tiles = N_PROMPT // tm
    tiles_per_req = DEC_SEQ // tm
    return jnp.where(tile < n_prompt_tiles, 0, 1 + jnp.maximum(tile - n_prompt_tiles, 0) // tiles_per_req)


def _mod_spec(layer, which, tm, tile_of=lambda i, *_: i):
    def index(*ids):
        return ((layer * N_MOD + which) * MOD_ROWS + _mod_row(tile_of(*ids), tm), 0, 0)
    return pl.BlockSpec((1, 1, D_MODEL), index)


def _pair_specs(tm, width, tile_of=lambda i, *_: i):
    npt = N_PROMPT // tm
    return [pl.BlockSpec((tm, width), lambda *ids: (jnp.minimum(tile_of(*ids), npt - 1), 0)),
            pl.BlockSpec((tm, width), lambda *ids: (jnp.maximum(tile_of(*ids) - npt, 0), 0))]


def _pair_shapes(width, dtype):
    return [jax.ShapeDtypeStruct((N_PROMPT, width), dtype), jax.ShapeDtypeStruct((N_SAMPLE, width), dtype)]


def _stream_specs(n_arrays, tm, tile_of=lambda i, *_: i):
    if n_arrays == 2:
        return _pair_specs(tm, D_MODEL, tile_of)
    return [pl.BlockSpec((tm, D_MODEL), lambda *ids: (tile_of(*ids), 0))]


def _read_stream(refs, is_prompt):
    if len(refs) == 2:
        return jnp.where(is_prompt, refs[0][...], refs[1][...])
    return refs[0][...]


def _split_bf16(a):
    hi = a.astype(BF16)
    lo = (a - hi.astype(F32)).astype(BF16)
    return hi, lo


def _dot(a, b):
    return jnp.dot(a, b, preferred_element_type=F32)


def _dot3(a, b):
    a_hi, a_lo = _split_bf16(a)
    b_hi, b_lo = _split_bf16(b)
    return _dot(a_hi, b_hi) + (_dot(a_lo, b_hi) + _dot(a_hi, b_lo))


def _silu(a):
    return a / (1.0 + jnp.exp(-a))


def _rmsnorm(x, g):
    return x * lax.rsqrt(jnp.mean(x * x, axis=-1, keepdims=True) + EPS) * g


def _modulate(x, g, shift, scale):
    return _rmsnorm(x, g) * (1.0 + scale) + shift


def _adaln_kernel(c_ref, w_ref, b_ref, o_ref):
    s = _silu(c_ref[...])
    m = _dot3(s, w_ref[0]) + b_ref[0]
    for v in range(m.shape[1] // D_MODEL):
        o_ref[v * MOD_ROWS:(v + 1) * MOD_ROWS, 0, :] = m[:, v * D_MODEL:(v + 1) * D_MODEL]


def _adaln(cond, mod_w, mod_b):
    per_step = 2
    tn = per_step * D_MODEL
    return pl.pallas_call(
        _adaln_kernel,
        grid=(DEPTH, N_MOD // per_step),
        in_specs=[
            pl.BlockSpec((MOD_ROWS, D_MODEL), lambda l, j: (0, 0)),
            pl.BlockSpec((1, D_MODEL, tn), lambda l, j: (l, 0, j)),
            pl.BlockSpec((1, 1, tn), lambda l, j: (l, 0, j)),
        ],
        out_specs=pl.BlockSpec((per_step * MOD_ROWS, 1, D_MODEL), lambda l, j: (l * (N_MOD // per_step) + j, 0, 0)),
        out_shape=jax.ShapeDtypeStruct((DEPTH * N_MOD * MOD_ROWS, 1, D_MODEL), F32),
        compiler_params=_params(2),
        name="adaln",
    )(cond, mod_w, mod_b.reshape(DEPTH, 1, N_MOD * D_MODEL))


def _cast_kernel(x_ref, o_ref):
    o_ref[...] = x_ref[...].astype(BF16)


def _cast_bf16(x, rows):
    m, n = x.shape
    return pl.pallas_call(
        _cast_kernel,
        grid=(m // rows,),
        in_specs=[pl.BlockSpec((rows, n), lambda i: (i, 0))],
        out_specs=pl.BlockSpec((rows, n), lambda i: (i, 0)),
        out_shape=jax.ShapeDtypeStruct((m, n), BF16),
        compiler_params=_params(1),
        name="cast_table",
    )(x)


def _in_proj_kernel(*refs, n_stream, n_prompt_tiles):
    x_refs, refs = refs[:n_stream], refs[n_stream:]
    (g_ref, sh_ref, sc_ref, w0_ref, w1_ref, w2_ref, cw0_ref, cw1_ref, cw2_ref, cb0_ref, cb1_ref, cb2_ref,
     x0_ref, u_ref, w_s) = refs
    i = pl.program_id(0)
    tm, tc = x0_ref.shape

    @pl.when(i == 0)
    def _():
        w_s[0] = w0_ref[...].astype(BF16)
        w_s[1] = w1_ref[...].astype(BF16)
        w_s[2] = w2_ref[...].astype(BF16)

    n_chain = tm // SEQ
    pad = jnp.zeros((CONV_PAD, tc), F32)

    def tile(seq_len):
        hs = []
        for b in range(n_chain):
            x = _read_stream([r.at[b * SEQ:(b + 1) * SEQ, :] for r in x_refs], i < n_prompt_tiles)
            hs.append(_modulate(x, g_ref[...], sh_ref[0], sc_ref[0]).astype(BF16))

        def short_conv(part, cw_ref, cb_ref):
            zs = [_dot(h, w_s[part]) for h in hs]
            out = []
            for b, z in enumerate(zs):
                before = zs[b - 1][SEQ - CONV_PAD:, :] if (b * SEQ) % seq_len else pad
                after = zs[b + 1][:CONV_PAD, :] if ((b + 1) * SEQ) % seq_len else pad
                ext = jnp.concatenate([before, z, after], axis=0)
                rows = ext.shape[0]
                z_prev = pltpu.roll(ext, 1, 0)[CONV_PAD:CONV_PAD + SEQ, :]
                z_next = pltpu.roll(ext, rows - 1, 0)[CONV_PAD:CONV_PAD + SEQ, :]
                out.append(z_prev * cw_ref[0:1, :] + z * cw_ref[1:2, :] + z_next * cw_ref[2:3, :] + cb_ref[...])
            return out

        for b, x0 in enumerate(short_conv(0, cw0_ref, cb0_ref)):
            x0_ref[b * SEQ:(b + 1) * SEQ, :] = x0.astype(BF16)
        for b, (x1, v) in enumerate(zip(short_conv(1, cw1_ref, cb1_ref), short_conv(2, cw2_ref, cb2_ref))):
            u_ref[b * SEQ:(b + 1) * SEQ, :] = (x1 * v).astype(BF16)

    @pl.when(i < n_prompt_tiles)
    def _():
        tile(SEQ)

    @pl.when(i >= n_prompt_tiles)
    def _():
        tile(DEC_SEQ)


def _in_proj(stream, norm_mix, mods, layer, w_in, conv_w, conv_b, jl):
    tm = 1024
    npt = N_PROMPT // tm

    def wspec(part):
        return pl.BlockSpec((None, D_MODEL, D_MODEL), lambda i: (jl, 0, part), pipeline_mode=pl.Buffered(1))

    def cwspec(part):
        return pl.BlockSpec((None, 3, D_MODEL), lambda i: (jl, 0, part))

    def cbspec(part):
        return pl.BlockSpec((None, 1, D_MODEL), lambda i: (jl, 0, part))

    out = pl.BlockSpec((tm, D_MODEL), lambda i: (i, 0))
    cb = conv_b.reshape(-1, 1, 3 * D_MODEL)
    return pl.pallas_call(
        functools.partial(_in_proj_kernel, n_stream=len(stream), n_prompt_tiles=npt),
        grid=(N_TOK // tm,),
        in_specs=_stream_specs(len(stream), tm) + [
            pl.BlockSpec((None, 1, D_MODEL), lambda i: (layer, 0, 0)),
            _mod_spec(layer, 0, tm),
            _mod_spec(layer, 1, tm),
            wspec(0), wspec(1), wspec(2),
            cwspec(0), cwspec(1), cwspec(2),
            cbspec(0), cbspec(1), cbspec(2),
        ],
        out_specs=[out, out],
        out_shape=[jax.ShapeDtypeStruct((N_TOK, D_MODEL), BF16)] * 2,
        scratch_shapes=[pltpu.VMEM((3, D_MODEL, D_MODEL), BF16)],
        compiler_params=_params(1),
        name="hyena_in_proj",
    )(*stream, norm_mix, mods, mods, w_in, w_in, w_in, conv_w, conv_w, conv_w, cb, cb, cb)


def _dft_tables(L):
    n = 2 * L
    k = np.arange(L, dtype=np.float64)[:, None]
    t = np.arange(L, dtype=np.float64)[None, :]
    ang = 2.0 * np.pi * k * t / n
    top = np.cos(ang)
    bot = -np.sin(ang)
    bot[0, :] = np.where(np.arange(L) % 2 == 0, 1.0, -1.0)
    fwd = np.concatenate([top, bot], axis=0)
    wk = np.full((L,), 2.0)
    wk[0] = 1.0
    inv_top = (np.cos(ang) * wk[:, None]).T / n
    inv_bot = (-2.0 * np.sin(ang)).T / n
    inv_bot[:, 0] = np.where(np.arange(L) % 2 == 0, 1.0, -1.0) / n
    inv = np.concatenate([inv_top, inv_bot], axis=1)
    return fwd.astype(np.float32), inv.astype(np.float32)


def _filter_feats(L):
    t = np.arange(L, dtype=np.float32) / np.float32(L)
    bands = np.arange(1, N_BANDS + 1, dtype=np.float32)
    ang = (np.float32(2.0 * math.pi) * t[:, None]) * bands[None, :]
    feats = np.concatenate([t[:, None], np.cos(ang), np.sin(ang)], axis=-1).astype(np.float32)
    return np.pad(feats, ((0, 0), (0, FEAT_PAD - FILTER_EMB)))


def _filter_kernel(feats_ref, w1_ref, b1_ref, fr_ref, w2_ref, b2_ref, w3f_ref, w3b_ref, dl_ref, fwd_ref,
                   g_ref, h_s, *, L):
    @pl.when(pl.program_id(0) == 0)
    def _():
        h1 = jnp.sin(fr_ref[0:1, :] * (_dot3(feats_ref[...], w1_ref[...]) + b1_ref[...]))
        h_s[...] = jnp.sin(fr_ref[1:2, :] * (_dot3(h1, w2_ref[...]) + b2_ref[...]))

    tn = g_ref.shape[1]
    h = h_s[...]
    rows = lax.broadcasted_iota(jnp.int32, (L, tn), 0)
    t = rows.astype(F32) / L
    window = jnp.exp(-t * dl_ref[...])
    hf = _dot3(h, w3f_ref[...]) * window
    hb = jnp.where(rows > 0, _dot3(h, w3b_ref[...]) * window, 0.0)
    norm = jnp.sqrt(jnp.sum(hf * hf + hb * hb, axis=0, keepdims=True) + EPS)
    hf = hf / norm
    hb = hb / norm

    g_top = _dot(fwd_ref[0:L, :], (hf + hb).astype(BF16))
    g_bot = _dot(fwd_ref[L:2 * L, :], (hf - hb).astype(BF16))
    sign = jnp.where(rows % 2 == 0, 1.0, -1.0)
    nyquist_b = jnp.sum(sign * hb, axis=0, keepdims=True)
    g_bot = g_bot + jnp.where(rows == 0, 2.0 * nyquist_b, 0.0)
    g_ref[0:L, :] = g_top
    g_ref[L:2 * L, :] = g_bot


def _hyena_spectrum(L, fwd, jl, f_w1, f_b1, f_freq, f_w2, f_b2, f_w3):
    tn = 512
    nj = D_MODEL // tn
    feats = jnp.asarray(_filter_feats(L))
    deltas = jnp.asarray(np.abs(np.linspace(MIN_DECAY, MAX_DECAY, D_MODEL, dtype=np.float32)).reshape(1, D_MODEL))
    const = lambda j: (0, 0)
    layer = lambda j: (jl, 0, 0)
    return pl.pallas_call(
        functools.partial(_filter_kernel, L=L),
        grid=(nj,),
        in_specs=[
            pl.BlockSpec((L, FEAT_PAD), const),
            pl.BlockSpec((None, FEAT_PAD, FILTER_HIDDEN), layer),
            pl.BlockSpec((None, 1, FILTER_HIDDEN), layer),
            pl.BlockSpec((None, 2, FILTER_HIDDEN), layer),
            pl.BlockSpec((None, FILTER_HIDDEN, FILTER_HIDDEN), layer),
            pl.BlockSpec((None, 1, FILTER_HIDDEN), layer),
            pl.BlockSpec((None, FILTER_HIDDEN, tn), lambda j: (jl, 0, j)),
            pl.BlockSpec((None, FILTER_HIDDEN, tn), lambda j: (jl, 0, nj + j)),
            pl.BlockSpec((1, tn), lambda j: (0, j)),
            pl.BlockSpec((2 * L, L), const),
        ],
        out_specs=pl.BlockSpec((2 * L, tn), lambda j: (0, j)),
        out_shape=jax.ShapeDtypeStruct((2 * L, D_MODEL), F32),
        scratch_shapes=[pltpu.VMEM((L, FILTER_HIDDEN), F32)],
        compiler_params=_params(1),
        name="hyena_spectrum_%d" % L,
    )(feats, f_w1, f_b1, f_freq, f_w2, f_b2, f_w3, f_w3, deltas, fwd)


def _hyena_conv_kernel(u_ref, x0_ref, g_ref, bias_ref, fwd_ref, inv_ref, o_ref, *, L, col_chains):
    n_seq = o_ref.shape[0] // L
    tn = o_ref.shape[1] // col_chains
    first = lax.broadcasted_iota(jnp.int32, (L, tn), 0) == 0
    for s in range(n_seq):
        for cc in range(col_chains):
            rows, cols = slice(s * L, (s + 1) * L), slice(cc * tn, (cc + 1) * tn)
            u = u_ref[rows, cols]
            spec = _dot(fwd_ref[...], u)
            u_top, u_bot = spec[0:L, :], spec[L:2 * L, :]
            g_top, g_bot = g_ref[0:L, cols], g_ref[L:2 * L, cols]
            y_top = u_top * g_top - jnp.where(first, 0.0, u_bot * g_bot)
            y_bot = jnp.where(first, u_bot * g_bot, u_top * g_bot + u_bot * g_top)
            y_spec = jnp.concatenate([y_top, y_bot], axis=0).astype(BF16)
            y = _dot(inv_ref[...], y_spec)
            gated = x0_ref[rows, cols].astype(F32) * (y + u.astype(F32) * bias_ref[:, cols])
            o_ref[rows, cols] = gated.astype(BF16)


def _hyena_conv(u, x0, spectrum, bias, jl, fwd, inv, L, row_block0, n_steps, seq_per_step, tn, col_chains):
    nj = D_MODEL // tn
    rows = seq_per_step * L
    act = pl.BlockSpec((rows, tn), lambda j, b: (row_block0 + b, j))
    return pl.pallas_call(
        functools.partial(_hyena_conv_kernel, L=L, col_chains=col_chains),
        grid=(nj, n_steps),
        in_specs=[
            act, act,
            pl.BlockSpec((2 * L, tn), lambda j, b: (0, j)),
            pl.BlockSpec((None, 1, tn), lambda j, b: (jl, 0, j)),
            pl.BlockSpec((2 * L, L), lambda j, b: (0, 0)),
            pl.BlockSpec((L, 2 * L), lambda j, b: (0, 0)),
        ],
        out_specs=pl.BlockSpec((rows, tn), lambda j, b: (b, j)),
        out_shape=jax.ShapeDtypeStruct((n_steps * rows, D_MODEL), BF16),
        compiler_params=_params(2),
        name="hyena_conv_%d" % L,
    )(u, x0, spectrum, bias, fwd, inv)


def _rope_tables():
    rows = DEC_SEQ // GRID_W
    row = np.repeat(np.arange(rows), GRID_W).astype(np.float32)
    col = np.tile(np.arange(GRID_W), rows).astype(np.float32)
    half = HEAD_DIM // 2
    freqs = (np.float32(ROPE_THETA) ** (-np.arange(0, half, 2, dtype=np.float32) / np.float32(half))).astype(np.float32)
    ang = np.concatenate([row[:, None] * freqs[None, :], col[:, None] * freqs[None, :]], axis=-1)
    cos = np.repeat(np.cos(ang), 2, axis=-1)
    sin = np.repeat(np.sin(ang), 2, axis=-1)
    sign = np.where(np.arange(HEAD_DIM) % 2 == 0, -1.0, 1.0)[None, :]
    pair = lambda a: np.tile(a, (1, 2)).astype(np.float32)
    return pair(cos), pair(sin * sign)


def _qkv_kernel(*refs, n_prev, n_prompt_tiles):
    x_ref, g_ref, sh_ref, sc_ref, w_ref, qg_ref, kg_ref, cos_ref, sin_ref = refs[:9]
    prev_refs, refs = refs[9:9 + 2 * (n_prev > 0)], refs[9 + 2 * (n_prev > 0):]
    q_ref, k_ref, v_ref, kf_ref, vf_ref, w_s = refs
    i = pl.program_id(0)
    tm = q_ref.shape[0]

    @pl.when(i == 0)
    def _():
        w_s[...] = w_ref[...].astype(BF16)

    k_col = N_HEADS * HEAD_DIM
    v_col = k_col + KV_DIM
    q_gain = qg_ref[...] * HEAD_DIM ** -0.5

    def project(rows):
        h = _modulate(x_ref[rows, :], g_ref[...], sh_ref[0], sc_ref[0]).astype(BF16)
        qkv = _dot(h, w_s[...])
        v = qkv[:, v_col:]
        v_ref[rows, :] = v.astype(BF16)

        def pair_norm(col, gain):
            return jnp.concatenate([_rmsnorm(qkv[:, c:c + HEAD_DIM], gain) for c in (col, col + HEAD_DIM)], axis=1)

        return pair_norm, v

    @pl.when(i < n_prompt_tiles)
    def _():
        if n_prev:
            kf_ref[:, 0:n_prev] = prev_refs[0][...]
            vf_ref[:, 0:n_prev] = prev_refs[1][...]
        for r in range(tm // SEQ):
            rows = slice(r * SEQ, (r + 1) * SEQ)
            pair_norm, v = project(rows)
            for col in range(0, k_col, HEAD_PAIR):
                q_ref[rows, col:col + HEAD_PAIR] = pair_norm(col, q_gain).astype(BF16)
            k = pair_norm(k_col, kg_ref[...])
            k_ref[rows, :] = k.astype(BF16)
            for kv in range(N_KV_HEADS):
                cols = slice(kv * HEAD_DIM, (kv + 1) * HEAD_DIM)
                head_rows = pl.ds(kv, SEQ, stride=N_KV_HEADS)
                kf_ref[r, n_prev, head_rows, :] = k[:, cols]
                vf_ref[r, n_prev, head_rows, :] = v[:, cols]

    @pl.when(i >= n_prompt_tiles)
    def _():
        r_id = lax.broadcasted_iota(jnp.int32, (HEAD_PAIR, HEAD_PAIR), 0)
        col_id = lax.broadcasted_iota(jnp.int32, (HEAD_PAIR, HEAD_PAIR), 1)
        swap = jnp.where((r_id ^ 1) == col_id, 1.0, 0.0).astype(BF16)
        for r in range(tm // SEQ):
            rows = slice(r * SEQ, (r + 1) * SEQ)
            pair_norm, _ = project(rows)

            def rotate(x):
                return x * cos_ref[rows, :] + _dot(x.astype(BF16), swap) * sin_ref[rows, :]

            for col in range(0, k_col, HEAD_PAIR):
                q_ref[rows, col:col + HEAD_PAIR] = rotate(pair_norm(col, q_gain)).astype(BF16)
            k_ref[rows, :] = rotate(pair_norm(k_col, kg_ref[...])).astype(BF16)


def _qkv(y, norm_mix, mods, layer, w_qkv, q_gain, k_gain, jl, prev_kv):
    tm = 1024
    npt = N_PROMPT // tm
    tiles_per_req = DEC_SEQ // tm
    cos, sin = _rope_tables()
    rope_spec = pl.BlockSpec((tm, HEAD_PAIR), lambda i: (jnp.maximum(i - npt, 0) % tiles_per_req, 0))
    row = lambda n: pl.BlockSpec((tm, n), lambda i: (i, 0))
    cache_rows = SEQ * N_KV_HEADS
    cache = lambda n: pl.BlockSpec((tm // SEQ, n, cache_rows, HEAD_DIM), lambda i: (jnp.minimum(i, npt - 1), 0, 0, 0))
    gain = pl.BlockSpec((None, 1, HEAD_DIM), lambda i: (jl, 0, 0))
    return pl.pallas_call(
        functools.partial(_qkv_kernel, n_prev=jl, n_prompt_tiles=npt),
        grid=(N_TOK // tm,),
        in_specs=[
            row(D_MODEL),
            pl.BlockSpec((None, 1, D_MODEL), lambda i: (layer, 0, 0)),
            _mod_spec(layer, 0, tm),
            _mod_spec(layer, 1, tm),
            pl.BlockSpec((None, D_MODEL, QKV_DIM), lambda i: (jl, 0, 0), pipeline_mode=pl.Buffered(1)),
            gain, gain,
            rope_spec, rope_spec,
        ] + [cache(jl)] * len(prev_kv),
        out_specs=[row(D_MODEL), row(KV_DIM), row(KV_DIM), cache(jl + 1), cache(jl + 1)],
        out_shape=[
            jax.ShapeDtypeStruct((N_TOK, D_MODEL), BF16),
            jax.ShapeDtypeStruct((N_TOK, KV_DIM), BF16),
            jax.ShapeDtypeStruct((N_TOK, KV_DIM), BF16),
            jax.ShapeDtypeStruct((BATCH, jl + 1, cache_rows, HEAD_DIM), F32),
            jax.ShapeDtypeStruct((BATCH, jl + 1, cache_rows, HEAD_DIM), F32),
        ],
        scratch_shapes=[pltpu.VMEM((D_MODEL, QKV_DIM), BF16)],
        compiler_params=_params(1),
        name="qkv_proj",
    )(y, norm_mix, mods, mods, w_qkv, q_gain, k_gain, jnp.asarray(cos), jnp.asarray(sin), *prev_kv)


def _attention_kernel(*refs, with_cache, seq):
    if with_cache:
        q_ref, k_ref, v_ref, ck_ref, cv_ref, o_ref = refs
    else:
        q_ref, k_ref, v_ref, o_ref = refs
    n_req = k_ref.shape[0] // seq
    tq = q_ref.shape[0] // n_req
    for r in range(n_req):
        q_rows, k_rows = slice(r * tq, (r + 1) * tq), slice(r * seq, (r + 1) * seq)
        for kv in range(k_ref.shape[1] // HEAD_DIM):
            kv_cols = slice(kv * HEAD_DIM, (kv + 1) * HEAD_DIM)
            k = k_ref[k_rows, kv_cols]
            v = v_ref[k_rows, kv_cols]
            if with_cache:
                head_rows = pl.ds(kv, PAST_LEN, stride=N_KV_HEADS)
                k = jnp.concatenate([k, ck_ref[0, 0, head_rows, :].astype(BF16)], axis=0)
                v = jnp.concatenate([v, cv_ref[0, 0, head_rows, :].astype(BF16)], axis=0)
            v_ones = jnp.concatenate([v, jnp.ones_like(v)], axis=1)
            for g in range(GROUP):
                cols = slice((kv * GROUP + g) * HEAD_DIM, (kv * GROUP + g + 1) * HEAD_DIM)
                s = lax.dot_general(q_ref[q_rows, cols], k, (((1,), (1,)), ((), ())), preferred_element_type=F32)
                p = jnp.exp(s - jnp.max(s, axis=-1, keepdims=True))
                o = _dot(p.astype(BF16), v_ones)
                o_ref[q_rows, cols] = (o[:, :HEAD_DIM] / o[:, HEAD_DIM:]).astype(BF16)


def _attention(q, k, v, row0, n_req, seq, tq, kv_per_step, req_per_step=1, cache_k=None, cache_v=None,
               cache_layer=0):
    nq = seq // tq
    n_req //= req_per_step
    tq, seq_rows = tq * req_per_step, seq * req_per_step
    qb0, kb0 = row0 // tq, row0 // seq_rows
    qw, kw = kv_per_step * GROUP * HEAD_DIM, kv_per_step * HEAD_DIM
    in_specs = [
        pl.BlockSpec((tq, qw), lambda b, h, t: (qb0 + b * nq + t, h)),
        pl.BlockSpec((seq_rows, kw), lambda b, h, t: (kb0 + b, h)),
        pl.BlockSpec((seq_rows, kw), lambda b, h, t: (kb0 + b, h)),
    ]
    args = [q, k, v]
    with_cache = cache_k is not None
    if with_cache:
        assert kv_per_step == N_KV_HEADS
        cspec = pl.BlockSpec((1, 1, PAST_LEN * N_KV_HEADS, HEAD_DIM), lambda b, h, t: (b, cache_layer, 0, 0))
        in_specs += [cspec, cspec]
        args += [cache_k, cache_v]
    return pl.pallas_call(
        functools.partial(_attention_kernel, with_cache=with_cache, seq=seq),
        grid=(n_req, N_KV_HEADS // kv_per_step, nq),
        in_specs=in_specs,
        out_specs=pl.BlockSpec((tq, qw), lambda b, h, t: (b * nq + t, h)),
        out_shape=jax.ShapeDtypeStruct((n_req * seq_rows, D_MODEL), BF16),
        compiler_params=_params(3),
        name="attention_%d" % seq,
    )(*args)


def _ffn_kernel(*refs, n_in, n_out, n_prompt_tiles, final):
    y_refs, refs = refs[:n_in], refs[n_in:]
    (ap_ref, as_ref, wo_ref, gm_ref, g_ref, sh_ref, sc_ref, gate_ref, wg_ref, wu_ref, wd_ref, fin_ref) = refs[:12]
    o_refs, refs = refs[12:12 + n_out], refs[12 + n_out:]
    wo_s, wg_s, wu_s, wd_s, ymid_s, f_s, acc_s = refs
    s = pl.program_id(0)
    nf = N_FF_CHUNKS
    tile = jnp.maximum(s - (nf - 1), 0)
    is_prompt = tile < n_prompt_tiles

    def prologue():
        a = jnp.where(is_prompt, ap_ref[...], as_ref[...])
        y = _read_stream(y_refs, is_prompt)
        y_mid = y + gm_ref[0] * _dot(a, wo_s[...])
        ymid_s[...] = y_mid
        f_s[...] = _modulate(y_mid, g_ref[...], sh_ref[0], sc_ref[0]).astype(BF16)

    def chunk(c):
        f = f_s[...]
        hidden = _silu(_dot(f, wg_s[c])) * _dot(f, wu_s[c])
        return _dot(hidden.astype(BF16), wd_s[c])

    def epilogue(acc):
        out = ymid_s[...] + gate_ref[0] * acc
        if final:
            out = _rmsnorm(out, fin_ref[...])
        if n_out == 1:
            o_refs[0][...] = out
        else:
            @pl.when(is_prompt)
            def _():
                o_refs[0][...] = out

            @pl.when(jnp.logical_not(is_prompt))
            def _():
                o_refs[1][...] = out

    @pl.when(s == 0)
    def _():
        wo_s[...] = wo_ref[...].astype(BF16)
        prologue()

    @pl.when(s < nf)
    def _():
        wg_s[s] = wg_ref[...].astype(BF16)
        wu_s[s] = wu_ref[...].astype(BF16)
        wd_s[s] = wd_ref[...].astype(BF16)
        part = chunk(s)

        @pl.when(s == 0)
        def _():
            acc_s[...] = part

        @pl.when(s > 0)
        def _():
            acc_s[...] += part

        @pl.when(s == nf - 1)
        def _():
            epilogue(acc_s[...])

    @pl.when(s >= nf)
    def _():
        prologue()
        acc = chunk(0)
        for c in range(1, nf):
            acc = acc + chunk(c)
        epilogue(acc)


def _ffn(stream, ap, as_, w_out, jl, norm_ffn, mods, layer, wg, wu, wd, final_norm, final):
    tm = 512
    nf = N_FF_CHUNKS
    npt = N_PROMPT // tm
    n_steps = nf + N_TOK // tm - 1
    n_out = 2 if final else 1
    tile_of = lambda s: jnp.maximum(s - (nf - 1), 0)
    chunk_of = lambda s: jnp.minimum(s, nf - 1)
    out_shape = _pair_shapes(D_MODEL, F32) if final else [jax.ShapeDtypeStruct((N_TOK, D_MODEL), F32)]
    return pl.pallas_call(
        functools.partial(_ffn_kernel, n_in=len(stream), n_out=n_out, n_prompt_tiles=npt, final=final),
        grid=(n_steps,),
        in_specs=_stream_specs(len(stream), tm, tile_of) + _pair_specs(tm, D_MODEL, tile_of) + [
            pl.BlockSpec((None, D_MODEL, D_MODEL), lambda s: (jl, 0, 0), pipeline_mode=pl.Buffered(1)),
            _mod_spec(layer, 2, tm, tile_of),
            pl.BlockSpec((None, 1, D_MODEL), lambda s: (layer, 0, 0)),
            _mod_spec(layer, 3, tm, tile_of),
            _mod_spec(layer, 4, tm, tile_of),
            _mod_spec(layer, 5, tm, tile_of),
            pl.BlockSpec((None, D_MODEL, FF_CHUNK), lambda s: (layer, 0, chunk_of(s))),
            pl.BlockSpec((None, D_MODEL, FF_CHUNK), lambda s: (layer, 0, chunk_of(s))),
            pl.BlockSpec((None, FF_CHUNK, D_MODEL), lambda s: (layer, chunk_of(s), 0)),
            pl.BlockSpec((1, D_MODEL), lambda s: (0, 0)),
        ],
        out_specs=_stream_specs(n_out, tm, tile_of),
        out_shape=out_shape,
        scratch_shapes=[
            pltpu.VMEM((D_MODEL, D_MODEL), BF16),
            pltpu.VMEM((nf, D_MODEL, FF_CHUNK), BF16),
            pltpu.VMEM((nf, D_MODEL, FF_CHUNK), BF16),
            pltpu.VMEM((nf, FF_CHUNK, D_MODEL), BF16),
            pltpu.VMEM((tm, D_MODEL), F32),
            pltpu.VMEM((tm, D_MODEL), BF16),
            pltpu.VMEM((tm, D_MODEL), F32),
        ],
        compiler_params=_params(1),
        name="ffn",
    )(*stream, ap, as_, w_out, mods, norm_ffn, mods, mods, mods, wg, wu, wd, final_norm.reshape(1, D_MODEL))


def kernel(x_prompt, x_sample, cache_k, cache_v, c, c_ctx, mod_w, mod_b, norm_mix, norm_ffn, hy_w_in, hy_conv_w, hy_conv_b, hy_f_w1, hy_f_b1, hy_f_freq, hy_f_w2, hy_f_b2, hy_f_w3, hy_bias, hy_w_out, at_w_qkv, at_q_norm, at_k_norm, at_w_out, ffn_w_gate, ffn_w_up, ffn_w_down, final_norm):
    stream = (x_prompt.reshape(N_PROMPT, D_MODEL), x_sample.reshape(N_SAMPLE, D_MODEL))
    cond = jnp.concatenate([c_ctx[None, :], c, jnp.zeros((MOD_ROWS - 1 - DEC_BATCH, D_MODEL), F32)], axis=0)
    mods = _adaln(cond, mod_w, mod_b)

    tables = {}
    for L in (SEQ, DEC_SEQ):
        fwd, inv = _dft_tables(L)
        tables[L] = (_cast_bf16(jnp.asarray(fwd), 2 * L // 4), _cast_bf16(jnp.asarray(inv), L // 4))

    norm_mix = norm_mix.reshape(DEPTH, 1, D_MODEL)
    norm_ffn = norm_ffn.reshape(DEPTH, 1, D_MODEL)
    f_w1 = jnp.pad(hy_f_w1, ((0, 0), (0, FEAT_PAD - FILTER_EMB), (0, 0)))
    f_b1 = hy_f_b1.reshape(-1, 1, FILTER_HIDDEN)
    f_b2 = hy_f_b2.reshape(-1, 1, FILTER_HIDDEN)
    bias = hy_bias.reshape(-1, 1, D_MODEL)
    q_gain = at_q_norm.reshape(-1, 1, HEAD_DIM)
    k_gain = at_k_norm.reshape(-1, 1, HEAD_DIM)
    cache_k = cache_k.reshape(DEC_BATCH, -1, PAST_LEN * N_KV_HEADS, HEAD_DIM)
    cache_v = cache_v.reshape(DEC_BATCH, -1, PAST_LEN * N_KV_HEADS, HEAD_DIM)

    new_kv = ()
    for layer in range(DEPTH):
        jl = layer // N_MIXERS
        if layer % N_MIXERS == 0:
            x0, u = _in_proj(stream, norm_mix, mods, layer, hy_w_in, hy_conv_w, hy_conv_b, jl)
            mixed = []
            for L, row_block0, n_steps, seq_per_step, tn, chains in (
                    (SEQ, 0, BATCH // 4, 4, D_MODEL, 1),
                    (DEC_SEQ, N_PROMPT // DEC_SEQ, DEC_BATCH, 1, 512, 1)):
                fwd_bf16, inv_bf16 = tables[L]
                spectrum = _hyena_spectrum(L, fwd_bf16, jl, f_w1, f_b1, hy_f_freq, hy_f_w2, f_b2, hy_f_w3)
                mixed.append(_hyena_conv(u, x0, spectrum, bias, jl, fwd_bf16, inv_bf16, L, row_block0, n_steps,
                                         seq_per_step, tn, chains))
            w_out = hy_w_out
        else:
            q, k, v, *new_kv = _qkv(stream[0], norm_mix, mods, layer, at_w_qkv, q_gain, k_gain, jl, new_kv)
            mixed = [_attention(q, k, v, 0, BATCH, SEQ, SEQ, N_KV_HEADS, req_per_step=4),
                     _attention(q, k, v, N_PROMPT, DEC_BATCH, DEC_SEQ, 512, N_KV_HEADS, cache_k=cache_k, cache_v=cache_v,
                                cache_layer=jl)]
            w_out = at_w_out
        stream = tuple(_ffn(stream, mixed[0], mixed[1], w_out, jl, norm_ffn, mods, layer,
                            ffn_w_gate, ffn_w_up, ffn_w_down, final_norm, final=(layer == DEPTH - 1)))

    y_prompt, y_sample = stream
    new_k, new_v = (a.reshape(BATCH, -1, SEQ, N_KV_HEADS, HEAD_DIM) for a in new_kv)
    return (y_prompt.reshape(BATCH, SEQ, D_MODEL), y_sample.reshape(DEC_BATCH, DEC_SEQ, D_MODEL), new_k, new_v)
```

```python
import functools
import math

import numpy as np
import jax
import jax.numpy as jnp
from jax import lax
from jax.experimental import pallas as pl
from jax.experimental.pallas import tpu as pltpu

D_MODEL = 1024
BATCH = 16
SEQ = 256
DEPTH = 4
DEC_BATCH = 4
DEC_SEQ = 1024
PAST_LEN = 512
GRID_W = 64
N_MIXERS = 2
HEAD_DIM = 128
N_HEADS = D_MODEL // HEAD_DIM
N_KV_HEADS = 2
GROUP = N_HEADS // N_KV_HEADS
KV_DIM = N_KV_HEADS * HEAD_DIM
HEAD_PAIR = 2 * HEAD_DIM
QKV_DIM = (N_HEADS + 2 * N_KV_HEADS) * HEAD_DIM
ROPE_THETA = 10000.0
D_FF = ((8 * D_MODEL + 3 * 256 - 1) // (3 * 256)) * 256
N_BANDS = 16
FILTER_EMB = 1 + 2 * N_BANDS
FILTER_HIDDEN = 64
MIN_DECAY = math.log(1e-2) / 1.5
MAX_DECAY = math.log(1e-2) / 0.3
EPS = 1e-6

N_PROMPT = BATCH * SEQ
N_SAMPLE = DEC_BATCH * DEC_SEQ
N_TOK = N_PROMPT + N_SAMPLE
N_MOD = 6
MOD_ROWS = 8
FEAT_PAD = 128
CONV_PAD = 8
FF_CHUNK = 256
N_FF_CHUNKS = D_FF // FF_CHUNK
V7X_VMEM_LIMIT = 56 * 1024 * 1024

F32 = jnp.float32
BF16 = jnp.bfloat16


def _params(n_axes, vmem=V7X_VMEM_LIMIT):
    return pltpu.CompilerParams(dimension_semantics=("arbitrary",) * n_axes, vmem_limit_bytes=vmem)


def _mod_row(tile, tm):
    n_prompt_tiles = N_PROMPT // tm
    tiles_per_req = DEC_SEQ // tm
    return jnp.where(tile < n_prompt_tiles, 0, 1 + jnp.maximum(tile - n_prompt_tiles, 0) // tiles_per_req)


def _mod_spec(layer, which, tm, tile_of=lambda i, *_: i):
    def index(*ids):
        return ((layer * N_MOD + which) * MOD_ROWS + _mod_row(tile_of(*ids), tm), 0, 0)
    return pl.BlockSpec((1, 1, D_MODEL), index)


def _pair_specs(tm, width, tile_of=lambda i, *_: i):
    npt = N_PROMPT // tm
    return [pl.BlockSpec((tm, width), lambda *ids: (jnp.minimum(tile_of(*ids), npt - 1), 0)),
            pl.BlockSpec((tm, width), lambda *ids: (jnp.maximum(tile_of(*ids) - npt, 0), 0))]


def _pair_shapes(width, dtype):
    return [jax.ShapeDtypeStruct((N_PROMPT, width), dtype), jax.ShapeDtypeStruct((N_SAMPLE, width), dtype)]


def _stream_specs(n_arrays, tm, tile_of=lambda i, *_: i):
    if n_arrays == 2:
        return _pair_specs(tm, D_MODEL, tile_of)
    return [pl.BlockSpec((tm, D_MODEL), lambda *ids: (tile_of(*ids), 0))]


def _read_stream(refs, is_prompt):
    if len(refs) == 2:
        return jnp.where(is_prompt, refs[0][...], refs[1][...])
    return refs[0][...]


def _split_bf16(a):
    hi = a.astype(BF16)
    lo = (a - hi.astype(F32)).astype(BF16)
    return hi, lo


def _dot(a, b):
    return jnp.dot(a, b, preferred_element_type=F32)


def _dot3(a, b):
    a_hi, a_lo = _split_bf16(a)
    b_hi, b_lo = _split_bf16(b)
    return _dot(a_hi, b_hi) + (_dot(a_lo, b_hi) + _dot(a_hi, b_lo))


def _silu(a):
    return a / (1.0 + jnp.exp(-a))


def _rmsnorm(x, g):
    return x * lax.rsqrt(jnp.mean(x * x, axis=-1, keepdims=True) + EPS) * g


def _modulate(x, g, shift, scale):
    return _rmsnorm(x, g) * (1.0 + scale) + shift


def _adaln_kernel(c_ref, w_ref, b_ref, o_ref):
    s = _silu(c_ref[...])
    m = _dot3(s, w_ref[0]) + b_ref[0]
    for v in range(m.shape[1] // D_MODEL):
        o_ref[v * MOD_ROWS:(v + 1) * MOD_ROWS, 0, :] = m[:, v * D_MODEL:(v + 1) * D_MODEL]


def _adaln(cond, mod_w, mod_b):
    per_step = 2
    tn = per_step * D_MODEL
    return pl.pallas_call(
        _adaln_kernel,
        grid=(DEPTH, N_MOD // per_step),
        in_specs=[
            pl.BlockSpec((MOD_ROWS, D_MODEL), lambda l, j: (0, 0)),
            pl.BlockSpec((1, D_MODEL, tn), lambda l, j: (l, 0, j)),
            pl.BlockSpec((1, 1, tn), lambda l, j: (l, 0, j)),
        ],
        out_specs=pl.BlockSpec((per_step * MOD_ROWS, 1, D_MODEL), lambda l, j: (l * (N_MOD // per_step) + j, 0, 0)),
        out_shape=jax.ShapeDtypeStruct((DEPTH * N_MOD * MOD_ROWS, 1, D_MODEL), F32),
        compiler_params=_params(2),
        name="adaln",
    )(cond, mod_w, mod_b.reshape(DEPTH, 1, N_MOD * D_MODEL))


def _cast_kernel(x_ref, o_ref):
    o_ref[...] = x_ref[...].astype(BF16)


def _cast_bf16(x, rows):
    m, n = x.shape
    return pl.pallas_call(
        _cast_kernel,
        grid=(m // rows,),
        in_specs=[pl.BlockSpec((rows, n), lambda i: (i, 0))],
        out_specs=pl.BlockSpec((rows, n), lambda i: (i, 0)),
        out_shape=jax.ShapeDtypeStruct((m, n), BF16),
        compiler_params=_params(1),
        name="cast_table",
    )(x)


def _in_proj_kernel(*refs, n_stream, n_prompt_tiles):
    x_refs, refs = refs[:n_stream], refs[n_stream:]
    (g_ref, sh_ref, sc_ref, w0_ref, w1_ref, w2_ref, cw0_ref, cw1_ref, cw2_ref, cb0_ref, cb1_ref, cb2_ref,
     x0_ref, u_ref, w_s) = refs
    i = pl.program_id(0)
    tm, tc = x0_ref.shape

    @pl.when(i == 0)
    def _():
        w_s[0] = w0_ref[...].astype(BF16)
        w_s[1] = w1_ref[...].astype(BF16)
        w_s[2] = w2_ref[...].astype(BF16)

    n_chain = tm // SEQ
    pad = jnp.zeros((CONV_PAD, tc), F32)

    def tile(seq_len):
        hs = []
        for b in range(n_chain):
            x = _read_stream([r.at[b * SEQ:(b + 1) * SEQ, :] for r in x_refs], i < n_prompt_tiles)
            hs.append(_modulate(x, g_ref[...], sh_ref[0], sc_ref[0]).astype(BF16))

        def short_conv(part, cw_ref, cb_ref):
            zs = [_dot(h, w_s[part]) for h in hs]
            out = []
            for b, z in enumerate(zs):
                before = zs[b - 1][SEQ - CONV_PAD:, :] if (b * SEQ) % seq_len else pad
                after = zs[b + 1][:CONV_PAD, :] if ((b + 1) * SEQ) % seq_len else pad
                ext = jnp.concatenate([before, z, after], axis=0)
                rows = ext.shape[0]
                z_prev = pltpu.roll(ext, 1, 0)[CONV_PAD:CONV_PAD + SEQ, :]
                z_next = pltpu.roll(ext, rows - 1, 0)[CONV_PAD:CONV_PAD + SEQ, :]
                out.append(z_prev * cw_ref[0:1, :] + z * cw_ref[1:2, :] + z_next * cw_ref[2:3, :] + cb_ref[...])
            return out

        for b, x0 in enumerate(short_conv(0, cw0_ref, cb0_ref)):
            x0_ref[b * SEQ:(b + 1) * SEQ, :] = x0.astype(BF16)
        for b, (x1, v) in enumerate(zip(short_conv(1, cw1_ref, cb1_ref), short_conv(2, cw2_ref, cb2_ref))):
            u_ref[b * SEQ:(b + 1) * SEQ, :] = (x1 * v).astype(BF16)

    @pl.when(i < n_prompt_tiles)
    def _():
        tile(SEQ)

    @pl.when(i >= n_prompt_tiles)
    def _():
        tile(DEC_SEQ)


def _in_proj(stream, norm_mix, mods, layer, w_in, conv_w, conv_b, jl):
    tm = 1024
    npt = N_PROMPT // tm

    def wspec(part):
        return pl.BlockSpec((None, D_MODEL, D_MODEL), lambda i: (jl, 0, part), pipeline_mode=pl.Buffered(1))

    def cwspec(part):
        return pl.BlockSpec((None, 3, D_MODEL), lambda i: (jl, 0, part))

    def cbspec(part):
        return pl.BlockSpec((None, 1, D_MODEL), lambda i: (jl, 0, part))

    out = pl.BlockSpec((tm, D_MODEL), lambda i: (i, 0))
    cb = conv_b.reshape(-1, 1, 3 * D_MODEL)
    return pl.pallas_call(
        functools.partial(_in_proj_kernel, n_stream=len(stream), n_prompt_tiles=npt),
        grid=(N_TOK // tm,),
        in_specs=_stream_specs(len(stream), tm) + [
            pl.BlockSpec((None, 1, D_MODEL), lambda i: (layer, 0, 0)),
            _mod_spec(layer, 0, tm),
            _mod_spec(layer, 1, tm),
            wspec(0), wspec(1), wspec(2),
            cwspec(0), cwspec(1), cwspec(2),
            cbspec(0), cbspec(1), cbspec(2),
        ],
        out_specs=[out, out],
        out_shape=[jax.ShapeDtypeStruct((N_TOK, D_MODEL), BF16)] * 2,
        scratch_shapes=[pltpu.VMEM((3, D_MODEL, D_MODEL), BF16)],
        compiler_params=_params(1),
        name="hyena_in_proj",
    )(*stream, norm_mix, mods, mods, w_in, w_in, w_in, conv_w, conv_w, conv_w, cb, cb, cb)


def _dft_tables(L):
    n = 2 * L
    k = np.arange(L, dtype=np.float64)[:, None]
    t = np.arange(L, dtype=np.float64)[None, :]
    ang = 2.0 * np.pi * k * t / n
    top = np.cos(ang)
    bot = -np.sin(ang)
    bot[0, :] = np.where(np.arange(L) % 2 == 0, 1.0, -1.0)
    fwd = np.concatenate([top, bot], axis=0)
    wk = np.full((L,), 2.0)
    wk[0] = 1.0
    inv_top = (np.cos(ang) * wk[:, None]).T / n
    inv_bot = (-2.0 * np.sin(ang)).T / n
    inv_bot[:, 0] = np.where(np.arange(L) % 2 == 0, 1.0, -1.0) / n
    inv = np.concatenate([inv_top, inv_bot], axis=1)
    return fwd.astype(np.float32), inv.astype(np.float32)


def _filter_feats(L):
    t = np.arange(L, dtype=np.float32) / np.float32(L)
    bands = np.arange(1, N_BANDS + 1, dtype=np.float32)
    ang = (np.float32(2.0 * math.pi) * t[:, None]) * bands[None, :]
    feats = np.concatenate([t[:, None], np.cos(ang), np.sin(ang)], axis=-1).astype(np.float32)
    return np.pad(feats, ((0, 0), (0, FEAT_PAD - FILTER_EMB)))


def _filter_spectrum(h, w3f_ref, w3b_ref, dl_ref, fwd_ref, L):
    tn = w3f_ref.shape[1]
    rows = lax.broadcasted_iota(jnp.int32, (L, tn), 0)
    t = rows.astype(F32) / L
    window = jnp.exp(-t * dl_ref[...])
    hf = _dot3(h, w3f_ref[...]) * window
    hb = jnp.where(rows > 0, _dot3(h, w3b_ref[...]) * window, 0.0)
    norm = jnp.sqrt(jnp.sum(hf * hf + hb * hb, axis=0, keepdims=True) + EPS)
    hf = hf / norm
    hb = hb / norm
    g_top = _dot(fwd_ref[0:L, :], (hf + hb).astype(BF16))
    g_bot = _dot(fwd_ref[L:2 * L, :], (hf - hb).astype(BF16))
    sign = jnp.where(rows % 2 == 0, 1.0, -1.0)
    nyquist_b = jnp.sum(sign * hb, axis=0, keepdims=True)
    return g_top, g_bot + jnp.where(rows == 0, 2.0 * nyquist_b, 0.0)


def _hyena_conv_kernel(u_ref, x0_ref, bias_ref, fwd_ref, inv_ref, feats_ref, w1_ref, b1_ref, fr_ref, w2_ref, b2_ref,
                       w3f_ref, w3b_ref, dl_ref, o_ref, h_s, g_s, *, L):
    j, b = pl.program_id(0), pl.program_id(1)

    @pl.when(jnp.logical_and(j == 0, b == 0))
    def _():
        h1 = jnp.sin(fr_ref[0:1, :] * (_dot3(feats_ref[...], w1_ref[...]) + b1_ref[...]))
        h_s[...] = jnp.sin(fr_ref[1:2, :] * (_dot3(h1, w2_ref[...]) + b2_ref[...]))

    @pl.when(b == 0)
    def _():
        g_top, g_bot = _filter_spectrum(h_s[...], w3f_ref, w3b_ref, dl_ref, fwd_ref, L)
        g_s[0:L, :] = g_top
        g_s[L:2 * L, :] = g_bot

    tn = o_ref.shape[1]
    first = lax.broadcasted_iota(jnp.int32, (L, tn), 0) == 0
    for s in range(o_ref.shape[0] // L):
        rows = slice(s * L, (s + 1) * L)
        u = u_ref[rows, :]
        spec = _dot(fwd_ref[...], u)
        u_top, u_bot = spec[0:L, :], spec[L:2 * L, :]
        g_top, g_bot = g_s[0:L, :], g_s[L:2 * L, :]
        y_top = u_top * g_top - jnp.where(first, 0.0, u_bot * g_bot)
        y_bot = jnp.where(first, u_bot * g_bot, u_top * g_bot + u_bot * g_top)
        y_spec = jnp.concatenate([y_top, y_bot], axis=0).astype(BF16)
        y = _dot(inv_ref[...], y_spec)
        gated = x0_ref[rows, :].astype(F32) * (y + u.astype(F32) * bias_ref[...])
        o_ref[rows, :] = gated.astype(BF16)


def _hyena_conv(u, x0, bias, jl, fwd, inv, filt, L, row_block0, n_steps, seq_per_step, tn):
    nj = D_MODEL // tn
    rows = seq_per_step * L
    feats = jnp.asarray(_filter_feats(L))
    deltas = jnp.asarray(np.abs(np.linspace(MIN_DECAY, MAX_DECAY, D_MODEL, dtype=np.float32)).reshape(1, D_MODEL))
    act = pl.BlockSpec((rows, tn), lambda j, b: (row_block0 + b, j))
    const = lambda j, b: (0, 0)
    layer = lambda j, b: (jl, 0, 0)
    f_w1, f_b1, f_freq, f_w2, f_b2, f_w3 = filt
    return pl.pallas_call(
        functools.partial(_hyena_conv_kernel, L=L),
        grid=(nj, n_steps),
        in_specs=[
            act, act,
            pl.BlockSpec((None, 1, tn), lambda j, b: (jl, 0, j)),
            pl.BlockSpec((2 * L, L), const),
            pl.BlockSpec((L, 2 * L), const),
            pl.BlockSpec((L, FEAT_PAD), const),
            pl.BlockSpec((None, FEAT_PAD, FILTER_HIDDEN), layer),
            pl.BlockSpec((None, 1, FILTER_HIDDEN), layer),
            pl.BlockSpec((None, 2, FILTER_HIDDEN), layer),
            pl.BlockSpec((None, FILTER_HIDDEN, FILTER_HIDDEN), layer),
            pl.BlockSpec((None, 1, FILTER_HIDDEN), layer),
            pl.BlockSpec((None, FILTER_HIDDEN, tn), lambda j, b: (jl, 0, j)),
            pl.BlockSpec((None, FILTER_HIDDEN, tn), lambda j, b: (jl, 0, nj + j)),
            pl.BlockSpec((1, tn), lambda j, b: (0, j)),
        ],
        out_specs=pl.BlockSpec((rows, tn), lambda j, b: (b, j)),
        out_shape=jax.ShapeDtypeStruct((n_steps * rows, D_MODEL), BF16),
        scratch_shapes=[pltpu.VMEM((L, FILTER_HIDDEN), F32), pltpu.VMEM((2 * L, tn), F32)],
        compiler_params=_params(2),
        name="hyena_conv_%d" % L,
    )(u, x0, bias, fwd, inv, feats, f_w1, f_b1, f_freq, f_w2, f_b2, f_w3, f_w3, deltas)


def _rope_tables():
    rows = DEC_SEQ // GRID_W
    row = np.repeat(np.arange(rows), GRID_W).astype(np.float32)
    col = np.tile(np.arange(GRID_W), rows).astype(np.float32)
    half = HEAD_DIM // 2
    freqs = (np.float32(ROPE_THETA) ** (-np.arange(0, half, 2, dtype=np.float32) / np.float32(half))).astype(np.float32)
    ang = np.concatenate([row[:, None] * freqs[None, :], col[:, None] * freqs[None, :]], axis=-1)
    cos = np.repeat(np.cos(ang), 2, axis=-1)
    sin = np.repeat(np.sin(ang), 2, axis=-1)
    sign = np.where(np.arange(HEAD_DIM) % 2 == 0, -1.0, 1.0)[None, :]
    pair = lambda a: np.tile(a, (1, 2)).astype(np.float32)
    return pair(cos), pair(sin * sign)


def _qkv_kernel(*refs, n_prev, n_prompt_tiles):
    x_ref, g_ref, sh_ref, sc_ref, w_ref, qg_ref, kg_ref, cos_ref, sin_ref = refs[:9]
    prev_refs, refs = refs[9:9 + 2 * (n_prev > 0)], refs[9 + 2 * (n_prev > 0):]
    q_ref, k_ref, v_ref, kf_ref, vf_ref, w_s = refs
    i = pl.program_id(0)
    tm = q_ref.shape[0]

    @pl.when(i == 0)
    def _():
        w_s[...] = w_ref[...].astype(BF16)

    k_col = N_HEADS * HEAD_DIM
    v_col = k_col + KV_DIM
    q_gain = qg_ref[...] * HEAD_DIM ** -0.5

    def project(rows):
        h = _modulate(x_ref[rows, :], g_ref[...], sh_ref[0], sc_ref[0]).astype(BF16)
        qkv = _dot(h, w_s[...])
        v = qkv[:, v_col:]
        v_ref[rows, :] = v.astype(BF16)

        def pair_norm(col, gain):
            return jnp.concatenate([_rmsnorm(qkv[:, c:c + HEAD_DIM], gain) for c in (col, col + HEAD_DIM)], axis=1)

        return pair_norm, v

    @pl.when(i < n_prompt_tiles)
    def _():
        if n_prev:
            kf_ref[:, 0:n_prev] = prev_refs[0][...]
            vf_ref[:, 0:n_prev] = prev_refs[1][...]
        for r in range(tm // SEQ):
            rows = slice(r * SEQ, (r + 1) * SEQ)
            pair_norm, v = project(rows)
            for col in range(0, k_col, HEAD_PAIR):
                q_ref[rows, col:col + HEAD_PAIR] = pair_norm(col, q_gain).astype(BF16)
            k = pair_norm(k_col, kg_ref[...])
            k_ref[rows, :] = k.astype(BF16)
            for kv in range(N_KV_HEADS):
                cols = slice(kv * HEAD_DIM, (kv + 1) * HEAD_DIM)
                head_rows = pl.ds(kv, SEQ, stride=N_KV_HEADS)
                kf_ref[r, n_prev, head_rows, :] = k[:, cols]
                vf_ref[r, n_prev, head_rows, :] = v[:, cols]

    @pl.when(i >= n_prompt_tiles)
    def _():
        r_id = lax.broadcasted_iota(jnp.int32, (HEAD_PAIR, HEAD_PAIR), 0)
        col_id = lax.broadcasted_iota(jnp.int32, (HEAD_PAIR, HEAD_PAIR), 1)
        swap = jnp.where((r_id ^ 1) == col_id, 1.0, 0.0).astype(BF16)
        for r in range(tm // SEQ):
            rows = slice(r * SEQ, (r + 1) * SEQ)
            pair_norm, _ = project(rows)

            def rotate(x):
                return x * cos_ref[rows, :] + _dot(x.astype(BF16), swap) * sin_ref[rows, :]

            for col in range(0, k_col, HEAD_PAIR):
                q_ref[rows, col:col + HEAD_PAIR] = rotate(pair_norm(col, q_gain)).astype(BF16)
            k_ref[rows, :] = rotate(pair_norm(k_col, kg_ref[...])).astype(BF16)


def _qkv(y, norm_mix, mods, layer, w_qkv, q_gain, k_gain, jl, prev_kv):
    tm = 1024
    npt = N_PROMPT // tm
    tiles_per_req = DEC_SEQ // tm
    cos, sin = _rope_tables()
    rope_spec = pl.BlockSpec((tm, HEAD_PAIR), lambda i: (jnp.maximum(i - npt, 0) % tiles_per_req, 0))
    row = lambda n: pl.BlockSpec((tm, n), lambda i: (i, 0))
    cache_rows = SEQ * N_KV_HEADS
    cache = lambda n: pl.BlockSpec((tm // SEQ, n, cache_rows, HEAD_DIM), lambda i: (jnp.minimum(i, npt - 1), 0, 0, 0))
    gain = pl.BlockSpec((None, 1, HEAD_DIM), lambda i: (jl, 0, 0))
    return pl.pallas_call(
        functools.partial(_qkv_kernel, n_prev=jl, n_prompt_tiles=npt),
        grid=(N_TOK // tm,),
        in_specs=[
            row(D_MODEL),
            pl.BlockSpec((None, 1, D_MODEL), lambda i: (layer, 0, 0)),
            _mod_spec(layer, 0, tm),
            _mod_spec(layer, 1, tm),
            pl.BlockSpec((None, D_MODEL, QKV_DIM), lambda i: (jl, 0, 0), pipeline_mode=pl.Buffered(1)),
            gain, gain,
            rope_spec, rope_spec,
        ] + [cache(jl)] * len(prev_kv),
        out_specs=[row(D_MODEL), row(KV_DIM), row(KV_DIM), cache(jl + 1), cache(jl + 1)],
        out_shape=[
            jax.ShapeDtypeStruct((N_TOK, D_MODEL), BF16),
            jax.ShapeDtypeStruct((N_TOK, KV_DIM), BF16),
            jax.ShapeDtypeStruct((N_TOK, KV_DIM), BF16),
            jax.ShapeDtypeStruct((BATCH, jl + 1, cache_rows, HEAD_DIM), F32),
            jax.ShapeDtypeStruct((BATCH, jl + 1, cache_rows, HEAD_DIM), F32),
        ],
        scratch_shapes=[pltpu.VMEM((D_MODEL, QKV_DIM), BF16)],
        compiler_params=_params(1),
        name="qkv_proj",
    )(y, norm_mix, mods, mods, w_qkv, q_gain, k_gain, jnp.asarray(cos), jnp.asarray(sin), *prev_kv)


def _attention_kernel(*refs, with_cache, seq):
    if with_cache:
        q_ref, k_ref, v_ref, ck_ref, cv_ref, o_ref = refs
    else:
        q_ref, k_ref, v_ref, o_ref = refs
    n_req = k_ref.shape[0] // seq
    tq = q_ref.shape[0] // n_req
    for r in range(n_req):
        q_rows, k_rows = slice(r * tq, (r + 1) * tq), slice(r * seq, (r + 1) * seq)
        for kv in range(k_ref.shape[1] // HEAD_DIM):
            kv_cols = slice(kv * HEAD_DIM, (kv + 1) * HEAD_DIM)
            k = k_ref[k_rows, kv_cols]
            v = v_ref[k_rows, kv_cols]
            if with_cache:
                head_rows = pl.ds(kv, PAST_LEN, stride=N_KV_HEADS)
                k = jnp.concatenate([k, ck_ref[0, 0, head_rows, :].astype(BF16)], axis=0)
                v = jnp.concatenate([v, cv_ref[0, 0, head_rows, :].astype(BF16)], axis=0)
            v_ones = jnp.concatenate([v, jnp.ones_like(v)], axis=1)
            for g in range(GROUP):
                cols = slice((kv * GROUP + g) * HEAD_DIM, (kv * GROUP + g + 1) * HEAD_DIM)
                s = lax.dot_general(q_ref[q_rows, cols], k, (((1,), (1,)), ((), ())), preferred_element_type=F32)
                p = jnp.exp(s - jnp.max(s, axis=-1, keepdims=True))
                o = _dot(p.astype(BF16), v_ones)
                o_ref[q_rows, cols] = (o[:, :HEAD_DIM] / o[:, HEAD_DIM:]).astype(BF16)


def _attention(q, k, v, row0, n_req, seq, tq, kv_per_step, req_per_step=1, cache_k=None, cache_v=None,
               cache_layer=0):
    nq = seq // tq
    n_req //= req_per_step
    tq, seq_rows = tq * req_per_step, seq * req_per_step
    qb0, kb0 = row0 // tq, row0 // seq_rows
    qw, kw = kv_per_step * GROUP * HEAD_DIM, kv_per_step * HEAD_DIM
    in_specs = [
        pl.BlockSpec((tq, qw), lambda b, h, t: (qb0 + b * nq + t, h)),
        pl.BlockSpec((seq_rows, kw), lambda b, h, t: (kb0 + b, h)),
        pl.BlockSpec((seq_rows, kw), lambda b, h, t: (kb0 + b, h)),
    ]
    args = [q, k, v]
    with_cache = cache_k is not None
    if with_cache:
        assert kv_per_step == N_KV_HEADS
        cspec = pl.BlockSpec((1, 1, PAST_LEN * N_KV_HEADS, HEAD_DIM), lambda b, h, t: (b, cache_layer, 0, 0))
        in_specs += [cspec, cspec]
        args += [cache_k, cache_v]
    return pl.pallas_call(
        functools.partial(_attention_kernel, with_cache=with_cache, seq=seq),
        grid=(n_req, N_KV_HEADS // kv_per_step, nq),
        in_specs=in_specs,
        out_specs=pl.BlockSpec((tq, qw), lambda b, h, t: (b * nq + t, h)),
        out_shape=jax.ShapeDtypeStruct((n_req * seq_rows, D_MODEL), BF16),
        compiler_params=_params(3),
        name="attention_%d" % seq,
    )(*args)


def _ffn_kernel(*refs, n_in, n_out, n_prompt_tiles, final):
    y_refs, refs = refs[:n_in], refs[n_in:]
    (ap_ref, as_ref, wo_ref, gm_ref, g_ref, sh_ref, sc_ref, gate_ref, wg_ref, wu_ref, wd_ref, fin_ref) = refs[:12]
    o_refs, refs = refs[12:12 + n_out], refs[12 + n_out:]
    wo_s, wg_s, wu_s, wd_s, ymid_s, f_s, acc_s = refs
    s = pl.program_id(0)
    nf = N_FF_CHUNKS
    tile = jnp.maximum(s - (nf - 1), 0)
    is_prompt = tile < n_prompt_tiles

    def prologue():
        a = jnp.where(is_prompt, ap_ref[...], as_ref[...])
        y = _read_stream(y_refs, is_prompt)
        y_mid = y + gm_ref[0] * _dot(a, wo_s[...])
        ymid_s[...] = y_mid
        f_s[...] = _modulate(y_mid, g_ref[...], sh_ref[0], sc_ref[0]).astype(BF16)

    def chunk(c):
        f = f_s[...]
        hidden = _silu(_dot(f, wg_s[c])) * _dot(f, wu_s[c])
        return _dot(hidden.astype(BF16), wd_s[c])

    def epilogue(acc):
        out = ymid_s[...] + gate_ref[0] * acc
        if final:
            out = _rmsnorm(out, fin_ref[...])
        if n_out == 1:
            o_refs[0][...] = out
        else:
            @pl.when(is_prompt)
            def _():
                o_refs[0][...] = out

            @pl.when(jnp.logical_not(is_prompt))
            def _():
                o_refs[1][...] = out

    @pl.when(s == 0)
    def _():
        wo_s[...] = wo_ref[...].astype(BF16)
        prologue()

    @pl.when(s < nf)
    def _():
        wg_s[s] = wg_ref[...].astype(BF16)
        wu_s[s] = wu_ref[...].astype(BF16)
        wd_s[s] = wd_ref[...].astype(BF16)
        part = chunk(s)

        @pl.when(s == 0)
        def _():
            acc_s[...] = part

        @pl.when(s > 0)
        def _():
            acc_s[...] += part

        @pl.when(s == nf - 1)
        def _():
            epilogue(acc_s[...])

    @pl.when(s >= nf)
    def _():
        prologue()
        acc = chunk(0)
        for c in range(1, nf):
            acc = acc + chunk(c)
        epilogue(acc)


def _ffn(stream, ap, as_, w_out, jl, norm_ffn, mods, layer, wg, wu, wd, final_norm, final):
    tm = 512
    nf = N_FF_CHUNKS
    npt = N_PROMPT // tm
    n_steps = nf + N_TOK // tm - 1
    n_out = 2 if final else 1
    tile_of = lambda s: jnp.maximum(s - (nf - 1), 0)
    chunk_of = lambda s: jnp.minimum(s, nf - 1)
    out_shape = _pair_shapes(D_MODEL, F32) if final else [jax.ShapeDtypeStruct((N_TOK, D_MODEL), F32)]
    return pl.pallas_call(
        functools.partial(_ffn_kernel, n_in=len(stream), n_out=n_out, n_prompt_tiles=npt, final=final),
        grid=(n_steps,),
        in_specs=_stream_specs(len(stream), tm, tile_of) + _pair_specs(tm, D_MODEL, tile_of) + [
            pl.BlockSpec((None, D_MODEL, D_MODEL), lambda s: (jl, 0, 0), pipeline_mode=pl.Buffered(1)),
            _mod_spec(layer, 2, tm, tile_of),
            pl.BlockSpec((None, 1, D_MODEL), lambda s: (layer, 0, 0)),
            _mod_spec(layer, 3, tm, tile_of),
            _mod_spec(layer, 4, tm, tile_of),
            _mod_spec(layer, 5, tm, tile_of),
            pl.BlockSpec((None, D_MODEL, FF_CHUNK), lambda s: (layer, 0, chunk_of(s))),
            pl.BlockSpec((None, D_MODEL, FF_CHUNK), lambda s: (layer, 0, chunk_of(s))),
            pl.BlockSpec((None, FF_CHUNK, D_MODEL), lambda s: (layer, chunk_of(s), 0)),
            pl.BlockSpec((1, D_MODEL), lambda s: (0, 0)),
        ],
        out_specs=_stream_specs(n_out, tm, tile_of),
        out_shape=out_shape,
        scratch_shapes=[
            pltpu.VMEM((D_MODEL, D_MODEL), BF16),
            pltpu.VMEM((nf, D_MODEL, FF_CHUNK), BF16),
            pltpu.VMEM((nf, D_MODEL, FF_CHUNK), BF16),
            pltpu.VMEM((nf, FF_CHUNK, D_MODEL), BF16),
            pltpu.VMEM((tm, D_MODEL), F32),
            pltpu.VMEM((tm, D_MODEL), BF16),
            pltpu.VMEM((tm, D_MODEL), F32),
        ],
        compiler_params=_params(1),
        name="ffn",
    )(*stream, ap, as_, w_out, mods, norm_ffn, mods, mods, mods, wg, wu, wd, final_norm.reshape(1, D_MODEL))


def kernel(x_prompt, x_sample, cache_k, cache_v, c, c_ctx, mod_w, mod_b, norm_mix, norm_ffn, hy_w_in, hy_conv_w, hy_conv_b, hy_f_w1, hy_f_b1, hy_f_freq, hy_f_w2, hy_f_b2, hy_f_w3, hy_bias, hy_w_out, at_w_qkv, at_q_norm, at_k_norm, at_w_out, ffn_w_gate, ffn_w_up, ffn_w_down, final_norm):
    stream = (x_prompt.reshape(N_PROMPT, D_MODEL), x_sample.reshape(N_SAMPLE, D_MODEL))
    cond = jnp.concatenate([c_ctx[None, :], c, jnp.zeros((MOD_ROWS - 1 - DEC_BATCH, D_MODEL), F32)], axis=0)
    mods = _adaln(cond, mod_w, mod_b)

    tables = {}
    for L in (SEQ, DEC_SEQ):
        fwd, inv = _dft_tables(L)
        tables[L] = (_cast_bf16(jnp.asarray(fwd), 2 * L // 4), _cast_bf16(jnp.asarray(inv), L // 4))

    norm_mix = norm_mix.reshape(DEPTH, 1, D_MODEL)
    norm_ffn = norm_ffn.reshape(DEPTH, 1, D_MODEL)
    f_w1 = jnp.pad(hy_f_w1, ((0, 0), (0, FEAT_PAD - FILTER_EMB), (0, 0)))
    f_b1 = hy_f_b1.reshape(-1, 1, FILTER_HIDDEN)
    f_b2 = hy_f_b2.reshape(-1, 1, FILTER_HIDDEN)
    bias = hy_bias.reshape(-1, 1, D_MODEL)
    q_gain = at_q_norm.reshape(-1, 1, HEAD_DIM)
    k_gain = at_k_norm.reshape(-1, 1, HEAD_DIM)
    cache_k = cache_k.reshape(DEC_BATCH, -1, PAST_LEN * N_KV_HEADS, HEAD_DIM)
    cache_v = cache_v.reshape(DEC_BATCH, -1, PAST_LEN * N_KV_HEADS, HEAD_DIM)

    new_kv = ()
    for layer in range(DEPTH):
        jl = layer // N_MIXERS
        if layer % N_MIXERS == 0:
            x0, u = _in_proj(stream, norm_mix, mods, layer, hy_w_in, hy_conv_w, hy_conv_b, jl)
            mixed = []
            filt = (f_w1, f_b1, hy_f_freq, hy_f_w2, f_b2, hy_f_w3)
            for L, row_block0, n_steps, seq_per_step, tn in (
                    (SEQ, 0, BATCH // 4, 4, D_MODEL),
                    (DEC_SEQ, N_PROMPT // DEC_SEQ, DEC_BATCH, 1, 512)):
                fwd_bf16, inv_bf16 = tables[L]
                mixed.append(_hyena_conv(u, x0, bias, jl, fwd_bf16, inv_bf16, filt, L, row_block0, n_steps,
                                         seq_per_step, tn))
            w_out = hy_w_out
        else:
            q, k, v, *new_kv = _qkv(stream[0], norm_mix, mods, layer, at_w_qkv, q_gain, k_gain, jl, new_kv)
            mixed = [_attention(q, k, v, 0, BATCH, SEQ, SEQ, N_KV_HEADS, req_per_step=4),
                     _attention(q, k, v, N_PROMPT, DEC_BATCH, DEC_SEQ, 512, N_KV_HEADS, cache_k=cache_k, cache_v=cache_v,
                                cache_layer=jl)]
            w_out = at_w_out
        stream = tuple(_ffn(stream, mixed[0], mixed[1], w_out, jl, norm_ffn, mods, layer,
                            ffn_w_gate, ffn_w_up, ffn_w_down, final_norm, final=(layer == DEPTH - 1)))

    y_prompt, y_sample = stream
    new_k, new_v = (a.reshape(BATCH, -1, SEQ, N_KV_HEADS, HEAD_DIM) for a in new_kv)
    return (y_prompt.reshape(BATCH, SEQ, D_MODEL), y_sample.reshape(DEC_BATCH, DEC_SEQ, D_MODEL), new_k, new_v)
```

```python
import functools
import math

import numpy as np
import jax
import jax.numpy as jnp
from jax import lax
from jax.experimental import pallas as pl
from jax.experimental.pallas import tpu as pltpu

D_MODEL = 1024
BATCH = 16
SEQ = 256
DEPTH = 4
DEC_BATCH = 4
DEC_SEQ = 1024
PAST_LEN = 512
GRID_W = 64
N_MIXERS = 2
HEAD_DIM = 128
N_HEADS = D_MODEL // HEAD_DIM
N_KV_HEADS = 2
GROUP = N_HEADS // N_KV_HEADS
KV_DIM = N_KV_HEADS * HEAD_DIM
HEAD_PAIR = 2 * HEAD_DIM
QKV_DIM = (N_HEADS + 2 * N_KV_HEADS) * HEAD_DIM
ROPE_THETA = 10000.0
D_FF = ((8 * D_MODEL + 3 * 256 - 1) // (3 * 256)) * 256
N_BANDS = 16
FILTER_EMB = 1 + 2 * N_BANDS
FILTER_HIDDEN = 64
MIN_DECAY = math.log(1e-2) / 1.5
MAX_DECAY = math.log(1e-2) / 0.3
EPS = 1e-6

N_PROMPT = BATCH * SEQ
N_SAMPLE = DEC_BATCH * DEC_SEQ
N_TOK = N_PROMPT + N_SAMPLE
N_MOD = 6
MOD_ROWS = 8
FEAT_PAD = 128
CONV_PAD = 8
FF_CHUNK = 256
N_FF_CHUNKS = D_FF // FF_CHUNK
V7X_VMEM_LIMIT = 56 * 1024 * 1024

F32 = jnp.float32
BF16 = jnp.bfloat16


def _params(n_axes, vmem=V7X_VMEM_LIMIT):
    return pltpu.CompilerParams(dimension_semantics=("arbitrary",) * n_axes, vmem_limit_bytes=vmem)


def _mod_row(tile, tm):
    n_prompt_tiles = N_PROMPT // tm
    tiles_per_req = DEC_SEQ // tm
    return jnp.where(tile < n_prompt_tiles, 0, 1 + jnp.maximum(tile - n_prompt_tiles, 0) // tiles_per_req)


def _mod_spec(layer, which, tm, tile_of=lambda i, *_: i):
    def index(*ids):
        return ((layer * N_MOD + which) * MOD_ROWS + _mod_row(tile_of(*ids), tm), 0, 0)
    return pl.BlockSpec((1, 1, D_MODEL), index)


def _pair_specs(tm, width, tile_of=lambda i, *_: i):
    npt = N_PROMPT // tm
    return [pl.BlockSpec((tm, width), lambda *ids: (jnp.minimum(tile_of(*ids), npt - 1), 0)),
            pl.BlockSpec((tm, width), lambda *ids: (jnp.maximum(tile_of(*ids) - npt, 0), 0))]


def _pair_shapes(width, dtype):
    return [jax.ShapeDtypeStruct((N_PROMPT, width), dtype), jax.ShapeDtypeStruct((N_SAMPLE, width), dtype)]


def _stream_specs(n_arrays, tm, tile_of=lambda i, *_: i):
    if n_arrays == 2:
        return _pair_specs(tm, D_MODEL, tile_of)
    return [pl.BlockSpec((tm, D_MODEL), lambda *ids: (tile_of(*ids), 0))]


def _read_stream(refs, is_prompt):
    if len(refs) == 2:
        return jnp.where(is_prompt, refs[0][...], refs[1][...])
    return refs[0][...]


def _split_bf16(a):
    hi = a.astype(BF16)
    lo = (a - hi.astype(F32)).astype(BF16)
    return hi, lo


def _dot(a, b):
    return jnp.dot(a, b, preferred_element_type=F32)


def _dot3(a, b):
    a_hi, a_lo = _split_bf16(a)
    b_hi, b_lo = _split_bf16(b)
    return _dot(a_hi, b_hi) + (_dot(a_lo, b_hi) + _dot(a_hi, b_lo))


def _silu(a):
    return a / (1.0 + jnp.exp(-a))


def _rmsnorm(x, g):
    return x * lax.rsqrt(jnp.mean(x * x, axis=-1, keepdims=True) + EPS) * g


def _modulate(x, g, shift, scale):
    return _rmsnorm(x, g) * (1.0 + scale) + shift


def _adaln_kernel(c_ref, w_ref, b_ref, o_ref):
    s = _silu(c_ref[...])
    m = _dot3(s, w_ref[0]) + b_ref[0]
    for v in range(m.shape[1] // D_MODEL):
        o_ref[v * MOD_ROWS:(v + 1) * MOD_ROWS, 0, :] = m[:, v * D_MODEL:(v + 1) * D_MODEL]


def _adaln(cond, mod_w, mod_b):
    per_step = 2
    tn = per_step * D_MODEL
    return pl.pallas_call(
        _adaln_kernel,
        grid=(DEPTH, N_MOD // per_step),
        in_specs=[
            pl.BlockSpec((MOD_ROWS, D_MODEL), lambda l, j: (0, 0)),
            pl.BlockSpec((1, D_MODEL, tn), lambda l, j: (l, 0, j)),
            pl.BlockSpec((1, 1, tn), lambda l, j: (l, 0, j)),
        ],
        out_specs=pl.BlockSpec((per_step * MOD_ROWS, 1, D_MODEL), lambda l, j: (l * (N_MOD // per_step) + j, 0, 0)),
        out_shape=jax.ShapeDtypeStruct((DEPTH * N_MOD * MOD_ROWS, 1, D_MODEL), F32),
        compiler_params=_params(2),
        name="adaln",
    )(cond, mod_w, mod_b.reshape(DEPTH, 1, N_MOD * D_MODEL))


def _cast_kernel(x_ref, o_ref):
    o_ref[...] = x_ref[...].astype(BF16)


def _cast_bf16(x, rows):
    m, n = x.shape
    return pl.pallas_call(
        _cast_kernel,
        grid=(m // rows,),
        in_specs=[pl.BlockSpec((rows, n), lambda i: (i, 0))],
        out_specs=pl.BlockSpec((rows, n), lambda i: (i, 0)),
        out_shape=jax.ShapeDtypeStruct((m, n), BF16),
        compiler_params=_params(1),
        name="cast_table",
    )(x)


def _in_proj_kernel(*refs, n_stream, n_prompt_tiles):
    x_refs, refs = refs[:n_stream], refs[n_stream:]
    (g_ref, sh_ref, sc_ref, w0_ref, w1_ref, w2_ref, cw0_ref, cw1_ref, cw2_ref, cb0_ref, cb1_ref, cb2_ref,
     x0_ref, u_ref, w_s) = refs
    i = pl.program_id(0)
    tm, tc = x0_ref.shape

    @pl.when(i == 0)
    def _():
        w_s[0] = w0_ref[...].astype(BF16)
        w_s[1] = w1_ref[...].astype(BF16)
        w_s[2] = w2_ref[...].astype(BF16)

    n_chain = tm // SEQ
    pad = jnp.zeros((CONV_PAD, tc), F32)

    def tile(seq_len):
        hs = []
        for b in range(n_chain):
            x = _read_stream([r.at[b * SEQ:(b + 1) * SEQ, :] for r in x_refs], i < n_prompt_tiles)
            hs.append(_modulate(x, g_ref[...], sh_ref[0], sc_ref[0]).astype(BF16))

        def short_conv(part, cw_ref, cb_ref):
            zs = [_dot(h, w_s[part]) for h in hs]
            out = []
            for b, z in enumerate(zs):
                before = zs[b - 1][SEQ - CONV_PAD:, :] if (b * SEQ) % seq_len else pad
                after = zs[b + 1][:CONV_PAD, :] if ((b + 1) * SEQ) % seq_len else pad
                ext = jnp.concatenate([before, z, after], axis=0)
                rows = ext.shape[0]
                z_prev = pltpu.roll(ext, 1, 0)[CONV_PAD:CONV_PAD + SEQ, :]
                z_next = pltpu.roll(ext, rows - 1, 0)[CONV_PAD:CONV_PAD + SEQ, :]
                out.append(z_prev * cw_ref[0:1, :] + z * cw_ref[1:2, :] + z_next * cw_ref[2:3, :] + cb_ref[...])
            return out

        for b, x0 in enumerate(short_conv(0, cw0_ref, cb0_ref)):
            x0_ref[b * SEQ:(b + 1) * SEQ, :] = x0.astype(BF16)
        for b, (x1, v) in enumerate(zip(short_conv(1, cw1_ref, cb1_ref), short_conv(2, cw2_ref, cb2_ref))):
            u_ref[b * SEQ:(b + 1) * SEQ, :] = (x1 * v).astype(BF16)

    @pl.when(i < n_prompt_tiles)
    def _():
        tile(SEQ)

    @pl.when(i >= n_prompt_tiles)
    def _():
        tile(DEC_SEQ)


def _in_proj(stream, norm_mix, mods, layer, w_in, conv_w, conv_b, jl):
    tm = 1024
    npt = N_PROMPT // tm

    def wspec(part):
        return pl.BlockSpec((None, D_MODEL, D_MODEL), lambda i: (jl, 0, part), pipeline_mode=pl.Buffered(1))

    def cwspec(part):
        return pl.BlockSpec((None, 3, D_MODEL), lambda i: (jl, 0, part))

    def cbspec(part):
        return pl.BlockSpec((None, 1, D_MODEL), lambda i: (jl, 0, part))

    out = pl.BlockSpec((tm, D_MODEL), lambda i: (i, 0))
    cb = conv_b.reshape(-1, 1, 3 * D_MODEL)
    return pl.pallas_call(
        functools.partial(_in_proj_kernel, n_stream=len(stream), n_prompt_tiles=npt),
        grid=(N_TOK // tm,),
        in_specs=_stream_specs(len(stream), tm) + [
            pl.BlockSpec((None, 1, D_MODEL), lambda i: (layer, 0, 0)),
            _mod_spec(layer, 0, tm),
            _mod_spec(layer, 1, tm),
            wspec(0), wspec(1), wspec(2),
            cwspec(0), cwspec(1), cwspec(2),
            cbspec(0), cbspec(1), cbspec(2),
        ],
        out_specs=[out, out],
        out_shape=[jax.ShapeDtypeStruct((N_TOK, D_MODEL), BF16)] * 2,
        scratch_shapes=[pltpu.VMEM((3, D_MODEL, D_MODEL), BF16)],
        compiler_params=_params(1),
        name="hyena_in_proj",
    )(*stream, norm_mix, mods, mods, w_in, w_in, w_in, conv_w, conv_w, conv_w, cb, cb, cb)


def _dft_tables(L):
    n = 2 * L
    k = np.arange(L, dtype=np.float64)[:, None]
    t = np.arange(L, dtype=np.float64)[None, :]
    ang = 2.0 * np.pi * k * t / n
    top = np.cos(ang)
    bot = -np.sin(ang)
    bot[0, :] = np.where(np.arange(L) % 2 == 0, 1.0, -1.0)
    fwd = np.concatenate([top, bot], axis=0)
    wk = np.full((L,), 2.0)
    wk[0] = 1.0
    inv_top = (np.cos(ang) * wk[:, None]).T / n
    inv_bot = (-2.0 * np.sin(ang)).T / n
    inv_bot[:, 0] = np.where(np.arange(L) % 2 == 0, 1.0, -1.0) / n
    inv = np.concatenate([inv_top, inv_bot], axis=1)
    return fwd.astype(np.float32), inv.astype(np.float32)


def _filter_feats(L):
    t = np.arange(L, dtype=np.float32) / np.float32(L)
    bands = np.arange(1, N_BANDS + 1, dtype=np.float32)
    ang = (np.float32(2.0 * math.pi) * t[:, None]) * bands[None, :]
    feats = np.concatenate([t[:, None], np.cos(ang), np.sin(ang)], axis=-1).astype(np.float32)
    return np.pad(feats, ((0, 0), (0, FEAT_PAD - FILTER_EMB)))


def _filter_spectrum(h, w3f_ref, w3b_ref, dl_ref, fwd_ref, L):
    tn = w3f_ref.shape[1]
    rows = lax.broadcasted_iota(jnp.int32, (L, tn), 0)
    t = rows.astype(F32) / L
    window = jnp.exp(-t * dl_ref[...])
    hf = _dot3(h, w3f_ref[...]) * window
    hb = jnp.where(rows > 0, _dot3(h, w3b_ref[...]) * window, 0.0)
    norm = jnp.sqrt(jnp.sum(hf * hf + hb * hb, axis=0, keepdims=True) + EPS)
    hf = hf / norm
    hb = hb / norm
    g_top = _dot(fwd_ref[0:L, :], (hf + hb).astype(BF16))
    g_bot = _dot(fwd_ref[L:2 * L, :], (hf - hb).astype(BF16))
    sign = jnp.where(rows % 2 == 0, 1.0, -1.0)
    nyquist_b = jnp.sum(sign * hb, axis=0, keepdims=True)
    return g_top, g_bot + jnp.where(rows == 0, 2.0 * nyquist_b, 0.0)


def _hyena_conv_kernel(u_ref, x0_ref, bias_ref, fwd_ref, inv_ref, feats_ref, w1_ref, b1_ref, fr_ref, w2_ref, b2_ref,
                       w3f_ref, w3b_ref, dl_ref, o_ref, h_s, g_s, *, L):
    j, b = pl.program_id(0), pl.program_id(1)

    @pl.when(jnp.logical_and(j == 0, b == 0))
    def _():
        def sin_rows(a):
            half = L // 2
            s = jnp.sin(jnp.concatenate([a[:half, :], a[half:, :]], axis=1))
            return jnp.concatenate([s[:, :FILTER_HIDDEN], s[:, FILTER_HIDDEN:]], axis=0)

        h1 = sin_rows(fr_ref[0:1, :] * (_dot3(feats_ref[...], w1_ref[...]) + b1_ref[...]))
        h_s[...] = sin_rows(fr_ref[1:2, :] * (_dot3(h1, w2_ref[...]) + b2_ref[...]))

    @pl.when(b == 0)
    def _():
        g_top, g_bot = _filter_spectrum(h_s[...], w3f_ref, w3b_ref, dl_ref, fwd_ref, L)
        g_s[0:L, :] = g_top
        g_s[L:2 * L, :] = g_bot

    tn = o_ref.shape[1]
    first = lax.broadcasted_iota(jnp.int32, (L, tn), 0) == 0
    for s in range(o_ref.shape[0] // L):
        rows = slice(s * L, (s + 1) * L)
        u = u_ref[rows, :]
        spec = _dot(fwd_ref[...], u)
        u_top, u_bot = spec[0:L, :], spec[L:2 * L, :]
        g_top, g_bot = g_s[0:L, :], g_s[L:2 * L, :]
        y_top = u_top * g_top - jnp.where(first, 0.0, u_bot * g_bot)
        y_bot = jnp.where(first, u_bot * g_bot, u_top * g_bot + u_bot * g_top)
        y_spec = jnp.concatenate([y_top, y_bot], axis=0).astype(BF16)
        y = _dot(inv_ref[...], y_spec)
        gated = x0_ref[rows, :].astype(F32) * (y + u.astype(F32) * bias_ref[...])
        o_ref[rows, :] = gated.astype(BF16)


def _hyena_conv(u, x0, bias, jl, fwd, inv, filt, L, row_block0, n_steps, seq_per_step, tn):
    nj = D_MODEL // tn
    rows = seq_per_step * L
    feats = jnp.asarray(_filter_feats(L))
    deltas = jnp.asarray(np.abs(np.linspace(MIN_DECAY, MAX_DECAY, D_MODEL, dtype=np.float32)).reshape(1, D_MODEL))
    act = pl.BlockSpec((rows, tn), lambda j, b: (row_block0 + b, j))
    const = lambda j, b: (0, 0)
    layer = lambda j, b: (jl, 0, 0)
    f_w1, f_b1, f_freq, f_w2, f_b2, f_w3 = filt
    return pl.pallas_call(
        functools.partial(_hyena_conv_kernel, L=L),
        grid=(nj, n_steps),
        in_specs=[
            act, act,
            pl.BlockSpec((None, 1, tn), lambda j, b: (jl, 0, j)),
            pl.BlockSpec((2 * L, L), const),
            pl.BlockSpec((L, 2 * L), const),
            pl.BlockSpec((L, FEAT_PAD), const),
            pl.BlockSpec((None, FEAT_PAD, FILTER_HIDDEN), layer),
            pl.BlockSpec((None, 1, FILTER_HIDDEN), layer),
            pl.BlockSpec((None, 2, FILTER_HIDDEN), layer),
            pl.BlockSpec((None, FILTER_HIDDEN, FILTER_HIDDEN), layer),
            pl.BlockSpec((None, 1, FILTER_HIDDEN), layer),
            pl.BlockSpec((None, FILTER_HIDDEN, tn), lambda j, b: (jl, 0, j)),
            pl.BlockSpec((None, FILTER_HIDDEN, tn), lambda j, b: (jl, 0, nj + j)),
            pl.BlockSpec((1, tn), lambda j, b: (0, j)),
        ],
        out_specs=pl.BlockSpec((rows, tn), lambda j, b: (b, j)),
        out_shape=jax.ShapeDtypeStruct((n_steps * rows, D_MODEL), BF16),
        scratch_shapes=[pltpu.VMEM((L, FILTER_HIDDEN), F32), pltpu.VMEM((2 * L, tn), F32)],
        compiler_params=_params(2),
        name="hyena_conv_%d" % L,
    )(u, x0, bias, fwd, inv, feats, f_w1, f_b1, f_freq, f_w2, f_b2, f_w3, f_w3, deltas)


def _rope_tables():
    rows = DEC_SEQ // GRID_W
    row = np.repeat(np.arange(rows), GRID_W).astype(np.float32)
    col = np.tile(np.arange(GRID_W), rows).astype(np.float32)
    half = HEAD_DIM // 2
    freqs = (np.float32(ROPE_THETA) ** (-np.arange(0, half, 2, dtype=np.float32) / np.float32(half))).astype(np.float32)
    ang = np.concatenate([row[:, None] * freqs[None, :], col[:, None] * freqs[None, :]], axis=-1)
    cos = np.repeat(np.cos(ang), 2, axis=-1)
    sin = np.repeat(np.sin(ang), 2, axis=-1)
    sign = np.where(np.arange(HEAD_DIM) % 2 == 0, -1.0, 1.0)[None, :]
    pair = lambda a: np.tile(a, (1, 2)).astype(np.float32)
    return pair(cos), pair(sin * sign)


def _qkv_kernel(*refs, n_prev, n_prompt_tiles):
    x_ref, g_ref, sh_ref, sc_ref, w_ref, qg_ref, kg_ref, cos_ref, sin_ref = refs[:9]
    prev_refs, refs = refs[9:9 + 2 * (n_prev > 0)], refs[9 + 2 * (n_prev > 0):]
    q_ref, k_ref, v_ref, kf_ref, vf_ref, w_s = refs
    i = pl.program_id(0)
    tm = q_ref.shape[0]

    @pl.when(i == 0)
    def _():
        w_s[...] = w_ref[...].astype(BF16)

    k_col = N_HEADS * HEAD_DIM
    v_col = k_col + KV_DIM
    q_gain = qg_ref[...] * HEAD_DIM ** -0.5

    def project(rows):
        h = _modulate(x_ref[rows, :], g_ref[...], sh_ref[0], sc_ref[0]).astype(BF16)
        qkv = _dot(h, w_s[...])
        v = qkv[:, v_col:]
        v_ref[rows, :] = v.astype(BF16)

        def pair_norm(col, gain):
            return jnp.concatenate([_rmsnorm(qkv[:, c:c + HEAD_DIM], gain) for c in (col, col + HEAD_DIM)], axis=1)

        return pair_norm, v

    @pl.when(i < n_prompt_tiles)
    def _():
        if n_prev:
            kf_ref[:, 0:n_prev] = prev_refs[0][...]
            vf_ref[:, 0:n_prev] = prev_refs[1][...]
        for r in range(tm // SEQ):
            rows = slice(r * SEQ, (r + 1) * SEQ)
            pair_norm, v = project(rows)
            for col in range(0, k_col, HEAD_PAIR):
                q_ref[rows, col:col + HEAD_PAIR] = pair_norm(col, q_gain).astype(BF16)
            k = pair_norm(k_col, kg_ref[...])
            k_ref[rows, :] = k.astype(BF16)
            for kv in range(N_KV_HEADS):
                cols = slice(kv * HEAD_DIM, (kv + 1) * HEAD_DIM)
                head_rows = pl.ds(kv, SEQ, stride=N_KV_HEADS)
                kf_ref[r, n_prev, head_rows, :] = k[:, cols]
                vf_ref[r, n_prev, head_rows, :] = v[:, cols]

    @pl.when(i >= n_prompt_tiles)
    def _():
        r_id = lax.broadcasted_iota(jnp.int32, (HEAD_PAIR, HEAD_PAIR), 0)
        col_id = lax.broadcasted_iota(jnp.int32, (HEAD_PAIR, HEAD_PAIR), 1)
        swap = jnp.where((r_id ^ 1) == col_id, 1.0, 0.0).astype(BF16)
        for r in range(tm // SEQ):
            rows = slice(r * SEQ, (r + 1) * SEQ)
            pair_norm, _ = project(rows)

            def rotate(x):
                return x * cos_ref[rows, :] + _dot(x.astype(BF16), swap) * sin_ref[rows, :]

            for col in range(0, k_col, HEAD_PAIR):
                q_ref[rows, col:col + HEAD_PAIR] = rotate(pair_norm(col, q_gain)).astype(BF16)
            k_ref[rows, :] = rotate(pair_norm(k_col, kg_ref[...])).astype(BF16)


def _qkv(y, norm_mix, mods, layer, w_qkv, q_gain, k_gain, jl, prev_kv):
    tm = 1024
    npt = N_PROMPT // tm
    tiles_per_req = DEC_SEQ // tm
    cos, sin = _rope_tables()
    rope_spec = pl.BlockSpec((tm, HEAD_PAIR), lambda i: (jnp.maximum(i - npt, 0) % tiles_per_req, 0))
    row = lambda n: pl.BlockSpec((tm, n), lambda i: (i, 0))
    cache_rows = SEQ * N_KV_HEADS
    cache = lambda n: pl.BlockSpec((tm // SEQ, n, cache_rows, HEAD_DIM), lambda i: (jnp.minimum(i, npt - 1), 0, 0, 0))
    gain = pl.BlockSpec((None, 1, HEAD_DIM), lambda i: (jl, 0, 0))
    return pl.pallas_call(
        functools.partial(_qkv_kernel, n_prev=jl, n_prompt_tiles=npt),
        grid=(N_TOK // tm,),
        in_specs=[
            row(D_MODEL),
            pl.BlockSpec((None, 1, D_MODEL), lambda i: (layer, 0, 0)),
            _mod_spec(layer, 0, tm),
            _mod_spec(layer, 1, tm),
            pl.BlockSpec((None, D_MODEL, QKV_DIM), lambda i: (jl, 0, 0), pipeline_mode=pl.Buffered(1)),
            gain, gain,
            rope_spec, rope_spec,
        ] + [cache(jl)] * len(prev_kv),
        out_specs=[row(D_MODEL), row(KV_DIM), row(KV_DIM), cache(jl + 1), cache(jl + 1)],
        out_shape=[
            jax.ShapeDtypeStruct((N_TOK, D_MODEL), BF16),
            jax.ShapeDtypeStruct((N_TOK, KV_DIM), BF16),
            jax.ShapeDtypeStruct((N_TOK, KV_DIM), BF16),
            jax.ShapeDtypeStruct((BATCH, jl + 1, cache_rows, HEAD_DIM), F32),
            jax.ShapeDtypeStruct((BATCH, jl + 1, cache_rows, HEAD_DIM), F32),
        ],
        scratch_shapes=[pltpu.VMEM((D_MODEL, QKV_DIM), BF16)],
        compiler_params=_params(1),
        name="qkv_proj",
    )(y, norm_mix, mods, mods, w_qkv, q_gain, k_gain, jnp.asarray(cos), jnp.asarray(sin), *prev_kv)


def _attention_kernel(*refs, with_cache, seq):
    if with_cache:
        q_ref, k_ref, v_ref, ck_ref, cv_ref, o_ref = refs
    else:
        q_ref, k_ref, v_ref, o_ref = refs
    n_req = k_ref.shape[0] // seq
    tq = q_ref.shape[0] // n_req
    for r in range(n_req):
        q_rows, k_rows = slice(r * tq, (r + 1) * tq), slice(r * seq, (r + 1) * seq)
        for kv in range(k_ref.shape[1] // HEAD_DIM):
            kv_cols = slice(kv * HEAD_DIM, (kv + 1) * HEAD_DIM)
            k = k_ref[k_rows, kv_cols]
            v = v_ref[k_rows, kv_cols]
            if with_cache:
                head_rows = pl.ds(kv, PAST_LEN, stride=N_KV_HEADS)
                k = jnp.concatenate([k, ck_ref[0, 0, head_rows, :].astype(BF16)], axis=0)
                v = jnp.concatenate([v, cv_ref[0, 0, head_rows, :].astype(BF16)], axis=0)
            v_ones = jnp.concatenate([v, jnp.ones_like(v)], axis=1)
            for g in range(GROUP):
                cols = slice((kv * GROUP + g) * HEAD_DIM, (kv * GROUP + g + 1) * HEAD_DIM)
                s = lax.dot_general(q_ref[q_rows, cols], k, (((1,), (1,)), ((), ())), preferred_element_type=F32)
                p = jnp.exp(s - jnp.max(s, axis=-1, keepdims=True))
                o = _dot(p.astype(BF16), v_ones)
                o_ref[q_rows, cols] = (o[:, :HEAD_DIM] / o[:, HEAD_DIM:]).astype(BF16)


def _attention(q, k, v, row0, n_req, seq, tq, kv_per_step, req_per_step=1, cache_k=None, cache_v=None,
               cache_layer=0):
    nq = seq // tq
    n_req //= req_per_step
    tq, seq_rows = tq * req_per_step, seq * req_per_step
    qb0, kb0 = row0 // tq, row0 // seq_rows
    qw, kw = kv_per_step * GROUP * HEAD_DIM, kv_per_step * HEAD_DIM
    in_specs = [
        pl.BlockSpec((tq, qw), lambda b, h, t: (qb0 + b * nq + t, h)),
        pl.BlockSpec((seq_rows, kw), lambda b, h, t: (kb0 + b, h)),
        pl.BlockSpec((seq_rows, kw), lambda b, h, t: (kb0 + b, h)),
    ]
    args = [q, k, v]
    with_cache = cache_k is not None
    if with_cache:
        assert kv_per_step == N_KV_HEADS
        cspec = pl.BlockSpec((1, 1, PAST_LEN * N_KV_HEADS, HEAD_DIM), lambda b, h, t: (b, cache_layer, 0, 0))
        in_specs += [cspec, cspec]
        args += [cache_k, cache_v]
    return pl.pallas_call(
        functools.partial(_attention_kernel, with_cache=with_cache, seq=seq),
        grid=(n_req, N_KV_HEADS // kv_per_step, nq),
        in_specs=in_specs,
        out_specs=pl.BlockSpec((tq, qw), lambda b, h, t: (b * nq + t, h)),
        out_shape=jax.ShapeDtypeStruct((n_req * seq_rows, D_MODEL), BF16),
        compiler_params=_params(3),
        name="attention_%d" % seq,
    )(*args)


def _ffn_kernel(*refs, n_in, n_out, n_prompt_tiles, final):
    y_refs, refs = refs[:n_in], refs[n_in:]
    (ap_ref, as_ref, wo_ref, gm_ref, g_ref, sh_ref, sc_ref, gate_ref, wg_ref, wu_ref, wd_ref, fin_ref) = refs[:12]
    o_refs, refs = refs[12:12 + n_out], refs[12 + n_out:]
    wo_s, wg_s, wu_s, wd_s, ymid_s, f_s, acc_s = refs
    s = pl.program_id(0)
    nf = N_FF_CHUNKS
    tile = jnp.maximum(s - (nf - 1), 0)
    is_prompt = tile < n_prompt_tiles

    def prologue():
        a = jnp.where(is_prompt, ap_ref[...], as_ref[...])
        y = _read_stream(y_refs, is_prompt)
        y_mid = y + gm_ref[0] * _dot(a, wo_s[...])
        ymid_s[...] = y_mid
        f_s[...] = _modulate(y_mid, g_ref[...], sh_ref[0], sc_ref[0]).astype(BF16)

    def chunk(c):
        f = f_s[...]
        hidden = _silu(_dot(f, wg_s[c])) * _dot(f, wu_s[c])
        return _dot(hidden.astype(BF16), wd_s[c])

    def epilogue(acc):
        out = ymid_s[...] + gate_ref[0] * acc
        if final:
            out = _rmsnorm(out, fin_ref[...])
        if n_out == 1:
            o_refs[0][...] = out
        else:
            @pl.when(is_prompt)
            def _():
                o_refs[0][...] = out

            @pl.when(jnp.logical_not(is_prompt))
            def _():
                o_refs[1][...] = out

    @pl.when(s == 0)
    def _():
        wo_s[...] = wo_ref[...].astype(BF16)
        prologue()

    @pl.when(s < nf)
    def _():
        wg_s[s] = wg_ref[...].astype(BF16)
        wu_s[s] = wu_ref[...].astype(BF16)
        wd_s[s] = wd_ref[...].astype(BF16)
        part = chunk(s)

        @pl.when(s == 0)
        def _():
            acc_s[...] = part

        @pl.when(s > 0)
        def _():
            acc_s[...] += part

        @pl.when(s == nf - 1)
        def _():
            epilogue(acc_s[...])

    @pl.when(s >= nf)
    def _():
        prologue()
        acc = chunk(0)
        for c in range(1, nf):
            acc = acc + chunk(c)
        epilogue(acc)


def _ffn(stream, ap, as_, w_out, jl, norm_ffn, mods, layer, wg, wu, wd, final_norm, final):
    tm = 512
    nf = N_FF_CHUNKS
    npt = N_PROMPT // tm
    n_steps = nf + N_TOK // tm - 1
    n_out = 2 if final else 1
    tile_of = lambda s: jnp.maximum(s - (nf - 1), 0)
    chunk_of = lambda s: jnp.minimum(s, nf - 1)
    out_shape = _pair_shapes(D_MODEL, F32) if final else [jax.ShapeDtypeStruct((N_TOK, D_MODEL), F32)]
    return pl.pallas_call(
        functools.partial(_ffn_kernel, n_in=len(stream), n_out=n_out, n_prompt_tiles=npt, final=final),
        grid=(n_steps,),
        in_specs=_stream_specs(len(stream), tm, tile_of) + _pair_specs(tm, D_MODEL, tile_of) + [
            pl.BlockSpec((None, D_MODEL, D_MODEL), lambda s: (jl, 0, 0), pipeline_mode=pl.Buffered(1)),
            _mod_spec(layer, 2, tm, tile_of),
            pl.BlockSpec((None, 1, D_MODEL), lambda s: (layer, 0, 0)),
            _mod_spec(layer, 3, tm, tile_of),
            _mod_spec(layer, 4, tm, tile_of),
            _mod_spec(layer, 5, tm, tile_of),
            pl.BlockSpec((None, D_MODEL, FF_CHUNK), lambda s: (layer, 0, chunk_of(s))),
            pl.BlockSpec((None, D_MODEL, FF_CHUNK), lambda s: (layer, 0, chunk_of(s))),
            pl.BlockSpec((None, FF_CHUNK, D_MODEL), lambda s: (layer, chunk_of(s), 0)),
            pl.BlockSpec((1, D_MODEL), lambda s: (0, 0)),
        ],
        out_specs=_stream_specs(n_out, tm, tile_of),
        out_shape=out_shape,
        scratch_shapes=[
            pltpu.VMEM((D_MODEL, D_MODEL), BF16),
            pltpu.VMEM((nf, D_MODEL, FF_CHUNK), BF16),
            pltpu.VMEM((nf, D_MODEL, FF_CHUNK), BF16),
            pltpu.VMEM((nf, FF_CHUNK, D_MODEL), BF16),
            pltpu.VMEM((tm, D_MODEL), F32),
            pltpu.VMEM((tm, D_MODEL), BF16),
            pltpu.VMEM((tm, D_MODEL), F32),
        ],
        compiler_params=_params(1),
        name="ffn",
    )(*stream, ap, as_, w_out, mods, norm_ffn, mods, mods, mods, wg, wu, wd, final_norm.reshape(1, D_MODEL))


def kernel(x_prompt, x_sample, cache_k, cache_v, c, c_ctx, mod_w, mod_b, norm_mix, norm_ffn, hy_w_in, hy_conv_w, hy_conv_b, hy_f_w1, hy_f_b1, hy_f_freq, hy_f_w2, hy_f_b2, hy_f_w3, hy_bias, hy_w_out, at_w_qkv, at_q_norm, at_k_norm, at_w_out, ffn_w_gate, ffn_w_up, ffn_w_down, final_norm):
    stream = (x_prompt.reshape(N_PROMPT, D_MODEL), x_sample.reshape(N_SAMPLE, D_MODEL))
    cond = jnp.concatenate([c_ctx[None, :], c, jnp.zeros((MOD_ROWS - 1 - DEC_BATCH, D_MODEL), F32)], axis=0)
    mods = _adaln(cond, mod_w, mod_b)

    tables = {}
    for L in (SEQ, DEC_SEQ):
        fwd, inv = _dft_tables(L)
        tables[L] = (jnp.asarray(fwd).astype(BF16), jnp.asarray(inv).astype(BF16))

    norm_mix = norm_mix.reshape(DEPTH, 1, D_MODEL)
    norm_ffn = norm_ffn.reshape(DEPTH, 1, D_MODEL)
    f_w1 = jnp.pad(hy_f_w1, ((0, 0), (0, FEAT_PAD - FILTER_EMB), (0, 0)))
    f_b1 = hy_f_b1.reshape(-1, 1, FILTER_HIDDEN)
    f_b2 = hy_f_b2.reshape(-1, 1, FILTER_HIDDEN)
    bias = hy_bias.reshape(-1, 1, D_MODEL)
    q_gain = at_q_norm.reshape(-1, 1, HEAD_DIM)
    k_gain = at_k_norm.reshape(-1, 1, HEAD_DIM)
    cache_k = cache_k.reshape(DEC_BATCH, -1, PAST_LEN * N_KV_HEADS, HEAD_DIM)
    cache_v = cache_v.reshape(DEC_BATCH, -1, PAST_LEN * N_KV_HEADS, HEAD_DIM)

    new_kv = ()
    for layer in range(DEPTH):
        jl = layer // N_MIXERS
        if layer % N_MIXERS == 0:
            x0, u = _in_proj(stream, norm_mix, mods, layer, hy_w_in, hy_conv_w, hy_conv_b, jl)
            mixed = []
            filt = (f_w1, f_b1, hy_f_freq, hy_f_w2, f_b2, hy_f_w3)
            for L, row_block0, n_steps, seq_per_step, tn in (
                    (SEQ, 0, BATCH // 4, 4, D_MODEL),
                    (DEC_SEQ, N_PROMPT // DEC_SEQ, DEC_BATCH, 1, 512)):
                fwd_bf16, inv_bf16 = tables[L]
                mixed.append(_hyena_conv(u, x0, bias, jl, fwd_bf16, inv_bf16, filt, L, row_block0, n_steps,
                                         seq_per_step, tn))
            w_out = hy_w_out
        else:
            q, k, v, *new_kv = _qkv(stream[0], norm_mix, mods, layer, at_w_qkv, q_gain, k_gain, jl, new_kv)
            mixed = [_attention(q, k, v, 0, BATCH, SEQ, SEQ, N_KV_HEADS, req_per_step=4),
                     _attention(q, k, v, N_PROMPT, DEC_BATCH, DEC_SEQ, 512, N_KV_HEADS, cache_k=cache_k, cache_v=cache_v,
                                cache_layer=jl)]
            w_out = at_w_out
        stream = tuple(_ffn(stream, mixed[0], mixed[1], w_out, jl, norm_ffn, mods, layer,
                            ffn_w_gate, ffn_w_up, ffn_w_down, final_norm, final=(layer == DEPTH - 1)))

    y_prompt, y_sample = stream
    new_k, new_v = (a.reshape(BATCH, -1, SEQ, N_KV_HEADS, HEAD_DIM) for a in new_kv)
    return (y_prompt.reshape(BATCH, SEQ, D_MODEL), y_sample.reshape(DEC_BATCH, DEC_SEQ, D_MODEL), new_k, new_v)
```

```python
import functools
import math

import numpy as np
import jax
import jax.numpy as jnp
from jax import lax
from jax.experimental import pallas as pl
from jax.experimental.pallas import tpu as pltpu

D_MODEL = 1024
BATCH = 16
SEQ = 256
DEPTH = 4
DEC_BATCH = 4
DEC_SEQ = 1024
PAST_LEN = 512
GRID_W = 64
N_MIXERS = 2
HEAD_DIM = 128
N_HEADS = D_MODEL // HEAD_DIM
N_KV_HEADS = 2
GROUP = N_HEADS // N_KV_HEADS
KV_DIM = N_KV_HEADS * HEAD_DIM
HEAD_PAIR = 2 * HEAD_DIM
QKV_DIM = (N_HEADS + 2 * N_KV_HEADS) * HEAD_DIM
ROPE_THETA = 10000.0
D_FF = ((8 * D_MODEL + 3 * 256 - 1) // (3 * 256)) * 256
N_BANDS = 16
FILTER_EMB = 1 + 2 * N_BANDS
FILTER_HIDDEN = 64
MIN_DECAY = math.log(1e-2) / 1.5
MAX_DECAY = math.log(1e-2) / 0.3
EPS = 1e-6

N_PROMPT = BATCH * SEQ
N_SAMPLE = DEC_BATCH * DEC_SEQ
N_TOK = N_PROMPT + N_SAMPLE
N_MOD = 6
MOD_ROWS = 8
FEAT_PAD = 128
CONV_PAD = 8
FF_CHUNK = 256
N_FF_CHUNKS = D_FF // FF_CHUNK
V7X_VMEM_LIMIT = 56 * 1024 * 1024

F32 = jnp.float32
BF16 = jnp.bfloat16


def _params(n_axes, vmem=V7X_VMEM_LIMIT):
    return pltpu.CompilerParams(dimension_semantics=("arbitrary",) * n_axes, vmem_limit_bytes=vmem)


def _mod_row(tile, tm):
    n_prompt_tiles = N_PROMPT // tm
    tiles_per_req = DEC_SEQ // tm
    return jnp.where(tile < n_prompt_tiles, 0, 1 + jnp.maximum(tile - n_prompt_tiles, 0) // tiles_per_req)


def _mod_spec(layer, which, tm, tile_of=lambda i, *_: i):
    def index(*ids):
        return ((layer * N_MOD + which) * MOD_ROWS + _mod_row(tile_of(*ids), tm), 0, 0)
    return pl.BlockSpec((1, 1, D_MODEL), index)


def _whole_spec(a):
    return pl.BlockSpec(a.shape, lambda *_: (0,) * a.ndim)


def _pair_specs(tm, width, tile_of=lambda i, *_: i):
    npt = N_PROMPT // tm
    return [pl.BlockSpec((tm, width), lambda *ids: (jnp.minimum(tile_of(*ids), npt - 1), 0)),
            pl.BlockSpec((tm, width), lambda *ids: (jnp.maximum(tile_of(*ids) - npt, 0), 0))]


def _pair_shapes(width, dtype):
    return [jax.ShapeDtypeStruct((N_PROMPT, width), dtype), jax.ShapeDtypeStruct((N_SAMPLE, width), dtype)]


def _stream_specs(n_arrays, tm, tile_of=lambda i, *_: i):
    if n_arrays == 2:
        return _pair_specs(tm, D_MODEL, tile_of)
    return [pl.BlockSpec((tm, D_MODEL), lambda *ids: (tile_of(*ids), 0))]


def _read_stream(refs, is_prompt):
    if len(refs) == 2:
        return jnp.where(is_prompt, refs[0][...], refs[1][...])
    return refs[0][...]


def _split_bf16(a):
    hi = a.astype(BF16)
    lo = (a - hi.astype(F32)).astype(BF16)
    return hi, lo


def _dot(a, b):
    return jnp.dot(a, b, preferred_element_type=F32)


def _dot3(a, b):
    a_hi, a_lo = _split_bf16(a)
    b_hi, b_lo = _split_bf16(b)
    return _dot(a_hi, b_hi) + (_dot(a_lo, b_hi) + _dot(a_hi, b_lo))


def _silu(a):
    return a / (1.0 + jnp.exp(-a))


def _rmsnorm(x, g):
    return x * lax.rsqrt(jnp.mean(x * x, axis=-1, keepdims=True) + EPS) * g


def _modulate(x, g, shift, scale):
    return _rmsnorm(x, g) * (1.0 + scale) + shift


def _adaln_kernel(cctx_ref, c_ref, w_ref, b_ref, o_ref):
    pad = jnp.zeros((MOD_ROWS - 1 - DEC_BATCH, D_MODEL), F32)
    s = _silu(jnp.concatenate([cctx_ref[...], c_ref[...], pad], axis=0))
    bias = b_ref[pl.ds(pl.program_id(0), 1), :]
    m = _dot3(s, w_ref[0]) + bias
    for v in range(m.shape[1] // D_MODEL):
        o_ref[v * MOD_ROWS:(v + 1) * MOD_ROWS, 0, :] = m[:, v * D_MODEL:(v + 1) * D_MODEL]


def _adaln(c_ctx, c, mod_w, mod_b):
    per_step = 2
    tn = per_step * D_MODEL
    return pl.pallas_call(
        _adaln_kernel,
        grid=(DEPTH, N_MOD // per_step),
        in_specs=[
            pl.BlockSpec((1, D_MODEL), lambda l, j: (0, 0)),
            pl.BlockSpec((DEC_BATCH, D_MODEL), lambda l, j: (0, 0)),
            pl.BlockSpec((1, D_MODEL, tn), lambda l, j: (l, 0, j)),
            pl.BlockSpec((DEPTH, tn), lambda l, j: (0, j)),
        ],
        out_specs=pl.BlockSpec((per_step * MOD_ROWS, 1, D_MODEL), lambda l, j: (l * (N_MOD // per_step) + j, 0, 0)),
        out_shape=jax.ShapeDtypeStruct((DEPTH * N_MOD * MOD_ROWS, 1, D_MODEL), F32),
        compiler_params=_params(2),
        name="adaln",
    )(c_ctx.reshape(1, D_MODEL), c, mod_w, mod_b)


def _cast_kernel(x_ref, o_ref):
    o_ref[...] = x_ref[...].astype(BF16)


def _cast_bf16(x, rows):
    m, n = x.shape
    return pl.pallas_call(
        _cast_kernel,
        grid=(m // rows,),
        in_specs=[pl.BlockSpec((rows, n), lambda i: (i, 0))],
        out_specs=pl.BlockSpec((rows, n), lambda i: (i, 0)),
        out_shape=jax.ShapeDtypeStruct((m, n), BF16),
        compiler_params=_params(1),
        name="cast_table",
    )(x)


def _in_proj_kernel(*refs, n_stream, n_prompt_tiles, layer, jl):
    x_refs, refs = refs[:n_stream], refs[n_stream:]
    (g_ref, sh_ref, sc_ref, w0_ref, w1_ref, w2_ref, cw0_ref, cw1_ref, cw2_ref, cb0_ref, cb1_ref, cb2_ref,
     x0_ref, u_ref, w_s) = refs
    i = pl.program_id(0)
    tm, tc = x0_ref.shape

    @pl.when(i == 0)
    def _():
        w_s[0] = w0_ref[...].astype(BF16)
        w_s[1] = w1_ref[...].astype(BF16)
        w_s[2] = w2_ref[...].astype(BF16)

    n_chain = tm // SEQ
    pad = jnp.zeros((CONV_PAD, tc), F32)

    def tile(seq_len):
        hs = []
        for b in range(n_chain):
            x = _read_stream([r.at[b * SEQ:(b + 1) * SEQ, :] for r in x_refs], i < n_prompt_tiles)
            hs.append(_modulate(x, g_ref[layer:layer + 1, :], sh_ref[0], sc_ref[0]).astype(BF16))

        def short_conv(part, cw_ref, cb_ref):
            zs = [_dot(h, w_s[part]) for h in hs]
            out = []
            for b, z in enumerate(zs):
                before = zs[b - 1][SEQ - CONV_PAD:, :] if (b * SEQ) % seq_len else pad
                after = zs[b + 1][:CONV_PAD, :] if ((b + 1) * SEQ) % seq_len else pad
                ext = jnp.concatenate([before, z, after], axis=0)
                rows = ext.shape[0]
                z_prev = pltpu.roll(ext, 1, 0)[CONV_PAD:CONV_PAD + SEQ, :]
                z_next = pltpu.roll(ext, rows - 1, 0)[CONV_PAD:CONV_PAD + SEQ, :]
                out.append(z_prev * cw_ref[0:1, :] + z * cw_ref[1:2, :] + z_next * cw_ref[2:3, :]
                           + cb_ref[jl:jl + 1, :])
            return out

        for b, x0 in enumerate(short_conv(0, cw0_ref, cb0_ref)):
            x0_ref[b * SEQ:(b + 1) * SEQ, :] = x0.astype(BF16)
        for b, (x1, v) in enumerate(zip(short_conv(1, cw1_ref, cb1_ref), short_conv(2, cw2_ref, cb2_ref))):
            u_ref[b * SEQ:(b + 1) * SEQ, :] = (x1 * v).astype(BF16)

    @pl.when(i < n_prompt_tiles)
    def _():
        tile(SEQ)

    @pl.when(i >= n_prompt_tiles)
    def _():
        tile(DEC_SEQ)


def _in_proj(stream, norm_mix, mods, layer, w_in, conv_w, conv_b, jl):
    tm = 1024
    npt = N_PROMPT // tm

    def wspec(part):
        return pl.BlockSpec((None, D_MODEL, D_MODEL), lambda i: (jl, 0, part), pipeline_mode=pl.Buffered(1))

    def cwspec(part):
        return pl.BlockSpec((None, 3, D_MODEL), lambda i: (jl, 0, part))

    def cbspec(part):
        return pl.BlockSpec((conv_b.shape[0], D_MODEL), lambda i: (0, part))

    out = pl.BlockSpec((tm, D_MODEL), lambda i: (i, 0))
    cb = conv_b
    return pl.pallas_call(
        functools.partial(_in_proj_kernel, n_stream=len(stream), n_prompt_tiles=npt, layer=layer, jl=jl),
        grid=(N_TOK // tm,),
        in_specs=_stream_specs(len(stream), tm) + [
            _whole_spec(norm_mix),
            _mod_spec(layer, 0, tm),
            _mod_spec(layer, 1, tm),
            wspec(0), wspec(1), wspec(2),
            cwspec(0), cwspec(1), cwspec(2),
            cbspec(0), cbspec(1), cbspec(2),
        ],
        out_specs=[out, out],
        out_shape=[jax.ShapeDtypeStruct((N_TOK, D_MODEL), BF16)] * 2,
        scratch_shapes=[pltpu.VMEM((3, D_MODEL, D_MODEL), BF16)],
        compiler_params=_params(1),
        name="hyena_in_proj",
    )(*stream, norm_mix, mods, mods, w_in, w_in, w_in, conv_w, conv_w, conv_w, cb, cb, cb)


def _dft_tables(L):
    n = 2 * L
    k = np.arange(L, dtype=np.float64)[:, None]
    t = np.arange(L, dtype=np.float64)[None, :]
    ang = 2.0 * np.pi * k * t / n
    top = np.cos(ang)
    bot = -np.sin(ang)
    bot[0, :] = np.where(np.arange(L) % 2 == 0, 1.0, -1.0)
    fwd = np.concatenate([top, bot], axis=0)
    wk = np.full((L,), 2.0)
    wk[0] = 1.0
    inv_top = (np.cos(ang) * wk[:, None]).T / n
    inv_bot = (-2.0 * np.sin(ang)).T / n
    inv_bot[:, 0] = np.where(np.arange(L) % 2 == 0, 1.0, -1.0) / n
    inv = np.concatenate([inv_top, inv_bot], axis=1)
    return fwd.astype(np.float32), inv.astype(np.float32)


def _filter_feats(L):
    t = np.arange(L, dtype=np.float32) / np.float32(L)
    bands = np.arange(1, N_BANDS + 1, dtype=np.float32)
    ang = (np.float32(2.0 * math.pi) * t[:, None]) * bands[None, :]
    feats = np.concatenate([t[:, None], np.cos(ang), np.sin(ang)], axis=-1).astype(np.float32)
    return np.pad(feats, ((0, 0), (0, FEAT_PAD - FILTER_EMB)))


def _filter_spectrum(h, w3f_ref, w3b_ref, dl_ref, fwd_ref, L):
    tn = w3f_ref.shape[1]
    rows = lax.broadcasted_iota(jnp.int32, (L, tn), 0)
    t = rows.astype(F32) / L
    window = jnp.exp(-t * dl_ref[...])
    hf = _dot3(h, w3f_ref[...]) * window
    hb = jnp.where(rows > 0, _dot3(h, w3b_ref[...]) * window, 0.0)
    norm = jnp.sqrt(jnp.sum(hf * hf + hb * hb, axis=0, keepdims=True) + EPS)
    hf = hf / norm
    hb = hb / norm
    g_top = _dot(fwd_ref[0:L, :], (hf + hb).astype(BF16))
    g_bot = _dot(fwd_ref[L:2 * L, :], (hf - hb).astype(BF16))
    sign = jnp.where(rows % 2 == 0, 1.0, -1.0)
    nyquist_b = jnp.sum(sign * hb, axis=0, keepdims=True)
    return g_top, g_bot + jnp.where(rows == 0, 2.0 * nyquist_b, 0.0)


def _hyena_conv_kernel(u_ref, x0_ref, bias_ref, fwd_ref, inv_ref, feats_ref, w1_ref, b1_ref, fr_ref, w2_ref, b2_ref,
                       w3f_ref, w3b_ref, dl_ref, o_ref, h_s, g_s, *, L, jl):
    j, b = pl.program_id(0), pl.program_id(1)
    layer_row = slice(jl, jl + 1)

    @pl.when(jnp.logical_and(j == 0, b == 0))
    def _():
        def sin_rows(a):
            half = L // 2
            s = jnp.sin(jnp.concatenate([a[:half, :], a[half:, :]], axis=1))
            return jnp.concatenate([s[:, :FILTER_HIDDEN], s[:, FILTER_HIDDEN:]], axis=0)

        h1 = sin_rows(fr_ref[0:1, :] * (_dot3(feats_ref[...], w1_ref[...]) + b1_ref[layer_row, :]))
        h_s[...] = sin_rows(fr_ref[1:2, :] * (_dot3(h1, w2_ref[...]) + b2_ref[layer_row, :]))

    @pl.when(b == 0)
    def _():
        g_top, g_bot = _filter_spectrum(h_s[...], w3f_ref, w3b_ref, dl_ref, fwd_ref, L)
        g_s[0:L, :] = g_top
        g_s[L:2 * L, :] = g_bot

    tn = o_ref.shape[1]
    first = lax.broadcasted_iota(jnp.int32, (L, tn), 0) == 0
    for s in range(o_ref.shape[0] // L):
        rows = slice(s * L, (s + 1) * L)
        u = u_ref[rows, :]
        spec = _dot(fwd_ref[...], u)
        u_top, u_bot = spec[0:L, :], spec[L:2 * L, :]
        g_top, g_bot = g_s[0:L, :], g_s[L:2 * L, :]
        y_top = u_top * g_top - jnp.where(first, 0.0, u_bot * g_bot)
        y_bot = jnp.where(first, u_bot * g_bot, u_top * g_bot + u_bot * g_top)
        y_spec = jnp.concatenate([y_top, y_bot], axis=0).astype(BF16)
        y = _dot(inv_ref[...], y_spec)
        gated = x0_ref[rows, :].astype(F32) * (y + u.astype(F32) * bias_ref[layer_row, :])
        o_ref[rows, :] = gated.astype(BF16)


def _hyena_conv(u, x0, bias, jl, fwd, inv, filt, L, row_block0, n_steps, seq_per_step, tn):
    nj = D_MODEL // tn
    rows = seq_per_step * L
    feats = jnp.asarray(_filter_feats(L))
    deltas = jnp.asarray(np.abs(np.linspace(MIN_DECAY, MAX_DECAY, D_MODEL, dtype=np.float32)).reshape(1, D_MODEL))
    act = pl.BlockSpec((rows, tn), lambda j, b: (row_block0 + b, j))
    const = lambda j, b: (0, 0)
    layer = lambda j, b: (jl, 0, 0)
    f_w1, f_b1, f_freq, f_w2, f_b2, f_w3 = filt
    return pl.pallas_call(
        functools.partial(_hyena_conv_kernel, L=L, jl=jl),
        grid=(nj, n_steps),
        in_specs=[
            act, act,
            pl.BlockSpec((bias.shape[0], tn), lambda j, b: (0, j)),
            pl.BlockSpec((2 * L, L), const),
            pl.BlockSpec((L, 2 * L), const),
            pl.BlockSpec((L, FEAT_PAD), const),
            pl.BlockSpec((None, FEAT_PAD, FILTER_HIDDEN), layer),
            _whole_spec(f_b1),
            pl.BlockSpec((None, 2, FILTER_HIDDEN), layer),
            pl.BlockSpec((None, FILTER_HIDDEN, FILTER_HIDDEN), layer),
            _whole_spec(f_b2),
            pl.BlockSpec((None, FILTER_HIDDEN, tn), lambda j, b: (jl, 0, j)),
            pl.BlockSpec((None, FILTER_HIDDEN, tn), lambda j, b: (jl, 0, nj + j)),
            pl.BlockSpec((1, tn), lambda j, b: (0, j)),
        ],
        out_specs=pl.BlockSpec((rows, tn), lambda j, b: (b, j)),
        out_shape=jax.ShapeDtypeStruct((n_steps * rows, D_MODEL), BF16),
        scratch_shapes=[pltpu.VMEM((L, FILTER_HIDDEN), F32), pltpu.VMEM((2 * L, tn), F32)],
        compiler_params=_params(2),
        name="hyena_conv_%d" % L,
    )(u, x0, bias, fwd, inv, feats, f_w1, f_b1, f_freq, f_w2, f_b2, f_w3, f_w3, deltas)


def _rope_tables():
    rows = DEC_SEQ // GRID_W
    row = np.repeat(np.arange(rows), GRID_W).astype(np.float32)
    col = np.tile(np.arange(GRID_W), rows).astype(np.float32)
    half = HEAD_DIM // 2
    freqs = (np.float32(ROPE_THETA) ** (-np.arange(0, half, 2, dtype=np.float32) / np.float32(half))).astype(np.float32)
    ang = np.concatenate([row[:, None] * freqs[None, :], col[:, None] * freqs[None, :]], axis=-1)
    cos = np.repeat(np.cos(ang), 2, axis=-1)
    sin = np.repeat(np.sin(ang), 2, axis=-1)
    sign = np.where(np.arange(HEAD_DIM) % 2 == 0, -1.0, 1.0)[None, :]
    pair = lambda a: np.tile(a, (1, 2)).astype(np.float32)
    return pair(cos), pair(sin * sign)


def _qkv_kernel(*refs, n_prev, n_prompt_tiles, layer):
    x_ref, g_ref, sh_ref, sc_ref, w_ref, qg_ref, kg_ref, cos_ref, sin_ref = refs[:9]
    prev_refs, refs = refs[9:9 + 2 * (n_prev > 0)], refs[9 + 2 * (n_prev > 0):]
    q_ref, k_ref, v_ref, kf_ref, vf_ref, w_s = refs
    i = pl.program_id(0)
    tm = q_ref.shape[0]

    @pl.when(i == 0)
    def _():
        w_s[...] = w_ref[...].astype(BF16)

    k_col = N_HEADS * HEAD_DIM
    v_col = k_col + KV_DIM
    jl = n_prev
    q_gain = qg_ref[jl:jl + 1, :] * HEAD_DIM ** -0.5
    k_gain = kg_ref[jl:jl + 1, :]

    def project(rows):
        h = _modulate(x_ref[rows, :], g_ref[layer:layer + 1, :], sh_ref[0], sc_ref[0]).astype(BF16)
        qkv = _dot(h, w_s[...])
        v = qkv[:, v_col:]
        v_ref[rows, :] = v.astype(BF16)

        def pair_norm(col, gain):
            return jnp.concatenate([_rmsnorm(qkv[:, c:c + HEAD_DIM], gain) for c in (col, col + HEAD_DIM)], axis=1)

        return pair_norm, v

    @pl.when(i < n_prompt_tiles)
    def _():
        if n_prev:
            kf_ref[:, 0:n_prev] = prev_refs[0][...]
            vf_ref[:, 0:n_prev] = prev_refs[1][...]
        for r in range(tm // SEQ):
            rows = slice(r * SEQ, (r + 1) * SEQ)
            pair_norm, v = project(rows)
            for col in range(0, k_col, HEAD_PAIR):
                q_ref[rows, col:col + HEAD_PAIR] = pair_norm(col, q_gain).astype(BF16)
            k = pair_norm(k_col, k_gain)
            k_ref[rows, :] = k.astype(BF16)
            for kv in range(N_KV_HEADS):
                cols = slice(kv * HEAD_DIM, (kv + 1) * HEAD_DIM)
                head_rows = pl.ds(kv, SEQ, stride=N_KV_HEADS)
                kf_ref[r, n_prev, head_rows, :] = k[:, cols]
                vf_ref[r, n_prev, head_rows, :] = v[:, cols]

    @pl.when(i >= n_prompt_tiles)
    def _():
        r_id = lax.broadcasted_iota(jnp.int32, (HEAD_PAIR, HEAD_PAIR), 0)
        col_id = lax.broadcasted_iota(jnp.int32, (HEAD_PAIR, HEAD_PAIR), 1)
        swap = jnp.where((r_id ^ 1) == col_id, 1.0, 0.0).astype(BF16)
        for r in range(tm // SEQ):
            rows = slice(r * SEQ, (r + 1) * SEQ)
            pair_norm, _ = project(rows)

            def rotate(x):
                return x * cos_ref[rows, :] + _dot(x.astype(BF16), swap) * sin_ref[rows, :]

            for col in range(0, k_col, HEAD_PAIR):
                q_ref[rows, col:col + HEAD_PAIR] = rotate(pair_norm(col, q_gain)).astype(BF16)
            k_ref[rows, :] = rotate(pair_norm(k_col, k_gain)).astype(BF16)


def _qkv(y, norm_mix, mods, layer, w_qkv, q_gain, k_gain, jl, prev_kv):
    tm = 1024
    npt = N_PROMPT // tm
    tiles_per_req = DEC_SEQ // tm
    cos, sin = _rope_tables()
    rope_spec = pl.BlockSpec((tm, HEAD_PAIR), lambda i: (jnp.maximum(i - npt, 0) % tiles_per_req, 0))
    row = lambda n: pl.BlockSpec((tm, n), lambda i: (i, 0))
    cache_rows = SEQ * N_KV_HEADS
    cache = lambda n: pl.BlockSpec((tm // SEQ, n, cache_rows, HEAD_DIM), lambda i: (jnp.minimum(i, npt - 1), 0, 0, 0))
    return pl.pallas_call(
        functools.partial(_qkv_kernel, n_prev=jl, n_prompt_tiles=npt, layer=layer),
        grid=(N_TOK // tm,),
        in_specs=[
            row(D_MODEL),
            _whole_spec(norm_mix),
            _mod_spec(layer, 0, tm),
            _mod_spec(layer, 1, tm),
            pl.BlockSpec((None, D_MODEL, QKV_DIM), lambda i: (jl, 0, 0), pipeline_mode=pl.Buffered(1)),
            _whole_spec(q_gain), _whole_spec(k_gain),
            rope_spec, rope_spec,
        ] + [cache(jl)] * len(prev_kv),
        out_specs=[row(D_MODEL), row(KV_DIM), row(KV_DIM), cache(jl + 1), cache(jl + 1)],
        out_shape=[
            jax.ShapeDtypeStruct((N_TOK, D_MODEL), BF16),
            jax.ShapeDtypeStruct((N_TOK, KV_DIM), BF16),
            jax.ShapeDtypeStruct((N_TOK, KV_DIM), BF16),
            jax.ShapeDtypeStruct((BATCH, jl + 1, cache_rows, HEAD_DIM), F32),
            jax.ShapeDtypeStruct((BATCH, jl + 1, cache_rows, HEAD_DIM), F32),
        ],
        scratch_shapes=[pltpu.VMEM((D_MODEL, QKV_DIM), BF16)],
        compiler_params=_params(1),
        name="qkv_proj",
    )(y, norm_mix, mods, mods, w_qkv, q_gain, k_gain, jnp.asarray(cos), jnp.asarray(sin), *prev_kv)


def _attention_kernel(*refs, with_cache, seq):
    if with_cache:
        q_ref, k_ref, v_ref, ck_ref, cv_ref, o_ref = refs
    else:
        q_ref, k_ref, v_ref, o_ref = refs
    n_req = k_ref.shape[0] // seq
    tq = q_ref.shape[0] // n_req
    for r in range(n_req):
        q_rows, k_rows = slice(r * tq, (r + 1) * tq), slice(r * seq, (r + 1) * seq)
        for kv in range(k_ref.shape[1] // HEAD_DIM):
            kv_cols = slice(kv * HEAD_DIM, (kv + 1) * HEAD_DIM)
            k = k_ref[k_rows, kv_cols]
            v = v_ref[k_rows, kv_cols]
            if with_cache:
                head_rows = pl.ds(kv, PAST_LEN, stride=N_KV_HEADS)
                k = jnp.concatenate([k, ck_ref[0, 0, head_rows, :].astype(BF16)], axis=0)
                v = jnp.concatenate([v, cv_ref[0, 0, head_rows, :].astype(BF16)], axis=0)
            v_ones = jnp.concatenate([v, jnp.ones_like(v)], axis=1)
            for g in range(GROUP):
                cols = slice((kv * GROUP + g) * HEAD_DIM, (kv * GROUP + g + 1) * HEAD_DIM)
                s = lax.dot_general(q_ref[q_rows, cols], k, (((1,), (1,)), ((), ())), preferred_element_type=F32)
                p = jnp.exp(s - jnp.max(s, axis=-1, keepdims=True))
                o = _dot(p.astype(BF16), v_ones)
                o_ref[q_rows, cols] = (o[:, :HEAD_DIM] / o[:, HEAD_DIM:]).astype(BF16)


def _attention(q, k, v, row0, n_req, seq, tq, kv_per_step, req_per_step=1, cache_k=None, cache_v=None,
               cache_layer=0):
    nq = seq // tq
    n_req //= req_per_step
    tq, seq_rows = tq * req_per_step, seq * req_per_step
    qb0, kb0 = row0 // tq, row0 // seq_rows
    qw, kw = kv_per_step * GROUP * HEAD_DIM, kv_per_step * HEAD_DIM
    in_specs = [
        pl.BlockSpec((tq, qw), lambda b, h, t: (qb0 + b * nq + t, h)),
        pl.BlockSpec((seq_rows, kw), lambda b, h, t: (kb0 + b, h)),
        pl.BlockSpec((seq_rows, kw), lambda b, h, t: (kb0 + b, h)),
    ]
    args = [q, k, v]
    with_cache = cache_k is not None
    if with_cache:
        assert kv_per_step == N_KV_HEADS
        cspec = pl.BlockSpec((1, 1, PAST_LEN * N_KV_HEADS, HEAD_DIM), lambda b, h, t: (b, cache_layer, 0, 0))
        in_specs += [cspec, cspec]
        args += [cache_k, cache_v]
    return pl.pallas_call(
        functools.partial(_attention_kernel, with_cache=with_cache, seq=seq),
        grid=(n_req, N_KV_HEADS // kv_per_step, nq),
        in_specs=in_specs,
        out_specs=pl.BlockSpec((tq, qw), lambda b, h, t: (b * nq + t, h)),
        out_shape=jax.ShapeDtypeStruct((n_req * seq_rows, D_MODEL), BF16),
        compiler_params=_params(3),
        name="attention_%d" % seq,
    )(*args)


def _ffn_kernel(*refs, n_in, n_out, n_prompt_tiles, final, layer):
    y_refs, refs = refs[:n_in], refs[n_in:]
    (ap_ref, as_ref, wo_ref, gm_ref, g_ref, sh_ref, sc_ref, gate_ref, wg_ref, wu_ref, wd_ref, fin_ref) = refs[:12]
    o_refs, refs = refs[12:12 + n_out], refs[12 + n_out:]
    wo_s, wg_s, wu_s, wd_s, ymid_s, f_s, acc_s = refs
    s = pl.program_id(0)
    nf = N_FF_CHUNKS
    tile = jnp.maximum(s - (nf - 1), 0)
    is_prompt = tile < n_prompt_tiles

    def prologue():
        a = jnp.where(is_prompt, ap_ref[...], as_ref[...])
        y = _read_stream(y_refs, is_prompt)
        y_mid = y + gm_ref[0] * _dot(a, wo_s[...])
        ymid_s[...] = y_mid
        f_s[...] = _modulate(y_mid, g_ref[layer:layer + 1, :], sh_ref[0], sc_ref[0]).astype(BF16)

    def chunk(c):
        f = f_s[...]
        hidden = _silu(_dot(f, wg_s[c])) * _dot(f, wu_s[c])
        return _dot(hidden.astype(BF16), wd_s[c])

    def epilogue(acc):
        out = ymid_s[...] + gate_ref[0] * acc
        if final:
            out = _rmsnorm(out, fin_ref[...])
        if n_out == 1:
            o_refs[0][...] = out
        else:
            @pl.when(is_prompt)
            def _():
                o_refs[0][...] = out

            @pl.when(jnp.logical_not(is_prompt))
            def _():
                o_refs[1][...] = out

    @pl.when(s == 0)
    def _():
        wo_s[...] = wo_ref[...].astype(BF16)
        prologue()

    @pl.when(s < nf)
    def _():
        wg_s[s] = wg_ref[...].astype(BF16)
        wu_s[s] = wu_ref[...].astype(BF16)
        wd_s[s] = wd_ref[...].astype(BF16)
        part = chunk(s)

        @pl.when(s == 0)
        def _():
            acc_s[...] = part

        @pl.when(s > 0)
        def _():
            acc_s[...] += part

        @pl.when(s == nf - 1)
        def _():
            epilogue(acc_s[...])

    @pl.when(s >= nf)
    def _():
        prologue()
        acc = chunk(0)
        for c in range(1, nf):
            acc = acc + chunk(c)
        epilogue(acc)


def _ffn(stream, ap, as_, w_out, jl, norm_ffn, mods, layer, wg, wu, wd, final_norm, final):
    tm = 512
    nf = N_FF_CHUNKS
    npt = N_PROMPT // tm
    n_steps = nf + N_TOK // tm - 1
    n_out = 2 if final else 1
    tile_of = lambda s: jnp.maximum(s - (nf - 1), 0)
    chunk_of = lambda s: jnp.minimum(s, nf - 1)
    out_shape = _pair_shapes(D_MODEL, F32) if final else [jax.ShapeDtypeStruct((N_TOK, D_MODEL), F32)]
    return pl.pallas_call(
        functools.partial(_ffn_kernel, n_in=len(stream), n_out=n_out, n_prompt_tiles=npt, final=final, layer=layer),
        grid=(n_steps,),
        in_specs=_stream_specs(len(stream), tm, tile_of) + _pair_specs(tm, D_MODEL, tile_of) + [
            pl.BlockSpec((None, D_MODEL, D_MODEL), lambda s: (jl, 0, 0), pipeline_mode=pl.Buffered(1)),
            _mod_spec(layer, 2, tm, tile_of),
            _whole_spec(norm_ffn),
            _mod_spec(layer, 3, tm, tile_of),
            _mod_spec(layer, 4, tm, tile_of),
            _mod_spec(layer, 5, tm, tile_of),
            pl.BlockSpec((None, D_MODEL, FF_CHUNK), lambda s: (layer, 0, chunk_of(s))),
            pl.BlockSpec((None, D_MODEL, FF_CHUNK), lambda s: (layer, 0, chunk_of(s))),
            pl.BlockSpec((None, FF_CHUNK, D_MODEL), lambda s: (layer, chunk_of(s), 0)),
            pl.BlockSpec((1, D_MODEL), lambda s: (0, 0)),
        ],
        out_specs=_stream_specs(n_out, tm, tile_of),
        out_shape=out_shape,
        scratch_shapes=[
            pltpu.VMEM((D_MODEL, D_MODEL), BF16),
            pltpu.VMEM((nf, D_MODEL, FF_CHUNK), BF16),
            pltpu.VMEM((nf, D_MODEL, FF_CHUNK), BF16),
            pltpu.VMEM((nf, FF_CHUNK, D_MODEL), BF16),
            pltpu.VMEM((tm, D_MODEL), F32),
            pltpu.VMEM((tm, D_MODEL), BF16),
            pltpu.VMEM((tm, D_MODEL), F32),
        ],
        compiler_params=_params(1),
        name="ffn",
    )(*stream, ap, as_, w_out, mods, norm_ffn, mods, mods, mods, wg, wu, wd, final_norm.reshape(1, D_MODEL))


def kernel(x_prompt, x_sample, cache_k, cache_v, c, c_ctx, mod_w, mod_b, norm_mix, norm_ffn, hy_w_in, hy_conv_w, hy_conv_b, hy_f_w1, hy_f_b1, hy_f_freq, hy_f_w2, hy_f_b2, hy_f_w3, hy_bias, hy_w_out, at_w_qkv, at_q_norm, at_k_norm, at_w_out, ffn_w_gate, ffn_w_up, ffn_w_down, final_norm):
    stream = (x_prompt.reshape(N_PROMPT, D_MODEL), x_sample.reshape(N_SAMPLE, D_MODEL))
    mods = _adaln(c_ctx, c, mod_w, mod_b)

    tables = {}
    for L in (SEQ, DEC_SEQ):
        fwd, inv = _dft_tables(L)
        tables[L] = (jnp.asarray(fwd).astype(BF16), jnp.asarray(inv).astype(BF16))

    f_w1 = jnp.pad(hy_f_w1, ((0, 0), (0, FEAT_PAD - FILTER_EMB), (0, 0)))
    f_b1, f_b2, bias, q_gain, k_gain = hy_f_b1, hy_f_b2, hy_bias, at_q_norm, at_k_norm
    cache_k = cache_k.reshape(DEC_BATCH, -1, PAST_LEN * N_KV_HEADS, HEAD_DIM)
    cache_v = cache_v.reshape(DEC_BATCH, -1, PAST_LEN * N_KV_HEADS, HEAD_DIM)

    new_kv = ()
    for layer in range(DEPTH):
        jl = layer // N_MIXERS
        if layer % N_MIXERS == 0:
            x0, u = _in_proj(stream, norm_mix, mods, layer, hy_w_in, hy_conv_w, hy_conv_b, jl)
            mixed = []
            filt = (f_w1, f_b1, hy_f_freq, hy_f_w2, f_b2, hy_f_w3)
            for L, row_block0, n_steps, seq_per_step, tn in (
                    (SEQ, 0, BATCH // 4, 4, D_MODEL),
                    (DEC_SEQ, N_PROMPT // DEC_SEQ, DEC_BATCH, 1, 512)):
                fwd_bf16, inv_bf16 = tables[L]
                mixed.append(_hyena_conv(u, x0, bias, jl, fwd_bf16, inv_bf16, filt, L, row_block0, n_steps,
                                         seq_per_step, tn))
            w_out = hy_w_out
        else:
            q, k, v, *new_kv = _qkv(stream[0], norm_mix, mods, layer, at_w_qkv, q_gain, k_gain, jl, new_kv)
            mixed = [_attention(q, k, v, 0, BATCH, SEQ, SEQ, N_KV_HEADS, req_per_step=4),
                     _attention(q, k, v, N_PROMPT, DEC_BATCH, DEC_SEQ, 512, N_KV_HEADS, cache_k=cache_k, cache_v=cache_v,
                                cache_layer=jl)]
            w_out = at_w_out
        stream = tuple(_ffn(stream, mixed[0], mixed[1], w_out, jl, norm_ffn, mods, layer,
                            ffn_w_gate, ffn_w_up, ffn_w_down, final_norm, final=(layer == DEPTH - 1)))

    y_prompt, y_sample = stream
    new_k, new_v = (a.reshape(BATCH, -1, SEQ, N_KV_HEADS, HEAD_DIM) for a in new_kv)
    return (y_prompt.reshape(BATCH, SEQ, D_MODEL), y_sample.reshape(DEC_BATCH, DEC_SEQ, D_MODEL), new_k, new_v)
```

```python
import functools
import math

import numpy as np
import jax
import jax.numpy as jnp
from jax import lax
from jax.experimental import pallas as pl
from jax.experimental.pallas import tpu as pltpu

D_MODEL = 1024
BATCH = 16
SEQ = 256
DEPTH = 4
DEC_BATCH = 4
DEC_SEQ = 1024
PAST_LEN = 512
GRID_W = 64
N_MIXERS = 2
HEAD_DIM = 128
N_HEADS = D_MODEL // HEAD_DIM
N_KV_HEADS = 2
GROUP = N_HEADS // N_KV_HEADS
KV_DIM = N_KV_HEADS * HEAD_DIM
HEAD_PAIR = 2 * HEAD_DIM
QKV_DIM = (N_HEADS + 2 * N_KV_HEADS) * HEAD_DIM
ROPE_THETA = 10000.0
D_FF = ((8 * D_MODEL + 3 * 256 - 1) // (3 * 256)) * 256
N_BANDS = 16
FILTER_EMB = 1 + 2 * N_BANDS
FILTER_HIDDEN = 64
MIN_DECAY = math.log(1e-2) / 1.5
MAX_DECAY = math.log(1e-2) / 0.3
EPS = 1e-6

N_PROMPT = BATCH * SEQ
N_SAMPLE = DEC_BATCH * DEC_SEQ
N_TOK = N_PROMPT + N_SAMPLE
N_MOD = 6
MOD_ROWS = 8
FEAT_PAD = 128
CONV_PAD = 8
FF_CHUNK = 256
N_FF_CHUNKS = D_FF // FF_CHUNK
V7X_VMEM_LIMIT = 56 * 1024 * 1024

F32 = jnp.float32
BF16 = jnp.bfloat16


def _params(n_axes, vmem=V7X_VMEM_LIMIT):
    return pltpu.CompilerParams(dimension_semantics=("arbitrary",) * n_axes, vmem_limit_bytes=vmem)


def _mod_row(tile, tm):
    n_prompt_tiles = N_PROMPT // tm
    tiles_per_req = DEC_SEQ // tm
    return jnp.where(tile < n_prompt_tiles, 0, 1 + jnp.maximum(tile - n_prompt_tiles, 0) // tiles_per_req)


def _mod_spec(layer, which, tm, tile_of=lambda i, *_: i):
    def index(*ids):
        return ((layer * N_MOD + which) * MOD_ROWS + _mod_row(tile_of(*ids), tm), 0, 0)
    return pl.BlockSpec((1, 1, D_MODEL), index)


def _whole_spec(a):
    return pl.BlockSpec(a.shape, lambda *_: (0,) * a.ndim)


def _pair_specs(tm, width, tile_of=lambda i, *_: i):
    npt = N_PROMPT // tm
    return [pl.BlockSpec((tm, width), lambda *ids: (jnp.minimum(tile_of(*ids), npt - 1), 0)),
            pl.BlockSpec((tm, width), lambda *ids: (jnp.maximum(tile_of(*ids) - npt, 0), 0))]


def _pair_shapes(width, dtype):
    return [jax.ShapeDtypeStruct((N_PROMPT, width), dtype), jax.ShapeDtypeStruct((N_SAMPLE, width), dtype)]


def _stream_specs(n_arrays, tm, tile_of=lambda i, *_: i):
    if n_arrays == 2:
        return _pair_specs(tm, D_MODEL, tile_of)
    return [pl.BlockSpec((tm, D_MODEL), lambda *ids: (tile_of(*ids), 0))]


def _read_stream(refs, is_prompt):
    if len(refs) == 2:
        return jnp.where(is_prompt, refs[0][...], refs[1][...])
    return refs[0][...]


def _split_bf16(a):
    hi = a.astype(BF16)
    lo = (a - hi.astype(F32)).astype(BF16)
    return hi, lo


def _dot(a, b):
    return jnp.dot(a, b, preferred_element_type=F32)


def _dot3(a, b):
    a_hi, a_lo = _split_bf16(a)
    b_hi, b_lo = _split_bf16(b)
    return _dot(a_hi, b_hi) + (_dot(a_lo, b_hi) + _dot(a_hi, b_lo))


def _silu(a):
    return a / (1.0 + jnp.exp(-a))


def _rmsnorm(x, g):
    return x * lax.rsqrt(jnp.mean(x * x, axis=-1, keepdims=True) + EPS) * g


def _modulate(x, g, shift, scale):
    return _rmsnorm(x, g) * (1.0 + scale) + shift


def _adaln_kernel(cctx_ref, c_ref, w_ref, b_ref, o_ref):
    pad = jnp.zeros((MOD_ROWS - 1 - DEC_BATCH, D_MODEL), F32)
    s = _silu(jnp.concatenate([cctx_ref[...], c_ref[...], pad], axis=0))
    bias = b_ref[pl.ds(pl.program_id(0), 1), :]
    m = _dot3(s, w_ref[0]) + bias
    for v in range(m.shape[1] // D_MODEL):
        o_ref[v * MOD_ROWS:(v + 1) * MOD_ROWS, 0, :] = m[:, v * D_MODEL:(v + 1) * D_MODEL]


def _adaln(c_ctx, c, mod_w, mod_b):
    per_step = 3
    tn = per_step * D_MODEL
    return pl.pallas_call(
        _adaln_kernel,
        grid=(DEPTH, N_MOD // per_step),
        in_specs=[
            pl.BlockSpec((1, D_MODEL), lambda l, j: (0, 0)),
            pl.BlockSpec((DEC_BATCH, D_MODEL), lambda l, j: (0, 0)),
            pl.BlockSpec((1, D_MODEL, tn), lambda l, j: (l, 0, j)),
            pl.BlockSpec((DEPTH, tn), lambda l, j: (0, j)),
        ],
        out_specs=pl.BlockSpec((per_step * MOD_ROWS, 1, D_MODEL), lambda l, j: (l * (N_MOD // per_step) + j, 0, 0)),
        out_shape=jax.ShapeDtypeStruct((DEPTH * N_MOD * MOD_ROWS, 1, D_MODEL), F32),
        compiler_params=_params(2),
        name="adaln",
    )(c_ctx.reshape(1, D_MODEL), c, mod_w, mod_b)


def _cast_kernel(x_ref, o_ref):
    o_ref[...] = x_ref[...].astype(BF16)


def _cast_bf16(x, rows):
    m, n = x.shape
    return pl.pallas_call(
        _cast_kernel,
        grid=(m // rows,),
        in_specs=[pl.BlockSpec((rows, n), lambda i: (i, 0))],
        out_specs=pl.BlockSpec((rows, n), lambda i: (i, 0)),
        out_shape=jax.ShapeDtypeStruct((m, n), BF16),
        compiler_params=_params(1),
        name="cast_table",
    )(x)


def _in_proj_kernel(*refs, n_stream, n_prompt_tiles, layer, jl):
    x_refs, refs = refs[:n_stream], refs[n_stream:]
    (g_ref, sh_ref, sc_ref, w0_ref, w1_ref, w2_ref, cw0_ref, cw1_ref, cw2_ref, cb0_ref, cb1_ref, cb2_ref,
     x0_ref, u_ref, w_s) = refs
    i = pl.program_id(0)
    tm, tc = x0_ref.shape

    @pl.when(i == 0)
    def _():
        w_s[0] = w0_ref[...].astype(BF16)
        w_s[1] = w1_ref[...].astype(BF16)
        w_s[2] = w2_ref[...].astype(BF16)

    n_chain = tm // SEQ
    pad = jnp.zeros((CONV_PAD, tc), F32)

    def tile(seq_len):
        hs = []
        for b in range(n_chain):
            x = _read_stream([r.at[b * SEQ:(b + 1) * SEQ, :] for r in x_refs], i < n_prompt_tiles)
            hs.append(_modulate(x, g_ref[layer:layer + 1, :], sh_ref[0], sc_ref[0]).astype(BF16))

        def short_conv(part, cw_ref, cb_ref):
            zs = [_dot(h, w_s[part]) for h in hs]
            out = []
            for b, z in enumerate(zs):
                before = zs[b - 1][SEQ - CONV_PAD:, :] if (b * SEQ) % seq_len else pad
                after = zs[b + 1][:CONV_PAD, :] if ((b + 1) * SEQ) % seq_len else pad
                ext = jnp.concatenate([before, z, after], axis=0)
                rows = ext.shape[0]
                z_prev = pltpu.roll(ext, 1, 0)[CONV_PAD:CONV_PAD + SEQ, :]
                z_next = pltpu.roll(ext, rows - 1, 0)[CONV_PAD:CONV_PAD + SEQ, :]
                out.append(z_prev * cw_ref[0:1, :] + z * cw_ref[1:2, :] + z_next * cw_ref[2:3, :]
                           + cb_ref[jl:jl + 1, :])
            return out

        for b, x0 in enumerate(short_conv(0, cw0_ref, cb0_ref)):
            x0_ref[b * SEQ:(b + 1) * SEQ, :] = x0.astype(BF16)
        for b, (x1, v) in enumerate(zip(short_conv(1, cw1_ref, cb1_ref), short_conv(2, cw2_ref, cb2_ref))):
            u_ref[b * SEQ:(b + 1) * SEQ, :] = (x1 * v).astype(BF16)

    @pl.when(i < n_prompt_tiles)
    def _():
        tile(SEQ)

    @pl.when(i >= n_prompt_tiles)
    def _():
        tile(DEC_SEQ)


def _in_proj(stream, norm_mix, mods, layer, w_in, conv_w, conv_b, jl):
    tm = 1024
    npt = N_PROMPT // tm

    def wspec(part):
        return pl.BlockSpec((None, D_MODEL, D_MODEL), lambda i: (jl, 0, part), pipeline_mode=pl.Buffered(1))

    def cwspec(part):
        return pl.BlockSpec((None, 3, D_MODEL), lambda i: (jl, 0, part))

    def cbspec(part):
        return pl.BlockSpec((conv_b.shape[0], D_MODEL), lambda i: (0, part))

    out = pl.BlockSpec((tm, D_MODEL), lambda i: (i, 0))
    cb = conv_b
    return pl.pallas_call(
        functools.partial(_in_proj_kernel, n_stream=len(stream), n_prompt_tiles=npt, layer=layer, jl=jl),
        grid=(N_TOK // tm,),
        in_specs=_stream_specs(len(stream), tm) + [
            _whole_spec(norm_mix),
            _mod_spec(layer, 0, tm),
            _mod_spec(layer, 1, tm),
            wspec(0), wspec(1), wspec(2),
            cwspec(0), cwspec(1), cwspec(2),
            cbspec(0), cbspec(1), cbspec(2),
        ],
        out_specs=[out, out],
        out_shape=[jax.ShapeDtypeStruct((N_TOK, D_MODEL), BF16)] * 2,
        scratch_shapes=[pltpu.VMEM((3, D_MODEL, D_MODEL), BF16)],
        compiler_params=_params(1),
        name="hyena_in_proj",
    )(*stream, norm_mix, mods, mods, w_in, w_in, w_in, conv_w, conv_w, conv_w, cb, cb, cb)


def _dft_tables(L):
    n = 2 * L
    k = np.arange(L, dtype=np.float64)[:, None]
    t = np.arange(L, dtype=np.float64)[None, :]
    ang = 2.0 * np.pi * k * t / n
    top = np.cos(ang)
    bot = -np.sin(ang)
    bot[0, :] = np.where(np.arange(L) % 2 == 0, 1.0, -1.0)
    fwd = np.concatenate([top, bot], axis=0)
    wk = np.full((L,), 2.0)
    wk[0] = 1.0
    inv_top = (np.cos(ang) * wk[:, None]).T / n
    inv_bot = (-2.0 * np.sin(ang)).T / n
    inv_bot[:, 0] = np.where(np.arange(L) % 2 == 0, 1.0, -1.0) / n
    inv = np.concatenate([inv_top, inv_bot], axis=1)
    return fwd.astype(np.float32), inv.astype(np.float32)


def _filter_feats(L):
    t = np.arange(L, dtype=np.float32) / np.float32(L)
    bands = np.arange(1, N_BANDS + 1, dtype=np.float32)
    ang = (np.float32(2.0 * math.pi) * t[:, None]) * bands[None, :]
    feats = np.concatenate([t[:, None], np.cos(ang), np.sin(ang)], axis=-1).astype(np.float32)
    return np.pad(feats, ((0, 0), (0, FEAT_PAD - FILTER_EMB)))


def _filter_spectrum(h, w3f_ref, w3b_ref, dl_ref, fwd_ref, L):
    tn = w3f_ref.shape[1]
    rows = lax.broadcasted_iota(jnp.int32, (L, tn), 0)
    t = rows.astype(F32) / L
    window = jnp.exp(-t * dl_ref[...])
    hf = _dot3(h, w3f_ref[...]) * window
    hb = jnp.where(rows > 0, _dot3(h, w3b_ref[...]) * window, 0.0)
    norm = jnp.sqrt(jnp.sum(hf * hf + hb * hb, axis=0, keepdims=True) + EPS)
    hf = hf / norm
    hb = hb / norm
    g_top = _dot(fwd_ref[0:L, :], (hf + hb).astype(BF16))
    g_bot = _dot(fwd_ref[L:2 * L, :], (hf - hb).astype(BF16))
    sign = jnp.where(rows % 2 == 0, 1.0, -1.0)
    nyquist_b = jnp.sum(sign * hb, axis=0, keepdims=True)
    return g_top, g_bot + jnp.where(rows == 0, 2.0 * nyquist_b, 0.0)


def _hyena_conv_kernel(u_ref, x0_ref, bias_ref, fwd_ref, inv_ref, feats_ref, w1_ref, b1_ref, fr_ref, w2_ref, b2_ref,
                       w3f_ref, w3b_ref, dl_ref, o_ref, h_s, g_s, *, L, jl):
    j, b = pl.program_id(0), pl.program_id(1)
    layer_row = slice(jl, jl + 1)

    @pl.when(jnp.logical_and(j == 0, b == 0))
    def _():
        def sin_rows(a):
            half = L // 2
            s = jnp.sin(jnp.concatenate([a[:half, :], a[half:, :]], axis=1))
            return jnp.concatenate([s[:, :FILTER_HIDDEN], s[:, FILTER_HIDDEN:]], axis=0)

        h1 = sin_rows(fr_ref[0:1, :] * (_dot3(feats_ref[...], w1_ref[...]) + b1_ref[layer_row, :]))
        h_s[...] = sin_rows(fr_ref[1:2, :] * (_dot3(h1, w2_ref[...]) + b2_ref[layer_row, :]))

    @pl.when(b == 0)
    def _():
        g_top, g_bot = _filter_spectrum(h_s[...], w3f_ref, w3b_ref, dl_ref, fwd_ref, L)
        g_s[0:L, :] = g_top
        g_s[L:2 * L, :] = g_bot

    tn = o_ref.shape[1]
    first = lax.broadcasted_iota(jnp.int32, (L, tn), 0) == 0
    for s in range(o_ref.shape[0] // L):
        rows = slice(s * L, (s + 1) * L)
        u = u_ref[rows, :]
        spec = _dot(fwd_ref[...], u)
        u_top, u_bot = spec[0:L, :], spec[L:2 * L, :]
        g_top, g_bot = g_s[0:L, :], g_s[L:2 * L, :]
        y_top = u_top * g_top - jnp.where(first, 0.0, u_bot * g_bot)
        y_bot = jnp.where(first, u_bot * g_bot, u_top * g_bot + u_bot * g_top)
        y_spec = jnp.concatenate([y_top, y_bot], axis=0).astype(BF16)
        y = _dot(inv_ref[...], y_spec)
        gated = x0_ref[rows, :].astype(F32) * (y + u.astype(F32) * bias_ref[layer_row, :])
        o_ref[rows, :] = gated.astype(BF16)


def _hyena_conv(u, x0, bias, jl, fwd, inv, filt, L, row_block0, n_steps, seq_per_step, tn):
    nj = D_MODEL // tn
    rows = seq_per_step * L
    feats = jnp.asarray(_filter_feats(L))
    deltas = jnp.asarray(np.abs(np.linspace(MIN_DECAY, MAX_DECAY, D_MODEL, dtype=np.float32)).reshape(1, D_MODEL))
    act = pl.BlockSpec((rows, tn), lambda j, b: (row_block0 + b, j))
    const = lambda j, b: (0, 0)
    layer = lambda j, b: (jl, 0, 0)
    f_w1, f_b1, f_freq, f_w2, f_b2, f_w3 = filt
    return pl.pallas_call(
        functools.partial(_hyena_conv_kernel, L=L, jl=jl),
        grid=(nj, n_steps),
        in_specs=[
            act, act,
            pl.BlockSpec((bias.shape[0], tn), lambda j, b: (0, j)),
            pl.BlockSpec((2 * L, L), const),
            pl.BlockSpec((L, 2 * L), const),
            pl.BlockSpec((L, FEAT_PAD), const),
            pl.BlockSpec((None, FEAT_PAD, FILTER_HIDDEN), layer),
            _whole_spec(f_b1),
            pl.BlockSpec((None, 2, FILTER_HIDDEN), layer),
            pl.BlockSpec((None, FILTER_HIDDEN, FILTER_HIDDEN), layer),
            _whole_spec(f_b2),
            pl.BlockSpec((None, FILTER_HIDDEN, tn), lambda j, b: (jl, 0, j)),
            pl.BlockSpec((None, FILTER_HIDDEN, tn), lambda j, b: (jl, 0, nj + j)),
            pl.BlockSpec((1, tn), lambda j, b: (0, j)),
        ],
        out_specs=pl.BlockSpec((rows, tn), lambda j, b: (b, j)),
        out_shape=jax.ShapeDtypeStruct((n_steps * rows, D_MODEL), BF16),
        scratch_shapes=[pltpu.VMEM((L, FILTER_HIDDEN), F32), pltpu.VMEM((2 * L, tn), F32)],
        compiler_params=_params(2),
        name="hyena_conv_%d" % L,
    )(u, x0, bias, fwd, inv, feats, f_w1, f_b1, f_freq, f_w2, f_b2, f_w3, f_w3, deltas)


def _rope_tables():
    rows = DEC_SEQ // GRID_W
    row = np.repeat(np.arange(rows), GRID_W).astype(np.float32)
    col = np.tile(np.arange(GRID_W), rows).astype(np.float32)
    half = HEAD_DIM // 2
    freqs = (np.float32(ROPE_THETA) ** (-np.arange(0, half, 2, dtype=np.float32) / np.float32(half))).astype(np.float32)
    ang = np.concatenate([row[:, None] * freqs[None, :], col[:, None] * freqs[None, :]], axis=-1)
    cos = np.repeat(np.cos(ang), 2, axis=-1)
    sin = np.repeat(np.sin(ang), 2, axis=-1)
    sign = np.where(np.arange(HEAD_DIM) % 2 == 0, -1.0, 1.0)[None, :]
    pair = lambda a: np.tile(a, (1, 2)).astype(np.float32)
    return pair(cos), pair(sin * sign)


def _qkv_kernel(*refs, n_prev, n_prompt_tiles, layer):
    x_ref, g_ref, sh_ref, sc_ref, w_ref, qg_ref, kg_ref, cos_ref, sin_ref = refs[:9]
    prev_refs, refs = refs[9:9 + 2 * (n_prev > 0)], refs[9 + 2 * (n_prev > 0):]
    q_ref, k_ref, v_ref, kf_ref, vf_ref, w_s = refs
    i = pl.program_id(0)
    tm = q_ref.shape[0]

    @pl.when(i == 0)
    def _():
        w_s[...] = w_ref[...].astype(BF16)

    k_col = N_HEADS * HEAD_DIM
    v_col = k_col + KV_DIM
    jl = n_prev
    q_gain = qg_ref[jl:jl + 1, :] * HEAD_DIM ** -0.5
    k_gain = kg_ref[jl:jl + 1, :]

    def project(rows):
        h = _modulate(x_ref[rows, :], g_ref[layer:layer + 1, :], sh_ref[0], sc_ref[0]).astype(BF16)
        qkv = _dot(h, w_s[...])
        v = qkv[:, v_col:]
        v_ref[rows, :] = v.astype(BF16)

        def pair_norm(col, gain):
            return jnp.concatenate([_rmsnorm(qkv[:, c:c + HEAD_DIM], gain) for c in (col, col + HEAD_DIM)], axis=1)

        return pair_norm, v

    @pl.when(i < n_prompt_tiles)
    def _():
        if n_prev:
            kf_ref[:, 0:n_prev] = prev_refs[0][...]
            vf_ref[:, 0:n_prev] = prev_refs[1][...]
        for r in range(tm // SEQ):
            rows = slice(r * SEQ, (r + 1) * SEQ)
            pair_norm, v = project(rows)
            for col in range(0, k_col, HEAD_PAIR):
                q_ref[rows, col:col + HEAD_PAIR] = pair_norm(col, q_gain).astype(BF16)
            k = pair_norm(k_col, k_gain)
            k_ref[rows, :] = k.astype(BF16)
            for kv in range(N_KV_HEADS):
                cols = slice(kv * HEAD_DIM, (kv + 1) * HEAD_DIM)
                head_rows = pl.ds(kv, SEQ, stride=N_KV_HEADS)
                kf_ref[r, n_prev, head_rows, :] = k[:, cols]
                vf_ref[r, n_prev, head_rows, :] = v[:, cols]

    @pl.when(i >= n_prompt_tiles)
    def _():
        r_id = lax.broadcasted_iota(jnp.int32, (HEAD_PAIR, HEAD_PAIR), 0)
        col_id = lax.broadcasted_iota(jnp.int32, (HEAD_PAIR, HEAD_PAIR), 1)
        swap = jnp.where((r_id ^ 1) == col_id, 1.0, 0.0).astype(BF16)
        for r in range(tm // SEQ):
            rows = slice(r * SEQ, (r + 1) * SEQ)
            pair_norm, _ = project(rows)

            def rotate(x):
                return x * cos_ref[rows, :] + _dot(x.astype(BF16), swap) * sin_ref[rows, :]

            for col in range(0, k_col, HEAD_PAIR):
                q_ref[rows, col:col + HEAD_PAIR] = rotate(pair_norm(col, q_gain)).astype(BF16)
            k_ref[rows, :] = rotate(pair_norm(k_col, k_gain)).astype(BF16)


def _qkv(y, norm_mix, mods, layer, w_qkv, q_gain, k_gain, jl, prev_kv):
    tm = 1024
    npt = N_PROMPT // tm
    tiles_per_req = DEC_SEQ // tm
    cos, sin = _rope_tables()
    rope_spec = pl.BlockSpec((tm, HEAD_PAIR), lambda i: (jnp.maximum(i - npt, 0) % tiles_per_req, 0))
    row = lambda n: pl.BlockSpec((tm, n), lambda i: (i, 0))
    cache_rows = SEQ * N_KV_HEADS
    cache = lambda n: pl.BlockSpec((tm // SEQ, n, cache_rows, HEAD_DIM), lambda i: (jnp.minimum(i, npt - 1), 0, 0, 0))
    return pl.pallas_call(
        functools.partial(_qkv_kernel, n_prev=jl, n_prompt_tiles=npt, layer=layer),
        grid=(N_TOK // tm,),
        in_specs=[
            row(D_MODEL),
            _whole_spec(norm_mix),
            _mod_spec(layer, 0, tm),
            _mod_spec(layer, 1, tm),
            pl.BlockSpec((None, D_MODEL, QKV_DIM), lambda i: (jl, 0, 0), pipeline_mode=pl.Buffered(1)),
            _whole_spec(q_gain), _whole_spec(k_gain),
            rope_spec, rope_spec,
        ] + [cache(jl)] * len(prev_kv),
        out_specs=[row(D_MODEL), row(KV_DIM), row(KV_DIM), cache(jl + 1), cache(jl + 1)],
        out_shape=[
            jax.ShapeDtypeStruct((N_TOK, D_MODEL), BF16),
            jax.ShapeDtypeStruct((N_TOK, KV_DIM), BF16),
            jax.ShapeDtypeStruct((N_TOK, KV_DIM), BF16),
            jax.ShapeDtypeStruct((BATCH, jl + 1, cache_rows, HEAD_DIM), F32),
            jax.ShapeDtypeStruct((BATCH, jl + 1, cache_rows, HEAD_DIM), F32),
        ],
        scratch_shapes=[pltpu.VMEM((D_MODEL, QKV_DIM), BF16)],
        compiler_params=_params(1),
        name="qkv_proj",
    )(y, norm_mix, mods, mods, w_qkv, q_gain, k_gain, jnp.asarray(cos), jnp.asarray(sin), *prev_kv)


def _attention_kernel(*refs, with_cache, seq):
    if with_cache:
        q_ref, k_ref, v_ref, ck_ref, cv_ref, o_ref = refs
    else:
        q_ref, k_ref, v_ref, o_ref = refs
    n_req = k_ref.shape[0] // seq
    tq = q_ref.shape[0] // n_req
    for r in range(n_req):
        q_rows, k_rows = slice(r * tq, (r + 1) * tq), slice(r * seq, (r + 1) * seq)
        for kv in range(k_ref.shape[1] // HEAD_DIM):
            kv_cols = slice(kv * HEAD_DIM, (kv + 1) * HEAD_DIM)
            k = k_ref[k_rows, kv_cols]
            v = v_ref[k_rows, kv_cols]
            if with_cache:
                head_rows = pl.ds(kv, PAST_LEN, stride=N_KV_HEADS)
                k = jnp.concatenate([k, ck_ref[0, 0, head_rows, :].astype(BF16)], axis=0)
                v = jnp.concatenate([v, cv_ref[0, 0, head_rows, :].astype(BF16)], axis=0)
            v_ones = jnp.concatenate([v, jnp.ones_like(v)], axis=1)
            for g in range(GROUP):
                cols = slice((kv * GROUP + g) * HEAD_DIM, (kv * GROUP + g + 1) * HEAD_DIM)
                s = lax.dot_general(q_ref[q_rows, cols], k, (((1,), (1,)), ((), ())), preferred_element_type=F32)
                p = jnp.exp(s - jnp.max(s, axis=-1, keepdims=True))
                o = _dot(p.astype(BF16), v_ones)
                o_ref[q_rows, cols] = (o[:, :HEAD_DIM] / o[:, HEAD_DIM:]).astype(BF16)


def _attention(q, k, v, row0, n_req, seq, tq, kv_per_step, req_per_step=1, cache_k=None, cache_v=None,
               cache_layer=0):
    nq = seq // tq
    n_req //= req_per_step
    tq, seq_rows = tq * req_per_step, seq * req_per_step
    qb0, kb0 = row0 // tq, row0 // seq_rows
    qw, kw = kv_per_step * GROUP * HEAD_DIM, kv_per_step * HEAD_DIM
    in_specs = [
        pl.BlockSpec((tq, qw), lambda b, h, t: (qb0 + b * nq + t, h)),
        pl.BlockSpec((seq_rows, kw), lambda b, h, t: (kb0 + b, h)),
        pl.BlockSpec((seq_rows, kw), lambda b, h, t: (kb0 + b, h)),
    ]
    args = [q, k, v]
    with_cache = cache_k is not None
    if with_cache:
        assert kv_per_step == N_KV_HEADS
        cspec = pl.BlockSpec((1, 1, PAST_LEN * N_KV_HEADS, HEAD_DIM), lambda b, h, t: (b, cache_layer, 0, 0))
        in_specs += [cspec, cspec]
        args += [cache_k, cache_v]
    return pl.pallas_call(
        functools.partial(_attention_kernel, with_cache=with_cache, seq=seq),
        grid=(n_req, N_KV_HEADS // kv_per_step, nq),
        in_specs=in_specs,
        out_specs=pl.BlockSpec((tq, qw), lambda b, h, t: (b * nq + t, h)),
        out_shape=jax.ShapeDtypeStruct((n_req * seq_rows, D_MODEL), BF16),
        compiler_params=_params(3),
        name="attention_%d" % seq,
    )(*args)


def _ffn_kernel(*refs, n_in, n_out, n_prompt_tiles, final, layer):
    y_refs, refs = refs[:n_in], refs[n_in:]
    (ap_ref, as_ref, wo_ref, gm_ref, g_ref, sh_ref, sc_ref, gate_ref, wg_ref, wu_ref, wd_ref, fin_ref) = refs[:12]
    o_refs, refs = refs[12:12 + n_out], refs[12 + n_out:]
    wo_s, wg_s, wu_s, wd_s, ymid_s, f_s, acc_s = refs
    s = pl.program_id(0)
    nf = N_FF_CHUNKS
    tile = jnp.maximum(s - (nf - 1), 0)
    is_prompt = tile < n_prompt_tiles

    def prologue():
        a = jnp.where(is_prompt, ap_ref[...], as_ref[...])
        y = _read_stream(y_refs, is_prompt)
        y_mid = y + gm_ref[0] * _dot(a, wo_s[...])
        ymid_s[...] = y_mid
        f_s[...] = _modulate(y_mid, g_ref[layer:layer + 1, :], sh_ref[0], sc_ref[0]).astype(BF16)

    def chunk(c):
        f = f_s[...]
        hidden = _silu(_dot(f, wg_s[c])) * _dot(f, wu_s[c])
        return _dot(hidden.astype(BF16), wd_s[c])

    def epilogue(acc):
        out = ymid_s[...] + gate_ref[0] * acc
        if final:
            out = _rmsnorm(out, fin_ref[...])
        if n_out == 1:
            o_refs[0][...] = out
        else:
            @pl.when(is_prompt)
            def _():
                o_refs[0][...] = out

            @pl.when(jnp.logical_not(is_prompt))
            def _():
                o_refs[1][...] = out

    @pl.when(s == 0)
    def _():
        wo_s[...] = wo_ref[...].astype(BF16)
        prologue()

    @pl.when(s < nf)
    def _():
        wg_s[s] = wg_ref[...].astype(BF16)
        wu_s[s] = wu_ref[...].astype(BF16)
        wd_s[s] = wd_ref[...].astype(BF16)
        part = chunk(s)

        @pl.when(s == 0)
        def _():
            acc_s[...] = part

        @pl.when(s > 0)
        def _():
            acc_s[...] += part

        @pl.when(s == nf - 1)
        def _():
            epilogue(acc_s[...])

    @pl.when(s >= nf)
    def _():
        prologue()
        acc = chunk(0)
        for c in range(1, nf):
            acc = acc + chunk(c)
        epilogue(acc)


def _ffn(stream, ap, as_, w_out, jl, norm_ffn, mods, layer, wg, wu, wd, final_norm, final):
    tm = 512
    nf = N_FF_CHUNKS
    npt = N_PROMPT // tm
    n_steps = nf + N_TOK // tm - 1
    n_out = 2 if final else 1
    tile_of = lambda s: jnp.maximum(s - (nf - 1), 0)
    chunk_of = lambda s: jnp.minimum(s, nf - 1)
    out_shape = _pair_shapes(D_MODEL, F32) if final else [jax.ShapeDtypeStruct((N_TOK, D_MODEL), F32)]
    return pl.pallas_call(
        functools.partial(_ffn_kernel, n_in=len(stream), n_out=n_out, n_prompt_tiles=npt, final=final, layer=layer),
        grid=(n_steps,),
        in_specs=_stream_specs(len(stream), tm, tile_of) + _pair_specs(tm, D_MODEL, tile_of) + [
            pl.BlockSpec((None, D_MODEL, D_MODEL), lambda s: (jl, 0, 0), pipeline_mode=pl.Buffered(1)),
            _mod_spec(layer, 2, tm, tile_of),
            _whole_spec(norm_ffn),
            _mod_spec(layer, 3, tm, tile_of),
            _mod_spec(layer, 4, tm, tile_of),
            _mod_spec(layer, 5, tm, tile_of),
            pl.BlockSpec((None, D_MODEL, FF_CHUNK), lambda s: (layer, 0, chunk_of(s))),
            pl.BlockSpec((None, D_MODEL, FF_CHUNK), lambda s: (layer, 0, chunk_of(s))),
            pl.BlockSpec((None, FF_CHUNK, D_MODEL), lambda s: (layer, chunk_of(s), 0)),
            pl.BlockSpec((1, D_MODEL), lambda s: (0, 0)),
        ],
        out_specs=_stream_specs(n_out, tm, tile_of),
        out_shape=out_shape,
        scratch_shapes=[
            pltpu.VMEM((D_MODEL, D_MODEL), BF16),
            pltpu.VMEM((nf, D_MODEL, FF_CHUNK), BF16),
            pltpu.VMEM((nf, D_MODEL, FF_CHUNK), BF16),
            pltpu.VMEM((nf, FF_CHUNK, D_MODEL), BF16),
            pltpu.VMEM((tm, D_MODEL), F32),
            pltpu.VMEM((tm, D_MODEL), BF16),
            pltpu.VMEM((tm, D_MODEL), F32),
        ],
        compiler_params=_params(1),
        name="ffn",
    )(*stream, ap, as_, w_out, mods, norm_ffn, mods, mods, mods, wg, wu, wd, final_norm.reshape(1, D_MODEL))


def kernel(x_prompt, x_sample, cache_k, cache_v, c, c_ctx, mod_w, mod_b, norm_mix, norm_ffn, hy_w_in, hy_conv_w, hy_conv_b, hy_f_w1, hy_f_b1, hy_f_freq, hy_f_w2, hy_f_b2, hy_f_w3, hy_bias, hy_w_out, at_w_qkv, at_q_norm, at_k_norm, at_w_out, ffn_w_gate, ffn_w_up, ffn_w_down, final_norm):
    stream = (x_prompt.reshape(N_PROMPT, D_MODEL), x_sample.reshape(N_SAMPLE, D_MODEL))
    mods = _adaln(c_ctx, c, mod_w, mod_b)

    tables = {}
    for L in (SEQ, DEC_SEQ):
        fwd, inv = _dft_tables(L)
        tables[L] = (jnp.asarray(fwd).astype(BF16), jnp.asarray(inv).astype(BF16))

    f_w1 = jnp.pad(hy_f_w1, ((0, 0), (0, FEAT_PAD - FILTER_EMB), (0, 0)))
    f_b1, f_b2, bias, q_gain, k_gain = hy_f_b1, hy_f_b2, hy_bias, at_q_norm, at_k_norm
    cache_k = cache_k.reshape(DEC_BATCH, -1, PAST_LEN * N_KV_HEADS, HEAD_DIM)
    cache_v = cache_v.reshape(DEC_BATCH, -1, PAST_LEN * N_KV_HEADS, HEAD_DIM)

    new_kv = ()
    for layer in range(DEPTH):
        jl = layer // N_MIXERS
        if layer % N_MIXERS == 0:
            x0, u = _in_proj(stream, norm_mix, mods, layer, hy_w_in, hy_conv_w, hy_conv_b, jl)
            mixed = []
            filt = (f_w1, f_b1, hy_f_freq, hy_f_w2, f_b2, hy_f_w3)
            for L, row_block0, n_steps, seq_per_step, tn in (
                    (SEQ, 0, BATCH // 4, 4, D_MODEL),
                    (DEC_SEQ, N_PROMPT // DEC_SEQ, DEC_BATCH, 1, 512)):
                fwd_bf16, inv_bf16 = tables[L]
                mixed.append(_hyena_conv(u, x0, bias, jl, fwd_bf16, inv_bf16, filt, L, row_block0, n_steps,
                                         seq_per_step, tn))
            w_out = hy_w_out
        else:
            q, k, v, *new_kv = _qkv(stream[0], norm_mix, mods, layer, at_w_qkv, q_gain, k_gain, jl, new_kv)
            mixed = [_attention(q, k, v, 0, BATCH, SEQ, SEQ, N_KV_HEADS, req_per_step=4),
                     _attention(q, k, v, N_PROMPT, DEC_BATCH, DEC_SEQ, 1024, N_KV_HEADS, cache_k=cache_k, cache_v=cache_v,
                                cache_layer=jl)]
            w_out = at_w_out
        stream = tuple(_ffn(stream, mixed[0], mixed[1], w_out, jl, norm_ffn, mods, layer,
                            ffn_w_gate, ffn_w_up, ffn_w_down, final_norm, final=(layer == DEPTH - 1)))

    y_prompt, y_sample = stream
    new_k, new_v = (a.reshape(BATCH, -1, SEQ, N_KV_HEADS, HEAD_DIM) for a in new_kv)
    return (y_prompt.reshape(BATCH, SEQ, D_MODEL), y_sample.reshape(DEC_BATCH, DEC_SEQ, D_MODEL), new_k, new_v)
```

```python
import functools
import math

import numpy as np
import jax
import jax.numpy as jnp
from jax import lax
from jax.experimental import pallas as pl
from jax.experimental.pallas import tpu as pltpu

D_MODEL = 1024
BATCH = 16
SEQ = 256
DEPTH = 4
DEC_BATCH = 4
DEC_SEQ = 1024
PAST_LEN = 512
GRID_W = 64
N_MIXERS = 2
HEAD_DIM = 128
N_HEADS = D_MODEL // HEAD_DIM
N_KV_HEADS = 2
GROUP = N_HEADS // N_KV_HEADS
KV_DIM = N_KV_HEADS * HEAD_DIM
HEAD_PAIR = 2 * HEAD_DIM
QKV_DIM = (N_HEADS + 2 * N_KV_HEADS) * HEAD_DIM
ROPE_THETA = 10000.0
D_FF = ((8 * D_MODEL + 3 * 256 - 1) // (3 * 256)) * 256
N_BANDS = 16
FILTER_EMB = 1 + 2 * N_BANDS
FILTER_HIDDEN = 64
MIN_DECAY = math.log(1e-2) / 1.5
MAX_DECAY = math.log(1e-2) / 0.3
EPS = 1e-6

N_PROMPT = BATCH * SEQ
N_SAMPLE = DEC_BATCH * DEC_SEQ
N_TOK = N_PROMPT + N_SAMPLE
N_MOD = 6
MOD_ROWS = 8
FEAT_PAD = 128
CONV_PAD = 8
FF_CHUNK = 256
N_FF_CHUNKS = D_FF // FF_CHUNK
V7X_VMEM_LIMIT = 56 * 1024 * 1024

F32 = jnp.float32
BF16 = jnp.bfloat16


def _params(n_axes, vmem=V7X_VMEM_LIMIT):
    return pltpu.CompilerParams(dimension_semantics=("arbitrary",) * n_axes, vmem_limit_bytes=vmem)


def _mod_row(tile, tm):
    n_prompt_tiles = N_PROMPT // tm
    tiles_per_req = DEC_SEQ // tm
    return jnp.where(tile < n_prompt_tiles, 0, 1 + jnp.maximum(tile - n_prompt_tiles, 0) // tiles_per_req)


def _mod_spec(layer, which, tm, tile_of=lambda i, *_: i):
    def index(*ids):
        return ((layer * N_MOD + which) * MOD_ROWS + _mod_row(tile_of(*ids), tm), 0, 0)
    return pl.BlockSpec((1, 1, D_MODEL), index)


def _whole_spec(a):
    return pl.BlockSpec(a.shape, lambda *_: (0,) * a.ndim)


def _pair_specs(tm, width, tile_of=lambda i, *_: i):
    npt = N_PROMPT // tm
    return [pl.BlockSpec((tm, width), lambda *ids: (jnp.minimum(tile_of(*ids), npt - 1), 0)),
            pl.BlockSpec((tm, width), lambda *ids: (jnp.maximum(tile_of(*ids) - npt, 0), 0))]


def _pair_shapes(width, dtype):
    return [jax.ShapeDtypeStruct((N_PROMPT, width), dtype), jax.ShapeDtypeStruct((N_SAMPLE, width), dtype)]


def _stream_specs(n_arrays, tm, tile_of=lambda i, *_: i):
    if n_arrays == 2:
        return _pair_specs(tm, D_MODEL, tile_of)
    return [pl.BlockSpec((tm, D_MODEL), lambda *ids: (tile_of(*ids), 0))]


def _read_stream(refs, is_prompt):
    if len(refs) == 2:
        return jnp.where(is_prompt, refs[0][...], refs[1][...])
    return refs[0][...]


def _split_bf16(a):
    hi = a.astype(BF16)
    lo = (a - hi.astype(F32)).astype(BF16)
    return hi, lo


def _dot(a, b):
    return jnp.dot(a, b, preferred_element_type=F32)


def _dot3(a, b):
    a_hi, a_lo = _split_bf16(a)
    b_hi, b_lo = _split_bf16(b)
    return _dot(a_hi, b_hi) + (_dot(a_lo, b_hi) + _dot(a_hi, b_lo))


def _silu(a):
    return a / (1.0 + jnp.exp(-a))


def _rmsnorm(x, g):
    return x * lax.rsqrt(jnp.mean(x * x, axis=-1, keepdims=True) + EPS) * g


def _modulate(x, g, shift, scale):
    return _rmsnorm(x, g) * (1.0 + scale) + shift


def _adaln_kernel(cctx_ref, c_ref, w_ref, b_ref, o_ref):
    pad = jnp.zeros((MOD_ROWS - 1 - DEC_BATCH, D_MODEL), F32)
    s = _silu(jnp.concatenate([cctx_ref[...], c_ref[...], pad], axis=0))
    bias = b_ref[pl.ds(pl.program_id(0), 1), :]
    m = _dot3(s, w_ref[0]) + bias
    for v in range(m.shape[1] // D_MODEL):
        o_ref[v * MOD_ROWS:(v + 1) * MOD_ROWS, 0, :] = m[:, v * D_MODEL:(v + 1) * D_MODEL]


def _adaln(c_ctx, c, mod_w, mod_b):
    per_step = 3
    tn = per_step * D_MODEL
    return pl.pallas_call(
        _adaln_kernel,
        grid=(DEPTH, N_MOD // per_step),
        in_specs=[
            pl.BlockSpec((1, D_MODEL), lambda l, j: (0, 0)),
            pl.BlockSpec((DEC_BATCH, D_MODEL), lambda l, j: (0, 0)),
            pl.BlockSpec((1, D_MODEL, tn), lambda l, j: (l, 0, j)),
            pl.BlockSpec((DEPTH, tn), lambda l, j: (0, j)),
        ],
        out_specs=pl.BlockSpec((per_step * MOD_ROWS, 1, D_MODEL), lambda l, j: (l * (N_MOD // per_step) + j, 0, 0)),
        out_shape=jax.ShapeDtypeStruct((DEPTH * N_MOD * MOD_ROWS, 1, D_MODEL), F32),
        compiler_params=_params(2),
        name="adaln",
    )(c_ctx.reshape(1, D_MODEL), c, mod_w, mod_b)


def _cast_kernel(x_ref, o_ref):
    o_ref[...] = x_ref[...].astype(BF16)


def _cast_bf16(x, rows):
    m, n = x.shape
    return pl.pallas_call(
        _cast_kernel,
        grid=(m // rows,),
        in_specs=[pl.BlockSpec((rows, n), lambda i: (i, 0))],
        out_specs=pl.BlockSpec((rows, n), lambda i: (i, 0)),
        out_shape=jax.ShapeDtypeStruct((m, n), BF16),
        compiler_params=_params(1),
        name="cast_table",
    )(x)


def _in_proj_kernel(*refs, n_stream, n_prompt_tiles, layer, jl):
    x_refs, refs = refs[:n_stream], refs[n_stream:]
    (g_ref, sh_ref, sc_ref, w0_ref, w1_ref, w2_ref, cw0_ref, cw1_ref, cw2_ref, cb0_ref, cb1_ref, cb2_ref,
     x0_ref, u_ref, w_s) = refs
    i = pl.program_id(0)
    tm, tc = x0_ref.shape

    @pl.when(i == 0)
    def _():
        w_s[0] = w0_ref[...].astype(BF16)
        w_s[1] = w1_ref[...].astype(BF16)
        w_s[2] = w2_ref[...].astype(BF16)

    n_chain = tm // SEQ
    pad = jnp.zeros((CONV_PAD, tc), F32)

    def tile(seq_len):
        hs = []
        for b in range(n_chain):
            x = _read_stream([r.at[b * SEQ:(b + 1) * SEQ, :] for r in x_refs], i < n_prompt_tiles)
            hs.append(_modulate(x, g_ref[layer:layer + 1, :], sh_ref[0], sc_ref[0]).astype(BF16))

        def short_conv(part, cw_ref, cb_ref):
            zs = [_dot(h, w_s[part]) for h in hs]
            out = []
            for b, z in enumerate(zs):
                before = zs[b - 1][SEQ - CONV_PAD:, :] if (b * SEQ) % seq_len else pad
                after = zs[b + 1][:CONV_PAD, :] if ((b + 1) * SEQ) % seq_len else pad
                ext = jnp.concatenate([before, z, after], axis=0)
                rows = ext.shape[0]
                z_prev = pltpu.roll(ext, 1, 0)[CONV_PAD:CONV_PAD + SEQ, :]
                z_next = pltpu.roll(ext, rows - 1, 0)[CONV_PAD:CONV_PAD + SEQ, :]
                out.append(z_prev * cw_ref[0:1, :] + z * cw_ref[1:2, :] + z_next * cw_ref[2:3, :]
                           + cb_ref[jl:jl + 1, :])
            return out

        for b, x0 in enumerate(short_conv(0, cw0_ref, cb0_ref)):
            x0_ref[b * SEQ:(b + 1) * SEQ, :] = x0.astype(BF16)
        for b, (x1, v) in enumerate(zip(short_conv(1, cw1_ref, cb1_ref), short_conv(2, cw2_ref, cb2_ref))):
            u_ref[b * SEQ:(b + 1) * SEQ, :] = (x1 * v).astype(BF16)

    @pl.when(i < n_prompt_tiles)
    def _():
        tile(SEQ)

    @pl.when(i >= n_prompt_tiles)
    def _():
        tile(DEC_SEQ)


def _in_proj(stream, norm_mix, mods, layer, w_in, conv_w, conv_b, jl):
    tm = 1024
    npt = N_PROMPT // tm

    def wspec(part):
        return pl.BlockSpec((None, D_MODEL, D_MODEL), lambda i: (jl, 0, part), pipeline_mode=pl.Buffered(1))

    def cwspec(part):
        return pl.BlockSpec((None, 3, D_MODEL), lambda i: (jl, 0, part))

    def cbspec(part):
        return pl.BlockSpec((conv_b.shape[0], D_MODEL), lambda i: (0, part))

    out = pl.BlockSpec((tm, D_MODEL), lambda i: (i, 0))
    cb = conv_b
    return pl.pallas_call(
        functools.partial(_in_proj_kernel, n_stream=len(stream), n_prompt_tiles=npt, layer=layer, jl=jl),
        grid=(N_TOK // tm,),
        in_specs=_stream_specs(len(stream), tm) + [
            _whole_spec(norm_mix),
            _mod_spec(layer, 0, tm),
            _mod_spec(layer, 1, tm),
            wspec(0), wspec(1), wspec(2),
            cwspec(0), cwspec(1), cwspec(2),
            cbspec(0), cbspec(1), cbspec(2),
        ],
        out_specs=[out, out],
        out_shape=[jax.ShapeDtypeStruct((N_TOK, D_MODEL), BF16)] * 2,
        scratch_shapes=[pltpu.VMEM((3, D_MODEL, D_MODEL), BF16)],
        compiler_params=_params(1),
        name="hyena_in_proj",
    )(*stream, norm_mix, mods, mods, w_in, w_in, w_in, conv_w, conv_w, conv_w, cb, cb, cb)


def _dft_tables(L):
    n = 2 * L
    k = np.arange(L, dtype=np.float64)[:, None]
    t = np.arange(L, dtype=np.float64)[None, :]
    ang = 2.0 * np.pi * k * t / n
    top = np.cos(ang)
    bot = -np.sin(ang)
    bot[0, :] = np.where(np.arange(L) % 2 == 0, 1.0, -1.0)
    fwd = np.concatenate([top, bot], axis=0)
    wk = np.full((L,), 2.0)
    wk[0] = 1.0
    inv_top = (np.cos(ang) * wk[:, None]).T / n
    inv_bot = (-2.0 * np.sin(ang)).T / n
    inv_bot[:, 0] = np.where(np.arange(L) % 2 == 0, 1.0, -1.0) / n
    inv = np.concatenate([inv_top, inv_bot], axis=1)
    return fwd.astype(np.float32), inv.astype(np.float32)


def _filter_feats(L):
    t = np.arange(L, dtype=np.float32) / np.float32(L)
    bands = np.arange(1, N_BANDS + 1, dtype=np.float32)
    ang = (np.float32(2.0 * math.pi) * t[:, None]) * bands[None, :]
    feats = np.concatenate([t[:, None], np.cos(ang), np.sin(ang)], axis=-1).astype(np.float32)
    return np.pad(feats, ((0, 0), (0, FEAT_PAD - FILTER_EMB)))


def _filter_spectrum(h, w3f_ref, w3b_ref, dl_ref, fwd_ref, L):
    tn = w3f_ref.shape[1]
    rows = lax.broadcasted_iota(jnp.int32, (L, tn), 0)
    t = rows.astype(F32) / L
    window = jnp.exp(-t * dl_ref[...])
    hf = _dot3(h, w3f_ref[...]) * window
    hb = jnp.where(rows > 0, _dot3(h, w3b_ref[...]) * window, 0.0)
    norm = jnp.sqrt(jnp.sum(hf * hf + hb * hb, axis=0, keepdims=True) + EPS)
    hf = hf / norm
    hb = hb / norm
    g_top = _dot(fwd_ref[0:L, :], (hf + hb).astype(BF16))
    g_bot = _dot(fwd_ref[L:2 * L, :], (hf - hb).astype(BF16))
    sign = jnp.where(rows % 2 == 0, 1.0, -1.0)
    nyquist_b = jnp.sum(sign * hb, axis=0, keepdims=True)
    return g_top, g_bot + jnp.where(rows == 0, 2.0 * nyquist_b, 0.0)


def _hyena_conv_kernel(u_ref, x0_ref, bias_ref, fwd_ref, inv_ref, feats_ref, w1_ref, b1_ref, fr_ref, w2_ref, b2_ref,
                       w3f_ref, w3b_ref, dl_ref, o_ref, h_s, g_s, *, L, jl):
    j, b = pl.program_id(0), pl.program_id(1)
    layer_row = slice(jl, jl + 1)

    @pl.when(jnp.logical_and(j == 0, b == 0))
    def _():
        def sin_rows(a):
            half = L // 2
            s = jnp.sin(jnp.concatenate([a[:half, :], a[half:, :]], axis=1))
            return jnp.concatenate([s[:, :FILTER_HIDDEN], s[:, FILTER_HIDDEN:]], axis=0)

        h1 = sin_rows(fr_ref[0:1, :] * (_dot3(feats_ref[...], w1_ref[...]) + b1_ref[layer_row, :]))
        h_s[...] = sin_rows(fr_ref[1:2, :] * (_dot3(h1, w2_ref[...]) + b2_ref[layer_row, :]))

    @pl.when(b == 0)
    def _():
        g_top, g_bot = _filter_spectrum(h_s[...], w3f_ref, w3b_ref, dl_ref, fwd_ref, L)
        g_s[0:L, :] = g_top
        g_s[L:2 * L, :] = g_bot

    tn = o_ref.shape[1]
    first = lax.broadcasted_iota(jnp.int32, (L, tn), 0) == 0
    for s in range(o_ref.shape[0] // L):
        rows = slice(s * L, (s + 1) * L)
        u = u_ref[rows, :]
        spec = _dot(fwd_ref[...], u)
        u_top, u_bot = spec[0:L, :], spec[L:2 * L, :]
        g_top, g_bot = g_s[0:L, :], g_s[L:2 * L, :]
        y_top = u_top * g_top - jnp.where(first, 0.0, u_bot * g_bot)
        y_bot = jnp.where(first, u_bot * g_bot, u_top * g_bot + u_bot * g_top)
        y_spec = jnp.concatenate([y_top, y_bot], axis=0).astype(BF16)
        y = _dot(inv_ref[...], y_spec)
        gated = x0_ref[rows, :].astype(F32) * (y + u.astype(F32) * bias_ref[layer_row, :])
        o_ref[rows, :] = gated.astype(BF16)


def _hyena_conv(u, x0, bias, jl, fwd, inv, filt, L, row_block0, n_steps, seq_per_step, tn):
    nj = D_MODEL // tn
    rows = seq_per_step * L
    feats = jnp.asarray(_filter_feats(L))
    deltas = jnp.asarray(np.abs(np.linspace(MIN_DECAY, MAX_DECAY, D_MODEL, dtype=np.float32)).reshape(1, D_MODEL))
    act = pl.BlockSpec((rows, tn), lambda j, b: (row_block0 + b, j))
    const = lambda j, b: (0, 0)
    layer = lambda j, b: (jl, 0, 0)
    f_w1, f_b1, f_freq, f_w2, f_b2, f_w3 = filt
    return pl.pallas_call(
        functools.partial(_hyena_conv_kernel, L=L, jl=jl),
        grid=(nj, n_steps),
        in_specs=[
            act, act,
            pl.BlockSpec((bias.shape[0], tn), lambda j, b: (0, j)),
            pl.BlockSpec((2 * L, L), const),
            pl.BlockSpec((L, 2 * L), const),
            pl.BlockSpec((L, FEAT_PAD), const),
            pl.BlockSpec((None, FEAT_PAD, FILTER_HIDDEN), layer),
            _whole_spec(f_b1),
            pl.BlockSpec((None, 2, FILTER_HIDDEN), layer),
            pl.BlockSpec((None, FILTER_HIDDEN, FILTER_HIDDEN), layer),
            _whole_spec(f_b2),
            pl.BlockSpec((None, FILTER_HIDDEN, tn), lambda j, b: (jl, 0, j)),
            pl.BlockSpec((None, FILTER_HIDDEN, tn), lambda j, b: (jl, 0, nj + j)),
            pl.BlockSpec((1, tn), lambda j, b: (0, j)),
        ],
        out_specs=pl.BlockSpec((rows, tn), lambda j, b: (b, j)),
        out_shape=jax.ShapeDtypeStruct((n_steps * rows, D_MODEL), BF16),
        scratch_shapes=[pltpu.VMEM((L, FILTER_HIDDEN), F32), pltpu.VMEM((2 * L, tn), F32)],
        compiler_params=_params(2),
        name="hyena_conv_%d" % L,
    )(u, x0, bias, fwd, inv, feats, f_w1, f_b1, f_freq, f_w2, f_b2, f_w3, f_w3, deltas)


def _rope_tables():
    rows = DEC_SEQ // GRID_W
    row = np.repeat(np.arange(rows), GRID_W).astype(np.float32)
    col = np.tile(np.arange(GRID_W), rows).astype(np.float32)
    half = HEAD_DIM // 2
    freqs = (np.float32(ROPE_THETA) ** (-np.arange(0, half, 2, dtype=np.float32) / np.float32(half))).astype(np.float32)
    ang = np.concatenate([row[:, None] * freqs[None, :], col[:, None] * freqs[None, :]], axis=-1)
    cos = np.repeat(np.cos(ang), 2, axis=-1)
    sin = np.repeat(np.sin(ang), 2, axis=-1)
    sign = np.where(np.arange(HEAD_DIM) % 2 == 0, -1.0, 1.0)[None, :]
    pair = lambda a: np.tile(a, (1, 2)).astype(np.float32)
    return pair(cos), pair(sin * sign)


def _qkv_kernel(*refs, n_prev, n_prompt_tiles, layer):
    x_ref, g_ref, sh_ref, sc_ref, w_ref, qg_ref, kg_ref, cos_ref, sin_ref = refs[:9]
    prev_refs, refs = refs[9:9 + 2 * (n_prev > 0)], refs[9 + 2 * (n_prev > 0):]
    q_ref, k_ref, v_ref, kf_ref, vf_ref, w_s = refs
    i = pl.program_id(0)
    tm = q_ref.shape[0]

    @pl.when(i == 0)
    def _():
        w_s[...] = w_ref[...].astype(BF16)

    k_col = N_HEADS * HEAD_DIM
    v_col = k_col + KV_DIM
    jl = n_prev
    q_gain = qg_ref[jl:jl + 1, :] * HEAD_DIM ** -0.5
    k_gain = kg_ref[jl:jl + 1, :]

    def project(rows):
        h = _modulate(x_ref[rows, :], g_ref[layer:layer + 1, :], sh_ref[0], sc_ref[0]).astype(BF16)
        qkv = _dot(h, w_s[...])
        v = qkv[:, v_col:]
        v_ref[rows, :] = v.astype(BF16)

        def pair_norm(col, gain):
            return jnp.concatenate([_rmsnorm(qkv[:, c:c + HEAD_DIM], gain) for c in (col, col + HEAD_DIM)], axis=1)

        return pair_norm, v

    @pl.when(i < n_prompt_tiles)
    def _():
        if n_prev:
            kf_ref[:, 0:n_prev] = prev_refs[0][...]
            vf_ref[:, 0:n_prev] = prev_refs[1][...]
        for r in range(tm // SEQ):
            rows = slice(r * SEQ, (r + 1) * SEQ)
            pair_norm, v = project(rows)
            for col in range(0, k_col, HEAD_PAIR):
                q_ref[rows, col:col + HEAD_PAIR] = pair_norm(col, q_gain).astype(BF16)
            k = pair_norm(k_col, k_gain)
            k_ref[rows, :] = k.astype(BF16)
            for kv in range(N_KV_HEADS):
                cols = slice(kv * HEAD_DIM, (kv + 1) * HEAD_DIM)
                head_rows = pl.ds(kv, SEQ, stride=N_KV_HEADS)
                kf_ref[r, n_prev, head_rows, :] = k[:, cols]
                vf_ref[r, n_prev, head_rows, :] = v[:, cols]

    @pl.when(i >= n_prompt_tiles)
    def _():
        r_id = lax.broadcasted_iota(jnp.int32, (HEAD_PAIR, HEAD_PAIR), 0)
        col_id = lax.broadcasted_iota(jnp.int32, (HEAD_PAIR, HEAD_PAIR), 1)
        swap = jnp.where((r_id ^ 1) == col_id, 1.0, 0.0).astype(BF16)
        for r in range(tm // SEQ):
            rows = slice(r * SEQ, (r + 1) * SEQ)
            pair_norm, _ = project(rows)

            def rotate(x):
                return x * cos_ref[rows, :] + _dot(x.astype(BF16), swap) * sin_ref[rows, :]

            for col in range(0, k_col, HEAD_PAIR):
                q_ref[rows, col:col + HEAD_PAIR] = rotate(pair_norm(col, q_gain)).astype(BF16)
            k_ref[rows, :] = rotate(pair_norm(k_col, k_gain)).astype(BF16)


def _qkv(y, norm_mix, mods, layer, w_qkv, q_gain, k_gain, jl, prev_kv):
    tm = 1024
    npt = N_PROMPT // tm
    tiles_per_req = DEC_SEQ // tm
    cos, sin = _rope_tables()
    rope_spec = pl.BlockSpec((tm, HEAD_PAIR), lambda i: (jnp.maximum(i - npt, 0) % tiles_per_req, 0))
    row = lambda n: pl.BlockSpec((tm, n), lambda i: (i, 0))
    cache_rows = SEQ * N_KV_HEADS
    cache = lambda n: pl.BlockSpec((tm // SEQ, n, cache_rows, HEAD_DIM), lambda i: (jnp.minimum(i, npt - 1), 0, 0, 0))
    return pl.pallas_call(
        functools.partial(_qkv_kernel, n_prev=jl, n_prompt_tiles=npt, layer=layer),
        grid=(N_TOK // tm,),
        in_specs=[
            row(D_MODEL),
            _whole_spec(norm_mix),
            _mod_spec(layer, 0, tm),
            _mod_spec(layer, 1, tm),
            pl.BlockSpec((None, D_MODEL, QKV_DIM), lambda i: (jl, 0, 0), pipeline_mode=pl.Buffered(1)),
            _whole_spec(q_gain), _whole_spec(k_gain),
            rope_spec, rope_spec,
        ] + [cache(jl)] * len(prev_kv),
        out_specs=[row(D_MODEL), row(KV_DIM), row(KV_DIM), cache(jl + 1), cache(jl + 1)],
        out_shape=[
            jax.ShapeDtypeStruct((N_TOK, D_MODEL), BF16),
            jax.ShapeDtypeStruct((N_TOK, KV_DIM), BF16),
            jax.ShapeDtypeStruct((N_TOK, KV_DIM), BF16),
            jax.ShapeDtypeStruct((BATCH, jl + 1, cache_rows, HEAD_DIM), F32),
            jax.ShapeDtypeStruct((BATCH, jl + 1, cache_rows, HEAD_DIM), F32),
        ],
        scratch_shapes=[pltpu.VMEM((D_MODEL, QKV_DIM), BF16)],
        compiler_params=_params(1),
        name="qkv_proj",
    )(y, norm_mix, mods, mods, w_qkv, q_gain, k_gain, jnp.asarray(cos), jnp.asarray(sin), *prev_kv)


def _attention_kernel(*refs, with_cache, seq):
    if with_cache:
        q_ref, k_ref, v_ref, ck_ref, cv_ref, o_ref = refs
    else:
        q_ref, k_ref, v_ref, o_ref = refs
    n_req = k_ref.shape[0] // seq
    tq = q_ref.shape[0] // n_req
    for r in range(n_req):
        q_rows, k_rows = slice(r * tq, (r + 1) * tq), slice(r * seq, (r + 1) * seq)
        for kv in range(k_ref.shape[1] // HEAD_DIM):
            kv_cols = slice(kv * HEAD_DIM, (kv + 1) * HEAD_DIM)
            k = k_ref[k_rows, kv_cols]
            v = v_ref[k_rows, kv_cols]
            if with_cache:
                head_rows = pl.ds(kv, PAST_LEN, stride=N_KV_HEADS)
                k = jnp.concatenate([k, ck_ref[0, 0, head_rows, :].astype(BF16)], axis=0)
                v = jnp.concatenate([v, cv_ref[0, 0, head_rows, :].astype(BF16)], axis=0)
            v_ones = jnp.concatenate([v, jnp.ones_like(v)], axis=1)
            for g in range(GROUP):
                cols = slice((kv * GROUP + g) * HEAD_DIM, (kv * GROUP + g + 1) * HEAD_DIM)
                s = lax.dot_general(q_ref[q_rows, cols], k, (((1,), (1,)), ((), ())), preferred_element_type=F32)
                p = jnp.exp(s - jnp.max(s, axis=-1, keepdims=True))
                o = _dot(p.astype(BF16), v_ones)
                o_ref[q_rows, cols] = (o[:, :HEAD_DIM] / o[:, HEAD_DIM:]).astype(BF16)


def _attention(q, k, v, row0, n_req, seq, tq, kv_per_step, req_per_step=1, cache_k=None, cache_v=None,
               cache_layer=0):
    nq = seq // tq
    n_req //= req_per_step
    tq, seq_rows = tq * req_per_step, seq * req_per_step
    qb0, kb0 = row0 // tq, row0 // seq_rows
    qw, kw = kv_per_step * GROUP * HEAD_DIM, kv_per_step * HEAD_DIM
    in_specs = [
        pl.BlockSpec((tq, qw), lambda b, h, t: (qb0 + b * nq + t, h)),
        pl.BlockSpec((seq_rows, kw), lambda b, h, t: (kb0 + b, h)),
        pl.BlockSpec((seq_rows, kw), lambda b, h, t: (kb0 + b, h)),
    ]
    args = [q, k, v]
    with_cache = cache_k is not None
    if with_cache:
        assert kv_per_step == N_KV_HEADS
        cspec = pl.BlockSpec((1, 1, PAST_LEN * N_KV_HEADS, HEAD_DIM), lambda b, h, t: (b, cache_layer, 0, 0))
        in_specs += [cspec, cspec]
        args += [cache_k, cache_v]
    return pl.pallas_call(
        functools.partial(_attention_kernel, with_cache=with_cache, seq=seq),
        grid=(n_req, N_KV_HEADS // kv_per_step, nq),
        in_specs=in_specs,
        out_specs=pl.BlockSpec((tq, qw), lambda b, h, t: (b * nq + t, h)),
        out_shape=jax.ShapeDtypeStruct((n_req * seq_rows, D_MODEL), BF16),
        compiler_params=_params(3),
        name="attention_%d" % seq,
    )(*args)


def _ffn_kernel(*refs, n_in, n_out, n_prompt_tiles, final, layer):
    y_refs, refs = refs[:n_in], refs[n_in:]
    (ap_ref, as_ref, wo_ref, gm_ref, g_ref, sh_ref, sc_ref, gate_ref, wg_ref, wu_ref, wd_ref, fin_ref) = refs[:12]
    o_refs, refs = refs[12:12 + n_out], refs[12 + n_out:]
    wo_s, wgu_s, wd_s, ymid_s, f_s, acc_s = refs
    s = pl.program_id(0)
    nf = N_FF_CHUNKS
    tile = jnp.maximum(s - (nf - 1), 0)
    is_prompt = tile < n_prompt_tiles

    def prologue():
        a = jnp.where(is_prompt, ap_ref[...], as_ref[...])
        y = _read_stream(y_refs, is_prompt)
        y_mid = y + gm_ref[0] * _dot(a, wo_s[...])
        ymid_s[...] = y_mid
        f_s[...] = _modulate(y_mid, g_ref[layer:layer + 1, :], sh_ref[0], sc_ref[0]).astype(BF16)

    def chunk(c):
        f = f_s[...]
        gate_up = _dot(f, wgu_s[c])
        hidden = _silu(gate_up[:, :FF_CHUNK]) * gate_up[:, FF_CHUNK:]
        return _dot(hidden.astype(BF16), wd_s[c])

    def epilogue(acc):
        out = ymid_s[...] + gate_ref[0] * acc
        if final:
            out = _rmsnorm(out, fin_ref[...])
        if n_out == 1:
            o_refs[0][...] = out
        else:
            @pl.when(is_prompt)
            def _():
                o_refs[0][...] = out

            @pl.when(jnp.logical_not(is_prompt))
            def _():
                o_refs[1][...] = out

    @pl.when(s == 0)
    def _():
        wo_s[...] = wo_ref[...].astype(BF16)
        prologue()

    @pl.when(s < nf)
    def _():
        wgu_s[s, :, :FF_CHUNK] = wg_ref[...].astype(BF16)
        wgu_s[s, :, FF_CHUNK:] = wu_ref[...].astype(BF16)
        wd_s[s] = wd_ref[...].astype(BF16)
        part = chunk(s)

        @pl.when(s == 0)
        def _():
            acc_s[...] = part

        @pl.when(s > 0)
        def _():
            acc_s[...] += part

        @pl.when(s == nf - 1)
        def _():
            epilogue(acc_s[...])

    @pl.when(s >= nf)
    def _():
        prologue()
        acc = chunk(0)
        for c in range(1, nf):
            acc = acc + chunk(c)
        epilogue(acc)


def _ffn(stream, ap, as_, w_out, jl, norm_ffn, mods, layer, wg, wu, wd, final_norm, final):
    tm = 512
    nf = N_FF_CHUNKS
    npt = N_PROMPT // tm
    n_steps = nf + N_TOK // tm - 1
    n_out = 2 if final else 1
    tile_of = lambda s: jnp.maximum(s - (nf - 1), 0)
    chunk_of = lambda s: jnp.minimum(s, nf - 1)
    out_shape = _pair_shapes(D_MODEL, F32) if final else [jax.ShapeDtypeStruct((N_TOK, D_MODEL), F32)]
    return pl.pallas_call(
        functools.partial(_ffn_kernel, n_in=len(stream), n_out=n_out, n_prompt_tiles=npt, final=final, layer=layer),
        grid=(n_steps,),
        in_specs=_stream_specs(len(stream), tm, tile_of) + _pair_specs(tm, D_MODEL, tile_of) + [
            pl.BlockSpec((None, D_MODEL, D_MODEL), lambda s: (jl, 0, 0), pipeline_mode=pl.Buffered(1)),
            _mod_spec(layer, 2, tm, tile_of),
            _whole_spec(norm_ffn),
            _mod_spec(layer, 3, tm, tile_of),
            _mod_spec(layer, 4, tm, tile_of),
            _mod_spec(layer, 5, tm, tile_of),
            pl.BlockSpec((None, D_MODEL, FF_CHUNK), lambda s: (layer, 0, chunk_of(s))),
            pl.BlockSpec((None, D_MODEL, FF_CHUNK), lambda s: (layer, 0, chunk_of(s))),
            pl.BlockSpec((None, FF_CHUNK, D_MODEL), lambda s: (layer, chunk_of(s), 0)),
            pl.BlockSpec((1, D_MODEL), lambda s: (0, 0)),
        ],
        out_specs=_stream_specs(n_out, tm, tile_of),
        out_shape=out_shape,
        scratch_shapes=[
            pltpu.VMEM((D_MODEL, D_MODEL), BF16),
            pltpu.VMEM((nf, D_MODEL, 2 * FF_CHUNK), BF16),
            pltpu.VMEM((nf, FF_CHUNK, D_MODEL), BF16),
            pltpu.VMEM((tm, D_MODEL), F32),
            pltpu.VMEM((tm, D_MODEL), BF16),
            pltpu.VMEM((tm, D_MODEL), F32),
        ],
        compiler_params=_params(1),
        name="ffn",
    )(*stream, ap, as_, w_out, mods, norm_ffn, mods, mods, mods, wg, wu, wd, final_norm.reshape(1, D_MODEL))


def kernel(x_prompt, x_sample, cache_k, cache_v, c, c_ctx, mod_w, mod_b, norm_mix, norm_ffn, hy_w_in, hy_conv_w, hy_conv_b, hy_f_w1, hy_f_b1, hy_f_freq, hy_f_w2, hy_f_b2, hy_f_w3, hy_bias, hy_w_out, at_w_qkv, at_q_norm, at_k_norm, at_w_out, ffn_w_gate, ffn_w_up, ffn_w_down, final_norm):
    stream = (x_prompt.reshape(N_PROMPT, D_MODEL), x_sample.reshape(N_SAMPLE, D_MODEL))
    mods = _adaln(c_ctx, c, mod_w, mod_b)

    tables = {}
    for L in (SEQ, DEC_SEQ):
        fwd, inv = _dft_tables(L)
        tables[L] = (jnp.asarray(fwd).astype(BF16), jnp.asarray(inv).astype(BF16))

    f_w1 = jnp.pad(hy_f_w1, ((0, 0), (0, FEAT_PAD - FILTER_EMB), (0, 0)))
    f_b1, f_b2, bias, q_gain, k_gain = hy_f_b1, hy_f_b2, hy_bias, at_q_norm, at_k_norm
    cache_k = cache_k.reshape(DEC_BATCH, -1, PAST_LEN * N_KV_HEADS, HEAD_DIM)
    cache_v = cache_v.reshape(DEC_BATCH, -1, PAST_LEN * N_KV_HEADS, HEAD_DIM)

    new_kv = ()
    for layer in range(DEPTH):
        jl = layer // N_MIXERS
        if layer % N_MIXERS == 0:
            x0, u = _in_proj(stream, norm_mix, mods, layer, hy_w_in, hy_conv_w, hy_conv_b, jl)
            mixed = []
            filt = (f_w1, f_b1, hy_f_freq, hy_f_w2, f_b2, hy_f_w3)
            for L, row_block0, n_steps, seq_per_step, tn in (
                    (SEQ, 0, BATCH // 4, 4, D_MODEL),
                    (DEC_SEQ, N_PROMPT // DEC_SEQ, DEC_BATCH, 1, 512)):
                fwd_bf16, inv_bf16 = tables[L]
                mixed.append(_hyena_conv(u, x0, bias, jl, fwd_bf16, inv_bf16, filt, L, row_block0, n_steps,
                                         seq_per_step, tn))
            w_out = hy_w_out
        else:
            q, k, v, *new_kv = _qkv(stream[0], norm_mix, mods, layer, at_w_qkv, q_gain, k_gain, jl, new_kv)
            mixed = [_attention(q, k, v, 0, BATCH, SEQ, SEQ, N_KV_HEADS, req_per_step=4),
                     _attention(q, k, v, N_PROMPT, DEC_BATCH, DEC_SEQ, 1024, N_KV_HEADS, cache_k=cache_k, cache_v=cache_v,
                                cache_layer=jl)]
            w_out = at_w_out
        stream = tuple(_ffn(stream, mixed[0], mixed[1], w_out, jl, norm_ffn, mods, layer,
                            ffn_w_gate, ffn_w_up, ffn_w_down, final_norm, final=(layer == DEPTH - 1)))

    y_prompt, y_sample = stream
    new_k, new_v = (a.reshape(BATCH, -1, SEQ, N_KV_HEADS, HEAD_DIM) for a in new_kv)
    return (y_prompt.reshape(BATCH, SEQ, D_MODEL), y_sample.reshape(DEC_BATCH, DEC_SEQ, D_MODEL), new_k, new_v)
```

```python
import functools
import math

import numpy as np
import jax
import jax.numpy as jnp
from jax import lax
from jax.experimental import pallas as pl
from jax.experimental.pallas import tpu as pltpu

D_MODEL = 1024
BATCH = 16
SEQ = 256
DEPTH = 4
DEC_BATCH = 4
DEC_SEQ = 1024
PAST_LEN = 512
GRID_W = 64
N_MIXERS = 2
HEAD_DIM = 128
N_HEADS = D_MODEL // HEAD_DIM
N_KV_HEADS = 2
GROUP = N_HEADS // N_KV_HEADS
KV_DIM = N_KV_HEADS * HEAD_DIM
HEAD_PAIR = 2 * HEAD_DIM
QKV_DIM = (N_HEADS + 2 * N_KV_HEADS) * HEAD_DIM
ROPE_THETA = 10000.0
D_FF = ((8 * D_MODEL + 3 * 256 - 1) // (3 * 256)) * 256
N_BANDS = 16
FILTER_EMB = 1 + 2 * N_BANDS
FILTER_HIDDEN = 64
MIN_DECAY = math.log(1e-2) / 1.5
MAX_DECAY = math.log(1e-2) / 0.3
EPS = 1e-6

N_PROMPT = BATCH * SEQ
N_SAMPLE = DEC_BATCH * DEC_SEQ
N_TOK = N_PROMPT + N_SAMPLE
N_MOD = 6
MOD_ROWS = 8
FEAT_PAD = 128
CONV_PAD = 8
FF_CHUNK = 256
N_FF_CHUNKS = D_FF // FF_CHUNK
V7X_VMEM_LIMIT = 56 * 1024 * 1024

F32 = jnp.float32
BF16 = jnp.bfloat16


def _params(n_axes, vmem=V7X_VMEM_LIMIT):
    return pltpu.CompilerParams(dimension_semantics=("arbitrary",) * n_axes, vmem_limit_bytes=vmem)


def _mod_row(tile, tm):
    n_prompt_tiles = N_PROMPT // tm
    tiles_per_req = DEC_SEQ // tm
    return jnp.where(tile < n_prompt_tiles, 0, 1 + jnp.maximum(tile - n_prompt_tiles, 0) // tiles_per_req)


def _mod_spec(layer, which, tm, tile_of=lambda i, *_: i):
    def index(*ids):
        return ((layer * N_MOD + which) * MOD_ROWS + _mod_row(tile_of(*ids), tm), 0, 0)
    return pl.BlockSpec((1, 1, D_MODEL), index)


def _whole_spec(a):
    return pl.BlockSpec(a.shape, lambda *_: (0,) * a.ndim)


def _pair_specs(tm, width, tile_of=lambda i, *_: i):
    npt = N_PROMPT // tm
    return [pl.BlockSpec((tm, width), lambda *ids: (jnp.minimum(tile_of(*ids), npt - 1), 0)),
            pl.BlockSpec((tm, width), lambda *ids: (jnp.maximum(tile_of(*ids) - npt, 0), 0))]


def _pair_shapes(width, dtype):
    return [jax.ShapeDtypeStruct((N_PROMPT, width), dtype), jax.ShapeDtypeStruct((N_SAMPLE, width), dtype)]


def _stream_specs(n_arrays, tm, tile_of=lambda i, *_: i):
    if n_arrays == 2:
        return _pair_specs(tm, D_MODEL, tile_of)
    return [pl.BlockSpec((tm, D_MODEL), lambda *ids: (tile_of(*ids), 0))]


def _read_stream(refs, is_prompt):
    if len(refs) == 2:
        return jnp.where(is_prompt, refs[0][...], refs[1][...])
    return refs[0][...]


def _split_bf16(a):
    hi = a.astype(BF16)
    lo = (a - hi.astype(F32)).astype(BF16)
    return hi, lo


def _dot(a, b):
    return jnp.dot(a, b, preferred_element_type=F32)


def _dot3(a, b):
    a_hi, a_lo = _split_bf16(a)
    b_hi, b_lo = _split_bf16(b)
    return _dot(a_hi, b_hi) + (_dot(a_lo, b_hi) + _dot(a_hi, b_lo))


def _silu(a):
    return a / (1.0 + jnp.exp(-a))


def _rmsnorm(x, g):
    return x * lax.rsqrt(jnp.mean(x * x, axis=-1, keepdims=True) + EPS) * g


def _modulate(x, g, shift, scale):
    return _rmsnorm(x, g) * (1.0 + scale) + shift


def _adaln_kernel(cctx_ref, c_ref, w_ref, b_ref, o_ref):
    pad = jnp.zeros((MOD_ROWS - 1 - DEC_BATCH, D_MODEL), F32)
    s = _silu(jnp.concatenate([cctx_ref[...], c_ref[...], pad], axis=0))
    bias = b_ref[pl.ds(pl.program_id(0), 1), :]
    m = _dot3(s, w_ref[0]) + bias
    for v in range(m.shape[1] // D_MODEL):
        o_ref[v * MOD_ROWS:(v + 1) * MOD_ROWS, 0, :] = m[:, v * D_MODEL:(v + 1) * D_MODEL]


def _adaln(c_ctx, c, mod_w, mod_b):
    per_step = 3
    tn = per_step * D_MODEL
    return pl.pallas_call(
        _adaln_kernel,
        grid=(DEPTH, N_MOD // per_step),
        in_specs=[
            pl.BlockSpec((1, D_MODEL), lambda l, j: (0, 0)),
            pl.BlockSpec((DEC_BATCH, D_MODEL), lambda l, j: (0, 0)),
            pl.BlockSpec((1, D_MODEL, tn), lambda l, j: (l, 0, j)),
            pl.BlockSpec((DEPTH, tn), lambda l, j: (0, j)),
        ],
        out_specs=pl.BlockSpec((per_step * MOD_ROWS, 1, D_MODEL), lambda l, j: (l * (N_MOD // per_step) + j, 0, 0)),
        out_shape=jax.ShapeDtypeStruct((DEPTH * N_MOD * MOD_ROWS, 1, D_MODEL), F32),
        compiler_params=_params(2),
        name="adaln",
    )(c_ctx.reshape(1, D_MODEL), c, mod_w, mod_b)


def _cast_kernel(x_ref, o_ref):
    o_ref[...] = x_ref[...].astype(BF16)


def _cast_bf16(x, rows):
    m, n = x.shape
    return pl.pallas_call(
        _cast_kernel,
        grid=(m // rows,),
        in_specs=[pl.BlockSpec((rows, n), lambda i: (i, 0))],
        out_specs=pl.BlockSpec((rows, n), lambda i: (i, 0)),
        out_shape=jax.ShapeDtypeStruct((m, n), BF16),
        compiler_params=_params(1),
        name="cast_table",
    )(x)


def _in_proj_kernel(*refs, n_stream, n_prompt_tiles, layer, jl):
    x_refs, refs = refs[:n_stream], refs[n_stream:]
    (g_ref, sh_ref, sc_ref, w0_ref, w1_ref, w2_ref, cw0_ref, cw1_ref, cw2_ref, cb0_ref, cb1_ref, cb2_ref,
     x0_ref, u_ref, w_s) = refs
    i = pl.program_id(0)
    tm, tc = x0_ref.shape

    @pl.when(i == 0)
    def _():
        w_s[0] = w0_ref[...].astype(BF16)
        w_s[1] = w1_ref[...].astype(BF16)
        w_s[2] = w2_ref[...].astype(BF16)

    n_chain = tm // SEQ
    pad = jnp.zeros((CONV_PAD, tc), F32)

    def tile(seq_len):
        hs = []
        for b in range(n_chain):
            x = _read_stream([r.at[b * SEQ:(b + 1) * SEQ, :] for r in x_refs], i < n_prompt_tiles)
            hs.append(_modulate(x, g_ref[layer:layer + 1, :], sh_ref[0], sc_ref[0]).astype(BF16))

        def short_conv(part, cw_ref, cb_ref):
            zs = [_dot(h, w_s[part]) for h in hs]
            out = []
            for b, z in enumerate(zs):
                before = zs[b - 1][SEQ - CONV_PAD:, :] if (b * SEQ) % seq_len else pad
                after = zs[b + 1][:CONV_PAD, :] if ((b + 1) * SEQ) % seq_len else pad
                ext = jnp.concatenate([before, z, after], axis=0)
                rows = ext.shape[0]
                z_prev = pltpu.roll(ext, 1, 0)[CONV_PAD:CONV_PAD + SEQ, :]
                z_next = pltpu.roll(ext, rows - 1, 0)[CONV_PAD:CONV_PAD + SEQ, :]
                out.append(z_prev * cw_ref[0:1, :] + z * cw_ref[1:2, :] + z_next * cw_ref[2:3, :]
                           + cb_ref[jl:jl + 1, :])
            return out

        for b, x0 in enumerate(short_conv(0, cw0_ref, cb0_ref)):
            x0_ref[b * SEQ:(b + 1) * SEQ, :] = x0.astype(BF16)
        for b, (x1, v) in enumerate(zip(short_conv(1, cw1_ref, cb1_ref), short_conv(2, cw2_ref, cb2_ref))):
            u_ref[b * SEQ:(b + 1) * SEQ, :] = (x1 * v).astype(BF16)

    @pl.when(i < n_prompt_tiles)
    def _():
        tile(SEQ)

    @pl.when(i >= n_prompt_tiles)
    def _():
        tile(DEC_SEQ)


def _in_proj(stream, norm_mix, mods, layer, w_in, conv_w, conv_b, jl):
    tm = 1024
    npt = N_PROMPT // tm

    def wspec(part):
        return pl.BlockSpec((None, D_MODEL, D_MODEL), lambda i: (jl, 0, part), pipeline_mode=pl.Buffered(1))

    def cwspec(part):
        return pl.BlockSpec((None, 3, D_MODEL), lambda i: (jl, 0, part))

    def cbspec(part):
        return pl.BlockSpec((conv_b.shape[0], D_MODEL), lambda i: (0, part))

    out = pl.BlockSpec((tm, D_MODEL), lambda i: (i, 0))
    cb = conv_b
    return pl.pallas_call(
        functools.partial(_in_proj_kernel, n_stream=len(stream), n_prompt_tiles=npt, layer=layer, jl=jl),
        grid=(N_TOK // tm,),
        in_specs=_stream_specs(len(stream), tm) + [
            _whole_spec(norm_mix),
            _mod_spec(layer, 0, tm),
            _mod_spec(layer, 1, tm),
            wspec(0), wspec(1), wspec(2),
            cwspec(0), cwspec(1), cwspec(2),
            cbspec(0), cbspec(1), cbspec(2),
        ],
        out_specs=[out, out],
        out_shape=[jax.ShapeDtypeStruct((N_TOK, D_MODEL), BF16)] * 2,
        scratch_shapes=[pltpu.VMEM((3, D_MODEL, D_MODEL), BF16)],
        compiler_params=_params(1),
        name="hyena_in_proj",
    )(*stream, norm_mix, mods, mods, w_in, w_in, w_in, conv_w, conv_w, conv_w, cb, cb, cb)


def _dft_tables(L):
    n = 2 * L
    k = np.arange(L, dtype=np.float64)[:, None]
    t = np.arange(L, dtype=np.float64)[None, :]
    ang = 2.0 * np.pi * k * t / n
    top = np.cos(ang)
    bot = -np.sin(ang)
    bot[0, :] = np.where(np.arange(L) % 2 == 0, 1.0, -1.0)
    fwd = np.concatenate([top, bot], axis=0)
    wk = np.full((L,), 2.0)
    wk[0] = 1.0
    inv_top = (np.cos(ang) * wk[:, None]).T / n
    inv_bot = (-2.0 * np.sin(ang)).T / n
    inv_bot[:, 0] = np.where(np.arange(L) % 2 == 0, 1.0, -1.0) / n
    inv = np.concatenate([inv_top, inv_bot], axis=1)
    return fwd.astype(np.float32), inv.astype(np.float32)


def _filter_feats(L):
    t = np.arange(L, dtype=np.float32) / np.float32(L)
    bands = np.arange(1, N_BANDS + 1, dtype=np.float32)
    ang = (np.float32(2.0 * math.pi) * t[:, None]) * bands[None, :]
    feats = np.concatenate([t[:, None], np.cos(ang), np.sin(ang)], axis=-1).astype(np.float32)
    return np.pad(feats, ((0, 0), (0, FEAT_PAD - FILTER_EMB)))


def _filter_spectrum(h, w3f_ref, w3b_ref, dl_ref, fwd_ref, L):
    tn = w3f_ref.shape[1]
    rows = lax.broadcasted_iota(jnp.int32, (L, tn), 0)
    t = rows.astype(F32) / L
    window = jnp.exp(-t * dl_ref[...])
    hf = _dot3(h, w3f_ref[...]) * window
    hb = jnp.where(rows > 0, _dot3(h, w3b_ref[...]) * window, 0.0)
    norm = jnp.sqrt(jnp.sum(hf * hf + hb * hb, axis=0, keepdims=True) + EPS)
    hf = hf / norm
    hb = hb / norm
    g_top = _dot(fwd_ref[0:L, :], (hf + hb).astype(BF16))
    g_bot = _dot(fwd_ref[L:2 * L, :], (hf - hb).astype(BF16))
    sign = jnp.where(rows % 2 == 0, 1.0, -1.0)
    nyquist_b = jnp.sum(sign * hb, axis=0, keepdims=True)
    return g_top, g_bot + jnp.where(rows == 0, 2.0 * nyquist_b, 0.0)


def _hyena_conv_kernel(u_ref, x0_ref, bias_ref, fwd_ref, inv_ref, feats_ref, w1_ref, b1_ref, fr_ref, w2_ref, b2_ref,
                       w3f_ref, w3b_ref, dl_ref, o_ref, h_s, g_s, *, L, jl):
    j, b = pl.program_id(0), pl.program_id(1)
    layer_row = slice(jl, jl + 1)

    @pl.when(jnp.logical_and(j == 0, b == 0))
    def _():
        def sin_rows(a):
            half = L // 2
            s = jnp.sin(jnp.concatenate([a[:half, :], a[half:, :]], axis=1))
            return jnp.concatenate([s[:, :FILTER_HIDDEN], s[:, FILTER_HIDDEN:]], axis=0)

        h1 = sin_rows(fr_ref[0:1, :] * (_dot3(feats_ref[...], w1_ref[...]) + b1_ref[layer_row, :]))
        h_s[...] = sin_rows(fr_ref[1:2, :] * (_dot3(h1, w2_ref[...]) + b2_ref[layer_row, :]))

    @pl.when(b == 0)
    def _():
        g_top, g_bot = _filter_spectrum(h_s[...], w3f_ref, w3b_ref, dl_ref, fwd_ref, L)
        g_s[0:L, :] = g_top
        g_s[L:2 * L, :] = g_bot

    tn = o_ref.shape[1]
    first = lax.broadcasted_iota(jnp.int32, (L, tn), 0) == 0
    for s in range(o_ref.shape[0] // L):
        rows = slice(s * L, (s + 1) * L)
        u = u_ref[rows, :]
        spec = _dot(fwd_ref[...], u)
        u_top, u_bot = spec[0:L, :], spec[L:2 * L, :]
        g_top, g_bot = g_s[0:L, :], g_s[L:2 * L, :]
        y_top = u_top * g_top - jnp.where(first, 0.0, u_bot * g_bot)
        y_bot = jnp.where(first, u_bot * g_bot, u_top * g_bot + u_bot * g_top)
        y_spec = jnp.concatenate([y_top, y_bot], axis=0).astype(BF16)
        y = _dot(inv_ref[...], y_spec)
        gated = x0_ref[rows, :].astype(F32) * (y + u.astype(F32) * bias_ref[layer_row, :])
        o_ref[rows, :] = gated.astype(BF16)


def _hyena_conv(u, x0, bias, jl, fwd, inv, filt, L, row_block0, n_steps, seq_per_step, tn):
    nj = D_MODEL // tn
    rows = seq_per_step * L
    feats = jnp.asarray(_filter_feats(L))
    deltas = jnp.asarray(np.abs(np.linspace(MIN_DECAY, MAX_DECAY, D_MODEL, dtype=np.float32)).reshape(1, D_MODEL))
    act = pl.BlockSpec((rows, tn), lambda j, b: (row_block0 + b, j))
    const = lambda j, b: (0, 0)
    layer = lambda j, b: (jl, 0, 0)
    f_w1, f_b1, f_freq, f_w2, f_b2, f_w3 = filt
    return pl.pallas_call(
        functools.partial(_hyena_conv_kernel, L=L, jl=jl),
        grid=(nj, n_steps),
        in_specs=[
            act, act,
            pl.BlockSpec((bias.shape[0], tn), lambda j, b: (0, j)),
            pl.BlockSpec((2 * L, L), const),
            pl.BlockSpec((L, 2 * L), const),
            pl.BlockSpec((L, FEAT_PAD), const),
            pl.BlockSpec((None, FEAT_PAD, FILTER_HIDDEN), layer),
            _whole_spec(f_b1),
            pl.BlockSpec((None, 2, FILTER_HIDDEN), layer),
            pl.BlockSpec((None, FILTER_HIDDEN, FILTER_HIDDEN), layer),
            _whole_spec(f_b2),
            pl.BlockSpec((None, FILTER_HIDDEN, tn), lambda j, b: (jl, 0, j)),
            pl.BlockSpec((None, FILTER_HIDDEN, tn), lambda j, b: (jl, 0, nj + j)),
            pl.BlockSpec((1, tn), lambda j, b: (0, j)),
        ],
        out_specs=pl.BlockSpec((rows, tn), lambda j, b: (b, j)),
        out_shape=jax.ShapeDtypeStruct((n_steps * rows, D_MODEL), BF16),
        scratch_shapes=[pltpu.VMEM((L, FILTER_HIDDEN), F32), pltpu.VMEM((2 * L, tn), F32)],
        compiler_params=_params(2),
        name="hyena_conv_%d" % L,
    )(u, x0, bias, fwd, inv, feats, f_w1, f_b1, f_freq, f_w2, f_b2, f_w3, f_w3, deltas)


def _rope_tables():
    rows = DEC_SEQ // GRID_W
    row = np.repeat(np.arange(rows), GRID_W).astype(np.float32)
    col = np.tile(np.arange(GRID_W), rows).astype(np.float32)
    half = HEAD_DIM // 2
    freqs = (np.float32(ROPE_THETA) ** (-np.arange(0, half, 2, dtype=np.float32) / np.float32(half))).astype(np.float32)
    ang = np.concatenate([row[:, None] * freqs[None, :], col[:, None] * freqs[None, :]], axis=-1)
    cos = np.repeat(np.cos(ang), 2, axis=-1)
    sin = np.repeat(np.sin(ang), 2, axis=-1)
    sign = np.where(np.arange(HEAD_DIM) % 2 == 0, -1.0, 1.0)[None, :]
    pair = lambda a: np.tile(a, (1, 2)).astype(np.float32)
    return pair(cos), pair(sin * sign)


def _qkv_kernel(*refs, n_prev, n_prompt_tiles, layer):
    x_ref, g_ref, sh_ref, sc_ref, w_ref, qg_ref, kg_ref, cos_ref, sin_ref = refs[:9]
    prev_refs, refs = refs[9:9 + 2 * (n_prev > 0)], refs[9 + 2 * (n_prev > 0):]
    q_ref, k_ref, v_ref, kf_ref, vf_ref, w_s = refs
    i = pl.program_id(0)
    tm = q_ref.shape[0]

    @pl.when(i == 0)
    def _():
        w_s[...] = w_ref[...].astype(BF16)

    k_col = N_HEADS * HEAD_DIM
    v_col = k_col + KV_DIM
    jl = n_prev
    q_gain = qg_ref[jl:jl + 1, :] * HEAD_DIM ** -0.5
    k_gain = kg_ref[jl:jl + 1, :]

    def project(rows):
        h = _modulate(x_ref[rows, :], g_ref[layer:layer + 1, :], sh_ref[0], sc_ref[0]).astype(BF16)
        qkv = _dot(h, w_s[...])
        v = qkv[:, v_col:]
        v_ref[rows, :] = v.astype(BF16)

        def pair_norm(col, gain):
            return jnp.concatenate([_rmsnorm(qkv[:, c:c + HEAD_DIM], gain) for c in (col, col + HEAD_DIM)], axis=1)

        return pair_norm, v

    @pl.when(i < n_prompt_tiles)
    def _():
        if n_prev:
            kf_ref[:, 0:n_prev] = prev_refs[0][...]
            vf_ref[:, 0:n_prev] = prev_refs[1][...]
        for r in range(tm // SEQ):
            rows = slice(r * SEQ, (r + 1) * SEQ)
            pair_norm, v = project(rows)
            for col in range(0, k_col, HEAD_PAIR):
                q_ref[rows, col:col + HEAD_PAIR] = pair_norm(col, q_gain).astype(BF16)
            k = pair_norm(k_col, k_gain)
            k_ref[rows, :] = k.astype(BF16)
            for kv in range(N_KV_HEADS):
                cols = slice(kv * HEAD_DIM, (kv + 1) * HEAD_DIM)
                head_rows = pl.ds(kv, SEQ, stride=N_KV_HEADS)
                kf_ref[r, n_prev, head_rows, :] = k[:, cols]
                vf_ref[r, n_prev, head_rows, :] = v[:, cols]

    @pl.when(i >= n_prompt_tiles)
    def _():
        r_id = lax.broadcasted_iota(jnp.int32, (HEAD_PAIR, HEAD_PAIR), 0)
        col_id = lax.broadcasted_iota(jnp.int32, (HEAD_PAIR, HEAD_PAIR), 1)
        swap = jnp.where((r_id ^ 1) == col_id, 1.0, 0.0).astype(BF16)
        for r in range(tm // 512):
            rows = slice(r * 512, (r + 1) * 512)
            pair_norm, _ = project(rows)

            def rotate(x):
                return x * cos_ref[rows, :] + _dot(x.astype(BF16), swap) * sin_ref[rows, :]

            for col in range(0, k_col, HEAD_PAIR):
                q_ref[rows, col:col + HEAD_PAIR] = rotate(pair_norm(col, q_gain)).astype(BF16)
            k_ref[rows, :] = rotate(pair_norm(k_col, k_gain)).astype(BF16)


def _qkv(y, norm_mix, mods, layer, w_qkv, q_gain, k_gain, jl, prev_kv):
    tm = 1024
    npt = N_PROMPT // tm
    tiles_per_req = DEC_SEQ // tm
    cos, sin = _rope_tables()
    rope_spec = pl.BlockSpec((tm, HEAD_PAIR), lambda i: (jnp.maximum(i - npt, 0) % tiles_per_req, 0))
    row = lambda n: pl.BlockSpec((tm, n), lambda i: (i, 0))
    cache_rows = SEQ * N_KV_HEADS
    cache = lambda n: pl.BlockSpec((tm // SEQ, n, cache_rows, HEAD_DIM), lambda i: (jnp.minimum(i, npt - 1), 0, 0, 0))
    return pl.pallas_call(
        functools.partial(_qkv_kernel, n_prev=jl, n_prompt_tiles=npt, layer=layer),
        grid=(N_TOK // tm,),
        in_specs=[
            row(D_MODEL),
            _whole_spec(norm_mix),
            _mod_spec(layer, 0, tm),
            _mod_spec(layer, 1, tm),
            pl.BlockSpec((None, D_MODEL, QKV_DIM), lambda i: (jl, 0, 0), pipeline_mode=pl.Buffered(1)),
            _whole_spec(q_gain), _whole_spec(k_gain),
            rope_spec, rope_spec,
        ] + [cache(jl)] * len(prev_kv),
        out_specs=[row(D_MODEL), row(KV_DIM), row(KV_DIM), cache(jl + 1), cache(jl + 1)],
        out_shape=[
            jax.ShapeDtypeStruct((N_TOK, D_MODEL), BF16),
            jax.ShapeDtypeStruct((N_TOK, KV_DIM), BF16),
            jax.ShapeDtypeStruct((N_TOK, KV_DIM), BF16),
            jax.ShapeDtypeStruct((BATCH, jl + 1, cache_rows, HEAD_DIM), F32),
            jax.ShapeDtypeStruct((BATCH, jl + 1, cache_rows, HEAD_DIM), F32),
        ],
        scratch_shapes=[pltpu.VMEM((D_MODEL, QKV_DIM), BF16)],
        compiler_params=_params(1),
        name="qkv_proj",
    )(y, norm_mix, mods, mods, w_qkv, q_gain, k_gain, jnp.asarray(cos), jnp.asarray(sin), *prev_kv)


def _attention_kernel(*refs, with_cache, seq):
    if with_cache:
        q_ref, k_ref, v_ref, ck_ref, cv_ref, o_ref = refs
    else:
        q_ref, k_ref, v_ref, o_ref = refs
    n_req = k_ref.shape[0] // seq
    tq = q_ref.shape[0] // n_req
    for r in range(n_req):
        q_rows, k_rows = slice(r * tq, (r + 1) * tq), slice(r * seq, (r + 1) * seq)
        for kv in range(k_ref.shape[1] // HEAD_DIM):
            kv_cols = slice(kv * HEAD_DIM, (kv + 1) * HEAD_DIM)
            k = k_ref[k_rows, kv_cols]
            v = v_ref[k_rows, kv_cols]
            if with_cache:
                head_rows = pl.ds(kv, PAST_LEN, stride=N_KV_HEADS)
                k = jnp.concatenate([k, ck_ref[0, 0, head_rows, :].astype(BF16)], axis=0)
                v = jnp.concatenate([v, cv_ref[0, 0, head_rows, :].astype(BF16)], axis=0)
            v_ones = jnp.concatenate([v, jnp.ones_like(v)], axis=1)
            for g in range(GROUP):
                cols = slice((kv * GROUP + g) * HEAD_DIM, (kv * GROUP + g + 1) * HEAD_DIM)
                s = lax.dot_general(q_ref[q_rows, cols], k, (((1,), (1,)), ((), ())), preferred_element_type=F32)
                p = jnp.exp(s - jnp.max(s, axis=-1, keepdims=True))
                o = _dot(p.astype(BF16), v_ones)
                o_ref[q_rows, cols] = (o[:, :HEAD_DIM] / o[:, HEAD_DIM:]).astype(BF16)


def _attention(q, k, v, row0, n_req, seq, tq, kv_per_step, req_per_step=1, cache_k=None, cache_v=None,
               cache_layer=0):
    nq = seq // tq
    n_req //= req_per_step
    tq, seq_rows = tq * req_per_step, seq * req_per_step
    qb0, kb0 = row0 // tq, row0 // seq_rows
    qw, kw = kv_per_step * GROUP * HEAD_DIM, kv_per_step * HEAD_DIM
    in_specs = [
        pl.BlockSpec((tq, qw), lambda b, h, t: (qb0 + b * nq + t, h)),
        pl.BlockSpec((seq_rows, kw), lambda b, h, t: (kb0 + b, h)),
        pl.BlockSpec((seq_rows, kw), lambda b, h, t: (kb0 + b, h)),
    ]
    args = [q, k, v]
    with_cache = cache_k is not None
    if with_cache:
        assert kv_per_step == N_KV_HEADS
        cspec = pl.BlockSpec((1, 1, PAST_LEN * N_KV_HEADS, HEAD_DIM), lambda b, h, t: (b, cache_layer, 0, 0))
        in_specs += [cspec, cspec]
        args += [cache_k, cache_v]
    return pl.pallas_call(
        functools.partial(_attention_kernel, with_cache=with_cache, seq=seq),
        grid=(n_req, N_KV_HEADS // kv_per_step, nq),
        in_specs=in_specs,
        out_specs=pl.BlockSpec((tq, qw), lambda b, h, t: (b * nq + t, h)),
        out_shape=jax.ShapeDtypeStruct((n_req * seq_rows, D_MODEL), BF16),
        compiler_params=_params(3),
        name="attention_%d" % seq,
    )(*args)


def _ffn_kernel(*refs, n_in, n_out, n_prompt_tiles, final, layer):
    y_refs, refs = refs[:n_in], refs[n_in:]
    (ap_ref, as_ref, wo_ref, gm_ref, g_ref, sh_ref, sc_ref, gate_ref, wg_ref, wu_ref, wd_ref, fin_ref) = refs[:12]
    o_refs, refs = refs[12:12 + n_out], refs[12 + n_out:]
    wo_s, wg_s, wu_s, wd_s, ymid_s, f_s, acc_s = refs
    s = pl.program_id(0)
    nf = N_FF_CHUNKS
    tile = jnp.maximum(s - (nf - 1), 0)
    is_prompt = tile < n_prompt_tiles

    def prologue():
        a = jnp.where(is_prompt, ap_ref[...], as_ref[...])
        y = _read_stream(y_refs, is_prompt)
        y_mid = y + gm_ref[0] * _dot(a, wo_s[...])
        ymid_s[...] = y_mid
        f_s[...] = _modulate(y_mid, g_ref[layer:layer + 1, :], sh_ref[0], sc_ref[0]).astype(BF16)

    def chunk(c):
        f = f_s[...]
        hidden = _silu(_dot(f, wg_s[c])) * _dot(f, wu_s[c])
        return _dot(hidden.astype(BF16), wd_s[c])

    def epilogue(acc):
        out = ymid_s[...] + gate_ref[0] * acc
        if final:
            out = _rmsnorm(out, fin_ref[...])
        if n_out == 1:
            o_refs[0][...] = out
        else:
            @pl.when(is_prompt)
            def _():
                o_refs[0][...] = out

            @pl.when(jnp.logical_not(is_prompt))
            def _():
                o_refs[1][...] = out

    @pl.when(s == 0)
    def _():
        wo_s[...] = wo_ref[...].astype(BF16)
        prologue()

    @pl.when(s < nf)
    def _():
        wg_s[s] = wg_ref[...].astype(BF16)
        wu_s[s] = wu_ref[...].astype(BF16)
        wd_s[s] = wd_ref[...].astype(BF16)
        part = chunk(s)

        @pl.when(s == 0)
        def _():
            acc_s[...] = part

        @pl.when(s > 0)
        def _():
            acc_s[...] += part

        @pl.when(s == nf - 1)
        def _():
            epilogue(acc_s[...])

    @pl.when(s >= nf)
    def _():
        prologue()
        acc = chunk(0)
        for c in range(1, nf):
            acc = acc + chunk(c)
        epilogue(acc)


def _ffn(stream, ap, as_, w_out, jl, norm_ffn, mods, layer, wg, wu, wd, final_norm, final):
    tm = 512
    nf = N_FF_CHUNKS
    npt = N_PROMPT // tm
    n_steps = nf + N_TOK // tm - 1
    n_out = 2 if final else 1
    tile_of = lambda s: jnp.maximum(s - (nf - 1), 0)
    chunk_of = lambda s: jnp.minimum(s, nf - 1)
    out_shape = _pair_shapes(D_MODEL, F32) if final else [jax.ShapeDtypeStruct((N_TOK, D_MODEL), F32)]
    return pl.pallas_call(
        functools.partial(_ffn_kernel, n_in=len(stream), n_out=n_out, n_prompt_tiles=npt, final=final, layer=layer),
        grid=(n_steps,),
        in_specs=_stream_specs(len(stream), tm, tile_of) + _pair_specs(tm, D_MODEL, tile_of) + [
            pl.BlockSpec((None, D_MODEL, D_MODEL), lambda s: (jl, 0, 0), pipeline_mode=pl.Buffered(1)),
            _mod_spec(layer, 2, tm, tile_of),
            _whole_spec(norm_ffn),
            _mod_spec(layer, 3, tm, tile_of),
            _mod_spec(layer, 4, tm, tile_of),
            _mod_spec(layer, 5, tm, tile_of),
            pl.BlockSpec((None, D_MODEL, FF_CHUNK), lambda s: (layer, 0, chunk_of(s))),
            pl.BlockSpec((None, D_MODEL, FF_CHUNK), lambda s: (layer, 0, chunk_of(s))),
            pl.BlockSpec((None, FF_CHUNK, D_MODEL), lambda s: (layer, chunk_of(s), 0)),
            pl.BlockSpec((1, D_MODEL), lambda s: (0, 0)),
        ],
        out_specs=_stream_specs(n_out, tm, tile_of),
        out_shape=out_shape,
        scratch_shapes=[
            pltpu.VMEM((D_MODEL, D_MODEL), BF16),
            pltpu.VMEM((nf, D_MODEL, FF_CHUNK), BF16),
            pltpu.VMEM((nf, D_MODEL, FF_CHUNK), BF16),
            pltpu.VMEM((nf, FF_CHUNK, D_MODEL), BF16),
            pltpu.VMEM((tm, D_MODEL), F32),
            pltpu.VMEM((tm, D_MODEL), BF16),
            pltpu.VMEM((tm, D_MODEL), F32),
        ],
        compiler_params=_params(1),
        name="ffn",
    )(*stream, ap, as_, w_out, mods, norm_ffn, mods, mods, mods, wg, wu, wd, final_norm.reshape(1, D_MODEL))


def kernel(x_prompt, x_sample, cache_k, cache_v, c, c_ctx, mod_w, mod_b, norm_mix, norm_ffn, hy_w_in, hy_conv_w, hy_conv_b, hy_f_w1, hy_f_b1, hy_f_freq, hy_f_w2, hy_f_b2, hy_f_w3, hy_bias, hy_w_out, at_w_qkv, at_q_norm, at_k_norm, at_w_out, ffn_w_gate, ffn_w_up, ffn_w_down, final_norm):
    stream = (x_prompt.reshape(N_PROMPT, D_MODEL), x_sample.reshape(N_SAMPLE, D_MODEL))
    mods = _adaln(c_ctx, c, mod_w, mod_b)

    tables = {}
    for L in (SEQ, DEC_SEQ):
        fwd, inv = _dft_tables(L)
        tables[L] = (jnp.asarray(fwd).astype(BF16), jnp.asarray(inv).astype(BF16))

    f_w1 = jnp.pad(hy_f_w1, ((0, 0), (0, FEAT_PAD - FILTER_EMB), (0, 0)))
    f_b1, f_b2, bias, q_gain, k_gain = hy_f_b1, hy_f_b2, hy_bias, at_q_norm, at_k_norm
    cache_k = cache_k.reshape(DEC_BATCH, -1, PAST_LEN * N_KV_HEADS, HEAD_DIM)
    cache_v = cache_v.reshape(DEC_BATCH, -1, PAST_LEN * N_KV_HEADS, HEAD_DIM)

    new_kv = ()
    for layer in range(DEPTH):
        jl = layer // N_MIXERS
        if layer % N_MIXERS == 0:
            x0, u = _in_proj(stream, norm_mix, mods, layer, hy_w_in, hy_conv_w, hy_conv_b, jl)
            mixed = []
            filt = (f_w1, f_b1, hy_f_freq, hy_f_w2, f_b2, hy_f_w3)
            for L, row_block0, n_steps, seq_per_step, tn in (
                    (SEQ, 0, BATCH // 8, 8, D_MODEL),
                    (DEC_SEQ, N_PROMPT // DEC_SEQ, DEC_BATCH, 1, 512)):
                fwd_bf16, inv_bf16 = tables[L]
                mixed.append(_hyena_conv(u, x0, bias, jl, fwd_bf16, inv_bf16, filt, L, row_block0, n_steps,
                                         seq_per_step, tn))
            w_out = hy_w_out
        else:
            q, k, v, *new_kv = _qkv(stream[0], norm_mix, mods, layer, at_w_qkv, q_gain, k_gain, jl, new_kv)
            mixed = [_attention(q, k, v, 0, BATCH, SEQ, SEQ, N_KV_HEADS, req_per_step=8),
                     _attention(q, k, v, N_PROMPT, DEC_BATCH, DEC_SEQ, 1024, N_KV_HEADS, cache_k=cache_k, cache_v=cache_v,
                                cache_layer=jl)]
            w_out = at_w_out
        stream = tuple(_ffn(stream, mixed[0], mixed[1], w_out, jl, norm_ffn, mods, layer,
                            ffn_w_gate, ffn_w_up, ffn_w_down, final_norm, final=(layer == DEPTH - 1)))

    y_prompt, y_sample = stream
    new_k, new_v = (a.reshape(BATCH, -1, SEQ, N_KV_HEADS, HEAD_DIM) for a in new_kv)
    return (y_prompt.reshape(BATCH, SEQ, D_MODEL), y_sample.reshape(DEC_BATCH, DEC_SEQ, D_MODEL), new_k, new_v)
```

```python
import functools
import math

import numpy as np
import jax
import jax.numpy as jnp
from jax import lax
from jax.experimental import pallas as pl
from jax.experimental.pallas import tpu as pltpu

D_MODEL = 1024
BATCH = 16
SEQ = 256
DEPTH = 4
DEC_BATCH = 4
DEC_SEQ = 1024
PAST_LEN = 512
GRID_W = 64
N_MIXERS = 2
HEAD_DIM = 128
N_HEADS = D_MODEL // HEAD_DIM
N_KV_HEADS = 2
GROUP = N_HEADS // N_KV_HEADS
KV_DIM = N_KV_HEADS * HEAD_DIM
HEAD_PAIR = 2 * HEAD_DIM
QKV_DIM = (N_HEADS + 2 * N_KV_HEADS) * HEAD_DIM
ROPE_THETA = 10000.0
D_FF = ((8 * D_MODEL + 3 * 256 - 1) // (3 * 256)) * 256
N_BANDS = 16
FILTER_EMB = 1 + 2 * N_BANDS
FILTER_HIDDEN = 64
MIN_DECAY = math.log(1e-2) / 1.5
MAX_DECAY = math.log(1e-2) / 0.3
EPS = 1e-6

N_PROMPT = BATCH * SEQ
N_SAMPLE = DEC_BATCH * DEC_SEQ
N_TOK = N_PROMPT + N_SAMPLE
N_MOD = 6
MOD_ROWS = 8
FEAT_PAD = 128
CONV_PAD = 8
FF_CHUNK = 256
N_FF_CHUNKS = D_FF // FF_CHUNK
V7X_VMEM_LIMIT = 56 * 1024 * 1024

F32 = jnp.float32
BF16 = jnp.bfloat16


def _params(n_axes, vmem=V7X_VMEM_LIMIT):
    return pltpu.CompilerParams(dimension_semantics=("arbitrary",) * n_axes, vmem_limit_bytes=vmem)


def _mod_row(tile, tm):
    n_prompt_tiles = N_PROMPT // tm
    tiles_per_req = DEC_SEQ // tm
    return jnp.where(tile < n_prompt_tiles, 0, 1 + jnp.maximum(tile - n_prompt_tiles, 0) // tiles_per_req)


def _mod_spec(layer, which, tm, tile_of=lambda i, *_: i):
    def index(*ids):
        return ((layer * N_MOD + which) * MOD_ROWS + _mod_row(tile_of(*ids), tm), 0, 0)
    return pl.BlockSpec((1, 1, D_MODEL), index)


def _whole_spec(a):
    return pl.BlockSpec(a.shape, lambda *_: (0,) * a.ndim)


def _pair_specs(tm, width, tile_of=lambda i, *_: i):
    npt = N_PROMPT // tm
    return [pl.BlockSpec((tm, width), lambda *ids: (jnp.minimum(tile_of(*ids), npt - 1), 0)),
            pl.BlockSpec((tm, width), lambda *ids: (jnp.maximum(tile_of(*ids) - npt, 0), 0))]


def _pair_shapes(width, dtype):
    return [jax.ShapeDtypeStruct((N_PROMPT, width), dtype), jax.ShapeDtypeStruct((N_SAMPLE, width), dtype)]


def _stream_specs(n_arrays, tm, tile_of=lambda i, *_: i):
    if n_arrays == 2:
        return _pair_specs(tm, D_MODEL, tile_of)
    return [pl.BlockSpec((tm, D_MODEL), lambda *ids: (tile_of(*ids), 0))]


def _read_stream(refs, is_prompt):
    if len(refs) == 2:
        return jnp.where(is_prompt, refs[0][...], refs[1][...])
    return refs[0][...]


def _split_bf16(a):
    hi = a.astype(BF16)
    lo = (a - hi.astype(F32)).astype(BF16)
    return hi, lo


def _dot(a, b):
    return jnp.dot(a, b, preferred_element_type=F32)


def _dot3(a, b):
    a_hi, a_lo = _split_bf16(a)
    b_hi, b_lo = _split_bf16(b)
    return _dot(a_hi, b_hi) + (_dot(a_lo, b_hi) + _dot(a_hi, b_lo))


def _silu(a):
    return a / (1.0 + jnp.exp(-a))


def _rmsnorm(x, g):
    return x * lax.rsqrt(jnp.mean(x * x, axis=-1, keepdims=True) + EPS) * g


def _modulate(x, g, shift, scale):
    return _rmsnorm(x, g) * (1.0 + scale) + shift


def _adaln_kernel(cctx_ref, c_ref, w_ref, b_ref, o_ref):
    pad = jnp.zeros((MOD_ROWS - 1 - DEC_BATCH, D_MODEL), F32)
    s = _silu(jnp.concatenate([cctx_ref[...], c_ref[...], pad], axis=0))
    bias = b_ref[pl.ds(pl.program_id(0), 1), :]
    m = _dot3(s, w_ref[0]) + bias
    for v in range(m.shape[1] // D_MODEL):
        o_ref[v * MOD_ROWS:(v + 1) * MOD_ROWS, 0, :] = m[:, v * D_MODEL:(v + 1) * D_MODEL]


def _adaln(c_ctx, c, mod_w, mod_b):
    per_step = 3
    tn = per_step * D_MODEL
    return pl.pallas_call(
        _adaln_kernel,
        grid=(DEPTH, N_MOD // per_step),
        in_specs=[
            pl.BlockSpec((1, D_MODEL), lambda l, j: (0, 0)),
            pl.BlockSpec((DEC_BATCH, D_MODEL), lambda l, j: (0, 0)),
            pl.BlockSpec((1, D_MODEL, tn), lambda l, j: (l, 0, j)),
            pl.BlockSpec((DEPTH, tn), lambda l, j: (0, j)),
        ],
        out_specs=pl.BlockSpec((per_step * MOD_ROWS, 1, D_MODEL), lambda l, j: (l * (N_MOD // per_step) + j, 0, 0)),
        out_shape=jax.ShapeDtypeStruct((DEPTH * N_MOD * MOD_ROWS, 1, D_MODEL), F32),
        compiler_params=_params(2),
        name="adaln",
    )(c_ctx.reshape(1, D_MODEL), c, mod_w, mod_b)


def _in_proj_kernel(*refs, n_stream, n_prompt_tiles, layer, jl):
    x_refs, refs = refs[:n_stream], refs[n_stream:]
    (g_ref, sh_ref, sc_ref, w0_ref, w1_ref, w2_ref, cw0_ref, cw1_ref, cw2_ref, cb0_ref, cb1_ref, cb2_ref,
     x0_ref, u_ref, w_s) = refs
    i = pl.program_id(0)
    tm, tc = x0_ref.shape

    @pl.when(i == 0)
    def _():
        w_s[0] = w0_ref[...].astype(BF16)
        w_s[1] = w1_ref[...].astype(BF16)
        w_s[2] = w2_ref[...].astype(BF16)

    n_chain = tm // SEQ
    pad = jnp.zeros((CONV_PAD, tc), F32)

    def tile(seq_len):
        hs = []
        for b in range(n_chain):
            x = _read_stream([r.at[b * SEQ:(b + 1) * SEQ, :] for r in x_refs], i < n_prompt_tiles)
            hs.append(_modulate(x, g_ref[layer:layer + 1, :], sh_ref[0], sc_ref[0]).astype(BF16))

        def short_conv(part, cw_ref, cb_ref):
            zs = [_dot(h, w_s[part]) for h in hs]
            out = []
            for b, z in enumerate(zs):
                before = zs[b - 1][SEQ - CONV_PAD:, :] if (b * SEQ) % seq_len else pad
                after = zs[b + 1][:CONV_PAD, :] if ((b + 1) * SEQ) % seq_len else pad
                ext = jnp.concatenate([before, z, after], axis=0)
                rows = ext.shape[0]
                z_prev = pltpu.roll(ext, 1, 0)[CONV_PAD:CONV_PAD + SEQ, :]
                z_next = pltpu.roll(ext, rows - 1, 0)[CONV_PAD:CONV_PAD + SEQ, :]
                out.append(z_prev * cw_ref[0:1, :] + z * cw_ref[1:2, :] + z_next * cw_ref[2:3, :]
                           + cb_ref[jl:jl + 1, :])
            return out

        for b, x0 in enumerate(short_conv(0, cw0_ref, cb0_ref)):
            x0_ref[b * SEQ:(b + 1) * SEQ, :] = x0.astype(BF16)
        for b, (x1, v) in enumerate(zip(short_conv(1, cw1_ref, cb1_ref), short_conv(2, cw2_ref, cb2_ref))):
            u_ref[b * SEQ:(b + 1) * SEQ, :] = (x1 * v).astype(BF16)

    @pl.when(i < n_prompt_tiles)
    def _():
        tile(SEQ)

    @pl.when(i >= n_prompt_tiles)
    def _():
        tile(DEC_SEQ)


def _in_proj(stream, norm_mix, mods, layer, w_in, conv_w, conv_b, jl):
    tm = 1024
    npt = N_PROMPT // tm

    def wspec(part):
        return pl.BlockSpec((None, D_MODEL, D_MODEL), lambda i: (jl, 0, part), pipeline_mode=pl.Buffered(1))

    def cwspec(part):
        return pl.BlockSpec((None, 3, D_MODEL), lambda i: (jl, 0, part))

    def cbspec(part):
        return pl.BlockSpec((conv_b.shape[0], D_MODEL), lambda i: (0, part))

    out = pl.BlockSpec((tm, D_MODEL), lambda i: (i, 0))
    cb = conv_b
    return pl.pallas_call(
        functools.partial(_in_proj_kernel, n_stream=len(stream), n_prompt_tiles=npt, layer=layer, jl=jl),
        grid=(N_TOK // tm,),
        in_specs=_stream_specs(len(stream), tm) + [
            _whole_spec(norm_mix),
            _mod_spec(layer, 0, tm),
            _mod_spec(layer, 1, tm),
            wspec(0), wspec(1), wspec(2),
            cwspec(0), cwspec(1), cwspec(2),
            cbspec(0), cbspec(1), cbspec(2),
        ],
        out_specs=[out, out],
        out_shape=[jax.ShapeDtypeStruct((N_TOK, D_MODEL), BF16)] * 2,
        scratch_shapes=[pltpu.VMEM((3, D_MODEL, D_MODEL), BF16)],
        compiler_params=_params(1),
        name="hyena_in_proj",
    )(*stream, norm_mix, mods, mods, w_in, w_in, w_in, conv_w, conv_w, conv_w, cb, cb, cb)


def _dft_tables(L):
    n = 2 * L
    k = np.arange(L, dtype=np.float64)[:, None]
    t = np.arange(L, dtype=np.float64)[None, :]
    ang = 2.0 * np.pi * k * t / n
    top = np.cos(ang)
    bot = -np.sin(ang)
    bot[0, :] = np.where(np.arange(L) % 2 == 0, 1.0, -1.0)
    fwd = np.concatenate([top, bot], axis=0)
    wk = np.full((L,), 2.0)
    wk[0] = 1.0
    inv_top = (np.cos(ang) * wk[:, None]).T / n
    inv_bot = (-2.0 * np.sin(ang)).T / n
    inv_bot[:, 0] = np.where(np.arange(L) % 2 == 0, 1.0, -1.0) / n
    inv = np.concatenate([inv_top, inv_bot], axis=1)
    return fwd.astype(np.float32), inv.astype(np.float32)


def _filter_feats(L):
    t = np.arange(L, dtype=np.float32) / np.float32(L)
    bands = np.arange(1, N_BANDS + 1, dtype=np.float32)
    ang = (np.float32(2.0 * math.pi) * t[:, None]) * bands[None, :]
    feats = np.concatenate([t[:, None], np.cos(ang), np.sin(ang)], axis=-1).astype(np.float32)
    return np.pad(feats, ((0, 0), (0, FEAT_PAD - FILTER_EMB)))


def _filter_spectrum(h, w3f_ref, w3b_ref, dl_ref, fwd_ref, L):
    tn = w3f_ref.shape[1]
    rows = lax.broadcasted_iota(jnp.int32, (L, tn), 0)
    t = rows.astype(F32) / L
    window = jnp.exp(-t * dl_ref[...])
    hf = _dot3(h, w3f_ref[...]) * window
    hb = jnp.where(rows > 0, _dot3(h, w3b_ref[...]) * window, 0.0)
    norm = jnp.sqrt(jnp.sum(hf * hf + hb * hb, axis=0, keepdims=True) + EPS)
    hf = hf / norm
    hb = hb / norm
    g_top = _dot(fwd_ref[0:L, :], (hf + hb).astype(BF16))
    g_bot = _dot(fwd_ref[L:2 * L, :], (hf - hb).astype(BF16))
    sign = jnp.where(rows % 2 == 0, 1.0, -1.0)
    nyquist_b = jnp.sum(sign * hb, axis=0, keepdims=True)
    return g_top, g_bot + jnp.where(rows == 0, 2.0 * nyquist_b, 0.0)


def _hyena_conv_kernel(u_ref, x0_ref, bias_ref, fwd_ref, inv_ref, feats_ref, w1_ref, b1_ref, fr_ref, w2_ref, b2_ref,
                       w3f_ref, w3b_ref, dl_ref, o_ref, h_s, g_s, *, L, jl):
    j, b = pl.program_id(0), pl.program_id(1)
    layer_row = slice(jl, jl + 1)

    @pl.when(jnp.logical_and(j == 0, b == 0))
    def _():
        def sin_rows(a):
            half = L // 2
            s = jnp.sin(jnp.concatenate([a[:half, :], a[half:, :]], axis=1))
            return jnp.concatenate([s[:, :FILTER_HIDDEN], s[:, FILTER_HIDDEN:]], axis=0)

        h1 = sin_rows(fr_ref[0:1, :] * (_dot3(feats_ref[...], w1_ref[...]) + b1_ref[layer_row, :]))
        h_s[...] = sin_rows(fr_ref[1:2, :] * (_dot3(h1, w2_ref[...]) + b2_ref[layer_row, :]))

    @pl.when(b == 0)
    def _():
        g_top, g_bot = _filter_spectrum(h_s[...], w3f_ref, w3b_ref, dl_ref, fwd_ref, L)
        g_s[0:L, :] = g_top
        g_s[L:2 * L, :] = g_bot

    tn = o_ref.shape[1]
    first = lax.broadcasted_iota(jnp.int32, (L, tn), 0) == 0
    for s in range(o_ref.shape[0] // L):
        rows = slice(s * L, (s + 1) * L)
        u = u_ref[rows, :]
        spec = _dot(fwd_ref[...], u)
        u_top, u_bot = spec[0:L, :], spec[L:2 * L, :]
        g_top, g_bot = g_s[0:L, :], g_s[L:2 * L, :]
        y_top = u_top * g_top - jnp.where(first, 0.0, u_bot * g_bot)
        y_bot = jnp.where(first, u_bot * g_bot, u_top * g_bot + u_bot * g_top)
        y_spec = jnp.concatenate([y_top, y_bot], axis=0).astype(BF16)
        y = _dot(inv_ref[...], y_spec)
        gated = x0_ref[rows, :].astype(F32) * (y + u.astype(F32) * bias_ref[layer_row, :])
        o_ref[rows, :] = gated.astype(BF16)


def _hyena_conv(u, x0, bias, jl, fwd, inv, filt, L, row_block0, n_steps, seq_per_step, tn):
    nj = D_MODEL // tn
    rows = seq_per_step * L
    feats = jnp.asarray(_filter_feats(L))
    deltas = jnp.asarray(np.abs(np.linspace(MIN_DECAY, MAX_DECAY, D_MODEL, dtype=np.float32)).reshape(1, D_MODEL))
    act = pl.BlockSpec((rows, tn), lambda j, b: (row_block0 + b, j))
    const = lambda j, b: (0, 0)
    layer = lambda j, b: (jl, 0, 0)
    f_w1, f_b1, f_freq, f_w2, f_b2, f_w3 = filt
    return pl.pallas_call(
        functools.partial(_hyena_conv_kernel, L=L, jl=jl),
        grid=(nj, n_steps),
        in_specs=[
            act, act,
            pl.BlockSpec((bias.shape[0], tn), lambda j, b: (0, j)),
            pl.BlockSpec((2 * L, L), const),
            pl.BlockSpec((L, 2 * L), const),
            pl.BlockSpec((L, FEAT_PAD), const),
            pl.BlockSpec((None, FEAT_PAD, FILTER_HIDDEN), layer),
            _whole_spec(f_b1),
            pl.BlockSpec((None, 2, FILTER_HIDDEN), layer),
            pl.BlockSpec((None, FILTER_HIDDEN, FILTER_HIDDEN), layer),
            _whole_spec(f_b2),
            pl.BlockSpec((None, FILTER_HIDDEN, tn), lambda j, b: (jl, 0, j)),
            pl.BlockSpec((None, FILTER_HIDDEN, tn), lambda j, b: (jl, 0, nj + j)),
            pl.BlockSpec((1, tn), lambda j, b: (0, j)),
        ],
        out_specs=pl.BlockSpec((rows, tn), lambda j, b: (b, j)),
        out_shape=jax.ShapeDtypeStruct((n_steps * rows, D_MODEL), BF16),
        scratch_shapes=[pltpu.VMEM((L, FILTER_HIDDEN), F32), pltpu.VMEM((2 * L, tn), F32)],
        compiler_params=_params(2),
        name="hyena_conv_%d" % L,
    )(u, x0, bias, fwd, inv, feats, f_w1, f_b1, f_freq, f_w2, f_b2, f_w3, f_w3, deltas)


def _rope_tables():
    rows = DEC_SEQ // GRID_W
    row = np.repeat(np.arange(rows), GRID_W).astype(np.float32)
    col = np.tile(np.arange(GRID_W), rows).astype(np.float32)
    half = HEAD_DIM // 2
    freqs = (np.float32(ROPE_THETA) ** (-np.arange(0, half, 2, dtype=np.float32) / np.float32(half))).astype(np.float32)
    ang = np.concatenate([row[:, None] * freqs[None, :], col[:, None] * freqs[None, :]], axis=-1)
    cos = np.repeat(np.cos(ang), 2, axis=-1)
    sin = np.repeat(np.sin(ang), 2, axis=-1)
    sign = np.where(np.arange(HEAD_DIM) % 2 == 0, -1.0, 1.0)[None, :]
    pair = lambda a: np.tile(a, (1, 2)).astype(np.float32)
    return pair(cos), pair(sin * sign)


def _qkv_kernel(*refs, n_prev, n_prompt_tiles, layer):
    x_ref, g_ref, sh_ref, sc_ref, w_ref, qg_ref, kg_ref, cos_ref, sin_ref = refs[:9]
    prev_refs, refs = refs[9:9 + 2 * (n_prev > 0)], refs[9 + 2 * (n_prev > 0):]
    q_ref, k_ref, v_ref, kf_ref, vf_ref, w_s = refs
    i = pl.program_id(0)
    tm = q_ref.shape[0]

    @pl.when(i == 0)
    def _():
        w_s[...] = w_ref[...].astype(BF16)

    k_col = N_HEADS * HEAD_DIM
    v_col = k_col + KV_DIM
    jl = n_prev
    q_gain = qg_ref[jl:jl + 1, :] * HEAD_DIM ** -0.5
    k_gain = kg_ref[jl:jl + 1, :]

    def project(rows):
        h = _modulate(x_ref[rows, :], g_ref[layer:layer + 1, :], sh_ref[0], sc_ref[0]).astype(BF16)
        qkv = _dot(h, w_s[...])
        v = qkv[:, v_col:]
        v_ref[rows, :] = v.astype(BF16)

        def pair_norm(col, gain):
            return jnp.concatenate([_rmsnorm(qkv[:, c:c + HEAD_DIM], gain) for c in (col, col + HEAD_DIM)], axis=1)

        return pair_norm, v

    @pl.when(i < n_prompt_tiles)
    def _():
        if n_prev:
            kf_ref[:, 0:n_prev] = prev_refs[0][...]
            vf_ref[:, 0:n_prev] = prev_refs[1][...]
        for r in range(tm // SEQ):
            rows = slice(r * SEQ, (r + 1) * SEQ)
            pair_norm, v = project(rows)
            for col in range(0, k_col, HEAD_PAIR):
                q_ref[rows, col:col + HEAD_PAIR] = pair_norm(col, q_gain).astype(BF16)
            k = pair_norm(k_col, k_gain)
            k_ref[rows, :] = k.astype(BF16)
            for kv in range(N_KV_HEADS):
                cols = slice(kv * HEAD_DIM, (kv + 1) * HEAD_DIM)
                head_rows = pl.ds(kv, SEQ, stride=N_KV_HEADS)
                kf_ref[r, n_prev, head_rows, :] = k[:, cols]
                vf_ref[r, n_prev, head_rows, :] = v[:, cols]

    @pl.when(i >= n_prompt_tiles)
    def _():
        r_id = lax.broadcasted_iota(jnp.int32, (HEAD_PAIR, HEAD_PAIR), 0)
        col_id = lax.broadcasted_iota(jnp.int32, (HEAD_PAIR, HEAD_PAIR), 1)
        swap = jnp.where((r_id ^ 1) == col_id, 1.0, 0.0).astype(BF16)
        for r in range(tm // SEQ):
            rows = slice(r * SEQ, (r + 1) * SEQ)
            pair_norm, _ = project(rows)

            def rotate(x):
                return x * cos_ref[rows, :] + _dot(x.astype(BF16), swap) * sin_ref[rows, :]

            for col in range(0, k_col, HEAD_PAIR):
                q_ref[rows, col:col + HEAD_PAIR] = rotate(pair_norm(col, q_gain)).astype(BF16)
            k_ref[rows, :] = rotate(pair_norm(k_col, k_gain)).astype(BF16)


def _qkv(y, norm_mix, mods, layer, w_qkv, q_gain, k_gain, jl, prev_kv):
    tm = 1024
    npt = N_PROMPT // tm
    tiles_per_req = DEC_SEQ // tm
    cos, sin = _rope_tables()
    rope_spec = pl.BlockSpec((tm, HEAD_PAIR), lambda i: (jnp.maximum(i - npt, 0) % tiles_per_req, 0))
    row = lambda n: pl.BlockSpec((tm, n), lambda i: (i, 0))
    cache_rows = SEQ * N_KV_HEADS
    cache = lambda n: pl.BlockSpec((tm // SEQ, n, cache_rows, HEAD_DIM), lambda i: (jnp.minimum(i, npt - 1), 0, 0, 0))
    return pl.pallas_call(
        functools.partial(_qkv_kernel, n_prev=jl, n_prompt_tiles=npt, layer=layer),
        grid=(N_TOK // tm,),
        in_specs=[
            row(D_MODEL),
            _whole_spec(norm_mix),
            _mod_spec(layer, 0, tm),
            _mod_spec(layer, 1, tm),
            pl.BlockSpec((None, D_MODEL, QKV_DIM), lambda i: (jl, 0, 0), pipeline_mode=pl.Buffered(1)),
            _whole_spec(q_gain), _whole_spec(k_gain),
            rope_spec, rope_spec,
        ] + [cache(jl)] * len(prev_kv),
        out_specs=[row(D_MODEL), row(KV_DIM), row(KV_DIM), cache(jl + 1), cache(jl + 1)],
        out_shape=[
            jax.ShapeDtypeStruct((N_TOK, D_MODEL), BF16),
            jax.ShapeDtypeStruct((N_TOK, KV_DIM), BF16),
            jax.ShapeDtypeStruct((N_TOK, KV_DIM), BF16),
            jax.ShapeDtypeStruct((BATCH, jl + 1, cache_rows, HEAD_DIM), F32),
            jax.ShapeDtypeStruct((BATCH, jl + 1, cache_rows, HEAD_DIM), F32),
        ],
        scratch_shapes=[pltpu.VMEM((D_MODEL, QKV_DIM), BF16)],
        compiler_params=_params(1),
        name="qkv_proj",
    )(y, norm_mix, mods, mods, w_qkv, q_gain, k_gain, jnp.asarray(cos), jnp.asarray(sin), *prev_kv)


def _attention_kernel(*refs, with_cache, seq):
    if with_cache:
        q_ref, k_ref, v_ref, ck_ref, cv_ref, o_ref = refs
    else:
        q_ref, k_ref, v_ref, o_ref = refs
    n_req = k_ref.shape[0] // seq
    tq = q_ref.shape[0] // n_req
    for r in range(n_req):
        q_rows, k_rows = slice(r * tq, (r + 1) * tq), slice(r * seq, (r + 1) * seq)
        for kv in range(k_ref.shape[1] // HEAD_DIM):
            kv_cols = slice(kv * HEAD_DIM, (kv + 1) * HEAD_DIM)
            k = k_ref[k_rows, kv_cols]
            v = v_ref[k_rows, kv_cols]
            if with_cache:
                head_rows = pl.ds(kv, PAST_LEN, stride=N_KV_HEADS)
                k = jnp.concatenate([k, ck_ref[0, 0, head_rows, :].astype(BF16)], axis=0)
                v = jnp.concatenate([v, cv_ref[0, 0, head_rows, :].astype(BF16)], axis=0)
            v_ones = jnp.concatenate([v, jnp.ones_like(v)], axis=1)
            for g in range(GROUP):
                cols = slice((kv * GROUP + g) * HEAD_DIM, (kv * GROUP + g + 1) * HEAD_DIM)
                s = lax.dot_general(q_ref[q_rows, cols], k, (((1,), (1,)), ((), ())), preferred_element_type=F32)
                p = jnp.exp(s - jnp.max(s, axis=-1, keepdims=True))
                o = _dot(p.astype(BF16), v_ones)
                o_ref[q_rows, cols] = (o[:, :HEAD_DIM] / o[:, HEAD_DIM:]).astype(BF16)


def _attention(q, k, v, row0, n_req, seq, tq, kv_per_step, req_per_step=1, cache_k=None, cache_v=None,
               cache_layer=0):
    nq = seq // tq
    n_req //= req_per_step
    tq, seq_rows = tq * req_per_step, seq * req_per_step
    qb0, kb0 = row0 // tq, row0 // seq_rows
    qw, kw = kv_per_step * GROUP * HEAD_DIM, kv_per_step * HEAD_DIM
    in_specs = [
        pl.BlockSpec((tq, qw), lambda b, h, t: (qb0 + b * nq + t, h)),
        pl.BlockSpec((seq_rows, kw), lambda b, h, t: (kb0 + b, h)),
        pl.BlockSpec((seq_rows, kw), lambda b, h, t: (kb0 + b, h)),
    ]
    args = [q, k, v]
    with_cache = cache_k is not None
    if with_cache:
        assert kv_per_step == N_KV_HEADS
        cspec = pl.BlockSpec((1, 1, PAST_LEN * N_KV_HEADS, HEAD_DIM), lambda b, h, t: (b, cache_layer, 0, 0))
        in_specs += [cspec, cspec]
        args += [cache_k, cache_v]
    return pl.pallas_call(
        functools.partial(_attention_kernel, with_cache=with_cache, seq=seq),
        grid=(n_req, N_KV_HEADS // kv_per_step, nq),
        in_specs=in_specs,
        out_specs=pl.BlockSpec((tq, qw), lambda b, h, t: (b * nq + t, h)),
        out_shape=jax.ShapeDtypeStruct((n_req * seq_rows, D_MODEL), BF16),
        compiler_params=_params(3),
        name="attention_%d" % seq,
    )(*args)


def _ffn_kernel(*refs, n_in, n_out, n_prompt_tiles, final, layer):
    y_refs, refs = refs[:n_in], refs[n_in:]
    (ap_ref, as_ref, wo_ref, gm_ref, g_ref, sh_ref, sc_ref, gate_ref, wg_ref, wu_ref, wd_ref, fin_ref) = refs[:12]
    o_refs, refs = refs[12:12 + n_out], refs[12 + n_out:]
    wo_s, wg_s, wu_s, wd_s, ymid_s, f_s, acc_s = refs
    s = pl.program_id(0)
    nf = N_FF_CHUNKS
    tile = jnp.maximum(s - (nf - 1), 0)
    is_prompt = tile < n_prompt_tiles

    def prologue():
        a = jnp.where(is_prompt, ap_ref[...], as_ref[...])
        y = _read_stream(y_refs, is_prompt)
        y_mid = y + gm_ref[0] * _dot(a, wo_s[...])
        ymid_s[...] = y_mid
        f_s[...] = _modulate(y_mid, g_ref[layer:layer + 1, :], sh_ref[0], sc_ref[0]).astype(BF16)

    def chunk(c):
        f = f_s[...]
        hidden = _silu(_dot(f, wg_s[c])) * _dot(f, wu_s[c])
        return _dot(hidden.astype(BF16), wd_s[c])

    def epilogue(acc):
        out = ymid_s[...] + gate_ref[0] * acc
        if final:
            out = _rmsnorm(out, fin_ref[...])
        if n_out == 1:
            o_refs[0][...] = out
        else:
            @pl.when(is_prompt)
            def _():
                o_refs[0][...] = out

            @pl.when(jnp.logical_not(is_prompt))
            def _():
                o_refs[1][...] = out

    @pl.when(s == 0)
    def _():
        wo_s[...] = wo_ref[...].astype(BF16)
        prologue()

    @pl.when(s < nf)
    def _():
        wg_s[s] = wg_ref[...].astype(BF16)
        wu_s[s] = wu_ref[...].astype(BF16)
        wd_s[s] = wd_ref[...].astype(BF16)
        part = chunk(s)

        @pl.when(s == 0)
        def _():
            acc_s[...] = part

        @pl.when(s > 0)
        def _():
            acc_s[...] += part

        @pl.when(s == nf - 1)
        def _():
            epilogue(acc_s[...])

    @pl.when(s >= nf)
    def _():
        prologue()
        acc = chunk(0)
        for c in range(1, nf):
            acc = acc + chunk(c)
        epilogue(acc)


def _ffn(stream, ap, as_, w_out, jl, norm_ffn, mods, layer, wg, wu, wd, final_norm, final):
    tm = 512
    nf = N_FF_CHUNKS
    npt = N_PROMPT // tm
    n_steps = nf + N_TOK // tm - 1
    n_out = 2 if final else 1
    tile_of = lambda s: jnp.maximum(s - (nf - 1), 0)
    chunk_of = lambda s: jnp.minimum(s, nf - 1)
    out_shape = _pair_shapes(D_MODEL, F32) if final else [jax.ShapeDtypeStruct((N_TOK, D_MODEL), F32)]
    return pl.pallas_call(
        functools.partial(_ffn_kernel, n_in=len(stream), n_out=n_out, n_prompt_tiles=npt, final=final, layer=layer),
        grid=(n_steps,),
        in_specs=_stream_specs(len(stream), tm, tile_of) + _pair_specs(tm, D_MODEL, tile_of) + [
            pl.BlockSpec((None, D_MODEL, D_MODEL), lambda s: (jl, 0, 0), pipeline_mode=pl.Buffered(1)),
            _mod_spec(layer, 2, tm, tile_of),
            _whole_spec(norm_ffn),
            _mod_spec(layer, 3, tm, tile_of),
            _mod_spec(layer, 4, tm, tile_of),
            _mod_spec(layer, 5, tm, tile_of),
            pl.BlockSpec((None, D_MODEL, FF_CHUNK), lambda s: (layer, 0, chunk_of(s))),
            pl.BlockSpec((None, D_MODEL, FF_CHUNK), lambda s: (layer, 0, chunk_of(s))),
            pl.BlockSpec((None, FF_CHUNK, D_MODEL), lambda s: (layer, chunk_of(s), 0)),
            pl.BlockSpec((1, D_MODEL), lambda s: (0, 0)),
        ],
        out_specs=_stream_specs(n_out, tm, tile_of),
        out_shape=out_shape,
        scratch_shapes=[
            pltpu.VMEM((D_MODEL, D_MODEL), BF16),
            pltpu.VMEM((nf, D_MODEL, FF_CHUNK), BF16),
            pltpu.VMEM((nf, D_MODEL, FF_CHUNK), BF16),
            pltpu.VMEM((nf, FF_CHUNK, D_MODEL), BF16),
            pltpu.VMEM((tm, D_MODEL), F32),
            pltpu.VMEM((tm, D_MODEL), BF16),
            pltpu.VMEM((tm, D_MODEL), F32),
        ],
        compiler_params=_params(1),
        name="ffn",
    )(*stream, ap, as_, w_out, mods, norm_ffn, mods, mods, mods, wg, wu, wd, final_norm.reshape(1, D_MODEL))


def kernel(x_prompt, x_sample, cache_k, cache_v, c, c_ctx, mod_w, mod_b, norm_mix, norm_ffn, hy_w_in, hy_conv_w, hy_conv_b, hy_f_w1, hy_f_b1, hy_f_freq, hy_f_w2, hy_f_b2, hy_f_w3, hy_bias, hy_w_out, at_w_qkv, at_q_norm, at_k_norm, at_w_out, ffn_w_gate, ffn_w_up, ffn_w_down, final_norm):
    stream = (x_prompt.reshape(N_PROMPT, D_MODEL), x_sample.reshape(N_SAMPLE, D_MODEL))
    mods = _adaln(c_ctx, c, mod_w, mod_b)

    tables = {}
    for L in (SEQ, DEC_SEQ):
        fwd, inv = _dft_tables(L)
        tables[L] = (jnp.asarray(fwd).astype(BF16), jnp.asarray(inv).astype(BF16))

    f_w1 = jnp.pad(hy_f_w1, ((0, 0), (0, FEAT_PAD - FILTER_EMB), (0, 0)))
    f_b1, f_b2, bias, q_gain, k_gain = hy_f_b1, hy_f_b2, hy_bias, at_q_norm, at_k_norm
    cache_k = cache_k.reshape(DEC_BATCH, -1, PAST_LEN * N_KV_HEADS, HEAD_DIM)
    cache_v = cache_v.reshape(DEC_BATCH, -1, PAST_LEN * N_KV_HEADS, HEAD_DIM)

    new_kv = ()
    for layer in range(DEPTH):
        jl = layer // N_MIXERS
        if layer % N_MIXERS == 0:
            x0, u = _in_proj(stream, norm_mix, mods, layer, hy_w_in, hy_conv_w, hy_conv_b, jl)
            mixed = []
            filt = (f_w1, f_b1, hy_f_freq, hy_f_w2, f_b2, hy_f_w3)
            for L, row_block0, n_steps, seq_per_step, tn in (
                    (SEQ, 0, BATCH // 4, 4, D_MODEL),
                    (DEC_SEQ, N_PROMPT // DEC_SEQ, DEC_BATCH, 1, 512)):
                fwd_bf16, inv_bf16 = tables[L]
                mixed.append(_hyena_conv(u, x0, bias, jl, fwd_bf16, inv_bf16, filt, L, row_block0, n_steps,
                                         seq_per_step, tn))
            w_out = hy_w_out
        else:
            q, k, v, *new_kv = _qkv(stream[0], norm_mix, mods, layer, at_w_qkv, q_gain, k_gain, jl, new_kv)
            mixed = [_attention(q, k, v, 0, BATCH, SEQ, SEQ, N_KV_HEADS, req_per_step=4),
                     _attention(q, k, v, N_PROMPT, DEC_BATCH, DEC_SEQ, 1024, N_KV_HEADS, cache_k=cache_k, cache_v=cache_v,
                                cache_layer=jl)]
            w_out = at_w_out
        stream = tuple(_ffn(stream, mixed[0], mixed[1], w_out, jl, norm_ffn, mods, layer,
                            ffn_w_gate, ffn_w_up, ffn_w_down, final_norm, final=(layer == DEPTH - 1)))

    y_prompt, y_sample = stream
    new_k, new_v = (a.reshape(BATCH, -1, SEQ, N_KV_HEADS, HEAD_DIM) for a in new_kv)
    return (y_prompt.reshape(BATCH, SEQ, D_MODEL), y_sample.reshape(DEC_BATCH, DEC_SEQ, D_MODEL), new_k, new_v)
```

```python
import functools
import math

import numpy as np
import jax
import jax.numpy as jnp
from jax import lax
from jax.experimental import pallas as pl
from jax.experimental.pallas import tpu as pltpu

D_MODEL = 1024
BATCH = 16
SEQ = 256
DEPTH = 4
DEC_BATCH = 4
DEC_SEQ = 1024
PAST_LEN = 512
GRID_W = 64
N_MIXERS = 2
HEAD_DIM = 128
N_HEADS = D_MODEL // HEAD_DIM
N_KV_HEADS = 2
GROUP = N_HEADS // N_KV_HEADS
KV_DIM = N_KV_HEADS * HEAD_DIM
HEAD_PAIR = 2 * HEAD_DIM
QKV_DIM = (N_HEADS + 2 * N_KV_HEADS) * HEAD_DIM
ROPE_THETA = 10000.0
D_FF = ((8 * D_MODEL + 3 * 256 - 1) // (3 * 256)) * 256
N_BANDS = 16
FILTER_EMB = 1 + 2 * N_BANDS
FILTER_HIDDEN = 64
MIN_DECAY = math.log(1e-2) / 1.5
MAX_DECAY = math.log(1e-2) / 0.3
EPS = 1e-6

N_PROMPT = BATCH * SEQ
N_SAMPLE = DEC_BATCH * DEC_SEQ
N_TOK = N_PROMPT + N_SAMPLE
N_MOD = 6
MOD_ROWS = 8
FEAT_PAD = 128
CONV_PAD = 8
LATENT_CHAIN = 512
FF_CHUNK = 256
N_FF_CHUNKS = D_FF // FF_CHUNK
V7X_VMEM_LIMIT = 56 * 1024 * 1024

F32 = jnp.float32
BF16 = jnp.bfloat16


def _params(n_axes, vmem=V7X_VMEM_LIMIT):
    return pltpu.CompilerParams(dimension_semantics=("arbitrary",) * n_axes, vmem_limit_bytes=vmem)


def _mod_row(tile, tm):
    n_prompt_tiles = N_PROMPT // tm
    tiles_per_req = DEC_SEQ // tm
    return jnp.where(tile < n_prompt_tiles, 0, 1 + jnp.maximum(tile - n_prompt_tiles, 0) // tiles_per_req)


def _mod_spec(layer, which, tm, tile_of=lambda i, *_: i):
    def index(*ids):
        return ((layer * N_MOD + which) * MOD_ROWS + _mod_row(tile_of(*ids), tm), 0, 0)
    return pl.BlockSpec((1, 1, D_MODEL), index)


def _whole_spec(a):
    return pl.BlockSpec(a.shape, lambda *_: (0,) * a.ndim)


def _pair_specs(tm, width, tile_of=lambda i, *_: i):
    npt = N_PROMPT // tm
    return [pl.BlockSpec((tm, width), lambda *ids: (jnp.minimum(tile_of(*ids), npt - 1), 0)),
            pl.BlockSpec((tm, width), lambda *ids: (jnp.maximum(tile_of(*ids) - npt, 0), 0))]


def _pair_shapes(width, dtype):
    return [jax.ShapeDtypeStruct((N_PROMPT, width), dtype), jax.ShapeDtypeStruct((N_SAMPLE, width), dtype)]


def _stream_specs(n_arrays, tm, tile_of=lambda i, *_: i):
    if n_arrays == 2:
        return _pair_specs(tm, D_MODEL, tile_of)
    return [pl.BlockSpec((tm, D_MODEL), lambda *ids: (tile_of(*ids), 0))]


def _read_stream(refs, is_prompt):
    if len(refs) == 2:
        return jnp.where(is_prompt, refs[0][...], refs[1][...])
    return refs[0][...]


def _split_bf16(a):
    hi = a.astype(BF16)
    lo = (a - hi.astype(F32)).astype(BF16)
    return hi, lo


def _dot(a, b):
    return jnp.dot(a, b, preferred_element_type=F32)


def _dot3(a, b):
    a_hi, a_lo = _split_bf16(a)
    b_hi, b_lo = _split_bf16(b)
    return _dot(a_hi, b_hi) + (_dot(a_lo, b_hi) + _dot(a_hi, b_lo))


def _silu(a):
    return a / (1.0 + jnp.exp(-a))


def _rmsnorm(x, g):
    return x * lax.rsqrt(jnp.mean(x * x, axis=-1, keepdims=True) + EPS) * g


def _modulate(x, g, shift, scale):
    return _rmsnorm(x, g) * (1.0 + scale) + shift


def _adaln_kernel(cctx_ref, c_ref, w_ref, b_ref, o_ref):
    pad = jnp.zeros((MOD_ROWS - 1 - DEC_BATCH, D_MODEL), F32)
    s = _silu(jnp.concatenate([cctx_ref[...], c_ref[...], pad], axis=0))
    bias = b_ref[pl.ds(pl.program_id(0), 1), :]
    m = _dot(s.astype(BF16), w_ref[0].astype(BF16)) + bias
    for v in range(m.shape[1] // D_MODEL):
        o_ref[v * MOD_ROWS:(v + 1) * MOD_ROWS, 0, :] = m[:, v * D_MODEL:(v + 1) * D_MODEL]


def _adaln(c_ctx, c, mod_w, mod_b):
    per_step = 3
    tn = per_step * D_MODEL
    return pl.pallas_call(
        _adaln_kernel,
        grid=(DEPTH, N_MOD // per_step),
        in_specs=[
            pl.BlockSpec((1, D_MODEL), lambda l, j: (0, 0)),
            pl.BlockSpec((DEC_BATCH, D_MODEL), lambda l, j: (0, 0)),
            pl.BlockSpec((1, D_MODEL, tn), lambda l, j: (l, 0, j)),
            pl.BlockSpec((DEPTH, tn), lambda l, j: (0, j)),
        ],
        out_specs=pl.BlockSpec((per_step * MOD_ROWS, 1, D_MODEL), lambda l, j: (l * (N_MOD // per_step) + j, 0, 0)),
        out_shape=jax.ShapeDtypeStruct((DEPTH * N_MOD * MOD_ROWS, 1, D_MODEL), F32),
        compiler_params=_params(2),
        name="adaln",
    )(c_ctx.reshape(1, D_MODEL), c, mod_w, mod_b)


def _in_proj_kernel(*refs, n_stream, n_prompt_tiles, layer, jl):
    x_refs, refs = refs[:n_stream], refs[n_stream:]
    (g_ref, sh_ref, sc_ref, w0_ref, w1_ref, w2_ref, cw0_ref, cw1_ref, cw2_ref, cb0_ref, cb1_ref, cb2_ref,
     x0_ref, u_ref, w_s) = refs
    i = pl.program_id(0)
    tm, tc = x0_ref.shape

    @pl.when(i == 0)
    def _():
        w_s[0] = w0_ref[...].astype(BF16)
        w_s[1] = w1_ref[...].astype(BF16)
        w_s[2] = w2_ref[...].astype(BF16)

    n_chain = tm // SEQ
    pad = jnp.zeros((CONV_PAD, tc), F32)

    def tile(seq_len):
        hs = []
        for b in range(n_chain):
            x = _read_stream([r.at[b * SEQ:(b + 1) * SEQ, :] for r in x_refs], i < n_prompt_tiles)
            hs.append(_modulate(x, g_ref[layer:layer + 1, :], sh_ref[0], sc_ref[0]).astype(BF16))

        def short_conv(part, cw_ref, cb_ref):
            zs = [_dot(h, w_s[part]) for h in hs]
            out = []
            for b, z in enumerate(zs):
                before = zs[b - 1][SEQ - CONV_PAD:, :] if (b * SEQ) % seq_len else pad
                after = zs[b + 1][:CONV_PAD, :] if ((b + 1) * SEQ) % seq_len else pad
                ext = jnp.concatenate([before, z, after], axis=0)
                rows = ext.shape[0]
                z_prev = pltpu.roll(ext, 1, 0)[CONV_PAD:CONV_PAD + SEQ, :]
                z_next = pltpu.roll(ext, rows - 1, 0)[CONV_PAD:CONV_PAD + SEQ, :]
                out.append(z_prev * cw_ref[0:1, :] + z * cw_ref[1:2, :] + z_next * cw_ref[2:3, :]
                           + cb_ref[jl:jl + 1, :])
            return out

        for b, x0 in enumerate(short_conv(0, cw0_ref, cb0_ref)):
            x0_ref[b * SEQ:(b + 1) * SEQ, :] = x0.astype(BF16)
        for b, (x1, v) in enumerate(zip(short_conv(1, cw1_ref, cb1_ref), short_conv(2, cw2_ref, cb2_ref))):
            u_ref[b * SEQ:(b + 1) * SEQ, :] = (x1 * v).astype(BF16)

    @pl.when(i < n_prompt_tiles)
    def _():
        tile(SEQ)

    @pl.when(i >= n_prompt_tiles)
    def _():
        tile(DEC_SEQ)


def _in_proj(stream, norm_mix, mods, layer, w_in, conv_w, conv_b, jl):
    tm = 1024
    npt = N_PROMPT // tm

    def wspec(part):
        return pl.BlockSpec((None, D_MODEL, D_MODEL), lambda i: (jl, 0, part), pipeline_mode=pl.Buffered(1))

    def cwspec(part):
        return pl.BlockSpec((None, 3, D_MODEL), lambda i: (jl, 0, part))

    def cbspec(part):
        return pl.BlockSpec((conv_b.shape[0], D_MODEL), lambda i: (0, part))

    out = pl.BlockSpec((tm, D_MODEL), lambda i: (i, 0))
    cb = conv_b
    return pl.pallas_call(
        functools.partial(_in_proj_kernel, n_stream=len(stream), n_prompt_tiles=npt, layer=layer, jl=jl),
        grid=(N_TOK // tm,),
        in_specs=_stream_specs(len(stream), tm) + [
            _whole_spec(norm_mix),
            _mod_spec(layer, 0, tm),
            _mod_spec(layer, 1, tm),
            wspec(0), wspec(1), wspec(2),
            cwspec(0), cwspec(1), cwspec(2),
            cbspec(0), cbspec(1), cbspec(2),
        ],
        out_specs=[out, out],
        out_shape=[jax.ShapeDtypeStruct((N_TOK, D_MODEL), BF16)] * 2,
        scratch_shapes=[pltpu.VMEM((3, D_MODEL, D_MODEL), BF16)],
        compiler_params=_params(1),
        name="hyena_in_proj",
    )(*stream, norm_mix, mods, mods, w_in, w_in, w_in, conv_w, conv_w, conv_w, cb, cb, cb)


def _dft_tables(L):
    n = 2 * L
    k = np.arange(L, dtype=np.float64)[:, None]
    t = np.arange(L, dtype=np.float64)[None, :]
    ang = 2.0 * np.pi * k * t / n
    top = np.cos(ang)
    bot = -np.sin(ang)
    bot[0, :] = np.where(np.arange(L) % 2 == 0, 1.0, -1.0)
    fwd = np.concatenate([top, bot], axis=0)
    wk = np.full((L,), 2.0)
    wk[0] = 1.0
    inv_top = (np.cos(ang) * wk[:, None]).T / n
    inv_bot = (-2.0 * np.sin(ang)).T / n
    inv_bot[:, 0] = np.where(np.arange(L) % 2 == 0, 1.0, -1.0) / n
    inv = np.concatenate([inv_top, inv_bot], axis=1)
    return fwd.astype(np.float32), inv.astype(np.float32)


def _filter_feats(L):
    t = np.arange(L, dtype=np.float32) / np.float32(L)
    bands = np.arange(1, N_BANDS + 1, dtype=np.float32)
    ang = (np.float32(2.0 * math.pi) * t[:, None]) * bands[None, :]
    feats = np.concatenate([t[:, None], np.cos(ang), np.sin(ang)], axis=-1).astype(np.float32)
    return np.pad(feats, ((0, 0), (0, FEAT_PAD - FILTER_EMB)))


def _filter_spectrum(h, w3f_ref, w3b_ref, dl_ref, fwd_ref, L):
    tn = w3f_ref.shape[1]
    rows = lax.broadcasted_iota(jnp.int32, (L, tn), 0)
    t = rows.astype(F32) / L
    window = jnp.exp(-t * dl_ref[...])
    hf = _dot3(h, w3f_ref[...]) * window
    hb = jnp.where(rows > 0, _dot3(h, w3b_ref[...]) * window, 0.0)
    norm = jnp.sqrt(jnp.sum(hf * hf + hb * hb, axis=0, keepdims=True) + EPS)
    hf = hf / norm
    hb = hb / norm
    g_top = _dot(fwd_ref[0:L, :], (hf + hb).astype(BF16))
    g_bot = _dot(fwd_ref[L:2 * L, :], (hf - hb).astype(BF16))
    sign = jnp.where(rows % 2 == 0, 1.0, -1.0)
    nyquist_b = jnp.sum(sign * hb, axis=0, keepdims=True)
    return g_top, g_bot + jnp.where(rows == 0, 2.0 * nyquist_b, 0.0)


def _hyena_conv_kernel(u_ref, x0_ref, bias_ref, fwd_ref, inv_ref, feats_ref, w1_ref, b1_ref, fr_ref, w2_ref, b2_ref,
                       w3f_ref, w3b_ref, dl_ref, o_ref, h_s, g_s, *, L, jl):
    j, b = pl.program_id(0), pl.program_id(1)
    layer_row = slice(jl, jl + 1)

    @pl.when(jnp.logical_and(j == 0, b == 0))
    def _():
        def sin_rows(a):
            half = L // 2
            s = jnp.sin(jnp.concatenate([a[:half, :], a[half:, :]], axis=1))
            return jnp.concatenate([s[:, :FILTER_HIDDEN], s[:, FILTER_HIDDEN:]], axis=0)

        h1 = sin_rows(fr_ref[0:1, :] * (_dot3(feats_ref[...], w1_ref[...]) + b1_ref[layer_row, :]))
        h_s[...] = sin_rows(fr_ref[1:2, :] * (_dot3(h1, w2_ref[...]) + b2_ref[layer_row, :]))

    @pl.when(b == 0)
    def _():
        g_top, g_bot = _filter_spectrum(h_s[...], w3f_ref, w3b_ref, dl_ref, fwd_ref, L)
        g_s[0:L, :] = g_top
        g_s[L:2 * L, :] = g_bot

    tn = o_ref.shape[1]
    first = lax.broadcasted_iota(jnp.int32, (L, tn), 0) == 0
    for s in range(o_ref.shape[0] // L):
        rows = slice(s * L, (s + 1) * L)
        u = u_ref[rows, :]
        spec = _dot(fwd_ref[...], u)
        u_top, u_bot = spec[0:L, :], spec[L:2 * L, :]
        g_top, g_bot = g_s[0:L, :], g_s[L:2 * L, :]
        y_top = u_top * g_top - jnp.where(first, 0.0, u_bot * g_bot)
        y_bot = jnp.where(first, u_bot * g_bot, u_top * g_bot + u_bot * g_top)
        y_spec = jnp.concatenate([y_top, y_bot], axis=0).astype(BF16)
        y = _dot(inv_ref[...], y_spec)
        gated = x0_ref[rows, :].astype(F32) * (y + u.astype(F32) * bias_ref[layer_row, :])
        o_ref[rows, :] = gated.astype(BF16)


def _hyena_conv(u, x0, bias, jl, fwd, inv, filt, L, row_block0, n_steps, seq_per_step, tn):
    nj = D_MODEL // tn
    rows = seq_per_step * L
    feats = jnp.asarray(_filter_feats(L))
    deltas = jnp.asarray(np.abs(np.linspace(MIN_DECAY, MAX_DECAY, D_MODEL, dtype=np.float32)).reshape(1, D_MODEL))
    act = pl.BlockSpec((rows, tn), lambda j, b: (row_block0 + b, j))
    const = lambda j, b: (0, 0)
    layer = lambda j, b: (jl, 0, 0)
    f_w1, f_b1, f_freq, f_w2, f_b2, f_w3 = filt
    return pl.pallas_call(
        functools.partial(_hyena_conv_kernel, L=L, jl=jl),
        grid=(nj, n_steps),
        in_specs=[
            act, act,
            pl.BlockSpec((bias.shape[0], tn), lambda j, b: (0, j)),
            pl.BlockSpec((2 * L, L), const),
            pl.BlockSpec((L, 2 * L), const),
            pl.BlockSpec((L, FEAT_PAD), const),
            pl.BlockSpec((None, FEAT_PAD, FILTER_HIDDEN), layer),
            _whole_spec(f_b1),
            pl.BlockSpec((None, 2, FILTER_HIDDEN), layer),
            pl.BlockSpec((None, FILTER_HIDDEN, FILTER_HIDDEN), layer),
            _whole_spec(f_b2),
            pl.BlockSpec((None, FILTER_HIDDEN, tn), lambda j, b: (jl, 0, j)),
            pl.BlockSpec((None, FILTER_HIDDEN, tn), lambda j, b: (jl, 0, nj + j)),
            pl.BlockSpec((1, tn), lambda j, b: (0, j)),
        ],
        out_specs=pl.BlockSpec((rows, tn), lambda j, b: (b, j)),
        out_shape=jax.ShapeDtypeStruct((n_steps * rows, D_MODEL), BF16),
        scratch_shapes=[pltpu.VMEM((L, FILTER_HIDDEN), F32), pltpu.VMEM((2 * L, tn), F32)],
        compiler_params=_params(2),
        name="hyena_conv_%d" % L,
    )(u, x0, bias, fwd, inv, feats, f_w1, f_b1, f_freq, f_w2, f_b2, f_w3, f_w3, deltas)


def _rope_tables():
    rows = DEC_SEQ // GRID_W
    row = np.repeat(np.arange(rows), GRID_W).astype(np.float32)
    col = np.tile(np.arange(GRID_W), rows).astype(np.float32)
    half = HEAD_DIM // 2
    freqs = (np.float32(ROPE_THETA) ** (-np.arange(0, half, 2, dtype=np.float32) / np.float32(half))).astype(np.float32)
    ang = np.concatenate([row[:, None] * freqs[None, :], col[:, None] * freqs[None, :]], axis=-1)
    cos = np.repeat(np.cos(ang), 2, axis=-1)
    sin = np.repeat(np.sin(ang), 2, axis=-1)
    sign = np.where(np.arange(HEAD_DIM) % 2 == 0, -1.0, 1.0)[None, :]
    pair = lambda a: np.tile(a, (1, 2)).astype(np.float32)
    return pair(cos), pair(sin * sign)


def _qkv_kernel(*refs, n_prev, n_prompt_tiles, layer):
    x_ref, g_ref, sh_ref, sc_ref, w_ref, qg_ref, kg_ref, cos_ref, sin_ref = refs[:9]
    prev_refs, refs = refs[9:9 + 2 * (n_prev > 0)], refs[9 + 2 * (n_prev > 0):]
    q_ref, k_ref, v_ref, kf_ref, vf_ref, w_s = refs
    i = pl.program_id(0)
    tm = q_ref.shape[0]

    @pl.when(i == 0)
    def _():
        w_s[...] = w_ref[...].astype(BF16)

    k_col = N_HEADS * HEAD_DIM
    v_col = k_col + KV_DIM
    jl = n_prev
    q_gain = qg_ref[jl:jl + 1, :] * HEAD_DIM ** -0.5
    k_gain = kg_ref[jl:jl + 1, :]

    def project(rows):
        h = _modulate(x_ref[rows, :], g_ref[layer:layer + 1, :], sh_ref[0], sc_ref[0]).astype(BF16)
        qkv = _dot(h, w_s[...])
        v = qkv[:, v_col:]
        v_ref[rows, :] = v.astype(BF16)

        def pair_norm(col, gain):
            return jnp.concatenate([_rmsnorm(qkv[:, c:c + HEAD_DIM], gain) for c in (col, col + HEAD_DIM)], axis=1)

        return pair_norm, v

    @pl.when(i < n_prompt_tiles)
    def _():
        if n_prev:
            kf_ref[:, 0:n_prev] = prev_refs[0][...]
            vf_ref[:, 0:n_prev] = prev_refs[1][...]
        for r in range(tm // SEQ):
            rows = slice(r * SEQ, (r + 1) * SEQ)
            pair_norm, v = project(rows)
            for col in range(0, k_col, HEAD_PAIR):
                q_ref[rows, col:col + HEAD_PAIR] = pair_norm(col, q_gain).astype(BF16)
            k = pair_norm(k_col, k_gain)
            k_ref[rows, :] = k.astype(BF16)
            for kv in range(N_KV_HEADS):
                cols = slice(kv * HEAD_DIM, (kv + 1) * HEAD_DIM)
                head_rows = pl.ds(kv, SEQ, stride=N_KV_HEADS)
                kf_ref[r, n_prev, head_rows, :] = k[:, cols]
                vf_ref[r, n_prev, head_rows, :] = v[:, cols]

    @pl.when(i >= n_prompt_tiles)
    def _():
        r_id = lax.broadcasted_iota(jnp.int32, (HEAD_PAIR, HEAD_PAIR), 0)
        col_id = lax.broadcasted_iota(jnp.int32, (HEAD_PAIR, HEAD_PAIR), 1)
        swap = jnp.where((r_id ^ 1) == col_id, 1.0, 0.0).astype(BF16)
        for r in range(tm // LATENT_CHAIN):
            rows = slice(r * LATENT_CHAIN, (r + 1) * LATENT_CHAIN)
            pair_norm, _ = project(rows)

            def rotate(x):
                return x * cos_ref[rows, :] + _dot(x.astype(BF16), swap) * sin_ref[rows, :]

            for col in range(0, k_col, HEAD_PAIR):
                q_ref[rows, col:col + HEAD_PAIR] = rotate(pair_norm(col, q_gain)).astype(BF16)
            k_ref[rows, :] = rotate(pair_norm(k_col, k_gain)).astype(BF16)


def _qkv(y, norm_mix, mods, layer, w_qkv, q_gain, k_gain, jl, prev_kv):
    tm = 1024
    npt = N_PROMPT // tm
    tiles_per_req = DEC_SEQ // tm
    cos, sin = _rope_tables()
    rope_spec = pl.BlockSpec((tm, HEAD_PAIR), lambda i: (jnp.maximum(i - npt, 0) % tiles_per_req, 0))
    row = lambda n: pl.BlockSpec((tm, n), lambda i: (i, 0))
    cache_rows = SEQ * N_KV_HEADS
    cache = lambda n: pl.BlockSpec((tm // SEQ, n, cache_rows, HEAD_DIM), lambda i: (jnp.minimum(i, npt - 1), 0, 0, 0))
    return pl.pallas_call(
        functools.partial(_qkv_kernel, n_prev=jl, n_prompt_tiles=npt, layer=layer),
        grid=(N_TOK // tm,),
        in_specs=[
            row(D_MODEL),
            _whole_spec(norm_mix),
            _mod_spec(layer, 0, tm),
            _mod_spec(layer, 1, tm),
            pl.BlockSpec((None, D_MODEL, QKV_DIM), lambda i: (jl, 0, 0), pipeline_mode=pl.Buffered(1)),
            _whole_spec(q_gain), _whole_spec(k_gain),
            rope_spec, rope_spec,
        ] + [cache(jl)] * len(prev_kv),
        out_specs=[row(D_MODEL), row(KV_DIM), row(KV_DIM), cache(jl + 1), cache(jl + 1)],
        out_shape=[
            jax.ShapeDtypeStruct((N_TOK, D_MODEL), BF16),
            jax.ShapeDtypeStruct((N_TOK, KV_DIM), BF16),
            jax.ShapeDtypeStruct((N_TOK, KV_DIM), BF16),
            jax.ShapeDtypeStruct((BATCH, jl + 1, cache_rows, HEAD_DIM), F32),
            jax.ShapeDtypeStruct((BATCH, jl + 1, cache_rows, HEAD_DIM), F32),
        ],
        scratch_shapes=[pltpu.VMEM((D_MODEL, QKV_DIM), BF16)],
        compiler_params=_params(1),
        name="qkv_proj",
    )(y, norm_mix, mods, mods, w_qkv, q_gain, k_gain, jnp.asarray(cos), jnp.asarray(sin), *prev_kv)


def _attention_kernel(*refs, with_cache, seq):
    if with_cache:
        q_ref, k_ref, v_ref, ck_ref, cv_ref, o_ref = refs
    else:
        q_ref, k_ref, v_ref, o_ref = refs
    n_req = k_ref.shape[0] // seq
    tq = q_ref.shape[0] // n_req
    for r in range(n_req):
        q_rows, k_rows = slice(r * tq, (r + 1) * tq), slice(r * seq, (r + 1) * seq)
        for kv in range(k_ref.shape[1] // HEAD_DIM):
            kv_cols = slice(kv * HEAD_DIM, (kv + 1) * HEAD_DIM)
            k = k_ref[k_rows, kv_cols]
            v = v_ref[k_rows, kv_cols]
            if with_cache:
                head_rows = pl.ds(kv, PAST_LEN, stride=N_KV_HEADS)
                k = jnp.concatenate([k, ck_ref[0, 0, head_rows, :].astype(BF16)], axis=0)
                v = jnp.concatenate([v, cv_ref[0, 0, head_rows, :].astype(BF16)], axis=0)
            v_ones = jnp.concatenate([v, jnp.ones_like(v)], axis=1)
            for g in range(GROUP):
                cols = slice((kv * GROUP + g) * HEAD_DIM, (kv * GROUP + g + 1) * HEAD_DIM)
                s = lax.dot_general(q_ref[q_rows, cols], k, (((1,), (1,)), ((), ())), preferred_element_type=F32)
                p = jnp.exp(s - jnp.max(s, axis=-1, keepdims=True))
                o = _dot(p.astype(BF16), v_ones)
                o_ref[q_rows, cols] = (o[:, :HEAD_DIM] / o[:, HEAD_DIM:]).astype(BF16)


def _attention(q, k, v, row0, n_req, seq, tq, kv_per_step, req_per_step=1, cache_k=None, cache_v=None,
               cache_layer=0):
    nq = seq // tq
    n_req //= req_per_step
    tq, seq_rows = tq * req_per_step, seq * req_per_step
    qb0, kb0 = row0 // tq, row0 // seq_rows
    qw, kw = kv_per_step * GROUP * HEAD_DIM, kv_per_step * HEAD_DIM
    in_specs = [
        pl.BlockSpec((tq, qw), lambda b, h, t: (qb0 + b * nq + t, h)),
        pl.BlockSpec((seq_rows, kw), lambda b, h, t: (kb0 + b, h)),
        pl.BlockSpec((seq_rows, kw), lambda b, h, t: (kb0 + b, h)),
    ]
    args = [q, k, v]
    with_cache = cache_k is not None
    if with_cache:
        assert kv_per_step == N_KV_HEADS
        cspec = pl.BlockSpec((1, 1, PAST_LEN * N_KV_HEADS, HEAD_DIM), lambda b, h, t: (b, cache_layer, 0, 0))
        in_specs += [cspec, cspec]
        args += [cache_k, cache_v]
    return pl.pallas_call(
        functools.partial(_attention_kernel, with_cache=with_cache, seq=seq),
        grid=(n_req, N_KV_HEADS // kv_per_step, nq),
        in_specs=in_specs,
        out_specs=pl.BlockSpec((tq, qw), lambda b, h, t: (b * nq + t, h)),
        out_shape=jax.ShapeDtypeStruct((n_req * seq_rows, D_MODEL), BF16),
        compiler_params=_params(3),
        name="attention_%d" % seq,
    )(*args)


def _ffn_kernel(*refs, n_in, n_out, n_prompt_tiles, final, layer):
    y_refs, refs = refs[:n_in], refs[n_in:]
    (ap_ref, as_ref, wo_ref, gm_ref, g_ref, sh_ref, sc_ref, gate_ref, wg_ref, wu_ref, wd_ref, fin_ref) = refs[:12]
    o_refs, refs = refs[12:12 + n_out], refs[12 + n_out:]
    wo_s, wg_s, wu_s, wd_s, ymid_s, f_s, acc_s = refs
    s = pl.program_id(0)
    nf = N_FF_CHUNKS
    tile = jnp.maximum(s - (nf - 1), 0)
    is_prompt = tile < n_prompt_tiles

    def prologue():
        a = jnp.where(is_prompt, ap_ref[...], as_ref[...])
        y = _read_stream(y_refs, is_prompt)
        y_mid = y + gm_ref[0] * _dot(a, wo_s[...])
        ymid_s[...] = y_mid
        f_s[...] = _modulate(y_mid, g_ref[layer:layer + 1, :], sh_ref[0], sc_ref[0]).astype(BF16)

    def chunk(c):
        f = f_s[...]
        hidden = _silu(_dot(f, wg_s[c])) * _dot(f, wu_s[c])
        return _dot(hidden.astype(BF16), wd_s[c])

    def epilogue(acc):
        out = ymid_s[...] + gate_ref[0] * acc
        if final:
            out = _rmsnorm(out, fin_ref[...])
        if n_out == 1:
            o_refs[0][...] = out
        else:
            @pl.when(is_prompt)
            def _():
                o_refs[0][...] = out

            @pl.when(jnp.logical_not(is_prompt))
            def _():
                o_refs[1][...] = out

    @pl.when(s == 0)
    def _():
        wo_s[...] = wo_ref[...].astype(BF16)
        prologue()

    @pl.when(s < nf)
    def _():
        wg_s[s] = wg_ref[...].astype(BF16)
        wu_s[s] = wu_ref[...].astype(BF16)
        wd_s[s] = wd_ref[...].astype(BF16)
        part = chunk(s)

        @pl.when(s == 0)
        def _():
            acc_s[...] = part

        @pl.when(s > 0)
        def _():
            acc_s[...] += part

        @pl.when(s == nf - 1)
        def _():
            epilogue(acc_s[...])

    @pl.when(s >= nf)
    def _():
        prologue()
        acc = chunk(0)
        for c in range(1, nf):
            acc = acc + chunk(c)
        epilogue(acc)


def _ffn(stream, ap, as_, w_out, jl, norm_ffn, mods, layer, wg, wu, wd, final_norm, final):
    tm = 512
    nf = N_FF_CHUNKS
    npt = N_PROMPT // tm
    n_steps = nf + N_TOK // tm - 1
    n_out = 2 if final else 1
    tile_of = lambda s: jnp.maximum(s - (nf - 1), 0)
    chunk_of = lambda s: jnp.minimum(s, nf - 1)
    out_shape = _pair_shapes(D_MODEL, F32) if final else [jax.ShapeDtypeStruct((N_TOK, D_MODEL), F32)]
    return pl.pallas_call(
        functools.partial(_ffn_kernel, n_in=len(stream), n_out=n_out, n_prompt_tiles=npt, final=final, layer=layer),
        grid=(n_steps,),
        in_specs=_stream_specs(len(stream), tm, tile_of) + _pair_specs(tm, D_MODEL, tile_of) + [
            pl.BlockSpec((None, D_MODEL, D_MODEL), lambda s: (jl, 0, 0), pipeline_mode=pl.Buffered(1)),
            _mod_spec(layer, 2, tm, tile_of),
            _whole_spec(norm_ffn),
            _mod_spec(layer, 3, tm, tile_of),
            _mod_spec(layer, 4, tm, tile_of),
            _mod_spec(layer, 5, tm, tile_of),
            pl.BlockSpec((None, D_MODEL, FF_CHUNK), lambda s: (layer, 0, chunk_of(s))),
            pl.BlockSpec((None, D_MODEL, FF_CHUNK), lambda s: (layer, 0, chunk_of(s))),
            pl.BlockSpec((None, FF_CHUNK, D_MODEL), lambda s: (layer, chunk_of(s), 0)),
            pl.BlockSpec((1, D_MODEL), lambda s: (0, 0)),
        ],
        out_specs=_stream_specs(n_out, tm, tile_of),
        out_shape=out_shape,
        scratch_shapes=[
            pltpu.VMEM((D_MODEL, D_MODEL), BF16),
            pltpu.VMEM((nf, D_MODEL, FF_CHUNK), BF16),
            pltpu.VMEM((nf, D_MODEL, FF_CHUNK), BF16),
            pltpu.VMEM((nf, FF_CHUNK, D_MODEL), BF16),
            pltpu.VMEM((tm, D_MODEL), F32),
            pltpu.VMEM((tm, D_MODEL), BF16),
            pltpu.VMEM((tm, D_MODEL), F32),
        ],
        compiler_params=_params(1),
        name="ffn",
    )(*stream, ap, as_, w_out, mods, norm_ffn, mods, mods, mods, wg, wu, wd, final_norm.reshape(1, D_MODEL))


def kernel(x_prompt, x_sample, cache_k, cache_v, c, c_ctx, mod_w, mod_b, norm_mix, norm_ffn, hy_w_in, hy_conv_w, hy_conv_b, hy_f_w1, hy_f_b1, hy_f_freq, hy_f_w2, hy_f_b2, hy_f_w3, hy_bias, hy_w_out, at_w_qkv, at_q_norm, at_k_norm, at_w_out, ffn_w_gate, ffn_w_up, ffn_w_down, final_norm):
    stream = (x_prompt.reshape(N_PROMPT, D_MODEL), x_sample.reshape(N_SAMPLE, D_MODEL))
    mods = _adaln(c_ctx, c, mod_w, mod_b)

    tables = {}
    for L in (SEQ, DEC_SEQ):
        fwd, inv = _dft_tables(L)
        tables[L] = (jnp.asarray(fwd).astype(BF16), jnp.asarray(inv).astype(BF16))

    f_w1 = jnp.pad(hy_f_w1, ((0, 0), (0, FEAT_PAD - FILTER_EMB), (0, 0)))
    f_b1, f_b2, bias, q_gain, k_gain = hy_f_b1, hy_f_b2, hy_bias, at_q_norm, at_k_norm
    cache_k = cache_k.reshape(DEC_BATCH, -1, PAST_LEN * N_KV_HEADS, HEAD_DIM)
    cache_v = cache_v.reshape(DEC_BATCH, -1, PAST_LEN * N_KV_HEADS, HEAD_DIM)

    new_kv = ()
    for layer in range(DEPTH):
        jl = layer // N_MIXERS
        if layer % N_MIXERS == 0:
            x0, u = _in_proj(stream, norm_mix, mods, layer, hy_w_in, hy_conv_w, hy_conv_b, jl)
            mixed = []
            filt = (f_w1, f_b1, hy_f_freq, hy_f_w2, f_b2, hy_f_w3)
            for L, row_block0, n_steps, seq_per_step, tn in (
                    (SEQ, 0, BATCH // 4, 4, D_MODEL),
                    (DEC_SEQ, N_PROMPT // DEC_SEQ, DEC_BATCH, 1, 512)):
                fwd_bf16, inv_bf16 = tables[L]
                mixed.append(_hyena_conv(u, x0, bias, jl, fwd_bf16, inv_bf16, filt, L, row_block0, n_steps,
                                         seq_per_step, tn))
            w_out = hy_w_out
        else:
            q, k, v, *new_kv = _qkv(stream[0], norm_mix, mods, layer, at_w_qkv, q_gain, k_gain, jl, new_kv)
            mixed = [_attention(q, k, v, 0, BATCH, SEQ, SEQ, N_KV_HEADS, req_per_step=4),
                     _attention(q, k, v, N_PROMPT, DEC_BATCH, DEC_SEQ, 1024, N_KV_HEADS, cache_k=cache_k, cache_v=cache_v,
                                cache_layer=jl)]
            w_out = at_w_out
        stream = tuple(_ffn(stream, mixed[0], mixed[1], w_out, jl, norm_ffn, mods, layer,
                            ffn_w_gate, ffn_w_up, ffn_w_down, final_norm, final=(layer == DEPTH - 1)))

    y_prompt, y_sample = stream
    new_k, new_v = (a.reshape(BATCH, -1, SEQ, N_KV_HEADS, HEAD_DIM) for a in new_kv)
    return (y_prompt.reshape(BATCH, SEQ, D_MODEL), y_sample.reshape(DEC_BATCH, DEC_SEQ, D_MODEL), new_k, new_v)
```

```python
import functools
import math

import numpy as np
import jax
import jax.numpy as jnp
from jax import lax
from jax.experimental import pallas as pl
from jax.experimental.pallas import tpu as pltpu

D_MODEL = 1024
BATCH = 16
SEQ = 256
DEPTH = 4
DEC_BATCH = 4
DEC_SEQ = 1024
PAST_LEN = 512
GRID_W = 64
N_MIXERS = 2
HEAD_DIM = 128
N_HEADS = D_MODEL // HEAD_DIM
N_KV_HEADS = 2
GROUP = N_HEADS // N_KV_HEADS
KV_DIM = N_KV_HEADS * HEAD_DIM
HEAD_PAIR = 2 * HEAD_DIM
QKV_DIM = (N_HEADS + 2 * N_KV_HEADS) * HEAD_DIM
ROPE_THETA = 10000.0
D_FF = ((8 * D_MODEL + 3 * 256 - 1) // (3 * 256)) * 256
N_BANDS = 16
FILTER_EMB = 1 + 2 * N_BANDS
FILTER_HIDDEN = 64
MIN_DECAY = math.log(1e-2) / 1.5
MAX_DECAY = math.log(1e-2) / 0.3
EPS = 1e-6

N_PROMPT = BATCH * SEQ
N_SAMPLE = DEC_BATCH * DEC_SEQ
N_TOK = N_PROMPT + N_SAMPLE
N_MOD = 6
MOD_ROWS = 8
FEAT_PAD = 128
CONV_PAD = 8
LATENT_CHAIN = 512
FF_CHUNK = 256
N_FF_CHUNKS = D_FF // FF_CHUNK
V7X_VMEM_LIMIT = 56 * 1024 * 1024

F32 = jnp.float32
BF16 = jnp.bfloat16


def _params(n_axes, vmem=V7X_VMEM_LIMIT):
    return pltpu.CompilerParams(dimension_semantics=("arbitrary",) * n_axes, vmem_limit_bytes=vmem)


def _mod_row(tile, tm):
    n_prompt_tiles = N_PROMPT // tm
    tiles_per_req = DEC_SEQ // tm
    return jnp.where(tile < n_prompt_tiles, 0, 1 + jnp.maximum(tile - n_prompt_tiles, 0) // tiles_per_req)


def _mod_spec(layer, which, tm, tile_of=lambda i, *_: i):
    def index(*ids):
        return ((layer * N_MOD + which) * MOD_ROWS + _mod_row(tile_of(*ids), tm), 0, 0)
    return pl.BlockSpec((1, 1, D_MODEL), index)


def _whole_spec(a):
    return pl.BlockSpec(a.shape, lambda *_: (0,) * a.ndim)


def _pair_specs(tm, width, tile_of=lambda i, *_: i):
    npt = N_PROMPT // tm
    return [pl.BlockSpec((tm, width), lambda *ids: (jnp.minimum(tile_of(*ids), npt - 1), 0)),
            pl.BlockSpec((tm, width), lambda *ids: (jnp.maximum(tile_of(*ids) - npt, 0), 0))]


def _pair_shapes(width, dtype):
    return [jax.ShapeDtypeStruct((N_PROMPT, width), dtype), jax.ShapeDtypeStruct((N_SAMPLE, width), dtype)]


def _stream_specs(n_arrays, tm, tile_of=lambda i, *_: i):
    if n_arrays == 2:
        return _pair_specs(tm, D_MODEL, tile_of)
    return [pl.BlockSpec((tm, D_MODEL), lambda *ids: (tile_of(*ids), 0))]


def _read_stream(refs, is_prompt):
    if len(refs) == 2:
        return jnp.where(is_prompt, refs[0][...], refs[1][...])
    return refs[0][...]


def _split_bf16(a):
    hi = a.astype(BF16)
    lo = (a - hi.astype(F32)).astype(BF16)
    return hi, lo


def _dot(a, b):
    return jnp.dot(a, b, preferred_element_type=F32)


def _dot3(a, b):
    a_hi, a_lo = _split_bf16(a)
    b_hi, b_lo = _split_bf16(b)
    return _dot(a_hi, b_hi) + (_dot(a_lo, b_hi) + _dot(a_hi, b_lo))


def _silu(a):
    return a / (1.0 + jnp.exp(-a))


def _rmsnorm(x, g):
    return x * lax.rsqrt(jnp.mean(x * x, axis=-1, keepdims=True) + EPS) * g


def _modulate(x, g, shift, scale):
    return _rmsnorm(x, g) * (1.0 + scale) + shift


def _adaln_kernel(cctx_ref, c_ref, w_ref, b_ref, fs_ref, fl_ref, w1_ref, b1_ref, fr_ref, w2_ref, b2_ref,
                  o_ref, hs_ref, hl_ref, *, n_filters):
    l, j = pl.program_id(0), pl.program_id(1)
    pad = jnp.zeros((MOD_ROWS - 1 - DEC_BATCH, D_MODEL), F32)
    s = _silu(jnp.concatenate([cctx_ref[...], c_ref[...], pad], axis=0))
    bias = b_ref[pl.ds(l, 1), :]
    m = _dot(s.astype(BF16), w_ref[0].astype(BF16)) + bias
    for v in range(m.shape[1] // D_MODEL):
        o_ref[v * MOD_ROWS:(v + 1) * MOD_ROWS, 0, :] = m[:, v * D_MODEL:(v + 1) * D_MODEL]

    def filter_mlp(feats_ref, h_ref):
        rows = feats_ref.shape[0]

        def sin_rows(a):
            t = jnp.sin(jnp.concatenate([a[:rows // 2, :], a[rows // 2:, :]], axis=1))
            return jnp.concatenate([t[:, :FILTER_HIDDEN], t[:, FILTER_HIDDEN:]], axis=0)

        h1 = sin_rows(fr_ref[0:1, :] * (_dot3(feats_ref[...], w1_ref[...]) + b1_ref[pl.ds(l, 1), :]))
        h_ref[...] = sin_rows(fr_ref[1:2, :] * (_dot3(h1, w2_ref[...]) + b2_ref[pl.ds(l, 1), :]))

    @pl.when(jnp.logical_and(l < n_filters, j == 0))
    def _():
        filter_mlp(fs_ref, hs_ref)

    @pl.when(jnp.logical_and(l < n_filters, j == 1))
    def _():
        filter_mlp(fl_ref, hl_ref)


def _adaln(c_ctx, c, mod_w, mod_b, f_w1, f_b1, f_freq, f_w2, f_b2):
    per_step = 3
    tn = per_step * D_MODEL
    nf = f_w1.shape[0]
    assert nf <= DEPTH and N_MOD // per_step == 2
    filt = lambda l, j: (jnp.minimum(l, nf - 1), 0, 0)
    return pl.pallas_call(
        functools.partial(_adaln_kernel, n_filters=nf),
        grid=(DEPTH, N_MOD // per_step),
        in_specs=[
            pl.BlockSpec((1, D_MODEL), lambda l, j: (0, 0)),
            pl.BlockSpec((DEC_BATCH, D_MODEL), lambda l, j: (0, 0)),
            pl.BlockSpec((1, D_MODEL, tn), lambda l, j: (l, 0, j)),
            pl.BlockSpec((DEPTH, tn), lambda l, j: (0, j)),
            pl.BlockSpec((SEQ, FEAT_PAD), lambda l, j: (0, 0)),
            pl.BlockSpec((DEC_SEQ, FEAT_PAD), lambda l, j: (0, 0)),
            pl.BlockSpec((None, FEAT_PAD, FILTER_HIDDEN), filt),
            _whole_spec(f_b1),
            pl.BlockSpec((None, 2, FILTER_HIDDEN), filt),
            pl.BlockSpec((None, FILTER_HIDDEN, FILTER_HIDDEN), filt),
            _whole_spec(f_b2),
        ],
        out_specs=[
            pl.BlockSpec((per_step * MOD_ROWS, 1, D_MODEL), lambda l, j: (l * (N_MOD // per_step) + j, 0, 0)),
            pl.BlockSpec((None, SEQ, FILTER_HIDDEN), filt),
            pl.BlockSpec((None, DEC_SEQ, FILTER_HIDDEN), filt),
        ],
        out_shape=[
            jax.ShapeDtypeStruct((DEPTH * N_MOD * MOD_ROWS, 1, D_MODEL), F32),
            jax.ShapeDtypeStruct((nf, SEQ, FILTER_HIDDEN), F32),
            jax.ShapeDtypeStruct((nf, DEC_SEQ, FILTER_HIDDEN), F32),
        ],
        compiler_params=_params(2),
        name="adaln",
    )(c_ctx.reshape(1, D_MODEL), c, mod_w, mod_b, jnp.asarray(_filter_feats(SEQ)), jnp.asarray(_filter_feats(DEC_SEQ)),
      f_w1, f_b1, f_freq, f_w2, f_b2)


def _in_proj_kernel(*refs, n_stream, n_prompt_tiles, layer, jl):
    x_refs, refs = refs[:n_stream], refs[n_stream:]
    (g_ref, sh_ref, sc_ref, w0_ref, w1_ref, w2_ref, cw0_ref, cw1_ref, cw2_ref, cb0_ref, cb1_ref, cb2_ref,
     x0_ref, u_ref, w_s) = refs
    i = pl.program_id(0)
    tm, tc = x0_ref.shape

    @pl.when(i == 0)
    def _():
        w_s[0] = w0_ref[...].astype(BF16)
        w_s[1] = w1_ref[...].astype(BF16)
        w_s[2] = w2_ref[...].astype(BF16)

    n_chain = tm // SEQ
    pad = jnp.zeros((CONV_PAD, tc), F32)

    def tile(seq_len):
        hs = []
        for b in range(n_chain):
            x = _read_stream([r.at[b * SEQ:(b + 1) * SEQ, :] for r in x_refs], i < n_prompt_tiles)
            hs.append(_modulate(x, g_ref[layer:layer + 1, :], sh_ref[0], sc_ref[0]).astype(BF16))

        def short_conv(part, cw_ref, cb_ref):
            zs = [_dot(h, w_s[part]) for h in hs]
            out = []
            for b, z in enumerate(zs):
                before = zs[b - 1][SEQ - CONV_PAD:, :] if (b * SEQ) % seq_len else pad
                after = zs[b + 1][:CONV_PAD, :] if ((b + 1) * SEQ) % seq_len else pad
                ext = jnp.concatenate([before, z, after], axis=0)
                rows = ext.shape[0]
                z_prev = pltpu.roll(ext, 1, 0)[CONV_PAD:CONV_PAD + SEQ, :]
                z_next = pltpu.roll(ext, rows - 1, 0)[CONV_PAD:CONV_PAD + SEQ, :]
                out.append(z_prev * cw_ref[0:1, :] + z * cw_ref[1:2, :] + z_next * cw_ref[2:3, :]
                           + cb_ref[jl:jl + 1, :])
            return out

        for b, x0 in enumerate(short_conv(0, cw0_ref, cb0_ref)):
            x0_ref[b * SEQ:(b + 1) * SEQ, :] = x0.astype(BF16)
        for b, (x1, v) in enumerate(zip(short_conv(1, cw1_ref, cb1_ref), short_conv(2, cw2_ref, cb2_ref))):
            u_ref[b * SEQ:(b + 1) * SEQ, :] = (x1 * v).astype(BF16)

    @pl.when(i < n_prompt_tiles)
    def _():
        tile(SEQ)

    @pl.when(i >= n_prompt_tiles)
    def _():
        tile(DEC_SEQ)


def _in_proj(stream, norm_mix, mods, layer, w_in, conv_w, conv_b, jl):
    tm = 1024
    npt = N_PROMPT // tm

    def wspec(part):
        return pl.BlockSpec((None, D_MODEL, D_MODEL), lambda i: (jl, 0, part), pipeline_mode=pl.Buffered(1))

    def cwspec(part):
        return pl.BlockSpec((None, 3, D_MODEL), lambda i: (jl, 0, part))

    def cbspec(part):
        return pl.BlockSpec((conv_b.shape[0], D_MODEL), lambda i: (0, part))

    out = pl.BlockSpec((tm, D_MODEL), lambda i: (i, 0))
    cb = conv_b
    return pl.pallas_call(
        functools.partial(_in_proj_kernel, n_stream=len(stream), n_prompt_tiles=npt, layer=layer, jl=jl),
        grid=(N_TOK // tm,),
        in_specs=_stream_specs(len(stream), tm) + [
            _whole_spec(norm_mix),
            _mod_spec(layer, 0, tm),
            _mod_spec(layer, 1, tm),
            wspec(0), wspec(1), wspec(2),
            cwspec(0), cwspec(1), cwspec(2),
            cbspec(0), cbspec(1), cbspec(2),
        ],
        out_specs=[out, out],
        out_shape=[jax.ShapeDtypeStruct((N_TOK, D_MODEL), BF16)] * 2,
        scratch_shapes=[pltpu.VMEM((3, D_MODEL, D_MODEL), BF16)],
        compiler_params=_params(1),
        name="hyena_in_proj",
    )(*stream, norm_mix, mods, mods, w_in, w_in, w_in, conv_w, conv_w, conv_w, cb, cb, cb)


def _dft_tables(L):
    n = 2 * L
    k = np.arange(L, dtype=np.float64)[:, None]
    t = np.arange(L, dtype=np.float64)[None, :]
    ang = 2.0 * np.pi * k * t / n
    top = np.cos(ang)
    bot = -np.sin(ang)
    bot[0, :] = np.where(np.arange(L) % 2 == 0, 1.0, -1.0)
    fwd = np.concatenate([top, bot], axis=0)
    wk = np.full((L,), 2.0)
    wk[0] = 1.0
    inv_top = (np.cos(ang) * wk[:, None]).T / n
    inv_bot = (-2.0 * np.sin(ang)).T / n
    inv_bot[:, 0] = np.where(np.arange(L) % 2 == 0, 1.0, -1.0) / n
    inv = np.concatenate([inv_top, inv_bot], axis=1)
    return fwd.astype(np.float32), inv.astype(np.float32)


def _filter_feats(L):
    t = np.arange(L, dtype=np.float32) / np.float32(L)
    bands = np.arange(1, N_BANDS + 1, dtype=np.float32)
    ang = (np.float32(2.0 * math.pi) * t[:, None]) * bands[None, :]
    feats = np.concatenate([t[:, None], np.cos(ang), np.sin(ang)], axis=-1).astype(np.float32)
    return np.pad(feats, ((0, 0), (0, FEAT_PAD - FILTER_EMB)))


def _filter_spectrum(h, w3f_ref, w3b_ref, dl_ref, fwd_ref, L):
    tn = w3f_ref.shape[1]
    rows = lax.broadcasted_iota(jnp.int32, (L, tn), 0)
    t = rows.astype(F32) / L
    window = jnp.exp(-t * dl_ref[...])
    hf = _dot3(h, w3f_ref[...]) * window
    hb = jnp.where(rows > 0, _dot3(h, w3b_ref[...]) * window, 0.0)
    norm = jnp.sqrt(jnp.sum(hf * hf + hb * hb, axis=0, keepdims=True) + EPS)
    hf = hf / norm
    hb = hb / norm
    g_top = _dot(fwd_ref[0:L, :], (hf + hb).astype(BF16))
    g_bot = _dot(fwd_ref[L:2 * L, :], (hf - hb).astype(BF16))
    sign = jnp.where(rows % 2 == 0, 1.0, -1.0)
    nyquist_b = jnp.sum(sign * hb, axis=0, keepdims=True)
    return g_top, g_bot + jnp.where(rows == 0, 2.0 * nyquist_b, 0.0)


def _hyena_conv_kernel(u_ref, x0_ref, bias_ref, fwd_ref, inv_ref, h_ref, w3f_ref, w3b_ref, dl_ref, o_ref, g_s, *, L, jl):
    b = pl.program_id(1)
    layer_row = slice(jl, jl + 1)

    @pl.when(b == 0)
    def _():
        g_top, g_bot = _filter_spectrum(h_ref[...], w3f_ref, w3b_ref, dl_ref, fwd_ref, L)
        g_s[0:L, :] = g_top
        g_s[L:2 * L, :] = g_bot

    tn = o_ref.shape[1]
    first = lax.broadcasted_iota(jnp.int32, (L, tn), 0) == 0
    for s in range(o_ref.shape[0] // L):
        rows = slice(s * L, (s + 1) * L)
        u = u_ref[rows, :]
        spec = _dot(fwd_ref[...], u)
        u_top, u_bot = spec[0:L, :], spec[L:2 * L, :]
        g_top, g_bot = g_s[0:L, :], g_s[L:2 * L, :]
        y_top = u_top * g_top - jnp.where(first, 0.0, u_bot * g_bot)
        y_bot = jnp.where(first, u_bot * g_bot, u_top * g_bot + u_bot * g_top)
        y_spec = jnp.concatenate([y_top, y_bot], axis=0).astype(BF16)
        y = _dot(inv_ref[...], y_spec)
        gated = x0_ref[rows, :].astype(F32) * (y + u.astype(F32) * bias_ref[layer_row, :])
        o_ref[rows, :] = gated.astype(BF16)


def _hyena_conv(u, x0, bias, jl, fwd, inv, h_filter, f_w3, L, row_block0, n_steps, seq_per_step, tn):
    nj = D_MODEL // tn
    rows = seq_per_step * L
    deltas = jnp.asarray(np.abs(np.linspace(MIN_DECAY, MAX_DECAY, D_MODEL, dtype=np.float32)).reshape(1, D_MODEL))
    act = pl.BlockSpec((rows, tn), lambda j, b: (row_block0 + b, j))
    const = lambda j, b: (0, 0)
    return pl.pallas_call(
        functools.partial(_hyena_conv_kernel, L=L, jl=jl),
        grid=(nj, n_steps),
        in_specs=[
            act, act,
            pl.BlockSpec((bias.shape[0], tn), lambda j, b: (0, j)),
            pl.BlockSpec((2 * L, L), const),
            pl.BlockSpec((L, 2 * L), const),
            pl.BlockSpec((None, L, FILTER_HIDDEN), lambda j, b: (jl, 0, 0)),
            pl.BlockSpec((None, FILTER_HIDDEN, tn), lambda j, b: (jl, 0, j)),
            pl.BlockSpec((None, FILTER_HIDDEN, tn), lambda j, b: (jl, 0, nj + j)),
            pl.BlockSpec((1, tn), lambda j, b: (0, j)),
        ],
        out_specs=pl.BlockSpec((rows, tn), lambda j, b: (b, j)),
        out_shape=jax.ShapeDtypeStruct((n_steps * rows, D_MODEL), BF16),
        scratch_shapes=[pltpu.VMEM((2 * L, tn), F32)],
        compiler_params=_params(2),
        name="hyena_conv_%d" % L,
    )(u, x0, bias, fwd, inv, h_filter, f_w3, f_w3, deltas)


def _rope_tables():
    rows = DEC_SEQ // GRID_W
    row = np.repeat(np.arange(rows), GRID_W).astype(np.float32)
    col = np.tile(np.arange(GRID_W), rows).astype(np.float32)
    half = HEAD_DIM // 2
    freqs = (np.float32(ROPE_THETA) ** (-np.arange(0, half, 2, dtype=np.float32) / np.float32(half))).astype(np.float32)
    ang = np.concatenate([row[:, None] * freqs[None, :], col[:, None] * freqs[None, :]], axis=-1)
    cos = np.repeat(np.cos(ang), 2, axis=-1)
    sin = np.repeat(np.sin(ang), 2, axis=-1)
    sign = np.where(np.arange(HEAD_DIM) % 2 == 0, -1.0, 1.0)[None, :]
    pair = lambda a: np.tile(a, (1, 2)).astype(np.float32)
    return pair(cos), pair(sin * sign)


def _qkv_kernel(*refs, n_prev, n_prompt_tiles, layer):
    x_ref, g_ref, sh_ref, sc_ref, w_ref, qg_ref, kg_ref, cos_ref, sin_ref = refs[:9]
    prev_refs, refs = refs[9:9 + 2 * (n_prev > 0)], refs[9 + 2 * (n_prev > 0):]
    q_ref, k_ref, v_ref, kf_ref, vf_ref, w_s = refs
    i = pl.program_id(0)
    tm = q_ref.shape[0]

    @pl.when(i == 0)
    def _():
        w_s[...] = w_ref[...].astype(BF16)

    k_col = N_HEADS * HEAD_DIM
    v_col = k_col + KV_DIM
    jl = n_prev
    q_gain = qg_ref[jl:jl + 1, :] * HEAD_DIM ** -0.5
    k_gain = kg_ref[jl:jl + 1, :]

    def project(rows):
        h = _modulate(x_ref[rows, :], g_ref[layer:layer + 1, :], sh_ref[0], sc_ref[0]).astype(BF16)
        qkv = _dot(h, w_s[...])
        v = qkv[:, v_col:]
        v_ref[rows, :] = v.astype(BF16)

        def pair_norm(col, gain):
            return jnp.concatenate([_rmsnorm(qkv[:, c:c + HEAD_DIM], gain) for c in (col, col + HEAD_DIM)], axis=1)

        return pair_norm, v

    @pl.when(i < n_prompt_tiles)
    def _():
        if n_prev:
            kf_ref[:, 0:n_prev] = prev_refs[0][...]
            vf_ref[:, 0:n_prev] = prev_refs[1][...]
        for r in range(tm // SEQ):
            rows = slice(r * SEQ, (r + 1) * SEQ)
            pair_norm, v = project(rows)
            for col in range(0, k_col, HEAD_PAIR):
                q_ref[rows, col:col + HEAD_PAIR] = pair_norm(col, q_gain).astype(BF16)
            k = pair_norm(k_col, k_gain)
            k_ref[rows, :] = k.astype(BF16)
            for kv in range(N_KV_HEADS):
                cols = slice(kv * HEAD_DIM, (kv + 1) * HEAD_DIM)
                head_rows = pl.ds(kv, SEQ, stride=N_KV_HEADS)
                kf_ref[r, n_prev, head_rows, :] = k[:, cols]
                vf_ref[r, n_prev, head_rows, :] = v[:, cols]

    @pl.when(i >= n_prompt_tiles)
    def _():
        r_id = lax.broadcasted_iota(jnp.int32, (HEAD_PAIR, HEAD_PAIR), 0)
        col_id = lax.broadcasted_iota(jnp.int32, (HEAD_PAIR, HEAD_PAIR), 1)
        swap = jnp.where((r_id ^ 1) == col_id, 1.0, 0.0).astype(BF16)
        for r in range(tm // LATENT_CHAIN):
            rows = slice(r * LATENT_CHAIN, (r + 1) * LATENT_CHAIN)
            pair_norm, _ = project(rows)

            def rotate(x):
                return x * cos_ref[rows, :] + _dot(x.astype(BF16), swap) * sin_ref[rows, :]

            for col in range(0, k_col, HEAD_PAIR):
                q_ref[rows, col:col + HEAD_PAIR] = rotate(pair_norm(col, q_gain)).astype(BF16)
            k_ref[rows, :] = rotate(pair_norm(k_col, k_gain)).astype(BF16)


def _qkv(y, norm_mix, mods, layer, w_qkv, q_gain, k_gain, jl, prev_kv):
    tm = 1024
    npt = N_PROMPT // tm
    tiles_per_req = DEC_SEQ // tm
    cos, sin = _rope_tables()
    rope_spec = pl.BlockSpec((tm, HEAD_PAIR), lambda i: (jnp.maximum(i - npt, 0) % tiles_per_req, 0))
    row = lambda n: pl.BlockSpec((tm, n), lambda i: (i, 0))
    cache_rows = SEQ * N_KV_HEADS
    cache = lambda n: pl.BlockSpec((tm // SEQ, n, cache_rows, HEAD_DIM), lambda i: (jnp.minimum(i, npt - 1), 0, 0, 0))
    return pl.pallas_call(
        functools.partial(_qkv_kernel, n_prev=jl, n_prompt_tiles=npt, layer=layer),
        grid=(N_TOK // tm,),
        in_specs=[
            row(D_MODEL),
            _whole_spec(norm_mix),
            _mod_spec(layer, 0, tm),
            _mod_spec(layer, 1, tm),
            pl.BlockSpec((None, D_MODEL, QKV_DIM), lambda i: (jl, 0, 0), pipeline_mode=pl.Buffered(1)),
            _whole_spec(q_gain), _whole_spec(k_gain),
            rope_spec, rope_spec,
        ] + [cache(jl)] * len(prev_kv),
        out_specs=[row(D_MODEL), row(KV_DIM), row(KV_DIM), cache(jl + 1), cache(jl + 1)],
        out_shape=[
            jax.ShapeDtypeStruct((N_TOK, D_MODEL), BF16),
            jax.ShapeDtypeStruct((N_TOK, KV_DIM), BF16),
            jax.ShapeDtypeStruct((N_TOK, KV_DIM), BF16),
            jax.ShapeDtypeStruct((BATCH, jl + 1, cache_rows, HEAD_DIM), F32),
            jax.ShapeDtypeStruct((BATCH, jl + 1, cache_rows, HEAD_DIM), F32),
        ],
        scratch_shapes=[pltpu.VMEM((D_MODEL, QKV_DIM), BF16)],
        compiler_params=_params(1),
        name="qkv_proj",
    )(y, norm_mix, mods, mods, w_qkv, q_gain, k_gain, jnp.asarray(cos), jnp.asarray(sin), *prev_kv)


def _attention_kernel(*refs, with_cache, seq):
    if with_cache:
        q_ref, k_ref, v_ref, ck_ref, cv_ref, o_ref = refs
    else:
        q_ref, k_ref, v_ref, o_ref = refs
    n_req = k_ref.shape[0] // seq
    tq = q_ref.shape[0] // n_req
    for r in range(n_req):
        q_rows, k_rows = slice(r * tq, (r + 1) * tq), slice(r * seq, (r + 1) * seq)
        for kv in range(k_ref.shape[1] // HEAD_DIM):
            kv_cols = slice(kv * HEAD_DIM, (kv + 1) * HEAD_DIM)
            k = k_ref[k_rows, kv_cols]
            v = v_ref[k_rows, kv_cols]
            if with_cache:
                head_rows = pl.ds(kv, PAST_LEN, stride=N_KV_HEADS)
                k = jnp.concatenate([k, ck_ref[0, 0, head_rows, :].astype(BF16)], axis=0)
                v = jnp.concatenate([v, cv_ref[0, 0, head_rows, :].astype(BF16)], axis=0)
            v_ones = jnp.concatenate([v, jnp.ones_like(v)], axis=1)
            for g in range(GROUP):
                cols = slice((kv * GROUP + g) * HEAD_DIM, (kv * GROUP + g + 1) * HEAD_DIM)
                s = lax.dot_general(q_ref[q_rows, cols], k, (((1,), (1,)), ((), ())), preferred_element_type=F32)
                p = jnp.exp(s - jnp.max(s, axis=-1, keepdims=True))
                o = _dot(p.astype(BF16), v_ones)
                o_ref[q_rows, cols] = (o[:, :HEAD_DIM] / o[:, HEAD_DIM:]).astype(BF16)


def _attention(q, k, v, row0, n_req, seq, tq, kv_per_step, req_per_step=1, cache_k=None, cache_v=None,
               cache_layer=0):
    nq = seq // tq
    n_req //= req_per_step
    tq, seq_rows = tq * req_per_step, seq * req_per_step
    qb0, kb0 = row0 // tq, row0 // seq_rows
    qw, kw = kv_per_step * GROUP * HEAD_DIM, kv_per_step * HEAD_DIM
    in_specs = [
        pl.BlockSpec((tq, qw), lambda b, h, t: (qb0 + b * nq + t, h)),
        pl.BlockSpec((seq_rows, kw), lambda b, h, t: (kb0 + b, h)),
        pl.BlockSpec((seq_rows, kw), lambda b, h, t: (kb0 + b, h)),
    ]
    args = [q, k, v]
    with_cache = cache_k is not None
    if with_cache:
        assert kv_per_step == N_KV_HEADS
        cspec = pl.BlockSpec((1, 1, PAST_LEN * N_KV_HEADS, HEAD_DIM), lambda b, h, t: (b, cache_layer, 0, 0))
        in_specs += [cspec, cspec]
        args += [cache_k, cache_v]
    return pl.pallas_call(
        functools.partial(_attention_kernel, with_cache=with_cache, seq=seq),
        grid=(n_req, N_KV_HEADS // kv_per_step, nq),
        in_specs=in_specs,
        out_specs=pl.BlockSpec((tq, qw), lambda b, h, t: (b * nq + t, h)),
        out_shape=jax.ShapeDtypeStruct((n_req * seq_rows, D_MODEL), BF16),
        compiler_params=_params(3),
        name="attention_%d" % seq,
    )(*args)


def _ffn_kernel(*refs, n_in, n_out, n_prompt_tiles, final, layer):
    y_refs, refs = refs[:n_in], refs[n_in:]
    (ap_ref, as_ref, wo_ref, gm_ref, g_ref, sh_ref, sc_ref, gate_ref, wg_ref, wu_ref, wd_ref, fin_ref) = refs[:12]
    o_refs, refs = refs[12:12 + n_out], refs[12 + n_out:]
    wo_s, wg_s, wu_s, wd_s, ymid_s, f_s, acc_s = refs
    s = pl.program_id(0)
    nf = N_FF_CHUNKS
    tile = jnp.maximum(s - (nf - 1), 0)
    is_prompt = tile < n_prompt_tiles

    def prologue():
        a = jnp.where(is_prompt, ap_ref[...], as_ref[...])
        y = _read_stream(y_refs, is_prompt)
        y_mid = y + gm_ref[0] * _dot(a, wo_s[...])
        ymid_s[...] = y_mid
        f_s[...] = _modulate(y_mid, g_ref[layer:layer + 1, :], sh_ref[0], sc_ref[0]).astype(BF16)

    def chunk(c):
        f = f_s[...]
        hidden = _silu(_dot(f, wg_s[c])) * _dot(f, wu_s[c])
        return _dot(hidden.astype(BF16), wd_s[c])

    def epilogue(acc):
        out = ymid_s[...] + gate_ref[0] * acc
        if final:
            out = _rmsnorm(out, fin_ref[...])
        if n_out == 1:
            o_refs[0][...] = out
        else:
            @pl.when(is_prompt)
            def _():
                o_refs[0][...] = out

            @pl.when(jnp.logical_not(is_prompt))
            def _():
                o_refs[1][...] = out

    @pl.when(s == 0)
    def _():
        wo_s[...] = wo_ref[...].astype(BF16)
        prologue()

    @pl.when(s < nf)
    def _():
        wg_s[s] = wg_ref[...].astype(BF16)
        wu_s[s] = wu_ref[...].astype(BF16)
        wd_s[s] = wd_ref[...].astype(BF16)
        part = chunk(s)

        @pl.when(s == 0)
        def _():
            acc_s[...] = part

        @pl.when(s > 0)
        def _():
            acc_s[...] += part

        @pl.when(s == nf - 1)
        def _():
            epilogue(acc_s[...])

    @pl.when(s >= nf)
    def _():
        prologue()
        acc = chunk(0)
        for c in range(1, nf):
            acc = acc + chunk(c)
        epilogue(acc)


def _ffn(stream, ap, as_, w_out, jl, norm_ffn, mods, layer, wg, wu, wd, final_norm, final):
    tm = 512
    nf = N_FF_CHUNKS
    npt = N_PROMPT // tm
    n_steps = nf + N_TOK // tm - 1
    n_out = 2 if final else 1
    tile_of = lambda s: jnp.maximum(s - (nf - 1), 0)
    chunk_of = lambda s: jnp.minimum(s, nf - 1)
    out_shape = _pair_shapes(D_MODEL, F32) if final else [jax.ShapeDtypeStruct((N_TOK, D_MODEL), F32)]
    return pl.pallas_call(
        functools.partial(_ffn_kernel, n_in=len(stream), n_out=n_out, n_prompt_tiles=npt, final=final, layer=layer),
        grid=(n_steps,),
        in_specs=_stream_specs(len(stream), tm, tile_of) + _pair_specs(tm, D_MODEL, tile_of) + [
            pl.BlockSpec((None, D_MODEL, D_MODEL), lambda s: (jl, 0, 0), pipeline_mode=pl.Buffered(1)),
            _mod_spec(layer, 2, tm, tile_of),
            _whole_spec(norm_ffn),
            _mod_spec(layer, 3, tm, tile_of),
            _mod_spec(layer, 4, tm, tile_of),
            _mod_spec(layer, 5, tm, tile_of),
            pl.BlockSpec((None, D_MODEL, FF_CHUNK), lambda s: (layer, 0, chunk_of(s))),
            pl.BlockSpec((None, D_MODEL, FF_CHUNK), lambda s: (layer, 0, chunk_of(s))),
            pl.BlockSpec((None, FF_CHUNK, D_MODEL), lambda s: (layer, chunk_of(s), 0)),
            pl.BlockSpec((1, D_MODEL), lambda s: (0, 0)),
        ],
        out_specs=_stream_specs(n_out, tm, tile_of),
        out_shape=out_shape,
        scratch_shapes=[
            pltpu.VMEM((D_MODEL, D_MODEL), BF16),
            pltpu.VMEM((nf, D_MODEL, FF_CHUNK), BF16),
            pltpu.VMEM((nf, D_MODEL, FF_CHUNK), BF16),
            pltpu.VMEM((nf, FF_CHUNK, D_MODEL), BF16),
            pltpu.VMEM((tm, D_MODEL), F32),
            pltpu.VMEM((tm, D_MODEL), BF16),
            pltpu.VMEM((tm, D_MODEL), F32),
        ],
        compiler_params=_params(1),
        name="ffn",
    )(*stream, ap, as_, w_out, mods, norm_ffn, mods, mods, mods, wg, wu, wd, final_norm.reshape(1, D_MODEL))


def kernel(x_prompt, x_sample, cache_k, cache_v, c, c_ctx, mod_w, mod_b, norm_mix, norm_ffn, hy_w_in, hy_conv_w, hy_conv_b, hy_f_w1, hy_f_b1, hy_f_freq, hy_f_w2, hy_f_b2, hy_f_w3, hy_bias, hy_w_out, at_w_qkv, at_q_norm, at_k_norm, at_w_out, ffn_w_gate, ffn_w_up, ffn_w_down, final_norm):
    stream = (x_prompt.reshape(N_PROMPT, D_MODEL), x_sample.reshape(N_SAMPLE, D_MODEL))
    f_w1 = jnp.pad(hy_f_w1, ((0, 0), (0, FEAT_PAD - FILTER_EMB), (0, 0)))
    mods, h_short, h_long = _adaln(c_ctx, c, mod_w, mod_b, f_w1, hy_f_b1, hy_f_freq, hy_f_w2, hy_f_b2)
    h_filter = {SEQ: h_short, DEC_SEQ: h_long}

    tables = {}
    for L in (SEQ, DEC_SEQ):
        fwd, inv = _dft_tables(L)
        tables[L] = (jnp.asarray(fwd).astype(BF16), jnp.asarray(inv).astype(BF16))

    bias, q_gain, k_gain = hy_bias, at_q_norm, at_k_norm
    cache_k = cache_k.reshape(DEC_BATCH, -1, PAST_LEN * N_KV_HEADS, HEAD_DIM)
    cache_v = cache_v.reshape(DEC_BATCH, -1, PAST_LEN * N_KV_HEADS, HEAD_DIM)

    new_kv = ()
    for layer in range(DEPTH):
        jl = layer // N_MIXERS
        if layer % N_MIXERS == 0:
            x0, u = _in_proj(stream, norm_mix, mods, layer, hy_w_in, hy_conv_w, hy_conv_b, jl)
            mixed = []
            for L, row_block0, n_steps, seq_per_step, tn in (
                    (SEQ, 0, BATCH // 4, 4, D_MODEL),
                    (DEC_SEQ, N_PROMPT // DEC_SEQ, DEC_BATCH, 1, 512)):
                fwd_bf16, inv_bf16 = tables[L]
                mixed.append(_hyena_conv(u, x0, bias, jl, fwd_bf16, inv_bf16, h_filter[L], hy_f_w3, L, row_block0,
                                         n_steps, seq_per_step, tn))
            w_out = hy_w_out
        else:
            q, k, v, *new_kv = _qkv(stream[0], norm_mix, mods, layer, at_w_qkv, q_gain, k_gain, jl, new_kv)
            mixed = [_attention(q, k, v, 0, BATCH, SEQ, SEQ, N_KV_HEADS, req_per_step=4),
                     _attention(q, k, v, N_PROMPT, DEC_BATCH, DEC_SEQ, 1024, N_KV_HEADS, cache_k=cache_k, cache_v=cache_v,
                                cache_layer=jl)]
            w_out = at_w_out
        stream = tuple(_ffn(stream, mixed[0], mixed[1], w_out, jl, norm_ffn, mods, layer,
                            ffn_w_gate, ffn_w_up, ffn_w_down, final_norm, final=(layer == DEPTH - 1)))

    y_prompt, y_sample = stream
    new_k, new_v = (a.reshape(BATCH, -1, SEQ, N_KV_HEADS, HEAD_DIM) for a in new_kv)
    return (y_prompt.reshape(BATCH, SEQ, D_MODEL), y_sample.reshape(DEC_BATCH, DEC_SEQ, D_MODEL), new_k, new_v)
```

```python
import functools
import math

import numpy as np
import jax
import jax.numpy as jnp
from jax import lax
from jax.experimental import pallas as pl
from jax.experimental.pallas import tpu as pltpu

D_MODEL = 1024
BATCH = 16
SEQ = 256
DEPTH = 4
DEC_BATCH = 4
DEC_SEQ = 1024
PAST_LEN = 512
GRID_W = 64
N_MIXERS = 2
HEAD_DIM = 128
N_HEADS = D_MODEL // HEAD_DIM
N_KV_HEADS = 2
GROUP = N_HEADS // N_KV_HEADS
KV_DIM = N_KV_HEADS * HEAD_DIM
HEAD_PAIR = 2 * HEAD_DIM
QKV_DIM = (N_HEADS + 2 * N_KV_HEADS) * HEAD_DIM
ROPE_THETA = 10000.0
D_FF = ((8 * D_MODEL + 3 * 256 - 1) // (3 * 256)) * 256
N_BANDS = 16
FILTER_EMB = 1 + 2 * N_BANDS
FILTER_HIDDEN = 64
MIN_DECAY = math.log(1e-2) / 1.5
MAX_DECAY = math.log(1e-2) / 0.3
EPS = 1e-6

N_PROMPT = BATCH * SEQ
N_SAMPLE = DEC_BATCH * DEC_SEQ
N_TOK = N_PROMPT + N_SAMPLE
N_MOD = 6
MOD_ROWS = 8
FEAT_PAD = 128
CONV_PAD = 8
LATENT_CHAIN = 512
FF_CHUNK = 256
N_FF_CHUNKS = D_FF // FF_CHUNK
V7X_VMEM_LIMIT = 56 * 1024 * 1024

F32 = jnp.float32
BF16 = jnp.bfloat16


def _params(n_axes, vmem=V7X_VMEM_LIMIT):
    return pltpu.CompilerParams(dimension_semantics=("arbitrary",) * n_axes, vmem_limit_bytes=vmem)


def _mod_row(tile, tm):
    n_prompt_tiles = N_PROMPT // tm
    tiles_per_req = DEC_SEQ // tm
    return jnp.where(tile < n_prompt_tiles, 0, 1 + jnp.maximum(tile - n_prompt_tiles, 0) // tiles_per_req)


def _mod_spec(layer, which, tm, tile_of=lambda i, *_: i):
    def index(*ids):
        return ((layer * N_MOD + which) * MOD_ROWS + _mod_row(tile_of(*ids), tm), 0, 0)
    return pl.BlockSpec((1, 1, D_MODEL), index)


def _whole_spec(a):
    return pl.BlockSpec(a.shape, lambda *_: (0,) * a.ndim)


def _pair_specs(tm, width, tile_of=lambda i, *_: i):
    npt = N_PROMPT // tm
    return [pl.BlockSpec((tm, width), lambda *ids: (jnp.minimum(tile_of(*ids), npt - 1), 0)),
            pl.BlockSpec((tm, width), lambda *ids: (jnp.maximum(tile_of(*ids) - npt, 0), 0))]


def _pair_shapes(width, dtype):
    return [jax.ShapeDtypeStruct((N_PROMPT, width), dtype), jax.ShapeDtypeStruct((N_SAMPLE, width), dtype)]


def _stream_specs(n_arrays, tm, tile_of=lambda i, *_: i):
    if n_arrays == 2:
        return _pair_specs(tm, D_MODEL, tile_of)
    return [pl.BlockSpec((tm, D_MODEL), lambda *ids: (tile_of(*ids), 0))]


def _read_stream(refs, is_prompt):
    if len(refs) == 2:
        return jnp.where(is_prompt, refs[0][...], refs[1][...])
    return refs[0][...]


def _split_bf16(a):
    hi = a.astype(BF16)
    lo = (a - hi.astype(F32)).astype(BF16)
    return hi, lo


def _dot(a, b):
    return jnp.dot(a, b, preferred_element_type=F32)


def _dot3(a, b):
    a_hi, a_lo = _split_bf16(a)
    b_hi, b_lo = _split_bf16(b)
    return _dot(a_hi, b_hi) + (_dot(a_lo, b_hi) + _dot(a_hi, b_lo))


def _silu(a):
    return a / (1.0 + jnp.exp(-a))


def _rmsnorm(x, g):
    return x * lax.rsqrt(jnp.mean(x * x, axis=-1, keepdims=True) + EPS) * g


def _modulate(x, g, shift, scale):
    return _rmsnorm(x, g) * (1.0 + scale) + shift


def _adaln_kernel(cctx_ref, c_ref, w_ref, b_ref, fs_ref, fl_ref, w1_ref, b1_ref, fr_ref, w2_ref, b2_ref,
                  o_ref, hs_ref, hl_ref, *, n_filters):
    l, j = pl.program_id(0), pl.program_id(1)
    pad = jnp.zeros((MOD_ROWS - 1 - DEC_BATCH, D_MODEL), F32)
    s = _silu(jnp.concatenate([cctx_ref[...], c_ref[...], pad], axis=0))
    bias = b_ref[pl.ds(l, 1), :]
    m = _dot(s.astype(BF16), w_ref[0].astype(BF16)) + bias
    for v in range(m.shape[1] // D_MODEL):
        o_ref[v * MOD_ROWS:(v + 1) * MOD_ROWS, 0, :] = m[:, v * D_MODEL:(v + 1) * D_MODEL]

    fl_idx = l // (DEPTH // n_filters)

    def filter_mlp(feats_ref, h_ref, row0, rows):
        def sin_rows(a):
            t = jnp.sin(jnp.concatenate([a[:rows // 2, :], a[rows // 2:, :]], axis=1))
            return jnp.concatenate([t[:, :FILTER_HIDDEN], t[:, FILTER_HIDDEN:]], axis=0)

        feats = feats_ref[row0:row0 + rows, :]
        h1 = sin_rows(fr_ref[0:1, :] * (_dot3(feats, w1_ref[...]) + b1_ref[pl.ds(fl_idx, 1), :]))
        h_ref[row0:row0 + rows, :] = sin_rows(fr_ref[1:2, :] * (_dot3(h1, w2_ref[...]) + b2_ref[pl.ds(fl_idx, 1), :]))

    first_of_layer = l % (DEPTH // n_filters) == 0
    half = DEC_SEQ // 2

    @pl.when(jnp.logical_and(first_of_layer, j == 0))
    def _():
        filter_mlp(fs_ref, hs_ref, 0, SEQ)

    @pl.when(jnp.logical_and(first_of_layer, j == 1))
    def _():
        filter_mlp(fl_ref, hl_ref, 0, half)

    @pl.when(jnp.logical_and(jnp.logical_not(first_of_layer), j == 0))
    def _():
        filter_mlp(fl_ref, hl_ref, half, half)


def _adaln(c_ctx, c, mod_w, mod_b, f_w1, f_b1, f_freq, f_w2, f_b2):
    per_step = 3
    tn = per_step * D_MODEL
    nf = f_w1.shape[0]
    assert DEPTH == 2 * nf and N_MOD // per_step == 2
    filt = lambda l, j: (l // (DEPTH // nf), 0, 0)
    return pl.pallas_call(
        functools.partial(_adaln_kernel, n_filters=nf),
        grid=(DEPTH, N_MOD // per_step),
        in_specs=[
            pl.BlockSpec((1, D_MODEL), lambda l, j: (0, 0)),
            pl.BlockSpec((DEC_BATCH, D_MODEL), lambda l, j: (0, 0)),
            pl.BlockSpec((1, D_MODEL, tn), lambda l, j: (l, 0, j)),
            pl.BlockSpec((DEPTH, tn), lambda l, j: (0, j)),
            pl.BlockSpec((SEQ, FEAT_PAD), lambda l, j: (0, 0)),
            pl.BlockSpec((DEC_SEQ, FEAT_PAD), lambda l, j: (0, 0)),
            pl.BlockSpec((None, FEAT_PAD, FILTER_HIDDEN), filt),
            _whole_spec(f_b1),
            pl.BlockSpec((None, 2, FILTER_HIDDEN), filt),
            pl.BlockSpec((None, FILTER_HIDDEN, FILTER_HIDDEN), filt),
            _whole_spec(f_b2),
        ],
        out_specs=[
            pl.BlockSpec((per_step * MOD_ROWS, 1, D_MODEL), lambda l, j: (l * (N_MOD // per_step) + j, 0, 0)),
            pl.BlockSpec((None, SEQ, FILTER_HIDDEN), filt),
            pl.BlockSpec((None, DEC_SEQ, FILTER_HIDDEN), filt),
        ],
        out_shape=[
            jax.ShapeDtypeStruct((DEPTH * N_MOD * MOD_ROWS, 1, D_MODEL), F32),
            jax.ShapeDtypeStruct((nf, SEQ, FILTER_HIDDEN), F32),
            jax.ShapeDtypeStruct((nf, DEC_SEQ, FILTER_HIDDEN), F32),
        ],
        compiler_params=_params(2),
        name="adaln",
    )(c_ctx.reshape(1, D_MODEL), c, mod_w, mod_b, jnp.asarray(_filter_feats(SEQ)), jnp.asarray(_filter_feats(DEC_SEQ)),
      f_w1, f_b1, f_freq, f_w2, f_b2)


def _in_proj_kernel(*refs, n_stream, n_prompt_tiles, layer, jl):
    x_refs, refs = refs[:n_stream], refs[n_stream:]
    (g_ref, sh_ref, sc_ref, w0_ref, w1_ref, w2_ref, cw0_ref, cw1_ref, cw2_ref, cb0_ref, cb1_ref, cb2_ref,
     x0_ref, u_ref, w_s) = refs
    i = pl.program_id(0)
    tm, tc = x0_ref.shape

    @pl.when(i == 0)
    def _():
        w_s[0] = w0_ref[...].astype(BF16)
        w_s[1] = w1_ref[...].astype(BF16)
        w_s[2] = w2_ref[...].astype(BF16)

    n_chain = tm // SEQ
    pad = jnp.zeros((CONV_PAD, tc), F32)

    def tile(seq_len):
        hs = []
        for b in range(n_chain):
            x = _read_stream([r.at[b * SEQ:(b + 1) * SEQ, :] for r in x_refs], i < n_prompt_tiles)
            hs.append(_modulate(x, g_ref[layer:layer + 1, :], sh_ref[0], sc_ref[0]).astype(BF16))

        def short_conv(part, cw_ref, cb_ref):
            zs = [_dot(h, w_s[part]) for h in hs]
            out = []
            for b, z in enumerate(zs):
                before = zs[b - 1][SEQ - CONV_PAD:, :] if (b * SEQ) % seq_len else pad
                after = zs[b + 1][:CONV_PAD, :] if ((b + 1) * SEQ) % seq_len else pad
                ext = jnp.concatenate([before, z, after], axis=0)
                rows = ext.shape[0]
                z_prev = pltpu.roll(ext, 1, 0)[CONV_PAD:CONV_PAD + SEQ, :]
                z_next = pltpu.roll(ext, rows - 1, 0)[CONV_PAD:CONV_PAD + SEQ, :]
                out.append(z_prev * cw_ref[0:1, :] + z * cw_ref[1:2, :] + z_next * cw_ref[2:3, :]
                           + cb_ref[jl:jl + 1, :])
            return out

        for b, x0 in enumerate(short_conv(0, cw0_ref, cb0_ref)):
            x0_ref[b * SEQ:(b + 1) * SEQ, :] = x0.astype(BF16)
        for b, (x1, v) in enumerate(zip(short_conv(1, cw1_ref, cb1_ref), short_conv(2, cw2_ref, cb2_ref))):
            u_ref[b * SEQ:(b + 1) * SEQ, :] = (x1 * v).astype(BF16)

    @pl.when(i < n_prompt_tiles)
    def _():
        tile(SEQ)

    @pl.when(i >= n_prompt_tiles)
    def _():
        tile(DEC_SEQ)


def _in_proj(stream, norm_mix, mods, layer, w_in, conv_w, conv_b, jl):
    tm = 1024
    npt = N_PROMPT // tm

    def wspec(part):
        return pl.BlockSpec((None, D_MODEL, D_MODEL), lambda i: (jl, 0, part), pipeline_mode=pl.Buffered(1))

    def cwspec(part):
        return pl.BlockSpec((None, 3, D_MODEL), lambda i: (jl, 0, part))

    def cbspec(part):
        return pl.BlockSpec((conv_b.shape[0], D_MODEL), lambda i: (0, part))

    out = pl.BlockSpec((tm, D_MODEL), lambda i: (i, 0))
    cb = conv_b
    return pl.pallas_call(
        functools.partial(_in_proj_kernel, n_stream=len(stream), n_prompt_tiles=npt, layer=layer, jl=jl),
        grid=(N_TOK // tm,),
        in_specs=_stream_specs(len(stream), tm) + [
            _whole_spec(norm_mix),
            _mod_spec(layer, 0, tm),
            _mod_spec(layer, 1, tm),
            wspec(0), wspec(1), wspec(2),
            cwspec(0), cwspec(1), cwspec(2),
            cbspec(0), cbspec(1), cbspec(2),
        ],
        out_specs=[out, out],
        out_shape=[jax.ShapeDtypeStruct((N_TOK, D_MODEL), BF16)] * 2,
        scratch_shapes=[pltpu.VMEM((3, D_MODEL, D_MODEL), BF16)],
        compiler_params=_params(1),
        name="hyena_in_proj",
    )(*stream, norm_mix, mods, mods, w_in, w_in, w_in, conv_w, conv_w, conv_w, cb, cb, cb)


def _dft_tables(L):
    n = 2 * L
    k = np.arange(L, dtype=np.float64)[:, None]
    t = np.arange(L, dtype=np.float64)[None, :]
    ang = 2.0 * np.pi * k * t / n
    top = np.cos(ang)
    bot = -np.sin(ang)
    bot[0, :] = np.where(np.arange(L) % 2 == 0, 1.0, -1.0)
    fwd = np.concatenate([top, bot], axis=0)
    wk = np.full((L,), 2.0)
    wk[0] = 1.0
    inv_top = (np.cos(ang) * wk[:, None]).T / n
    inv_bot = (-2.0 * np.sin(ang)).T / n
    inv_bot[:, 0] = np.where(np.arange(L) % 2 == 0, 1.0, -1.0) / n
    inv = np.concatenate([inv_top, inv_bot], axis=1)
    return fwd.astype(np.float32), inv.astype(np.float32)


def _filter_feats(L):
    t = np.arange(L, dtype=np.float32) / np.float32(L)
    bands = np.arange(1, N_BANDS + 1, dtype=np.float32)
    ang = (np.float32(2.0 * math.pi) * t[:, None]) * bands[None, :]
    feats = np.concatenate([t[:, None], np.cos(ang), np.sin(ang)], axis=-1).astype(np.float32)
    return np.pad(feats, ((0, 0), (0, FEAT_PAD - FILTER_EMB)))


def _filter_spectrum(h, w3f_ref, w3b_ref, dl_ref, fwd_ref, L):
    tn = w3f_ref.shape[1]
    rows = lax.broadcasted_iota(jnp.int32, (L, tn), 0)
    t = rows.astype(F32) / L
    window = jnp.exp(-t * dl_ref[...])
    hf = _dot3(h, w3f_ref[...]) * window
    hb = jnp.where(rows > 0, _dot3(h, w3b_ref[...]) * window, 0.0)
    norm = jnp.sqrt(jnp.sum(hf * hf + hb * hb, axis=0, keepdims=True) + EPS)
    hf = hf / norm
    hb = hb / norm
    g_top = _dot(fwd_ref[0:L, :], (hf + hb).astype(BF16))
    g_bot = _dot(fwd_ref[L:2 * L, :], (hf - hb).astype(BF16))
    sign = jnp.where(rows % 2 == 0, 1.0, -1.0)
    nyquist_b = jnp.sum(sign * hb, axis=0, keepdims=True)
    return g_top, g_bot + jnp.where(rows == 0, 2.0 * nyquist_b, 0.0)


def _hyena_conv_kernel(u_ref, x0_ref, bias_ref, fwd_ref, inv_ref, h_ref, w3f_ref, w3b_ref, dl_ref, o_ref, g_s, *, L, jl):
    b = pl.program_id(1)
    layer_row = slice(jl, jl + 1)

    @pl.when(b == 0)
    def _():
        g_top, g_bot = _filter_spectrum(h_ref[...], w3f_ref, w3b_ref, dl_ref, fwd_ref, L)
        g_s[0:L, :] = g_top
        g_s[L:2 * L, :] = g_bot

    tn = o_ref.shape[1]
    first = lax.broadcasted_iota(jnp.int32, (L, tn), 0) == 0
    for s in range(o_ref.shape[0] // L):
        rows = slice(s * L, (s + 1) * L)
        u = u_ref[rows, :]
        spec = _dot(fwd_ref[...], u)
        u_top, u_bot = spec[0:L, :], spec[L:2 * L, :]
        g_top, g_bot = g_s[0:L, :], g_s[L:2 * L, :]
        y_top = u_top * g_top - jnp.where(first, 0.0, u_bot * g_bot)
        y_bot = jnp.where(first, u_bot * g_bot, u_top * g_bot + u_bot * g_top)
        y_spec = jnp.concatenate([y_top, y_bot], axis=0).astype(BF16)
        y = _dot(inv_ref[...], y_spec)
        gated = x0_ref[rows, :].astype(F32) * (y + u.astype(F32) * bias_ref[layer_row, :])
        o_ref[rows, :] = gated.astype(BF16)


def _hyena_conv(u, x0, bias, jl, fwd, inv, h_filter, f_w3, L, row_block0, n_steps, seq_per_step, tn):
    nj = D_MODEL // tn
    rows = seq_per_step * L
    deltas = jnp.asarray(np.abs(np.linspace(MIN_DECAY, MAX_DECAY, D_MODEL, dtype=np.float32)).reshape(1, D_MODEL))
    act = pl.BlockSpec((rows, tn), lambda j, b: (row_block0 + b, j))
    const = lambda j, b: (0, 0)
    return pl.pallas_call(
        functools.partial(_hyena_conv_kernel, L=L, jl=jl),
        grid=(nj, n_steps),
        in_specs=[
            act, act,
            pl.BlockSpec((bias.shape[0], tn), lambda j, b: (0, j)),
            pl.BlockSpec((2 * L, L), const),
            pl.BlockSpec((L, 2 * L), const),
            pl.BlockSpec((None, L, FILTER_HIDDEN), lambda j, b: (jl, 0, 0)),
            pl.BlockSpec((None, FILTER_HIDDEN, tn), lambda j, b: (jl, 0, j)),
            pl.BlockSpec((None, FILTER_HIDDEN, tn), lambda j, b: (jl, 0, nj + j)),
            pl.BlockSpec((1, tn), lambda j, b: (0, j)),
        ],
        out_specs=pl.BlockSpec((rows, tn), lambda j, b: (b, j)),
        out_shape=jax.ShapeDtypeStruct((n_steps * rows, D_MODEL), BF16),
        scratch_shapes=[pltpu.VMEM((2 * L, tn), F32)],
        compiler_params=_params(2),
        name="hyena_conv_%d" % L,
    )(u, x0, bias, fwd, inv, h_filter, f_w3, f_w3, deltas)


def _rope_tables():
    rows = DEC_SEQ // GRID_W
    row = np.repeat(np.arange(rows), GRID_W).astype(np.float32)
    col = np.tile(np.arange(GRID_W), rows).astype(np.float32)
    half = HEAD_DIM // 2
    freqs = (np.float32(ROPE_THETA) ** (-np.arange(0, half, 2, dtype=np.float32) / np.float32(half))).astype(np.float32)
    ang = np.concatenate([row[:, None] * freqs[None, :], col[:, None] * freqs[None, :]], axis=-1)
    cos = np.repeat(np.cos(ang), 2, axis=-1)
    sin = np.repeat(np.sin(ang), 2, axis=-1)
    sign = np.where(np.arange(HEAD_DIM) % 2 == 0, -1.0, 1.0)[None, :]
    pair = lambda a: np.tile(a, (1, 2)).astype(np.float32)
    return pair(cos), pair(sin * sign)


def _qkv_kernel(*refs, n_prev, n_prompt_tiles, layer):
    x_ref, g_ref, sh_ref, sc_ref, w_ref, qg_ref, kg_ref, cos_ref, sin_ref = refs[:9]
    prev_refs, refs = refs[9:9 + 2 * (n_prev > 0)], refs[9 + 2 * (n_prev > 0):]
    q_ref, k_ref, v_ref, kf_ref, vf_ref, w_s = refs
    i = pl.program_id(0)
    tm = q_ref.shape[0]

    @pl.when(i == 0)
    def _():
        w_s[...] = w_ref[...].astype(BF16)

    k_col = N_HEADS * HEAD_DIM
    v_col = k_col + KV_DIM
    jl = n_prev
    q_gain = qg_ref[jl:jl + 1, :] * HEAD_DIM ** -0.5
    k_gain = kg_ref[jl:jl + 1, :]

    def project(rows):
        h = _modulate(x_ref[rows, :], g_ref[layer:layer + 1, :], sh_ref[0], sc_ref[0]).astype(BF16)
        qkv = _dot(h, w_s[...])
        v = qkv[:, v_col:]
        v_ref[rows, :] = v.astype(BF16)

        def pair_norm(col, gain):
            return jnp.concatenate([_rmsnorm(qkv[:, c:c + HEAD_DIM], gain) for c in (col, col + HEAD_DIM)], axis=1)

        return pair_norm, v

    @pl.when(i < n_prompt_tiles)
    def _():
        if n_prev:
            kf_ref[:, 0:n_prev] = prev_refs[0][...]
            vf_ref[:, 0:n_prev] = prev_refs[1][...]
        for r in range(tm // SEQ):
            rows = slice(r * SEQ, (r + 1) * SEQ)
            pair_norm, v = project(rows)
            for col in range(0, k_col, HEAD_PAIR):
                q_ref[rows, col:col + HEAD_PAIR] = pair_norm(col, q_gain).astype(BF16)
            k = pair_norm(k_col, k_gain)
            k_ref[rows, :] = k.astype(BF16)
            for kv in range(N_KV_HEADS):
                cols = slice(kv * HEAD_DIM, (kv + 1) * HEAD_DIM)
                head_rows = pl.ds(kv, SEQ, stride=N_KV_HEADS)
                kf_ref[r, n_prev, head_rows, :] = k[:, cols]
                vf_ref[r, n_prev, head_rows, :] = v[:, cols]

    @pl.when(i >= n_prompt_tiles)
    def _():
        r_id = lax.broadcasted_iota(jnp.int32, (HEAD_PAIR, HEAD_PAIR), 0)
        col_id = lax.broadcasted_iota(jnp.int32, (HEAD_PAIR, HEAD_PAIR), 1)
        swap = jnp.where((r_id ^ 1) == col_id, 1.0, 0.0).astype(BF16)
        for r in range(tm // LATENT_CHAIN):
            rows = slice(r * LATENT_CHAIN, (r + 1) * LATENT_CHAIN)
            pair_norm, _ = project(rows)

            def rotate(x):
                return x * cos_ref[rows, :] + _dot(x.astype(BF16), swap) * sin_ref[rows, :]

            for col in range(0, k_col, HEAD_PAIR):
                q_ref[rows, col:col + HEAD_PAIR] = rotate(pair_norm(col, q_gain)).astype(BF16)
            k_ref[rows, :] = rotate(pair_norm(k_col, k_gain)).astype(BF16)


def _qkv(y, norm_mix, mods, layer, w_qkv, q_gain, k_gain, jl, prev_kv):
    tm = 1024
    npt = N_PROMPT // tm
    tiles_per_req = DEC_SEQ // tm
    cos, sin = _rope_tables()
    rope_spec = pl.BlockSpec((tm, HEAD_PAIR), lambda i: (jnp.maximum(i - npt, 0) % tiles_per_req, 0))
    row = lambda n: pl.BlockSpec((tm, n), lambda i: (i, 0))
    cache_rows = SEQ * N_KV_HEADS
    cache = lambda n: pl.BlockSpec((tm // SEQ, n, cache_rows, HEAD_DIM), lambda i: (jnp.minimum(i, npt - 1), 0, 0, 0))
    return pl.pallas_call(
        functools.partial(_qkv_kernel, n_prev=jl, n_prompt_tiles=npt, layer=layer),
        grid=(N_TOK // tm,),
        in_specs=[
            row(D_MODEL),
            _whole_spec(norm_mix),
            _mod_spec(layer, 0, tm),
            _mod_spec(layer, 1, tm),
            pl.BlockSpec((None, D_MODEL, QKV_DIM), lambda i: (jl, 0, 0), pipeline_mode=pl.Buffered(1)),
            _whole_spec(q_gain), _whole_spec(k_gain),
            rope_spec, rope_spec,
        ] + [cache(jl)] * len(prev_kv),
        out_specs=[row(D_MODEL), row(KV_DIM), row(KV_DIM), cache(jl + 1), cache(jl + 1)],
        out_shape=[
            jax.ShapeDtypeStruct((N_TOK, D_MODEL), BF16),
            jax.ShapeDtypeStruct((N_TOK, KV_DIM), BF16),
            jax.ShapeDtypeStruct((N_TOK, KV_DIM), BF16),
            jax.ShapeDtypeStruct((BATCH, jl + 1, cache_rows, HEAD_DIM), F32),
            jax.ShapeDtypeStruct((BATCH, jl + 1, cache_rows, HEAD_DIM), F32),
        ],
        scratch_shapes=[pltpu.VMEM((D_MODEL, QKV_DIM), BF16)],
        compiler_params=_params(1),
        name="qkv_proj",
    )(y, norm_mix, mods, mods, w_qkv, q_gain, k_gain, jnp.asarray(cos), jnp.asarray(sin), *prev_kv)


def _attention_kernel(*refs, with_cache, seq):
    if with_cache:
        q_ref, k_ref, v_ref, ck_ref, cv_ref, o_ref = refs
    else:
        q_ref, k_ref, v_ref, o_ref = refs
    n_req = k_ref.shape[0] // seq
    tq = q_ref.shape[0] // n_req
    for r in range(n_req):
        q_rows, k_rows = slice(r * tq, (r + 1) * tq), slice(r * seq, (r + 1) * seq)
        for kv in range(k_ref.shape[1] // HEAD_DIM):
            kv_cols = slice(kv * HEAD_DIM, (kv + 1) * HEAD_DIM)
            k = k_ref[k_rows, kv_cols]
            v = v_ref[k_rows, kv_cols]
            if with_cache:
                head_rows = pl.ds(kv, PAST_LEN, stride=N_KV_HEADS)
                k = jnp.concatenate([k, ck_ref[0, 0, head_rows, :].astype(BF16)], axis=0)
                v = jnp.concatenate([v, cv_ref[0, 0, head_rows, :].astype(BF16)], axis=0)
            v_ones = jnp.concatenate([v, jnp.ones_like(v)], axis=1)
            for g in range(GROUP):
                cols = slice((kv * GROUP + g) * HEAD_DIM, (kv * GROUP + g + 1) * HEAD_DIM)
                s = lax.dot_general(q_ref[q_rows, cols], k, (((1,), (1,)), ((), ())), preferred_element_type=F32)
                p = jnp.exp(s - jnp.max(s, axis=-1, keepdims=True))
                o = _dot(p.astype(BF16), v_ones)
                o_ref[q_rows, cols] = (o[:, :HEAD_DIM] / o[:, HEAD_DIM:]).astype(BF16)


def _attention(q, k, v, row0, n_req, seq, tq, kv_per_step, req_per_step=1, cache_k=None, cache_v=None,
               cache_layer=0):
    nq = seq // tq
    n_req //= req_per_step
    tq, seq_rows = tq * req_per_step, seq * req_per_step
    qb0, kb0 = row0 // tq, row0 // seq_rows
    qw, kw = kv_per_step * GROUP * HEAD_DIM, kv_per_step * HEAD_DIM
    in_specs = [
        pl.BlockSpec((tq, qw), lambda b, h, t: (qb0 + b * nq + t, h)),
        pl.BlockSpec((seq_rows, kw), lambda b, h, t: (kb0 + b, h)),
        pl.BlockSpec((seq_rows, kw), lambda b, h, t: (kb0 + b, h)),
    ]
    args = [q, k, v]
    with_cache = cache_k is not None
    if with_cache:
        assert kv_per_step == N_KV_HEADS
        cspec = pl.BlockSpec((1, 1, PAST_LEN * N_KV_HEADS, HEAD_DIM), lambda b, h, t: (b, cache_layer, 0, 0))
        in_specs += [cspec, cspec]
        args += [cache_k, cache_v]
    return pl.pallas_call(
        functools.partial(_attention_kernel, with_cache=with_cache, seq=seq),
        grid=(n_req, N_KV_HEADS // kv_per_step, nq),
        in_specs=in_specs,
        out_specs=pl.BlockSpec((tq, qw), lambda b, h, t: (b * nq + t, h)),
        out_shape=jax.ShapeDtypeStruct((n_req * seq_rows, D_MODEL), BF16),
        compiler_params=_params(3),
        name="attention_%d" % seq,
    )(*args)


def _ffn_kernel(*refs, n_in, n_out, n_prompt_tiles, final, layer):
    y_refs, refs = refs[:n_in], refs[n_in:]
    (ap_ref, as_ref, wo_ref, gm_ref, g_ref, sh_ref, sc_ref, gate_ref, wg_ref, wu_ref, wd_ref, fin_ref) = refs[:12]
    o_refs, refs = refs[12:12 + n_out], refs[12 + n_out:]
    wo_s, wg_s, wu_s, wd_s, ymid_s, f_s, acc_s = refs
    s = pl.program_id(0)
    nf = N_FF_CHUNKS
    tile = jnp.maximum(s - (nf - 1), 0)
    is_prompt = tile < n_prompt_tiles

    def prologue():
        a = jnp.where(is_prompt, ap_ref[...], as_ref[...])
        y = _read_stream(y_refs, is_prompt)
        y_mid = y + gm_ref[0] * _dot(a, wo_s[...])
        ymid_s[...] = y_mid
        f_s[...] = _modulate(y_mid, g_ref[layer:layer + 1, :], sh_ref[0], sc_ref[0]).astype(BF16)

    def chunk(c):
        f = f_s[...]
        hidden = _silu(_dot(f, wg_s[c])) * _dot(f, wu_s[c])
        return _dot(hidden.astype(BF16), wd_s[c])

    def epilogue(acc):
        out = ymid_s[...] + gate_ref[0] * acc
        if final:
            out = _rmsnorm(out, fin_ref[...])
        if n_out == 1:
            o_refs[0][...] = out
        else:
            @pl.when(is_prompt)
            def _():
                o_refs[0][...] = out

            @pl.when(jnp.logical_not(is_prompt))
            def _():
                o_refs[1][...] = out

    @pl.when(s == 0)
    def _():
        wo_s[...] = wo_ref[...].astype(BF16)
        prologue()

    @pl.when(s < nf)
    def _():
        wg_s[s] = wg_ref[...].astype(BF16)
        wu_s[s] = wu_ref[...].astype(BF16)
        wd_s[s] = wd_ref[...].astype(BF16)
        part = chunk(s)

        @pl.when(s == 0)
        def _():
            acc_s[...] = part

        @pl.when(s > 0)
        def _():
            acc_s[...] += part

        @pl.when(s == nf - 1)
        def _():
            epilogue(acc_s[...])

    @pl.when(s >= nf)
    def _():
        prologue()
        acc = chunk(0)
        for c in range(1, nf):
            acc = acc + chunk(c)
        epilogue(acc)


def _ffn(stream, ap, as_, w_out, jl, norm_ffn, mods, layer, wg, wu, wd, final_norm, final):
    tm = 512
    nf = N_FF_CHUNKS
    npt = N_PROMPT // tm
    n_steps = nf + N_TOK // tm - 1
    n_out = 2 if final else 1
    tile_of = lambda s: jnp.maximum(s - (nf - 1), 0)
    chunk_of = lambda s: jnp.minimum(s, nf - 1)
    out_shape = _pair_shapes(D_MODEL, F32) if final else [jax.ShapeDtypeStruct((N_TOK, D_MODEL), F32)]
    return pl.pallas_call(
        functools.partial(_ffn_kernel, n_in=len(stream), n_out=n_out, n_prompt_tiles=npt, final=final, layer=layer),
        grid=(n_steps,),
        in_specs=_stream_specs(len(stream), tm, tile_of) + _pair_specs(tm, D_MODEL, tile_of) + [
            pl.BlockSpec((None, D_MODEL, D_MODEL), lambda s: (jl, 0, 0), pipeline_mode=pl.Buffered(1)),
            _mod_spec(layer, 2, tm, tile_of),
            _whole_spec(norm_ffn),
            _mod_spec(layer, 3, tm, tile_of),
            _mod_spec(layer, 4, tm, tile_of),
            _mod_spec(layer, 5, tm, tile_of),
            pl.BlockSpec((None, D_MODEL, FF_CHUNK), lambda s: (layer, 0, chunk_of(s))),
            pl.BlockSpec((None, D_MODEL, FF_CHUNK), lambda s: (layer, 0, chunk_of(s))),
            pl.BlockSpec((None, FF_CHUNK, D_MODEL), lambda s: (layer, chunk_of(s), 0)),
            pl.BlockSpec((1, D_MODEL), lambda s: (0, 0)),
        ],
        out_specs=_stream_specs(n_out, tm, tile_of),
        out_shape=out_shape,
        scratch_shapes=[
            pltpu.VMEM((D_MODEL, D_MODEL), BF16),
            pltpu.VMEM((nf, D_MODEL, FF_CHUNK), BF16),
            pltpu.VMEM((nf, D_MODEL, FF_CHUNK), BF16),
            pltpu.VMEM((nf, FF_CHUNK, D_MODEL), BF16),
            pltpu.VMEM((tm, D_MODEL), F32),
            pltpu.VMEM((tm, D_MODEL), BF16),
            pltpu.VMEM((tm, D_MODEL), F32),
        ],
        compiler_params=_params(1),
        name="ffn",
    )(*stream, ap, as_, w_out, mods, norm_ffn, mods, mods, mods, wg, wu, wd, final_norm.reshape(1, D_MODEL))


def kernel(x_prompt, x_sample, cache_k, cache_v, c, c_ctx, mod_w, mod_b, norm_mix, norm_ffn, hy_w_in, hy_conv_w, hy_conv_b, hy_f_w1, hy_f_b1, hy_f_freq, hy_f_w2, hy_f_b2, hy_f_w3, hy_bias, hy_w_out, at_w_qkv, at_q_norm, at_k_norm, at_w_out, ffn_w_gate, ffn_w_up, ffn_w_down, final_norm):
    stream = (x_prompt.reshape(N_PROMPT, D_MODEL), x_sample.reshape(N_SAMPLE, D_MODEL))
    f_w1 = jnp.pad(hy_f_w1, ((0, 0), (0, FEAT_PAD - FILTER_EMB), (0, 0)))
    mods, h_short, h_long = _adaln(c_ctx, c, mod_w, mod_b, f_w1, hy_f_b1, hy_f_freq, hy_f_w2, hy_f_b2)
    h_filter = {SEQ: h_short, DEC_SEQ: h_long}

    tables = {}
    for L in (SEQ, DEC_SEQ):
        fwd, inv = _dft_tables(L)
        tables[L] = (jnp.asarray(fwd).astype(BF16), jnp.asarray(inv).astype(BF16))

    bias, q_gain, k_gain = hy_bias, at_q_norm, at_k_norm
    cache_k = cache_k.reshape(DEC_BATCH, -1, PAST_LEN * N_KV_HEADS, HEAD_DIM)
    cache_v = cache_v.reshape(DEC_BATCH, -1, PAST_LEN * N_KV_HEADS, HEAD_DIM)

    new_kv = ()
    for layer in range(DEPTH):
        jl = layer // N_MIXERS
        if layer % N_MIXERS == 0:
            x0, u = _in_proj(stream, norm_mix, mods, layer, hy_w_in, hy_conv_w, hy_conv_b, jl)
            mixed = []
            for L, row_block0, n_steps, seq_per_step, tn in (
                    (SEQ, 0, BATCH // 4, 4, D_MODEL),
                    (DEC_SEQ, N_PROMPT // DEC_SEQ, DEC_BATCH, 1, 512)):
                fwd_bf16, inv_bf16 = tables[L]
                mixed.append(_hyena_conv(u, x0, bias, jl, fwd_bf16, inv_bf16, h_filter[L], hy_f_w3, L, row_block0,
                                         n_steps, seq_per_step, tn))
            w_out = hy_w_out
        else:
            q, k, v, *new_kv = _qkv(stream[0], norm_mix, mods, layer, at_w_qkv, q_gain, k_gain, jl, new_kv)
            mixed = [_attention(q, k, v, 0, BATCH, SEQ, SEQ, N_KV_HEADS, req_per_step=4),
                     _attention(q, k, v, N_PROMPT, DEC_BATCH, DEC_SEQ, 1024, N_KV_HEADS, cache_k=cache_k, cache_v=cache_v,
                                cache_layer=jl)]
            w_out = at_w_out
        stream = tuple(_ffn(stream, mixed[0], mixed[1], w_out, jl, norm_ffn, mods, layer,
                            ffn_w_gate, ffn_w_up, ffn_w_down, final_norm, final=(layer == DEPTH - 1)))

    y_prompt, y_sample = stream
    new_k, new_v = (a.reshape(BATCH, -1, SEQ, N_KV_HEADS, HEAD_DIM) for a in new_kv)
    return (y_prompt.reshape(BATCH, SEQ, D_MODEL), y_sample.reshape(DEC_BATCH, DEC_SEQ, D_MODEL), new_k, new_v)
```
